```python
import math
import jax
import jax.numpy as jnp
from jax import lax
import numpy as np

D_MODEL = 1024
BATCH = 4
SEQ = 4096
DEPTH = 4

GRID_W = 64
CTX_LEN = 256
N_MIXERS = 3
FFN_DIM = 256 * ((8 * D_MODEL // 3 + 255) // 256)
ADA_CHUNKS = 9
NORM_EPS = 1e-6
RET_HEADS = 4
RET_QK_HEAD = D_MODEL // RET_HEADS
RET_V_HEAD = 2 * D_MODEL // RET_HEADS
RET_QK = RET_HEADS * RET_QK_HEAD
RET_V = RET_HEADS * RET_V_HEAD
RET_CHUNK = 128
ROPE_BASE = 10000.0
POOL_WINDOWS = (2, 4, 8, 16)
POOL_GROUPS = len(POOL_WINDOWS)
POOL_GROUP_DIM = D_MODEL // POOL_GROUPS
HYENA_ORDER = 2
HYENA_EMB = 33
HYENA_BANDS = (HYENA_EMB - 1) // 2
HYENA_FILTER_WIDTH = 64
HYENA_SHORT = 3
HYENA_TARGET = 1e-2
HYENA_FAST = 0.3
HYENA_SLOW = 1.5
HYENA_MAX_DECAY = math.log(HYENA_TARGET) / HYENA_FAST
HYENA_MIN_DECAY = math.log(HYENA_TARGET) / HYENA_SLOW
N_RET = (DEPTH + 2) // 3
N_POOL = (DEPTH + 1) // 3
N_HYENA = DEPTH // 3

kernel_name = 'hybrid_retention_pool_hyena_macaron_dit'

F32 = jnp.float32


def _rmsnorm(x, g):
    xf = x.astype(F32)
    y = xf * lax.rsqrt(jnp.mean(xf * xf, axis=-1, keepdims=True) + NORM_EPS)
    return (y * g.astype(F32)).astype(x.dtype)


def _ada(cond, w, b):
    m = jax.nn.silu(cond) @ w + b
    return jnp.split(m[..., None, :], ADA_CHUNKS, axis=-1)


def _adaln(h, g, shift, scale):
    return _rmsnorm(h, g) * (1.0 + scale) + shift


def _half_ffn(h, mods, g, w1, w3, w2):
    shift, scale, gate = mods
    u = _adaln(h, g, shift, scale)
    return h + 0.5 * gate * ((jax.nn.silu(u @ w1) * (u @ w3)) @ w2)


def _rotary(x, pos):
    half = x.shape[-1] // 2
    inv = 1.0 / (ROPE_BASE ** jnp.linspace(0.0, 1.0, half, dtype=F32))
    ang = pos[:, None] * inv[None, :]
    cos = jnp.cos(ang)[None, :, None, :]
    sin = jnp.sin(ang)[None, :, None, :]
    x1, x2 = x[..., :half], x[..., half:]
    return jnp.concatenate([x1 * cos - x2 * sin, x2 * cos + x1 * sin], axis=-1)


def _retention_scan(q, k, v, log_gamma, state0, strict):
    bsz, nh, t, _ = q.shape
    dv = v.shape[-1]
    nc = t // RET_CHUNK

    def chunks(a):
        return jnp.moveaxis(a.reshape(bsz, nh, nc, RET_CHUNK, a.shape[-1]), 2, 0)

    n = jnp.arange(RET_CHUNK, dtype=F32)
    rel = n[:, None] - n[None, :]
    mask = (rel > 0) if strict else (rel >= 0)
    dmat = jnp.where(mask, jnp.exp(log_gamma[:, None, None] * jnp.maximum(rel, 0.0)), 0.0)
    q_dec = jnp.exp(log_gamma[:, None] * (n + 1.0))[..., None]
    k_dec = jnp.exp(log_gamma[:, None] * (RET_CHUNK - 1.0 - n))[..., None]
    c_dec = jnp.exp(log_gamma * RET_CHUNK)[:, None, None]

    def step(state, blk):
        qb, kb, vb = blk
        scores = jnp.einsum('bhnd,bhmd->bhnm', qb, kb) * dmat
        out = (jnp.einsum('bhnm,bhme->bhne', scores, vb)
               + jnp.einsum('bhnd,bhde->bhne', qb, state) * q_dec)
        state = state * c_dec + jnp.einsum('bhmd,bhme->bhde', kb * k_dec, vb)
        return state, out

    state, out = lax.scan(step, state0, (chunks(q), chunks(k), chunks(v)))
    out = jnp.moveaxis(out, 0, 2).reshape(bsz, nh, t, dv)
    return out, state


def _retention_readout(o, g, w_out):
    mu = jnp.mean(o, axis=-1, keepdims=True)
    var = jnp.mean(jnp.square(o - mu), axis=-1, keepdims=True)
    o = (o - mu) * lax.rsqrt(var + NORM_EPS)
    bsz, nh, t, dv = o.shape
    y = jnp.transpose(o, (0, 2, 1, 3)).reshape(bsz, t, nh * dv)
    y = jax.nn.silu(g) * y
    return y.astype(w_out.dtype) @ w_out


def _retention_mixer(uc, ul, w_in, w_out, decay, ctx_out):
    bsz, ctx_len, _ = uc.shape
    log_gamma = jax.nn.log_sigmoid(decay.astype(F32))

    def project(u, offset):
        t = u.shape[1]
        p = (u @ w_in).astype(F32)
        q, k, v, g = jnp.split(p, [RET_QK, 2 * RET_QK, 2 * RET_QK + RET_V], axis=-1)
        pos = jnp.arange(t, dtype=F32) + offset
        q = _rotary(q.reshape(bsz, t, RET_HEADS, RET_QK_HEAD), pos)
        k = _rotary(k.reshape(bsz, t, RET_HEADS, RET_QK_HEAD), pos) * (RET_QK_HEAD ** -0.5)
        v = v.reshape(bsz, t, RET_HEADS, RET_V_HEAD)
        heads = lambda a: jnp.transpose(a, (0, 2, 1, 3))
        return heads(q), heads(k), heads(v), g

    flip = lambda a: jnp.flip(a, axis=2)
    zero = jnp.zeros((bsz, RET_HEADS, RET_QK_HEAD, RET_V_HEAD), F32)
    qc, kc, vc, gc = project(uc, 0.0)
    oc_f, s_f = _retention_scan(qc, kc, vc, log_gamma[0], zero, False)
    oc_b, s_b = _retention_scan(flip(qc), flip(kc), flip(vc), log_gamma[1], zero, True)
    ql, kl, vl, gl = project(ul, float(ctx_len))
    ol_f, _ = _retention_scan(ql, kl, vl, log_gamma[0], s_f, False)
    ol_b, _ = _retention_scan(flip(ql), flip(kl), flip(vl), log_gamma[1], s_b, True)
    yl = _retention_readout(ol_f + flip(ol_b), gl, w_out).astype(ul.dtype)
    yc = _retention_readout(oc_f + flip(oc_b), gc, w_out).astype(uc.dtype) if ctx_out else None
    return yc, yl


def _multiscale_pool(u, w_grp, b_grp, scale):
    bsz, r, w, d = u.shape
    uf = u.astype(F32)
    cs = jnp.pad(lax.cumsum(uf, axis=2), ((0, 0), (0, 0), (1, 0), (0, 0)))
    pos = jnp.arange(w)
    diffs = []
    for gi, win in enumerate(POOL_WINDOWS):
        lo = jnp.clip(pos - win // 2, 0, w)
        hi = jnp.clip(pos - win // 2 + win, 0, w)
        sl = slice(gi * POOL_GROUP_DIM, (gi + 1) * POOL_GROUP_DIM)
        csg = cs[..., sl]
        mean = (jnp.take(csg, hi, axis=2) - jnp.take(csg, lo, axis=2)) / (hi - lo).astype(F32)[:, None]
        diffs.append(mean - uf[..., sl])
    dlt = jnp.stack(diffs, axis=-2)
    y = jnp.einsum('brwgc,gce->brwge', dlt, w_grp.astype(F32)).reshape(bsz, r, w, d)
    return ((y + b_grp.astype(F32)) * scale.astype(F32)).astype(u.dtype)


def _hyena_spectra(length, w_pos, b_pos, w_mid, b_mid, freq, w_filt):
    t = jnp.linspace(0.0, 1.0, length, dtype=F32)[:, None]
    bands = jnp.linspace(1e-4, HYENA_BANDS - 1, HYENA_BANDS, dtype=F32)
    ang = (2.0 * math.pi / length) * jnp.arange(length, dtype=F32)[:, None] * bands[None, :]
    feat = jnp.concatenate([t, jnp.cos(ang), -jnp.sin(ang)], axis=-1)
    fr = freq.astype(F32)
    hdn = jnp.sin(fr * (feat @ w_pos.astype(F32) + b_pos.astype(F32)))
    hdn = jnp.sin(fr * (hdn @ w_mid.astype(F32) + b_mid.astype(F32)))
    h = (hdn @ w_filt.astype(F32)).reshape(length, HYENA_ORDER, 2, D_MODEL)
    deltas = jnp.abs(jnp.linspace(HYENA_MIN_DECAY, HYENA_MAX_DECAY, D_MODEL, dtype=F32))
    h = h * jnp.exp(-t * deltas[None, :])[:, None, None, :]
    h_fwd, h_bwd = h[:, :, 0], h[:, :, 1]
    two_sided = jnp.concatenate(
        [h_fwd, jnp.zeros((1, HYENA_ORDER, D_MODEL), F32), h_bwd[:0:-1]], axis=0)
    two_sided = two_sided * lax.rsqrt(jnp.sum(jnp.square(two_sided), axis=0, keepdims=True) + NORM_EPS)
    return jnp.fft.rfft(two_sided, axis=0)


def _long_conv(u, spec, bias):
    length = u.shape[1]
    uf = jnp.fft.rfft(u, n=2 * length, axis=1)
    y = jnp.fft.irfft(uf * spec[None], n=2 * length, axis=1)[:, :length]
    return y + u * bias


def _hyena_mixer(u, w_in, b_in, w_short, b_short, w_pos, b_pos, w_mid, b_mid, freq, w_filt, fbias, w_out, b_out):
    bsz, t, d = u.shape
    p = u @ w_in + b_in
    p = lax.conv_general_dilated(
        p, w_short[:, None, :], window_strides=(1,), padding=[(1, 1)],
        dimension_numbers=('NWC', 'WIO', 'NWC'), feature_group_count=3 * d) + b_short
    v, x1, x2 = jnp.split(p.astype(F32), 3, axis=-1)
    spec = _hyena_spectra(t, w_pos, b_pos, w_mid, b_mid, freq, w_filt)
    fb = fbias.astype(F32)
    z = x1 * _long_conv(v, spec[:, 0], fb[0])
    z = x2 * _long_conv(z, spec[:, 1], fb[1])
    return z.astype(u.dtype) @ w_out + b_out


def setup_inputs(seed: int = 0) -> dict:
    key = jax.random.key(seed)
    keys = list(jax.random.split(key, 32))

    def nrm(shape, scale):
        return scale * jax.random.normal(keys.pop(), shape, F32)

    d = D_MODEL
    decay_init = jnp.log(2.0 ** (5.0 + jnp.arange(RET_HEADS, dtype=F32)) - 1.0)
    return {
        'x': nrm((BATCH, SEQ, d), 1.0),
        'c': nrm((BATCH, d), 1.0),
        'ctx': nrm((BATCH, CTX_LEN, d), 1.0),
        'c_ctx': nrm((d,), 1.0),
        'ada_w': nrm((DEPTH, d, ADA_CHUNKS * d), 0.5 * d ** -0.5),
        'ada_b': nrm((DEPTH, ADA_CHUNKS * d), 0.02),
        'norm_g': 1.0 + nrm((DEPTH, 3, d), 0.02),
        'ffn_w1': nrm((DEPTH, 2, d, FFN_DIM), d ** -0.5),
        'ffn_w3': nrm((DEPTH, 2, d, FFN_DIM), d ** -0.5),
        'ffn_w2': nrm((DEPTH, 2, FFN_DIM, d), FFN_DIM ** -0.5),
        'ret_w_in': nrm((N_RET, d, 2 * RET_QK + 2 * RET_V), d ** -0.5),
        'ret_w_out': nrm((N_RET, RET_V, d), RET_V ** -0.5),
        'ret_decay': decay_init + nrm((N_RET, 2, RET_HEADS), 0.1),
        'pool_w': nrm((N_POOL, POOL_GROUPS, POOL_GROUP_DIM, POOL_GROUP_DIM), POOL_GROUP_DIM ** -0.5),
        'pool_b': nrm((N_POOL, d), 0.02),
        'pool_scale': 1.0 + nrm((N_POOL, d), 0.02),
        'hy_w_in': nrm((N_HYENA, d, 3 * d), d ** -0.5),
        'hy_b_in': nrm((N_HYENA, 3 * d), 0.02),
        'hy_w_short': nrm((N_HYENA, HYENA_SHORT, 3 * d), HYENA_SHORT ** -0.5),
        'hy_b_short': nrm((N_HYENA, 3 * d), 0.02),
        'hy_w_pos': nrm((N_HYENA, HYENA_EMB, HYENA_FILTER_WIDTH), HYENA_EMB ** -0.5),
        'hy_b_pos': nrm((N_HYENA, HYENA_FILTER_WIDTH), 0.5),
        'hy_w_mid': nrm((N_HYENA, HYENA_FILTER_WIDTH, HYENA_FILTER_WIDTH), HYENA_FILTER_WIDTH ** -0.5),
        'hy_b_mid': nrm((N_HYENA, HYENA_FILTER_WIDTH), 0.5),
        'hy_freq': 1.0 + nrm((N_HYENA, HYENA_FILTER_WIDTH), 0.02),
        'hy_w_filt': nrm((N_HYENA, HYENA_FILTER_WIDTH, HYENA_ORDER * 2 * d), HYENA_FILTER_WIDTH ** -0.5),
        'hy_bias': nrm((N_HYENA, HYENA_ORDER, d), 0.5),
        'hy_w_out': nrm((N_HYENA, d, d), d ** -0.5),
        'hy_b_out': nrm((N_HYENA, d), 0.02),
        'final_g': 1.0 + nrm((d,), 0.02),
    }


def reference(x, c, ctx, c_ctx, ada_w, ada_b, norm_g, ffn_w1, ffn_w3, ffn_w2,
              ret_w_in, ret_w_out, ret_decay, pool_w, pool_b, pool_scale,
              hy_w_in, hy_b_in, hy_w_short, hy_b_short, hy_w_pos, hy_b_pos, hy_w_mid, hy_b_mid,
              hy_freq, hy_w_filt, hy_bias, hy_w_out, hy_b_out, final_g):
    bsz, seq, d = x.shape
    rows = seq // GRID_W
    h_lat, h_ctx = x, ctx
    for layer in range(DEPTH):
        kind = layer % N_MIXERS
        slot = layer // N_MIXERS
        last = layer == DEPTH - 1
        ctx_out = not last
        ctx_live = ctx_out or kind == 0
        ml = _ada(c, ada_w[layer], ada_b[layer])
        mc = _ada(c_ctx, ada_w[layer], ada_b[layer])
        h_lat = _half_ffn(h_lat, ml[0:3], norm_g[layer, 0], ffn_w1[layer, 0], ffn_w3[layer, 0], ffn_w2[layer, 0])
        if ctx_live:
            h_ctx = _half_ffn(h_ctx, mc[0:3], norm_g[layer, 0], ffn_w1[layer, 0], ffn_w3[layer, 0], ffn_w2[layer, 0])
        ul = _adaln(h_lat, norm_g[layer, 1], ml[3], ml[4])
        uc = _adaln(h_ctx, norm_g[layer, 1], mc[3], mc[4]) if ctx_live else None
        if kind == 0:
            yc, yl = _retention_mixer(uc, ul, ret_w_in[slot], ret_w_out[slot], ret_decay[slot], ctx_out)
        elif kind == 1:
            pp = (pool_w[slot], pool_b[slot], pool_scale[slot])
            yl = _multiscale_pool(ul.reshape(bsz, rows, GRID_W, d), *pp).reshape(bsz, seq, d)
            yc = _multiscale_pool(uc[:, None], *pp)[:, 0] if ctx_out else None
        else:
            hp = (hy_w_in[slot], hy_b_in[slot], hy_w_short[slot], hy_b_short[slot], hy_w_pos[slot],
                  hy_b_pos[slot], hy_w_mid[slot], hy_b_mid[slot], hy_freq[slot], hy_w_filt[slot],
                  hy_bias[slot], hy_w_out[slot], hy_b_out[slot])
            yl = _hyena_mixer(ul, *hp)
            yc = _hyena_mixer(uc, *hp) if ctx_out else None
        h_lat = h_lat + ml[5] * yl
        if ctx_out:
            h_ctx = h_ctx + mc[5] * yc
        h_lat = _half_ffn(h_lat, ml[6:9], norm_g[layer, 2], ffn_w1[layer, 1], ffn_w3[layer, 1], ffn_w2[layer, 1])
        if ctx_out:
            h_ctx = _half_ffn(h_ctx, mc[6:9], norm_g[layer, 2], ffn_w1[layer, 1], ffn_w3[layer, 1], ffn_w2[layer, 1])
    return _rmsnorm(h_lat, final_g)
```

```python
import functools
import math

import numpy as np
import jax
import jax.numpy as jnp
from jax import lax
from jax.experimental import pallas as pl
from jax.experimental.pallas import tpu as pltpu

F32 = jnp.float32
BF16 = jnp.bfloat16

GRID_W = 64
ADA_CHUNKS = 9
NORM_EPS = 1e-6
RET_HEADS = 4
RET_CHUNK = 128
ROPE_BASE = 10000.0
POOL_WINDOWS = (2, 4, 8, 16)
HYENA_ORDER = 2
HYENA_EMB = 33
HYENA_BANDS = (HYENA_EMB - 1) // 2
HYENA_TARGET = 1e-2
HYENA_FAST = 0.3
HYENA_SLOW = 1.5
HYENA_MAX_DECAY = math.log(HYENA_TARGET) / HYENA_FAST
HYENA_MIN_DECAY = math.log(HYENA_TARGET) / HYENA_SLOW

V7X_LANES = 128
V7X_VMEM_LIMIT_BYTES = 56 * 1024 * 1024
MOD_ROWS = 8
POOL_TILE = 256
RET_TILE = 256
DFT_FREQ_TILE = 512
DFT_TIME_TILE = 512
DFT_COL_TILE = 256
HIGHEST = lax.Precision.HIGHEST


def _params(n_axes):
    return pltpu.CompilerParams(
        dimension_semantics=("arbitrary",) * n_axes,
        vmem_limit_bytes=V7X_VMEM_LIMIT_BYTES)


def _const_spec(shape):
    zeros = (0,) * len(shape)
    return pl.BlockSpec(shape, lambda *_: zeros, pipeline_mode=pl.Buffered(1))


def _mod_spec(d, layer, row_fn):
    return pl.BlockSpec((None, None, ADA_CHUNKS, d), lambda *idx: (layer, row_fn(*idx), 0, 0))


def _mm(a, b):
    return jnp.dot(a, b, preferred_element_type=F32)


def _mm_f32(a, b):
    return jnp.dot(a, b, preferred_element_type=F32, precision=HIGHEST)


def _silu(x):
    return x * jax.nn.sigmoid(x)


def _adaln(x, g, shift, scale):
    ms = jnp.mean(x * x, axis=-1, keepdims=True)
    y = x * lax.rsqrt(ms + NORM_EPS) * g
    return y * (1.0 + scale) + shift


def _ada_kernel(c_ref, w_ref, b_ref, o_ref):
    s = _silu(c_ref[...]).astype(BF16)
    o_ref[...] = _mm(s, w_ref[...].astype(BF16)) + b_ref[...]


def _ada_all(cond, ada_w, ada_b):
    depth, d, nd = ada_w.shape
    return pl.pallas_call(
        _ada_kernel,
        grid=(depth, nd // d),
        in_specs=[
            pl.BlockSpec((MOD_ROWS, d), lambda l, j: (0, 0)),
            pl.BlockSpec((None, d, d), lambda l, j: (l, 0, j)),
            pl.BlockSpec((None, 1, d), lambda l, j: (l, 0, j)),
        ],
        out_specs=pl.BlockSpec((None, MOD_ROWS, d), lambda l, j: (l, 0, j)),
        out_shape=jax.ShapeDtypeStruct((depth, MOD_ROWS, nd), F32),
        compiler_params=_params(2),
        name="ada",
    )(cond, ada_w, ada_b.reshape(depth, 1, nd))


def _ffn_kernel(x_ref, mod_ref, g_ref, w1_ref, w3_ref, w2_ref, fg_ref, o_ref, *, base, final):
    x = x_ref[...]
    u = _adaln(x, g_ref[...], mod_ref[base:base + 1, :], mod_ref[base + 1:base + 2, :]).astype(BF16)
    h1 = _mm(u, w1_ref[...])
    h3 = _mm(u, w3_ref[...])
    a = (_silu(h1) * h3).astype(BF16)
    y = x + (0.5 * mod_ref[base + 2:base + 3, :]) * _mm(a, w2_ref[...])
    if final:
        ms = jnp.mean(y * y, axis=-1, keepdims=True)
        y = y * lax.rsqrt(ms + NORM_EPS) * fg_ref[...]
    o_ref[...] = y


def _half_ffn(x, mods, layer, row_fn, base, g, w1, w3, w2, final_g, final=False, tm=512):
    n_g, t, d = x.shape
    f = w1.shape[1]
    tm = min(tm, t)
    return pl.pallas_call(
        functools.partial(_ffn_kernel, base=base, final=final),
        grid=(n_g, t // tm),
        in_specs=[
            pl.BlockSpec((None, tm, d), lambda b, i: (b, i, 0)),
            _mod_spec(d, layer, row_fn),
            _const_spec((1, d)),
            _const_spec((d, f)),
            _const_spec((d, f)),
            _const_spec((f, d)),
            _const_spec((1, d)),
        ],
        out_specs=pl.BlockSpec((None, tm, d), lambda b, i: (b, i, 0)),
        out_shape=jax.ShapeDtypeStruct(x.shape, F32),
        compiler_params=_params(2),
        name="half_ffn",
    )(x, mods, g.reshape(1, d), w1, w3, w2, final_g.reshape(1, d))


def _ret_proj_kernel(x_ref, mod_ref, g_ref, w_ref, cos_ref, sin_ref, q_ref, k_ref, v_ref, gate_ref,
                     *, n_heads, dk, dv):
    u = _adaln(x_ref[...], g_ref[...], mod_ref[3:4, :], mod_ref[4:5, :]).astype(BF16)
    p = _mm(u, w_ref[...])
    cos = cos_ref[...]
    sin = sin_ref[...]
    half = dk // 2
    qk = n_heads * dk
    k_scale = dk ** -0.5

    def rot(a):
        a1, a2 = a[:, :half], a[:, half:]
        return jnp.concatenate([a1 * cos - a2 * sin, a2 * cos + a1 * sin], axis=-1)

    for h in range(n_heads):
        q_ref[:, h * dk:(h + 1) * dk] = rot(p[:, h * dk:(h + 1) * dk]).astype(BF16)
        k_ref[:, h * dk:(h + 1) * dk] = (rot(p[:, qk + h * dk:qk + (h + 1) * dk]) * k_scale).astype(BF16)
    v_ref[...] = p[:, 2 * qk:2 * qk + n_heads * dv].astype(BF16)
    gate_ref[...] = p[:, 2 * qk + n_heads * dv:].astype(BF16)


def _ret_scan_kernel(q_ref, k_ref, v_ref, dmat_ref, qdec_ref, kdec_ref, cdec_ref, o_ref,
                     state_ref, fwd_ref, *, n_chunks, bwd_chunk):
    s = pl.program_id(2)

    @pl.when((s == 0) | (s == n_chunks))
    def _():
        state_ref[...] = jnp.zeros_like(state_ref)

    q = q_ref[...]
    k = k_ref[...]
    v = v_ref[...]
    scores = lax.dot_general(q, k, (((1,), (1,)), ((), ())), preferred_element_type=F32) * dmat_ref[...]
    state = state_ref[...]
    out = _mm(scores.astype(BF16), v) + _mm(q, state.astype(BF16)) * qdec_ref[...]
    kd = (k.astype(F32) * kdec_ref[...]).astype(BF16)
    state_ref[...] = state * cdec_ref[...] + lax.dot_general(
        kd, v, (((0,), (0,)), ((), ())), preferred_element_type=F32)

    @pl.when(s < n_chunks)
    def _():
        fwd_ref[pl.ds(pl.multiple_of(s * RET_CHUNK, RET_CHUNK), RET_CHUNK), :] = out

    @pl.when(s >= n_chunks)
    def _():
        c = bwd_chunk(s - n_chunks)
        o = out + fwd_ref[pl.ds(pl.multiple_of(c * RET_CHUNK, RET_CHUNK), RET_CHUNK), :]
        mu = jnp.mean(o, axis=-1, keepdims=True)
        oc = o - mu
        var = jnp.mean(oc * oc, axis=-1, keepdims=True)
        o_ref[...] = (oc * lax.rsqrt(var + NORM_EPS)).astype(o_ref.dtype)


def _ret_out_kernel(x_ref, mod_ref, y_ref, gate_ref, w_ref, o_ref):
    z = (_silu(gate_ref[...].astype(F32)) * y_ref[...].astype(F32)).astype(BF16)
    o_ref[...] = x_ref[...] + mod_ref[5:6, :] * _mm(z, w_ref[...])


def _retention_layer(h_ctx, h_lat, mods, layer, g, w_in, w_out, decay, ctx_out):
    bsz, t_ctx, d = h_ctx.shape
    t_lat = h_lat.shape[1]
    t_all = t_ctx + t_lat
    n_heads = RET_HEADS
    dk = d // n_heads
    dv = 2 * d // n_heads
    qk = n_heads * dk
    vw = n_heads * dv
    tm = RET_TILE
    assert t_ctx % tm == 0 and t_lat % tm == 0
    ctx_tiles = t_ctx // tm
    n_chunks = t_all // RET_CHUNK
    ctx_chunks = t_ctx // RET_CHUNK
    ctx_row = bsz

    hcat = jnp.concatenate([h_ctx, h_lat], axis=1)
    pos = jnp.arange(t_all, dtype=F32)
    inv = 1.0 / (ROPE_BASE ** jnp.linspace(0.0, 1.0, dk // 2, dtype=F32))
    ang = pos[:, None] * inv[None, :]

    q, k, v, gate = pl.pallas_call(
        functools.partial(_ret_proj_kernel, n_heads=n_heads, dk=dk, dv=dv),
        grid=(bsz, t_all // tm),
        in_specs=[
            pl.BlockSpec((None, tm, d), lambda b, i: (b, i, 0)),
            _mod_spec(d, layer, lambda b, i: jnp.where(i < ctx_tiles, ctx_row, b)),
            _const_spec((1, d)),
            _const_spec((d, 2 * qk + 2 * vw)),
            pl.BlockSpec((tm, dk // 2), lambda b, i: (i, 0)),
            pl.BlockSpec((tm, dk // 2), lambda b, i: (i, 0)),
        ],
        out_specs=[
            pl.BlockSpec((None, tm, qk), lambda b, i: (b, i, 0)),
            pl.BlockSpec((None, tm, qk), lambda b, i: (b, i, 0)),
            pl.BlockSpec((None, tm, vw), lambda b, i: (b, i, 0)),
            pl.BlockSpec((None, tm, vw), lambda b, i: (b, i, 0)),
        ],
        out_shape=[
            jax.ShapeDtypeStruct((bsz, t_all, qk), BF16),
            jax.ShapeDtypeStruct((bsz, t_all, qk), BF16),
            jax.ShapeDtypeStruct((bsz, t_all, vw), BF16),
            jax.ShapeDtypeStruct((bsz, t_all, vw), BF16),
        ],
        compiler_params=_params(2),
        name="ret_proj",
    )(hcat, mods, g.reshape(1, d), w_in, jnp.cos(ang), jnp.sin(ang))

    lg = jax.nn.log_sigmoid(decay.astype(F32))[:, :, None]
    n = jnp.arange(RET_CHUNK, dtype=F32)
    rel = n[:, None] - n[None, :]
    dm_f = jnp.where(rel >= 0, jnp.exp(lg[0][:, :, None] * jnp.maximum(rel, 0.0)), 0.0)
    dm_b = jnp.where(rel < 0, jnp.exp(lg[1][:, :, None] * jnp.maximum(-rel, 0.0)), 0.0)
    dmat = jnp.stack([dm_f, dm_b])
    qd = jnp.stack([jnp.exp(lg[0] * (n + 1.0)), jnp.exp(lg[1] * (RET_CHUNK - n))])
    kd = jnp.stack([jnp.exp(lg[0] * (RET_CHUNK - 1.0 - n)), jnp.exp(lg[1] * n)])
    cd = jnp.exp(lg * RET_CHUNK)
    qdec = jnp.broadcast_to(qd[..., None], (2, n_heads, RET_CHUNK, dv))
    kdec = jnp.broadcast_to(kd[..., None], (2, n_heads, RET_CHUNK, dk))
    cdec = jnp.broadcast_to(cd[..., None], (2, n_heads, 1, dv))

    def bwd_chunk(j):
        return jnp.where(j < ctx_chunks, ctx_chunks - 1 - j, n_chunks - 1 + ctx_chunks - j)

    def chunk(s):
        return jnp.where(s < n_chunks, s, bwd_chunk(s - n_chunks))

    def out_chunk(s):
        return bwd_chunk(jnp.maximum(s - n_chunks, 0))

    y = pl.pallas_call(
        functools.partial(_ret_scan_kernel, n_chunks=n_chunks, bwd_chunk=bwd_chunk),
        grid=(bsz, n_heads, 2 * n_chunks),
        in_specs=[
            pl.BlockSpec((None, RET_CHUNK, dk), lambda b, h, s: (b, chunk(s), h)),
            pl.BlockSpec((None, RET_CHUNK, dk), lambda b, h, s: (b, chunk(s), h)),
            pl.BlockSpec((None, RET_CHUNK, dv), lambda b, h, s: (b, chunk(s), h)),
            pl.BlockSpec((None, None, RET_CHUNK, RET_CHUNK), lambda b, h, s: (s // n_chunks, h, 0, 0)),
            pl.BlockSpec((None, None, RET_CHUNK, dv), lambda b, h, s: (s // n_chunks, h, 0, 0)),
            pl.BlockSpec((None, None, RET_CHUNK, dk), lambda b, h, s: (s // n_chunks, h, 0, 0)),
            pl.BlockSpec((None, None, 1, dv), lambda b, h, s: (s // n_chunks, h, 0, 0)),
        ],
        out_specs=pl.BlockSpec((None, RET_CHUNK, dv), lambda b, h, s: (b, out_chunk(s), h)),
        out_shape=jax.ShapeDtypeStruct((bsz, t_all, vw), BF16),
        scratch_shapes=[pltpu.VMEM((dk, dv), F32), pltpu.VMEM((t_all, dv), F32)],
        compiler_params=_params(3),
        name="ret_scan",
    )(q, k, v, dmat, qdec, kdec, cdec)

    def readout(x, tile0, row_fn):
        return pl.pallas_call(
            _ret_out_kernel,
            grid=(bsz, x.shape[1] // tm),
            in_specs=[
                pl.BlockSpec((None, tm, d), lambda b, i: (b, i, 0)),
                _mod_spec(d, layer, row_fn),
                pl.BlockSpec((None, tm, vw), lambda b, i: (b, i + tile0, 0)),
                pl.BlockSpec((None, tm, vw), lambda b, i: (b, i + tile0, 0)),
                _const_spec((vw, d)),
            ],
            out_specs=pl.BlockSpec((None, tm, d), lambda b, i: (b, i, 0)),
            out_shape=jax.ShapeDtypeStruct(x.shape, F32),
            compiler_params=_params(2),
            name="ret_out",
        )(x, mods, y, gate, w_out)

    new_lat = readout(h_lat, ctx_tiles, lambda b, i: b)
    new_ctx = readout(h_ctx, 0, lambda b, i: ctx_row) if ctx_out else None
    return new_ctx, new_lat


def _pool_tables(seg_len):
    assert POOL_TILE % seg_len == 0
    r = np.arange(POOL_TILE)
    seg, pos = r // seg_len, r % seg_len
    mats, invs = [], []
    for win in POOL_WINDOWS:
        lo = np.clip(pos - win // 2, 0, seg_len)
        hi = np.clip(pos - win // 2 + win, 0, seg_len)
        col_seg, col_pos = seg[None, :], pos[None, :]
        member = (col_seg == seg[:, None]) & (col_pos >= lo[:, None]) & (col_pos < hi[:, None])
        mats.append(member.astype(np.float32))
        invs.append((1.0 / (hi - lo)).astype(np.float32))
    return np.stack(mats), np.stack(invs)


def _pool_kernel(x_ref, mod_ref, g_ref, s_ref, inv_ref, w_ref, b_ref, sc_ref, o_ref, *, n_groups, dg):
    x = x_ref[...]
    u = _adaln(x, g_ref[...], mod_ref[3:4, :], mod_ref[4:5, :])
    gate = mod_ref[5:6, :]
    for gi in range(n_groups):
        sl = slice(gi * dg, (gi + 1) * dg)
        ug = u[:, sl]
        hi = ug.astype(BF16)
        lo = (ug - hi.astype(F32)).astype(BF16)
        win_sum = _mm(s_ref[gi], hi) + _mm(s_ref[gi], lo)
        dlt = win_sum * inv_ref[gi] - ug
        yg = _mm(dlt.astype(BF16), w_ref[gi])
        o_ref[:, sl] = x[:, sl] + gate[:, sl] * ((yg + b_ref[:, sl]) * sc_ref[:, sl])


def _pool_layer(x, seg_len, mods, layer, row_fn, g, w_grp, b_grp, scale):
    bsz, t, d = x.shape
    n_groups = len(POOL_WINDOWS)
    dg = d // n_groups
    mats, invs = _pool_tables(seg_len)
    s_tab = jnp.asarray(mats, BF16)
    inv_tab = jnp.asarray(np.broadcast_to(invs[:, :, None], (n_groups, POOL_TILE, dg)).copy(), F32)
    return pl.pallas_call(
        functools.partial(_pool_kernel, n_groups=n_groups, dg=dg),
        grid=(bsz, t // POOL_TILE),
        in_specs=[
            pl.BlockSpec((None, POOL_TILE, d), lambda b, i: (b, i, 0)),
            _mod_spec(d, layer, row_fn),
            _const_spec((1, d)),
            _const_spec((n_groups, POOL_TILE, POOL_TILE)),
            _const_spec((n_groups, POOL_TILE, dg)),
            _const_spec((n_groups, dg, dg)),
            _const_spec((1, d)),
            _const_spec((1, d)),
        ],
        out_specs=pl.BlockSpec((None, POOL_TILE, d), lambda b, i: (b, i, 0)),
        out_shape=jax.ShapeDtypeStruct(x.shape, F32),
        compiler_params=_params(2),
        name="pool",
    )(x, mods, g.reshape(1, d), s_tab, inv_tab, w_grp.astype(BF16), b_grp.reshape(1, d), scale.reshape(1, d))


def _dft_tables(length):
    n_full = 2 * length
    n = jnp.arange(length, dtype=jnp.int32)[None, :]
    a = jnp.arange(length // V7X_LANES, dtype=jnp.int32)[:, None]
    j = jnp.arange(V7X_LANES, dtype=jnp.int32)[:, None]
    theta = 2.0 * math.pi / n_full

    def cs(m):
        ph = (m % n_full).astype(F32) * theta
        return jnp.cos(ph), jnp.sin(ph)

    return cs(V7X_LANES * a * n) + cs(j * n)


def _dft_fwd_gen_kernel(pc_ref, ps_ref, qc_ref, qs_ref, o_ref, *, tf):
    ft = pl.program_id(0)
    blocks = tf // V7X_LANES
    qc = qc_ref[...]
    qs = qs_ref[...]
    for jb in range(blocks):
        fb = ft * blocks + jb
        pc = pc_ref[pl.ds(fb, 1), :]
        ps = ps_ref[pl.ds(fb, 1), :]
        cos_rows = pc * qc - ps * qs
        sin_rows = ps * qc + pc * qs
        if jb == 0:
            row = lax.broadcasted_iota(jnp.int32, sin_rows.shape, 0)
            col = lax.broadcasted_iota(jnp.int32, sin_rows.shape, 1)
            nyq = (1 - 2 * (col % 2)).astype(F32)
            sin_rows = jnp.where((row == 0) & (ft == 0), nyq, sin_rows)
        o_ref[jb * V7X_LANES:(jb + 1) * V7X_LANES, :] = cos_rows.astype(BF16)
        o_ref[tf + jb * V7X_LANES:tf + (jb + 1) * V7X_LANES, :] = sin_rows.astype(BF16)


def _dft_inv_gen_kernel(pc_ref, ps_ref, qc_ref, qs_ref, o_ref, *, tf, length):
    nb = pl.program_id(0)
    pc = pc_ref[pl.ds(nb, 1), :]
    ps = ps_ref[pl.ds(nb, 1), :]
    qc = qc_ref[...]
    qs = qs_ref[...]
    cos_all = pc * qc - ps * qs
    sin_all = ps * qc + pc * qs
    row = lax.broadcasted_iota(jnp.int32, sin_all.shape, 0)
    col = lax.broadcasted_iota(jnp.int32, sin_all.shape, 1)
    nyq = (1 - 2 * (row % 2)).astype(F32)
    sin_all = jnp.where(col == 0, nyq, sin_all)
    for ft in range(length // tf):
        o_ref[:, 2 * ft * tf:(2 * ft + 1) * tf] = cos_all[:, ft * tf:(ft + 1) * tf].astype(BF16)
        o_ref[:, (2 * ft + 1) * tf:(2 * ft + 2) * tf] = sin_all[:, ft * tf:(ft + 1) * tf].astype(BF16)


def _dft_matrices(length, tf):
    pc, ps, qc, qs = _dft_tables(length)
    nb = length // V7X_LANES
    tabs = [_const_spec((nb, length)), _const_spec((nb, length)),
            _const_spec((V7X_LANES, length)), _const_spec((V7X_LANES, length))]
    fwd = pl.pallas_call(
        functools.partial(_dft_fwd_gen_kernel, tf=tf),
        grid=(length // tf,),
        in_specs=tabs,
        out_specs=pl.BlockSpec((2 * tf, length), lambda i: (i, 0)),
        out_shape=jax.ShapeDtypeStruct((2 * length, length), BF16),
        compiler_params=_params(1),
        name="dft_fwd_gen",
    )(pc, ps, qc, qs)
    inv = pl.pallas_call(
        functools.partial(_dft_inv_gen_kernel, tf=tf, length=length),
        grid=(nb,),
        in_specs=tabs,
        out_specs=pl.BlockSpec((V7X_LANES, 2 * length), lambda i: (i, 0)),
        out_shape=jax.ShapeDtypeStruct((length, 2 * length), BF16),
        compiler_params=_params(1),
        name="dft_inv_gen",
    )(pc, ps, qc, qs)
    return fwd, inv


def _hy_proj_kernel(x_ref, mod_ref, g_ref, w_ref, b_ref, o_ref):
    u = _adaln(x_ref[...], g_ref[...], mod_ref[3:4, :], mod_ref[4:5, :]).astype(BF16)
    o_ref[...] = _mm(u, w_ref[...]) + b_ref[...]


def _hy_short_kernel(pv_ref, p1_ref, p2_ref, wv_ref, w1_ref, w2_ref, bv_ref, b1_ref, b2_ref,
                     vb_ref, v_ref, x1_ref, x2_ref):
    def conv(p_ref, w_ref, b_ref):
        p = p_ref[...]
        t = p.shape[0]
        row = lax.broadcasted_iota(jnp.int32, p.shape, 0)
        prev = jnp.where(row == 0, 0.0, pltpu.roll(p, 1, axis=0))
        nxt = jnp.where(row == t - 1, 0.0, pltpu.roll(p, t - 1, axis=0))
        return prev * w_ref[0:1, :] + p * w_ref[1:2, :] + nxt * w_ref[2:3, :] + b_ref[...]

    v = conv(pv_ref, wv_ref, bv_ref)
    v_ref[...] = v
    vb_ref[...] = v.astype(BF16)
    x1_ref[...] = conv(p1_ref, w1_ref, b1_ref)
    x2_ref[...] = conv(p2_ref, w2_ref, b2_ref)


def _hy_filter_kernel(feat_ref, wp_ref, bp_ref, wm_ref, bm_ref, fr_ref, wf_ref, dl_ref, hb_ref, ss_ref,
                      *, d, n_parts):
    i = pl.program_id(0)
    feat = feat_ref[...]
    fr = fr_ref[...]
    hdn = jnp.sin(fr * (_mm_f32(feat, wp_ref[...]) + bp_ref[...]))
    hdn = jnp.sin(fr * (_mm_f32(hdn, wm_ref[...]) + bm_ref[...]))
    h = _mm_f32(hdn, wf_ref[...])
    decay = jnp.exp(-feat[:, 0:1] * dl_ref[...])
    row = lax.broadcasted_iota(jnp.int32, decay.shape, 0)
    first = (row == 0) & (i == 0)

    @pl.when(i == 0)
    def _():
        ss_ref[...] = jnp.zeros_like(ss_ref)

    for part in range(n_parts):
        sl = slice(part * d, (part + 1) * d)
        hp = h[:, sl] * decay
        if part % 2 == 1:
            hp = jnp.where(first, 0.0, hp)
        hb_ref[:, sl] = hp.astype(BF16)
        ss_ref[:, sl] += jnp.sum(hp * hp, axis=0, keepdims=True)


def _hy_spec_kernel(f_ref, hf_ref, hr_ref, sf_ref, sr_ref, o_ref, *, tf, n_full):
    ft = pl.program_id(0)
    fwd = _mm(f_ref[...], hf_ref[...])
    rev = _mm(f_ref[...], hr_ref[...])
    norm = lax.rsqrt(sf_ref[...] + sr_ref[...] + NORM_EPS)
    row = lax.broadcasted_iota(jnp.int32, (tf, fwd.shape[1]), 0)
    edge = (row == 0) & (ft == 0)
    wgt = jnp.where(edge, 1.0 / n_full, 2.0 / n_full) * norm
    o_ref[0:tf, :] = (fwd[0:tf] + rev[0:tf]) * wgt
    o_ref[tf:, :] = jnp.where(edge, fwd[tf:] + rev[tf:], fwd[tf:] - rev[tf:]) * wgt


def _hy_fwd_kernel(f_ref, u_ref, h_ref, y_ref, *, tf):
    ft = pl.program_id(0)
    uf = _mm(f_ref[...], u_ref[...])
    ure, us = uf[0:tf], uf[tf:]
    hre, hs = h_ref[0:tf, :], h_ref[tf:, :]
    row = lax.broadcasted_iota(jnp.int32, ure.shape, 0)
    edge = (row == 0) & (ft == 0)
    y_ref[0:tf, :] = jnp.where(edge, ure * hre, ure * hre - us * hs).astype(BF16)
    y_ref[tf:, :] = jnp.where(edge, us * hs, ure * hs + us * hre).astype(BF16)


def _hy_inv_kernel(f_ref, y_ref, a_ref, b_ref, fb_ref, *o_refs):
    conv = _mm(f_ref[...], y_ref[...])
    z = a_ref[...] * (conv + b_ref[...] * fb_ref[...])
    for o_ref in o_refs:
        o_ref[...] = z.astype(o_ref.dtype)


def _hy_out_kernel(x_ref, mod_ref, z_ref, w_ref, b_ref, o_ref):
    o_ref[...] = x_ref[...] + mod_ref[5:6, :] * (_mm(z_ref[...], w_ref[...]) + b_ref[...])


def _hyena_layer(x, mods, layer, row_fn, g, w_in, b_in, w_short, b_short, w_pos, b_pos, w_mid, b_mid,
                 freq, w_filt, fbias, w_out, b_out):
    bsz, length, d = x.shape
    n_full = 2 * length
    tm = min(512, length)
    tf = min(DFT_FREQ_TILE, length)
    tn = min(DFT_TIME_TILE, length)
    tc = DFT_COL_TILE
    n_ct = d // tc
    width = w_mid.shape[0]
    n_parts = 2 * HYENA_ORDER

    p = pl.pallas_call(
        _hy_proj_kernel,
        grid=(bsz, length // tm),
        in_specs=[
            pl.BlockSpec((None, tm, d), lambda b, i: (b, i, 0)),
            _mod_spec(d, layer, row_fn),
            _const_spec((1, d)),
            _const_spec((d, 3 * d)),
            _const_spec((1, 3 * d)),
        ],
        out_specs=pl.BlockSpec((None, tm, 3 * d), lambda b, i: (b, i, 0)),
        out_shape=jax.ShapeDtypeStruct((bsz, length, 3 * d), F32),
        compiler_params=_params(2),
        name="hy_proj",
    )(x, mods, g.reshape(1, d), w_in, b_in.reshape(1, 3 * d))

    def part_spec(shape_rows, part):
        return pl.BlockSpec((None, shape_rows, tc), lambda b, j: (b, 0, part * n_ct + j))

    def wpart_spec(rows, part):
        return pl.BlockSpec((rows, tc), lambda b, j: (0, part * n_ct + j))

    tile_spec = pl.BlockSpec((None, length, tc), lambda b, j: (b, 0, j))
    b_short2 = b_short.reshape(1, 3 * d)
    vb, v, x1, x2 = pl.pallas_call(
        _hy_short_kernel,
        grid=(bsz, n_ct),
        in_specs=[part_spec(length, 0), part_spec(length, 1), part_spec(length, 2),
                  wpart_spec(3, 0), wpart_spec(3, 1), wpart_spec(3, 2),
                  wpart_spec(1, 0), wpart_spec(1, 1), wpart_spec(1, 2)],
        out_specs=[tile_spec, tile_spec, tile_spec, tile_spec],
        out_shape=[jax.ShapeDtypeStruct((bsz, length, d), BF16)] + [jax.ShapeDtypeStruct((bsz, length, d), F32)] * 3,
        compiler_params=_params(2),
        name="hy_short",
    )(p, p, p, w_short, w_short, w_short, b_short2, b_short2, b_short2)

    t = jnp.linspace(0.0, 1.0, length, dtype=F32)[:, None]
    bands = jnp.linspace(1e-4, HYENA_BANDS - 1, HYENA_BANDS, dtype=F32)
    ang = (2.0 * math.pi / length) * jnp.arange(length, dtype=F32)[:, None] * bands[None, :]
    feat = jnp.concatenate([t, jnp.cos(ang), -jnp.sin(ang)], axis=-1)
    feat = jnp.pad(feat, ((0, 0), (0, V7X_LANES - HYENA_EMB)))
    w_pos_p = jnp.pad(w_pos.astype(F32), ((0, V7X_LANES - HYENA_EMB), (0, 0)))
    deltas = jnp.abs(jnp.linspace(HYENA_MIN_DECAY, HYENA_MAX_DECAY, d, dtype=F32)).reshape(1, d)
    tl = min(512, length)
    hb, ss = pl.pallas_call(
        functools.partial(_hy_filter_kernel, d=d, n_parts=n_parts),
        grid=(length // tl,),
        in_specs=[
            pl.BlockSpec((tl, V7X_LANES), lambda i: (i, 0)),
            _const_spec((V7X_LANES, width)),
            _const_spec((1, width)),
            _const_spec((width, width)),
            _const_spec((1, width)),
            _const_spec((1, width)),
            _const_spec((width, n_parts * d)),
            _const_spec((1, d)),
        ],
        out_specs=[pl.BlockSpec((tl, n_parts * d), lambda i: (i, 0)),
                   pl.BlockSpec((1, n_parts * d), lambda i: (0, 0))],
        out_shape=[jax.ShapeDtypeStruct((length, n_parts * d), BF16),
                   jax.ShapeDtypeStruct((1, n_parts * d), F32)],
        compiler_params=_params(1),
        name="hy_filter",
    )(feat, w_pos_p, b_pos.reshape(1, width), w_mid, b_mid.reshape(1, width), freq.reshape(1, width),
      w_filt, deltas)

    dft_fwd, dft_inv = _dft_matrices(length, tf)

    spec = pl.pallas_call(
        functools.partial(_hy_spec_kernel, tf=tf, n_full=n_full),
        grid=(length // tf, HYENA_ORDER, n_ct),
        in_specs=[
            pl.BlockSpec((2 * tf, length), lambda f, o, j: (f, 0)),
            pl.BlockSpec((length, tc), lambda f, o, j: (0, (2 * o) * n_ct + j)),
            pl.BlockSpec((length, tc), lambda f, o, j: (0, (2 * o + 1) * n_ct + j)),
            pl.BlockSpec((1, tc), lambda f, o, j: (0, (2 * o) * n_ct + j)),
            pl.BlockSpec((1, tc), lambda f, o, j: (0, (2 * o + 1) * n_ct + j)),
        ],
        out_specs=pl.BlockSpec((None, 2 * tf, tc), lambda f, o, j: (o, f, j)),
        out_shape=jax.ShapeDtypeStruct((HYENA_ORDER, n_full, d), F32),
        compiler_params=_params(3),
        name="hy_spec",
    )(dft_fwd, hb, hb, ss, ss)

    def long_conv(order, ub, a, bsrc, want_f32):
        yspec = pl.pallas_call(
            functools.partial(_hy_fwd_kernel, tf=tf),
            grid=(length // tf, bsz, n_ct),
            in_specs=[
                pl.BlockSpec((2 * tf, length), lambda f, b, j: (f, 0)),
                pl.BlockSpec((None, length, tc), lambda f, b, j: (b, 0, j)),
                pl.BlockSpec((None, 2 * tf, tc), lambda f, b, j: (order, f, j)),
            ],
            out_specs=pl.BlockSpec((None, 2 * tf, tc), lambda f, b, j: (b, f, j)),
            out_shape=jax.ShapeDtypeStruct((bsz, n_full, d), BF16),
            compiler_params=_params(3),
            name="hy_fwd",
        )(dft_fwd, ub, spec)
        row_spec = pl.BlockSpec((None, tn, tc), lambda n, b, j: (b, n, j))
        out_shape = [jax.ShapeDtypeStruct((bsz, length, d), BF16)]
        if want_f32:
            out_shape.append(jax.ShapeDtypeStruct((bsz, length, d), F32))
        return pl.pallas_call(
            _hy_inv_kernel,
            grid=(length // tn, bsz, n_ct),
            in_specs=[
                pl.BlockSpec((tn, n_full), lambda n, b, j: (n, 0)),
                pl.BlockSpec((None, n_full, tc), lambda n, b, j: (b, 0, j)),
                row_spec,
                row_spec,
                pl.BlockSpec((None, 1, tc), lambda n, b, j: (order, 0, j)),
            ],
            out_specs=[row_spec] * len(out_shape),
            out_shape=out_shape,
            compiler_params=_params(3),
            name="hy_inv",
        )(dft_inv, yspec, a, bsrc, fbias.astype(F32).reshape(HYENA_ORDER, 1, d))

    zb, z = long_conv(0, vb, x1, v, True)
    (zzb,) = long_conv(1, zb, x2, z, False)

    return pl.pallas_call(
        _hy_out_kernel,
        grid=(bsz, length // tm),
        in_specs=[
            pl.BlockSpec((None, tm, d), lambda b, i: (b, i, 0)),
            _mod_spec(d, layer, row_fn),
            pl.BlockSpec((None, tm, d), lambda b, i: (b, i, 0)),
            _const_spec((d, d)),
            _const_spec((1, d)),
        ],
        out_specs=pl.BlockSpec((None, tm, d), lambda b, i: (b, i, 0)),
        out_shape=jax.ShapeDtypeStruct(x.shape, F32),
        compiler_params=_params(2),
        name="hy_out",
    )(x, mods, zzb, w_out, b_out.reshape(1, d))


def kernel(x, c, ctx, c_ctx, ada_w, ada_b, norm_g, ffn_w1, ffn_w3, ffn_w2, ret_w_in, ret_w_out, ret_decay,
           pool_w, pool_b, pool_scale, hy_w_in, hy_b_in, hy_w_short, hy_b_short, hy_w_pos, hy_b_pos,
           hy_w_mid, hy_b_mid, hy_freq, hy_w_filt, hy_bias, hy_w_out, hy_b_out, final_g):
    bsz, seq, d = x.shape
    t_ctx = ctx.shape[1]
    depth = ada_w.shape[0]
    n_mixers = 3
    assert bsz + 1 <= MOD_ROWS
    ctx_row = bsz

    cond = jnp.concatenate([c, c_ctx[None, :], jnp.zeros((MOD_ROWS - bsz - 1, d), F32)], axis=0)
    mods = _ada_all(cond, ada_w, ada_b).reshape(depth, MOD_ROWS, ADA_CHUNKS, d)

    w1b, w3b, w2b = ffn_w1.astype(BF16), ffn_w3.astype(BF16), ffn_w2.astype(BF16)
    lat_row = lambda b, i: b
    ctx_flat_row = lambda b, i: ctx_row

    def ffn_lat(h, layer, half, final=False):
        return _half_ffn(h, mods, layer, lat_row, 6 * half, norm_g[layer, 2 * half],
                         w1b[layer, half], w3b[layer, half], w2b[layer, half], final_g, final=final)

    def ffn_ctx(h, layer, half):
        flat = h.reshape(1, bsz * t_ctx, d)
        out = _half_ffn(flat, mods, layer, ctx_flat_row, 6 * half, norm_g[layer, 2 * half],
                        w1b[layer, half], w3b[layer, half], w2b[layer, half], final_g)
        return out.reshape(bsz, t_ctx, d)

    h_lat, h_ctx = x, ctx
    for layer in range(depth):
        kind = layer % n_mixers
        slot = layer // n_mixers
        last = layer == depth - 1
        ctx_out = not last
        ctx_live = ctx_out or kind == 0
        h_lat = ffn_lat(h_lat, layer, 0)
        if ctx_live:
            h_ctx = ffn_ctx(h_ctx, layer, 0)
        g_mix = norm_g[layer, 1]
        if kind == 0:
            h_ctx, h_lat = _retention_layer(h_ctx, h_lat, mods, layer, g_mix, ret_w_in[slot].astype(BF16),
                                            ret_w_out[slot].astype(BF16), ret_decay[slot], ctx_out)
        elif kind == 1:
            pp = (pool_w[slot], pool_b[slot], pool_scale[slot])
            h_lat = _pool_layer(h_lat, GRID_W, mods, layer, lat_row, g_mix, *pp)
            if ctx_out:
                h_ctx = _pool_layer(h_ctx, t_ctx, mods, layer, ctx_flat_row, g_mix, *pp)
        else:
            hp = (hy_w_in[slot].astype(BF16), hy_b_in[slot], hy_w_short[slot], hy_b_short[slot], hy_w_pos[slot],
                  hy_b_pos[slot], hy_w_mid[slot], hy_b_mid[slot], hy_freq[slot], hy_w_filt[slot],
                  hy_bias[slot], hy_w_out[slot].astype(BF16), hy_b_out[slot])
            h_lat = _hyena_layer(h_lat, mods, layer, lat_row, g_mix, *hp)
            if ctx_out:
                h_ctx = _hyena_layer(h_ctx, mods, layer, ctx_flat_row, g_mix, *hp)
        h_lat = ffn_lat(h_lat, layer, 1, final=last)
        if ctx_out:
            h_ctx = ffn_ctx(h_ctx, layer, 1)
    return h_lat
```

```python
import functools
import math

import numpy as np
import jax
import jax.numpy as jnp
from jax import lax
from jax.experimental import pallas as pl
from jax.experimental.pallas import tpu as pltpu

F32 = jnp.float32
BF16 = jnp.bfloat16

GRID_W = 64
ADA_CHUNKS = 9
NORM_EPS = 1e-6
RET_HEADS = 4
RET_CHUNK = 128
ROPE_BASE = 10000.0
POOL_WINDOWS = (2, 4, 8, 16)
HYENA_ORDER = 2
HYENA_EMB = 33
HYENA_BANDS = (HYENA_EMB - 1) // 2
HYENA_TARGET = 1e-2
HYENA_FAST = 0.3
HYENA_SLOW = 1.5
HYENA_MAX_DECAY = math.log(HYENA_TARGET) / HYENA_FAST
HYENA_MIN_DECAY = math.log(HYENA_TARGET) / HYENA_SLOW

V7X_LANES = 128
V7X_VMEM_LIMIT_BYTES = 56 * 1024 * 1024
MOD_ROWS = 8
POOL_TILE = 256
RET_TILE = 256
SCAN_CHUNK = 256
DFT_FREQ_TILE = 512
DFT_TIME_TILE = 2048
DFT_COL_TILE = 256
HIGHEST = lax.Precision.HIGHEST


def _params(n_axes):
    return pltpu.CompilerParams(
        dimension_semantics=("arbitrary",) * n_axes,
        vmem_limit_bytes=V7X_VMEM_LIMIT_BYTES)


def _const_spec(shape):
    zeros = (0,) * len(shape)
    return pl.BlockSpec(shape, lambda *_: zeros, pipeline_mode=pl.Buffered(1))


def _mod_spec(d, layer, row_fn):
    return pl.BlockSpec((None, None, ADA_CHUNKS, d), lambda *idx: (layer, row_fn(*idx), 0, 0))


def _mm(a, b):
    return jnp.dot(a, b, preferred_element_type=F32)


def _mm_f32(a, b):
    return jnp.dot(a, b, preferred_element_type=F32, precision=HIGHEST)


def _silu(x):
    return x * jax.nn.sigmoid(x)


def _adaln(x, g, shift, scale):
    ms = jnp.mean(x * x, axis=-1, keepdims=True)
    y = x * lax.rsqrt(ms + NORM_EPS) * g
    return y * (1.0 + scale) + shift


def _ada_kernel(c_ref, w_ref, b_ref, o_ref):
    s = _silu(c_ref[...]).astype(BF16)
    o_ref[...] = _mm(s, w_ref[...].astype(BF16)) + b_ref[...]


def _ada_all(cond, ada_w, ada_b):
    depth, d, nd = ada_w.shape
    return pl.pallas_call(
        _ada_kernel,
        grid=(depth, nd // d),
        in_specs=[
            pl.BlockSpec((MOD_ROWS, d), lambda l, j: (0, 0)),
            pl.BlockSpec((None, d, d), lambda l, j: (l, 0, j)),
            pl.BlockSpec((None, 1, d), lambda l, j: (l, 0, j)),
        ],
        out_specs=pl.BlockSpec((None, MOD_ROWS, d), lambda l, j: (l, 0, j)),
        out_shape=jax.ShapeDtypeStruct((depth, MOD_ROWS, nd), F32),
        compiler_params=_params(2),
        name="ada",
    )(cond, ada_w, ada_b.reshape(depth, 1, nd))


def _ffn_kernel(x_ref, mod_ref, g_ref, w1_ref, w3_ref, w2_ref, fg_ref, o_ref, *, base, final):
    x = x_ref[...]
    u = _adaln(x, g_ref[...], mod_ref[base:base + 1, :], mod_ref[base + 1:base + 2, :]).astype(BF16)
    h1 = _mm(u, w1_ref[...])
    h3 = _mm(u, w3_ref[...])
    a = (_silu(h1) * h3).astype(BF16)
    y = x + (0.5 * mod_ref[base + 2:base + 3, :]) * _mm(a, w2_ref[...])
    if final:
        ms = jnp.mean(y * y, axis=-1, keepdims=True)
        y = y * lax.rsqrt(ms + NORM_EPS) * fg_ref[...]
    o_ref[...] = y


def _half_ffn(x, mods, layer, row_fn, base, g, w1, w3, w2, final_g, final=False, tm=512):
    n_g, t, d = x.shape
    f = w1.shape[1]
    tm = min(tm, t)
    return pl.pallas_call(
        functools.partial(_ffn_kernel, base=base, final=final),
        grid=(n_g, t // tm),
        in_specs=[
            pl.BlockSpec((None, tm, d), lambda b, i: (b, i, 0)),
            _mod_spec(d, layer, row_fn),
            _const_spec((1, d)),
            _const_spec((d, f)),
            _const_spec((d, f)),
            _const_spec((f, d)),
            _const_spec((1, d)),
        ],
        out_specs=pl.BlockSpec((None, tm, d), lambda b, i: (b, i, 0)),
        out_shape=jax.ShapeDtypeStruct(x.shape, F32),
        compiler_params=_params(2),
        name="half_ffn",
    )(x, mods, g.reshape(1, d), w1, w3, w2, final_g.reshape(1, d))


def _ret_proj_kernel(x_ref, mod_ref, g_ref, w_ref, cos_ref, sin_ref, q_ref, k_ref, v_ref, gate_ref,
                     *, n_heads, dk, dv):
    u = _adaln(x_ref[...], g_ref[...], mod_ref[3:4, :], mod_ref[4:5, :]).astype(BF16)
    p = _mm(u, w_ref[...])
    cos = cos_ref[...]
    sin = sin_ref[...]
    half = dk // 2
    qk = n_heads * dk
    k_scale = dk ** -0.5

    def rot(a):
        a1, a2 = a[:, :half], a[:, half:]
        return jnp.concatenate([a1 * cos - a2 * sin, a2 * cos + a1 * sin], axis=-1)

    for h in range(n_heads):
        q_ref[:, h * dk:(h + 1) * dk] = rot(p[:, h * dk:(h + 1) * dk]).astype(BF16)
        k_ref[:, h * dk:(h + 1) * dk] = (rot(p[:, qk + h * dk:qk + (h + 1) * dk]) * k_scale).astype(BF16)
    v_ref[...] = p[:, 2 * qk:2 * qk + n_heads * dv].astype(BF16)
    gate_ref[...] = p[:, 2 * qk + n_heads * dv:].astype(BF16)


def _ret_scan_kernel(*refs, n_heads, dk, dv, backward):
    if backward:
        q_ref, k_ref, v_ref, dmat_ref, qdec_ref, kdec_ref, cdec_ref, fwd_ref, o_ref, state_ref = refs
    else:
        q_ref, k_ref, v_ref, dmat_ref, qdec_ref, kdec_ref, cdec_ref, o_ref, state_ref = refs

    @pl.when(pl.program_id(1) == 0)
    def _():
        state_ref[...] = jnp.zeros_like(state_ref)

    for h in range(n_heads):
        q = q_ref[:, h * dk:(h + 1) * dk]
        k = k_ref[:, h * dk:(h + 1) * dk]
        v = v_ref[:, h * dv:(h + 1) * dv]
        scores = lax.dot_general(q, k, (((1,), (1,)), ((), ())), preferred_element_type=F32) * dmat_ref[h]
        state = state_ref[h]
        out = _mm(scores.astype(BF16), v) + _mm(q, state.astype(BF16)) * qdec_ref[h]
        kd = (k.astype(F32) * kdec_ref[h]).astype(BF16)
        state_ref[h] = state * cdec_ref[h] + lax.dot_general(
            kd, v, (((0,), (0,)), ((), ())), preferred_element_type=F32)
        if backward:
            o = out + fwd_ref[:, h * dv:(h + 1) * dv]
            mu = jnp.mean(o, axis=-1, keepdims=True)
            oc = o - mu
            var = jnp.mean(oc * oc, axis=-1, keepdims=True)
            o_ref[:, h * dv:(h + 1) * dv] = (oc * lax.rsqrt(var + NORM_EPS)).astype(o_ref.dtype)
        else:
            o_ref[:, h * dv:(h + 1) * dv] = out


def _ret_out_kernel(x_ref, mod_ref, y_ref, gate_ref, w_ref, o_ref):
    z = (_silu(gate_ref[...].astype(F32)) * y_ref[...].astype(F32)).astype(BF16)
    o_ref[...] = x_ref[...] + mod_ref[5:6, :] * _mm(z, w_ref[...])


def _retention_layer(h_ctx, h_lat, mods, layer, g, w_in, w_out, decay, ctx_out):
    bsz, t_ctx, d = h_ctx.shape
    t_lat = h_lat.shape[1]
    t_all = t_ctx + t_lat
    n_heads = RET_HEADS
    dk = d // n_heads
    dv = 2 * d // n_heads
    qk = n_heads * dk
    vw = n_heads * dv
    tm = RET_TILE
    assert t_ctx % tm == 0 and t_lat % tm == 0
    ctx_tiles = t_ctx // tm
    ctx_row = bsz

    hcat = jnp.concatenate([h_ctx, h_lat], axis=1)
    pos = jnp.arange(t_all, dtype=F32)
    inv = 1.0 / (ROPE_BASE ** jnp.linspace(0.0, 1.0, dk // 2, dtype=F32))
    ang = pos[:, None] * inv[None, :]

    q, k, v, gate = pl.pallas_call(
        functools.partial(_ret_proj_kernel, n_heads=n_heads, dk=dk, dv=dv),
        grid=(bsz, t_all // tm),
        in_specs=[
            pl.BlockSpec((None, tm, d), lambda b, i: (b, i, 0)),
            _mod_spec(d, layer, lambda b, i: jnp.where(i < ctx_tiles, ctx_row, b)),
            _const_spec((1, d)),
            _const_spec((d, 2 * qk + 2 * vw)),
            pl.BlockSpec((tm, dk // 2), lambda b, i: (i, 0)),
            pl.BlockSpec((tm, dk // 2), lambda b, i: (i, 0)),
        ],
        out_specs=[
            pl.BlockSpec((None, tm, qk), lambda b, i: (b, i, 0)),
            pl.BlockSpec((None, tm, qk), lambda b, i: (b, i, 0)),
            pl.BlockSpec((None, tm, vw), lambda b, i: (b, i, 0)),
            pl.BlockSpec((None, tm, vw), lambda b, i: (b, i, 0)),
        ],
        out_shape=[
            jax.ShapeDtypeStruct((bsz, t_all, qk), BF16),
            jax.ShapeDtypeStruct((bsz, t_all, qk), BF16),
            jax.ShapeDtypeStruct((bsz, t_all, vw), BF16),
            jax.ShapeDtypeStruct((bsz, t_all, vw), BF16),
        ],
        compiler_params=_params(2),
        name="ret_proj",
    )(hcat, mods, g.reshape(1, d), w_in, jnp.cos(ang), jnp.sin(ang))

    cs = SCAN_CHUNK
    assert t_ctx % cs == 0 and t_lat % cs == 0
    n_chunks = t_all // cs
    ctx_chunks = t_ctx // cs
    lg = jax.nn.log_sigmoid(decay.astype(F32))[:, :, None]
    n = jnp.arange(cs, dtype=F32)
    rel = n[:, None] - n[None, :]
    dm_f = jnp.where(rel >= 0, jnp.exp(lg[0][:, :, None] * jnp.maximum(rel, 0.0)), 0.0)
    dm_b = jnp.where(rel < 0, jnp.exp(lg[1][:, :, None] * jnp.maximum(-rel, 0.0)), 0.0)
    dmat = (dm_f, dm_b)
    qd = (jnp.exp(lg[0] * (n + 1.0)), jnp.exp(lg[1] * (cs - n)))
    kd = (jnp.exp(lg[0] * (cs - 1.0 - n)), jnp.exp(lg[1] * n))
    cd = jnp.exp(lg * cs)

    def bwd_chunk(j):
        return jnp.where(j < ctx_chunks, ctx_chunks - 1 - j, n_chunks - 1 + ctx_chunks - j)

    def scan(direction, chunk, extra_in, extra_specs, out_dtype):
        return pl.pallas_call(
            functools.partial(_ret_scan_kernel, n_heads=n_heads, dk=dk, dv=dv, backward=direction == 1),
            grid=(bsz, n_chunks),
            in_specs=[
                pl.BlockSpec((None, cs, qk), lambda b, s: (b, chunk(s), 0)),
                pl.BlockSpec((None, cs, qk), lambda b, s: (b, chunk(s), 0)),
                pl.BlockSpec((None, cs, vw), lambda b, s: (b, chunk(s), 0)),
                _const_spec((n_heads, cs, cs)),
                _const_spec((n_heads, cs, dv)),
                _const_spec((n_heads, cs, dk)),
                _const_spec((n_heads, 1, dv)),
            ] + extra_specs,
            out_specs=pl.BlockSpec((None, cs, vw), lambda b, s: (b, chunk(s), 0)),
            out_shape=jax.ShapeDtypeStruct((bsz, t_all, vw), out_dtype),
            scratch_shapes=[pltpu.VMEM((n_heads, dk, dv), F32)],
            compiler_params=_params(2),
            name="ret_scan_bwd" if direction else "ret_scan_fwd",
        )(q, k, v, dmat[direction],
          jnp.broadcast_to(qd[direction][..., None], (n_heads, cs, dv)),
          jnp.broadcast_to(kd[direction][..., None], (n_heads, cs, dk)),
          jnp.broadcast_to(cd[direction][..., None], (n_heads, 1, dv)), *extra_in)

    out_fwd = scan(0, lambda s: s, [], [], F32)
    y = scan(1, bwd_chunk, [out_fwd], [pl.BlockSpec((None, cs, vw), lambda b, s: (b, bwd_chunk(s), 0))], BF16)

    def readout(x, tile0, row_fn):
        return pl.pallas_call(
            _ret_out_kernel,
            grid=(bsz, x.shape[1] // tm),
            in_specs=[
                pl.BlockSpec((None, tm, d), lambda b, i: (b, i, 0)),
                _mod_spec(d, layer, row_fn),
                pl.BlockSpec((None, tm, vw), lambda b, i: (b, i + tile0, 0)),
                pl.BlockSpec((None, tm, vw), lambda b, i: (b, i + tile0, 0)),
                _const_spec((vw, d)),
            ],
            out_specs=pl.BlockSpec((None, tm, d), lambda b, i: (b, i, 0)),
            out_shape=jax.ShapeDtypeStruct(x.shape, F32),
            compiler_params=_params(2),
            name="ret_out",
        )(x, mods, y, gate, w_out)

    new_lat = readout(h_lat, ctx_tiles, lambda b, i: b)
    new_ctx = readout(h_ctx, 0, lambda b, i: ctx_row) if ctx_out else None
    return new_ctx, new_lat


def _pool_tables(seg_len):
    assert POOL_TILE % seg_len == 0
    r = np.arange(POOL_TILE)
    seg, pos = r // seg_len, r % seg_len
    mats, invs = [], []
    for win in POOL_WINDOWS:
        lo = np.clip(pos - win // 2, 0, seg_len)
        hi = np.clip(pos - win // 2 + win, 0, seg_len)
        col_seg, col_pos = seg[None, :], pos[None, :]
        member = (col_seg == seg[:, None]) & (col_pos >= lo[:, None]) & (col_pos < hi[:, None])
        mats.append(member.astype(np.float32))
        invs.append((1.0 / (hi - lo)).astype(np.float32))
    return np.stack(mats), np.stack(invs)


def _pool_kernel(x_ref, mod_ref, g_ref, s_ref, inv_ref, w_ref, b_ref, sc_ref, o_ref, *, n_groups, dg):
    x = x_ref[...]
    u = _adaln(x, g_ref[...], mod_ref[3:4, :], mod_ref[4:5, :])
    gate = mod_ref[5:6, :]
    for gi in range(n_groups):
        sl = slice(gi * dg, (gi + 1) * dg)
        ug = u[:, sl]
        hi = ug.astype(BF16)
        lo = (ug - hi.astype(F32)).astype(BF16)
        win_sum = _mm(s_ref[gi], hi) + _mm(s_ref[gi], lo)
        dlt = win_sum * inv_ref[gi] - ug
        yg = _mm(dlt.astype(BF16), w_ref[gi])
        o_ref[:, sl] = x[:, sl] + gate[:, sl] * ((yg + b_ref[:, sl]) * sc_ref[:, sl])


def _pool_layer(x, seg_len, mods, layer, row_fn, g, w_grp, b_grp, scale):
    bsz, t, d = x.shape
    n_groups = len(POOL_WINDOWS)
    dg = d // n_groups
    mats, invs = _pool_tables(seg_len)
    s_tab = jnp.asarray(mats, BF16)
    inv_tab = jnp.asarray(np.broadcast_to(invs[:, :, None], (n_groups, POOL_TILE, dg)).copy(), F32)
    return pl.pallas_call(
        functools.partial(_pool_kernel, n_groups=n_groups, dg=dg),
        grid=(bsz, t // POOL_TILE),
        in_specs=[
            pl.BlockSpec((None, POOL_TILE, d), lambda b, i: (b, i, 0)),
            _mod_spec(d, layer, row_fn),
            _const_spec((1, d)),
            _const_spec((n_groups, POOL_TILE, POOL_TILE)),
            _const_spec((n_groups, POOL_TILE, dg)),
            _const_spec((n_groups, dg, dg)),
            _const_spec((1, d)),
            _const_spec((1, d)),
        ],
        out_specs=pl.BlockSpec((None, POOL_TILE, d), lambda b, i: (b, i, 0)),
        out_shape=jax.ShapeDtypeStruct(x.shape, F32),
        compiler_params=_params(2),
        name="pool",
    )(x, mods, g.reshape(1, d), s_tab, inv_tab, w_grp.astype(BF16), b_grp.reshape(1, d), scale.reshape(1, d))


def _dft_tables(n_in):
    mod = 4 * n_in
    idx = jnp.arange(n_in, dtype=jnp.int32)[None, :]
    blk = jnp.arange(n_in // V7X_LANES, dtype=jnp.int32)[:, None]
    sub = jnp.arange(V7X_LANES, dtype=jnp.int32)[:, None]
    theta = 2.0 * math.pi / mod

    def cs(p):
        ph = (p % mod).astype(F32) * theta
        return jnp.cos(ph), jnp.sin(ph)

    fwd = cs(2 * V7X_LANES * blk * idx) + cs((2 * sub + 1) * idx)
    inv = cs(V7X_LANES * blk * (2 * idx + 1)) + cs(sub * (2 * idx + 1))
    return fwd, inv


def _dft_fwd_gen_kernel(pc_ref, ps_ref, qc_ref, qs_ref, o_ref, *, tf):
    ft = pl.program_id(0)
    blocks = tf // V7X_LANES
    qc = qc_ref[...]
    qs = qs_ref[...]
    for jb in range(blocks):
        kb = ft * blocks + jb
        pc = pc_ref[pl.ds(kb, 1), :]
        ps = ps_ref[pl.ds(kb, 1), :]
        o_ref[jb * V7X_LANES:(jb + 1) * V7X_LANES, :] = (pc * qc - ps * qs).astype(BF16)
        o_ref[tf + jb * V7X_LANES:tf + (jb + 1) * V7X_LANES, :] = (ps * qc + pc * qs).astype(BF16)


def _dft_inv_gen_kernel(pc_ref, ps_ref, qc_ref, qs_ref, o_ref, *, tf, n_in):
    mb = pl.program_id(0)
    pc = pc_ref[pl.ds(mb, 1), :]
    ps = ps_ref[pl.ds(mb, 1), :]
    qc = qc_ref[...]
    qs = qs_ref[...]
    cos_all = pc * qc - ps * qs
    sin_all = ps * qc + pc * qs
    for ft in range(n_in // tf):
        o_ref[:, 2 * ft * tf:(2 * ft + 1) * tf] = cos_all[:, ft * tf:(ft + 1) * tf].astype(BF16)
        o_ref[:, (2 * ft + 1) * tf:(2 * ft + 2) * tf] = sin_all[:, ft * tf:(ft + 1) * tf].astype(BF16)


def _dft_matrices(n_in, tf):
    fwd_tabs, inv_tabs = _dft_tables(n_in)
    nb = n_in // V7X_LANES
    tabs = [_const_spec((nb, n_in)), _const_spec((nb, n_in)),
            _const_spec((V7X_LANES, n_in)), _const_spec((V7X_LANES, n_in))]
    fwd = pl.pallas_call(
        functools.partial(_dft_fwd_gen_kernel, tf=tf),
        grid=(n_in // tf,),
        in_specs=tabs,
        out_specs=pl.BlockSpec((2 * tf, n_in), lambda i: (i, 0)),
        out_shape=jax.ShapeDtypeStruct((2 * n_in, n_in), BF16),
        compiler_params=_params(1),
        name="dft_fwd_gen",
    )(*fwd_tabs)
    inv = pl.pallas_call(
        functools.partial(_dft_inv_gen_kernel, tf=tf, n_in=n_in),
        grid=(nb,),
        in_specs=tabs,
        out_specs=pl.BlockSpec((V7X_LANES, 2 * n_in), lambda i: (i, 0)),
        out_shape=jax.ShapeDtypeStruct((n_in, 2 * n_in), BF16),
        compiler_params=_params(1),
        name="dft_inv_gen",
    )(*inv_tabs)
    return fwd, inv


def _hy_proj_kernel(x_ref, mod_ref, g_ref, w_ref, b_ref, o_ref):
    u = _adaln(x_ref[...], g_ref[...], mod_ref[3:4, :], mod_ref[4:5, :]).astype(BF16)
    o_ref[...] = _mm(u, w_ref[...]) + b_ref[...]


def _hy_short_kernel(pv_ref, p1_ref, p2_ref, wv_ref, w1_ref, w2_ref, bv_ref, b1_ref, b2_ref,
                     vb_ref, v_ref, x1_ref, x2_ref):
    def conv(p_ref, w_ref, b_ref):
        p = p_ref[...]
        t = p.shape[0]
        row = lax.broadcasted_iota(jnp.int32, p.shape, 0)
        prev = jnp.where(row == 0, 0.0, pltpu.roll(p, 1, axis=0))
        nxt = jnp.where(row == t - 1, 0.0, pltpu.roll(p, t - 1, axis=0))
        return prev * w_ref[0:1, :] + p * w_ref[1:2, :] + nxt * w_ref[2:3, :] + b_ref[...]

    v = conv(pv_ref, wv_ref, bv_ref)
    v_ref[...] = v
    vb_ref[...] = v.astype(BF16)
    x1_ref[...] = conv(p1_ref, w1_ref, b1_ref)
    x2_ref[...] = conv(p2_ref, w2_ref, b2_ref)


def _hy_filter_kernel(feat_ref, wp_ref, bp_ref, wm_ref, bm_ref, fr_ref, wf_ref, dl_ref,
                      hf_ref, hb_ref, ss_ref, *, d, n_orders):
    i = pl.program_id(0)
    feat = feat_ref[...]
    fr = fr_ref[...]
    hdn = jnp.sin(fr * (_mm_f32(feat, wp_ref[...]) + bp_ref[...]))
    hdn = jnp.sin(fr * (_mm_f32(hdn, wm_ref[...]) + bm_ref[...]))
    h = _mm_f32(hdn, wf_ref[...])
    decay = jnp.exp(-feat[:, 0:1] * dl_ref[...])
    row = lax.broadcasted_iota(jnp.int32, decay.shape, 0)
    first = (row == 0) & (i == 0)

    @pl.when(i == 0)
    def _():
        ss_ref[...] = jnp.zeros_like(ss_ref)

    for o in range(n_orders):
        sl = slice(o * d, (o + 1) * d)
        hf = h[:, 2 * o * d:(2 * o + 1) * d] * decay
        hb = jnp.where(first, 0.0, h[:, (2 * o + 1) * d:(2 * o + 2) * d] * decay)
        hf_ref[:, sl] = hf.astype(BF16)
        hb_ref[:, sl] = hb.astype(BF16)
        ss_ref[:, sl] += jnp.sum(hf * hf + hb * hb, axis=0, keepdims=True)


def _butterfly(e_re, e_s, o_re, o_s, c, s):
    t_re = c * o_re - s * o_s
    t_s = c * o_s + s * o_re
    return e_re + t_re, e_s + t_s, e_re - t_re, e_s - t_s


def _hy_spec_kernel(f_ref, fe_ref, fo_ref, be_ref, bo_ref, c_ref, s_ref, ss_ref, o_ref, *, tf, scale):
    dft = f_ref[...]
    c = c_ref[...]
    s = s_ref[...]

    def blocks(even_ref, odd_ref):
        e = _mm(dft, even_ref[...])
        o = _mm(dft, odd_ref[...])
        return _butterfly(e[0:tf], e[tf:], o[0:tf], o[tf:], c, s)

    fa_re, fa_s, fb_re, fb_s = blocks(fe_ref, fo_ref)
    ba_re, ba_s, bb_re, bb_s = blocks(be_ref, bo_ref)
    wgt = scale * lax.rsqrt(ss_ref[...] + NORM_EPS)
    o_ref[0, 0:tf, :] = (fa_re + ba_re) * wgt
    o_ref[0, tf:, :] = (fa_s - ba_s) * wgt
    o_ref[1, 0:tf, :] = (fb_re + bb_re) * wgt
    o_ref[1, tf:, :] = (fb_s - bb_s) * wgt


def _hy_fwd_kernel(f_ref, ue_ref, uo_ref, h_ref, c_ref, s_ref, y_ref, *, tf):
    dft = f_ref[...]
    c = c_ref[...]
    s = s_ref[...]
    e = _mm(dft, ue_ref[...])
    o = _mm(dft, uo_ref[...])
    a_re, a_s, b_re, b_s = _butterfly(e[0:tf], e[tf:], o[0:tf], o[tf:], c, s)
    ha_re, ha_s = h_ref[0, 0:tf, :], h_ref[0, tf:, :]
    hb_re, hb_s = h_ref[1, 0:tf, :], h_ref[1, tf:, :]
    ya_re = a_re * ha_re - a_s * ha_s
    ya_s = a_re * ha_s + a_s * ha_re
    yb_re = b_re * hb_re - b_s * hb_s
    yb_s = b_re * hb_s + b_s * hb_re
    y_ref[0, 0:tf, :] = (ya_re + yb_re).astype(BF16)
    y_ref[0, tf:, :] = (ya_s + yb_s).astype(BF16)
    d_re = ya_re - yb_re
    d_s = ya_s - yb_s
    y_ref[1, 0:tf, :] = (c * d_re + s * d_s).astype(BF16)
    y_ref[1, tf:, :] = (c * d_s - s * d_re).astype(BF16)


def _hy_inv_kernel(f_ref, y_ref, a_ref, b_ref, fb_ref, *o_refs):
    conv = _mm(f_ref[...], y_ref[...])
    z = a_ref[...] * (conv + b_ref[...] * fb_ref[...])
    for o_ref in o_refs:
        o_ref[...] = z.astype(o_ref.dtype)


def _hy_out_kernel(x_ref, mod_ref, z_ref, w_ref, b_ref, o_ref):
    o_ref[...] = x_ref[...] + mod_ref[5:6, :] * (_mm(z_ref[...], w_ref[...]) + b_ref[...])


def _hyena_layer(x, mods, layer, row_fn, g, w_in, b_in, w_short, b_short, w_pos, b_pos, w_mid, b_mid,
                 freq, w_filt, fbias, w_out, b_out):
    bsz, length, d = x.shape
    n_full = 2 * length
    tm = min(512, length)
    tc = DFT_COL_TILE
    n_ct = d // tc
    width = w_mid.shape[0]
    n_parts = 2 * HYENA_ORDER

    p = pl.pallas_call(
        _hy_proj_kernel,
        grid=(bsz, length // tm),
        in_specs=[
            pl.BlockSpec((None, tm, d), lambda b, i: (b, i, 0)),
            _mod_spec(d, layer, row_fn),
            _const_spec((1, d)),
            _const_spec((d, 3 * d)),
            _const_spec((1, 3 * d)),
        ],
        out_specs=pl.BlockSpec((None, tm, 3 * d), lambda b, i: (b, i, 0)),
        out_shape=jax.ShapeDtypeStruct((bsz, length, 3 * d), F32),
        compiler_params=_params(2),
        name="hy_proj",
    )(x, mods, g.reshape(1, d), w_in, b_in.reshape(1, 3 * d))

    def part_spec(shape_rows, part):
        return pl.BlockSpec((None, shape_rows, tc), lambda b, j: (b, 0, part * n_ct + j))

    def wpart_spec(rows, part):
        return pl.BlockSpec((rows, tc), lambda b, j: (0, part * n_ct + j))

    tile_spec = pl.BlockSpec((None, length, tc), lambda b, j: (b, 0, j))
    b_short2 = b_short.reshape(1, 3 * d)
    vb, v, x1, x2 = pl.pallas_call(
        _hy_short_kernel,
        grid=(bsz, n_ct),
        in_specs=[part_spec(length, 0), part_spec(length, 1), part_spec(length, 2),
                  wpart_spec(3, 0), wpart_spec(3, 1), wpart_spec(3, 2),
                  wpart_spec(1, 0), wpart_spec(1, 1), wpart_spec(1, 2)],
        out_specs=[tile_spec, tile_spec, tile_spec, tile_spec],
        out_shape=[jax.ShapeDtypeStruct((bsz, length, d), BF16)] + [jax.ShapeDtypeStruct((bsz, length, d), F32)] * 3,
        compiler_params=_params(2),
        name="hy_short",
    )(p, p, p, w_short, w_short, w_short, b_short2, b_short2, b_short2)

    t = jnp.linspace(0.0, 1.0, length, dtype=F32)[:, None]
    bands = jnp.linspace(1e-4, HYENA_BANDS - 1, HYENA_BANDS, dtype=F32)
    ang = (2.0 * math.pi / length) * jnp.arange(length, dtype=F32)[:, None] * bands[None, :]
    feat = jnp.concatenate([t, jnp.cos(ang), -jnp.sin(ang)], axis=-1)
    feat = jnp.pad(feat, ((0, 0), (0, V7X_LANES - HYENA_EMB)))
    w_pos_p = jnp.pad(w_pos.astype(F32), ((0, V7X_LANES - HYENA_EMB), (0, 0)))
    deltas = jnp.abs(jnp.linspace(HYENA_MIN_DECAY, HYENA_MAX_DECAY, d, dtype=F32)).reshape(1, d)
    tl = min(512, length)
    od = HYENA_ORDER * d
    tap_spec = pl.BlockSpec((tl, od), lambda i: (i, 0))
    taps_f, taps_b, ss = pl.pallas_call(
        functools.partial(_hy_filter_kernel, d=d, n_orders=HYENA_ORDER),
        grid=(length // tl,),
        in_specs=[
            pl.BlockSpec((tl, V7X_LANES), lambda i: (i, 0)),
            _const_spec((V7X_LANES, width)),
            _const_spec((1, width)),
            _const_spec((width, width)),
            _const_spec((1, width)),
            _const_spec((1, width)),
            _const_spec((width, n_parts * d)),
            _const_spec((1, d)),
        ],
        out_specs=[tap_spec, tap_spec, pl.BlockSpec((1, od), lambda i: (0, 0))],
        out_shape=[jax.ShapeDtypeStruct((length, od), BF16), jax.ShapeDtypeStruct((length, od), BF16),
                   jax.ShapeDtypeStruct((1, od), F32)],
        compiler_params=_params(1),
        name="hy_filter",
    )(feat, w_pos_p, b_pos.reshape(1, width), w_mid, b_mid.reshape(1, width), freq.reshape(1, width),
      w_filt, deltas)

    half = length // 2
    tf = min(DFT_FREQ_TILE, half)
    tn = min(DFT_TIME_TILE, half)
    dft_fwd, dft_inv = _dft_matrices(half, tf)
    tw = (math.pi / n_full) * (2.0 * jnp.arange(half, dtype=F32) + 1.0)
    tw_c = jnp.broadcast_to(jnp.cos(tw)[:, None], (half, tc))
    tw_s = jnp.broadcast_to(jnp.sin(tw)[:, None], (half, tc))

    n_oc = od // tc
    even_spec = pl.BlockSpec((half, tc), lambda f, o, j: (0, o * n_ct + j))
    odd_spec = pl.BlockSpec((half, tc), lambda f, o, j: (0, n_oc + o * n_ct + j))
    tw_spec = pl.BlockSpec((tf, tc), lambda f, *_: (f, 0))
    taps_f2 = taps_f.reshape(half, 2 * od)
    taps_b2 = taps_b.reshape(half, 2 * od)
    spec = pl.pallas_call(
        functools.partial(_hy_spec_kernel, tf=tf, scale=1.0 / length),
        grid=(half // tf, HYENA_ORDER, n_ct),
        in_specs=[pl.BlockSpec((2 * tf, half), lambda f, o, j: (f, 0)),
                  even_spec, odd_spec, even_spec, odd_spec, tw_spec, tw_spec,
                  pl.BlockSpec((1, tc), lambda f, o, j: (0, o * n_ct + j))],
        out_specs=pl.BlockSpec((None, 2, 2 * tf, tc), lambda f, o, j: (o, 0, f, j)),
        out_shape=jax.ShapeDtypeStruct((HYENA_ORDER, 2, length, d), F32),
        compiler_params=_params(3),
        name="hy_spec",
    )(dft_fwd, taps_f2, taps_f2, taps_b2, taps_b2, tw_c, tw_s, ss)

    def pairs(a):
        return a.reshape(bsz, half, 2 * d)

    def long_conv(order, ub, a, bsrc, want_f32):
        u2 = pairs(ub)
        yspec = pl.pallas_call(
            functools.partial(_hy_fwd_kernel, tf=tf),
            grid=(half // tf, n_ct, bsz),
            in_specs=[
                pl.BlockSpec((2 * tf, half), lambda f, j, b: (f, 0)),
                pl.BlockSpec((None, half, tc), lambda f, j, b: (b, 0, j)),
                pl.BlockSpec((None, half, tc), lambda f, j, b: (b, 0, n_ct + j)),
                pl.BlockSpec((None, 2, 2 * tf, tc), lambda f, j, b: (order, 0, f, j)),
                tw_spec, tw_spec,
            ],
            out_specs=pl.BlockSpec((None, 2, 2 * tf, tc), lambda f, j, b: (b, 0, f, j)),
            out_shape=jax.ShapeDtypeStruct((bsz, 2, length, d), BF16),
            compiler_params=_params(3),
            name="hy_fwd",
        )(dft_fwd, u2, u2, spec, tw_c, tw_s)
        row_spec = pl.BlockSpec((None, tn, tc), lambda n, j, b, p: (b, n, p * n_ct + j))
        out_shape = [jax.ShapeDtypeStruct((bsz, half, 2 * d), BF16)]
        if want_f32:
            out_shape.append(jax.ShapeDtypeStruct((bsz, half, 2 * d), F32))
        outs = pl.pallas_call(
            _hy_inv_kernel,
            grid=(half // tn, n_ct, bsz, 2),
            in_specs=[
                pl.BlockSpec((tn, length), lambda n, j, b, p: (n, 0)),
                pl.BlockSpec((None, None, length, tc), lambda n, j, b, p: (b, p, 0, j)),
                row_spec,
                row_spec,
                pl.BlockSpec((None, 1, tc), lambda n, j, b, p: (order, 0, j)),
            ],
            out_specs=[row_spec] * len(out_shape),
            out_shape=out_shape,
            compiler_params=_params(4),
            name="hy_inv",
        )(dft_inv, yspec, pairs(a), pairs(bsrc), fbias.astype(F32).reshape(HYENA_ORDER, 1, d))
        return [o.reshape(bsz, length, d) for o in outs]

    zb, z = long_conv(0, vb, x1, v, True)
    (zzb,) = long_conv(1, zb, x2, z, False)

    return pl.pallas_call(
        _hy_out_kernel,
        grid=(bsz, length // tm),
        in_specs=[
            pl.BlockSpec((None, tm, d), lambda b, i: (b, i, 0)),
            _mod_spec(d, layer, row_fn),
            pl.BlockSpec((None, tm, d), lambda b, i: (b, i, 0)),
            _const_spec((d, d)),
            _const_spec((1, d)),
        ],
        out_specs=pl.BlockSpec((None, tm, d), lambda b, i: (b, i, 0)),
        out_shape=jax.ShapeDtypeStruct(x.shape, F32),
        compiler_params=_params(2),
        name="hy_out",
    )(x, mods, zzb, w_out, b_out.reshape(1, d))


def kernel(x, c, ctx, c_ctx, ada_w, ada_b, norm_g, ffn_w1, ffn_w3, ffn_w2, ret_w_in, ret_w_out, ret_decay,
           pool_w, pool_b, pool_scale, hy_w_in, hy_b_in, hy_w_short, hy_b_short, hy_w_pos, hy_b_pos,
           hy_w_mid, hy_b_mid, hy_freq, hy_w_filt, hy_bias, hy_w_out, hy_b_out, final_g):
    bsz, seq, d = x.shape
    t_ctx = ctx.shape[1]
    depth = ada_w.shape[0]
    n_mixers = 3
    assert bsz + 1 <= MOD_ROWS
    ctx_row = bsz

    cond = jnp.concatenate([c, c_ctx[None, :], jnp.zeros((MOD_ROWS - bsz - 1, d), F32)], axis=0)
    mods = _ada_all(cond, ada_w, ada_b).reshape(depth, MOD_ROWS, ADA_CHUNKS, d)

    w1b, w3b, w2b = ffn_w1.astype(BF16), ffn_w3.astype(BF16), ffn_w2.astype(BF16)
    lat_row = lambda b, i: b
    ctx_flat_row = lambda b, i: ctx_row

    def ffn_lat(h, layer, half, final=False):
        return _half_ffn(h, mods, layer, lat_row, 6 * half, norm_g[layer, 2 * half],
                         w1b[layer, half], w3b[layer, half], w2b[layer, half], final_g, final=final)

    def ffn_ctx(h, layer, half):
        flat = h.reshape(1, bsz * t_ctx, d)
        out = _half_ffn(flat, mods, layer, ctx_flat_row, 6 * half, norm_g[layer, 2 * half],
                        w1b[layer, half], w3b[layer, half], w2b[layer, half], final_g)
        return out.reshape(bsz, t_ctx, d)

    h_lat, h_ctx = x, ctx
    for layer in range(depth):
        kind = layer % n_mixers
        slot = layer // n_mixers
        last = layer == depth - 1
        ctx_out = not last
        ctx_live = ctx_out or kind == 0
        h_lat = ffn_lat(h_lat, layer, 0)
        if ctx_live:
            h_ctx = ffn_ctx(h_ctx, layer, 0)
        g_mix = norm_g[layer, 1]
        if kind == 0:
            h_ctx, h_lat = _retention_layer(h_ctx, h_lat, mods, layer, g_mix, ret_w_in[slot].astype(BF16),
                                            ret_w_out[slot].astype(BF16), ret_decay[slot], ctx_out)
        elif kind == 1:
            pp = (pool_w[slot], pool_b[slot], pool_scale[slot])
            h_lat = _pool_layer(h_lat, GRID_W, mods, layer, lat_row, g_mix, *pp)
            if ctx_out:
                h_ctx = _pool_layer(h_ctx, t_ctx, mods, layer, ctx_flat_row, g_mix, *pp)
        else:
            hp = (hy_w_in[slot].astype(BF16), hy_b_in[slot], hy_w_short[slot], hy_b_short[slot], hy_w_pos[slot],
                  hy_b_pos[slot], hy_w_mid[slot], hy_b_mid[slot], hy_freq[slot], hy_w_filt[slot],
                  hy_bias[slot], hy_w_out[slot].astype(BF16), hy_b_out[slot])
            h_lat = _hyena_layer(h_lat, mods, layer, lat_row, g_mix, *hp)
            if ctx_out:
                h_ctx = _hyena_layer(h_ctx, mods, layer, ctx_flat_row, g_mix, *hp)
        h_lat = ffn_lat(h_lat, layer, 1, final=last)
        if ctx_out:
            h_ctx = ffn_ctx(h_ctx, layer, 1)
    return h_lat
```

```python
import functools
import math

import numpy as np
import jax
import jax.numpy as jnp
from jax import lax
from jax.experimental import pallas as pl
from jax.experimental.pallas import tpu as pltpu

F32 = jnp.float32
BF16 = jnp.bfloat16

GRID_W = 64
ADA_CHUNKS = 9
NORM_EPS = 1e-6
RET_HEADS = 4
RET_CHUNK = 128
ROPE_BASE = 10000.0
POOL_WINDOWS = (2, 4, 8, 16)
HYENA_ORDER = 2
HYENA_EMB = 33
HYENA_BANDS = (HYENA_EMB - 1) // 2
HYENA_TARGET = 1e-2
HYENA_FAST = 0.3
HYENA_SLOW = 1.5
HYENA_MAX_DECAY = math.log(HYENA_TARGET) / HYENA_FAST
HYENA_MIN_DECAY = math.log(HYENA_TARGET) / HYENA_SLOW

V7X_LANES = 128
V7X_VMEM_LIMIT_BYTES = 56 * 1024 * 1024
MOD_ROWS = 8
POOL_TILE = 256
RET_TILE = 256
SCAN_CHUNK = 256
DFT_FREQ_TILE = 512
DFT_PACK = 256
DFT_ROW_GROUP = 512
DFT_TIME_TILE = 2048
DFT_COL_TILE = 256
HIGHEST = lax.Precision.HIGHEST


def _params(n_axes):
    return pltpu.CompilerParams(
        dimension_semantics=("arbitrary",) * n_axes,
        vmem_limit_bytes=V7X_VMEM_LIMIT_BYTES)


def _const_spec(shape):
    zeros = (0,) * len(shape)
    return pl.BlockSpec(shape, lambda *_: zeros, pipeline_mode=pl.Buffered(1))


def _mod_spec(d, layer, row_fn):
    return pl.BlockSpec((None, None, ADA_CHUNKS, d), lambda *idx: (layer, row_fn(*idx), 0, 0))


def _mm(a, b):
    return jnp.dot(a, b, preferred_element_type=F32)


def _mm_f32(a, b):
    return jnp.dot(a, b, preferred_element_type=F32, precision=HIGHEST)


def _silu(x):
    return x * jax.nn.sigmoid(x)


def _adaln(x, g, shift, scale):
    ms = jnp.mean(x * x, axis=-1, keepdims=True)
    y = x * lax.rsqrt(ms + NORM_EPS) * g
    return y * (1.0 + scale) + shift


def _ada_kernel(c_ref, w_ref, b_ref, o_ref):
    s = _silu(c_ref[...]).astype(BF16)
    o_ref[...] = _mm(s, w_ref[...].astype(BF16)) + b_ref[...]


def _ada_all(cond, ada_w, ada_b):
    depth, d, nd = ada_w.shape
    return pl.pallas_call(
        _ada_kernel,
        grid=(depth, nd // d),
        in_specs=[
            pl.BlockSpec((MOD_ROWS, d), lambda l, j: (0, 0)),
            pl.BlockSpec((None, d, d), lambda l, j: (l, 0, j)),
            pl.BlockSpec((None, 1, d), lambda l, j: (l, 0, j)),
        ],
        out_specs=pl.BlockSpec((None, MOD_ROWS, d), lambda l, j: (l, 0, j)),
        out_shape=jax.ShapeDtypeStruct((depth, MOD_ROWS, nd), F32),
        compiler_params=_params(2),
        name="ada",
    )(cond, ada_w, ada_b.reshape(depth, 1, nd))


def _ffn_kernel(x_ref, mod_ref, g_ref, w1_ref, w3_ref, w2_ref, fg_ref, o_ref, *, base, final):
    x = x_ref[...]
    u = _adaln(x, g_ref[...], mod_ref[base:base + 1, :], mod_ref[base + 1:base + 2, :]).astype(BF16)
    h1 = _mm(u, w1_ref[...])
    h3 = _mm(u, w3_ref[...])
    a = (_silu(h1) * h3).astype(BF16)
    y = x + (0.5 * mod_ref[base + 2:base + 3, :]) * _mm(a, w2_ref[...])
    if final:
        ms = jnp.mean(y * y, axis=-1, keepdims=True)
        y = y * lax.rsqrt(ms + NORM_EPS) * fg_ref[...]
    o_ref[...] = y


def _half_ffn(x, mods, layer, half, row_fn, g, w1, w3, w2, final_g, final=False, tm=512):
    n_g, t, d = x.shape
    f = w1.shape[-1]
    tm = min(tm, t)
    base = 6 * half

    def weight_spec(rows, cols):
        return pl.BlockSpec((None, None, rows, cols), lambda *_: (layer, half, 0, 0),
                            pipeline_mode=pl.Buffered(1))

    return pl.pallas_call(
        functools.partial(_ffn_kernel, base=base, final=final),
        grid=(n_g, t // tm),
        in_specs=[
            pl.BlockSpec((None, tm, d), lambda b, i: (b, i, 0)),
            _mod_spec(d, layer, row_fn),
            _const_spec((1, d)),
            weight_spec(d, f),
            weight_spec(d, f),
            weight_spec(f, d),
            _const_spec((1, d)),
        ],
        out_specs=pl.BlockSpec((None, tm, d), lambda b, i: (b, i, 0)),
        out_shape=jax.ShapeDtypeStruct(x.shape, F32),
        compiler_params=_params(2),
        name="half_ffn",
    )(x, mods, g.reshape(1, d), w1, w3, w2, final_g.reshape(1, d))


def _ret_proj_kernel(xc_ref, xl_ref, mod_ref, g_ref, w_ref, cos_ref, sin_ref, q_ref, k_ref, v_ref, gate_ref,
                     *, n_heads, dk, dv, ctx_tiles):
    x = jnp.where(pl.program_id(1) < ctx_tiles, xc_ref[...], xl_ref[...])
    u = _adaln(x, g_ref[...], mod_ref[3:4, :], mod_ref[4:5, :]).astype(BF16)
    p = _mm(u, w_ref[...])
    cos = cos_ref[...]
    sin = sin_ref[...]
    half = dk // 2
    qk = n_heads * dk
    k_scale = dk ** -0.5

    def rot(a):
        a1, a2 = a[:, :half], a[:, half:]
        return jnp.concatenate([a1 * cos - a2 * sin, a2 * cos + a1 * sin], axis=-1)

    for h in range(n_heads):
        q_ref[:, h * dk:(h + 1) * dk] = rot(p[:, h * dk:(h + 1) * dk]).astype(BF16)
        k_ref[:, h * dk:(h + 1) * dk] = (rot(p[:, qk + h * dk:qk + (h + 1) * dk]) * k_scale).astype(BF16)
    v_ref[...] = p[:, 2 * qk:2 * qk + n_heads * dv].astype(BF16)
    gate_ref[...] = p[:, 2 * qk + n_heads * dv:].astype(BF16)


def _ret_scan_kernel(q_ref, k_ref, v_ref, dmat_ref, qdec_ref, kdec_ref, cdec_ref, o_ref,
                     state_ref, fwd_ref, *, n_heads, dk, dv, n_chunks, bwd_chunk):
    s = pl.program_id(1)
    cs = q_ref.shape[0]

    @pl.when((s == 0) | (s == n_chunks))
    def _():
        state_ref[...] = jnp.zeros_like(state_ref)

    chunk = jnp.where(s < n_chunks, s, bwd_chunk(jnp.maximum(s - n_chunks, 0)))
    rows = pl.ds(pl.multiple_of(chunk * cs, cs), cs)
    outs = []
    for h in range(n_heads):
        q = q_ref[:, h * dk:(h + 1) * dk]
        k = k_ref[:, h * dk:(h + 1) * dk]
        v = v_ref[:, h * dv:(h + 1) * dv]
        scores = lax.dot_general(q, k, (((1,), (1,)), ((), ())), preferred_element_type=F32) * dmat_ref[h]
        state = state_ref[h]
        outs.append(_mm(scores.astype(BF16), v) + _mm(q, state.astype(BF16)) * qdec_ref[h])
        kd = (k.astype(F32) * kdec_ref[h]).astype(BF16)
        state_ref[h] = state * cdec_ref[h] + lax.dot_general(
            kd, v, (((0,), (0,)), ((), ())), preferred_element_type=F32)

    @pl.when(s < n_chunks)
    def _():
        for h in range(n_heads):
            fwd_ref[rows, h * dv:(h + 1) * dv] = outs[h].astype(fwd_ref.dtype)

    @pl.when(s >= n_chunks)
    def _():
        for h in range(n_heads):
            o = outs[h] + fwd_ref[rows, h * dv:(h + 1) * dv].astype(F32)
            mu = jnp.mean(o, axis=-1, keepdims=True)
            oc = o - mu
            var = jnp.mean(oc * oc, axis=-1, keepdims=True)
            o_ref[:, h * dv:(h + 1) * dv] = (oc * lax.rsqrt(var + NORM_EPS)).astype(o_ref.dtype)


def _ret_out_kernel(x_ref, mod_ref, y_ref, gate_ref, w_ref, o_ref):
    z = (_silu(gate_ref[...].astype(F32)) * y_ref[...].astype(F32)).astype(BF16)
    o_ref[...] = x_ref[...] + mod_ref[5:6, :] * _mm(z, w_ref[...])


def _retention_layer(h_ctx, h_lat, mods, layer, g, w_in, w_out, decay, ctx_out):
    bsz, t_ctx, d = h_ctx.shape
    t_lat = h_lat.shape[1]
    t_all = t_ctx + t_lat
    n_heads = RET_HEADS
    dk = d // n_heads
    dv = 2 * d // n_heads
    qk = n_heads * dk
    vw = n_heads * dv
    tm = RET_TILE
    assert t_ctx % tm == 0 and t_lat % tm == 0
    ctx_tiles = t_ctx // tm
    ctx_row = bsz

    pos = jnp.arange(t_all, dtype=F32)
    inv = 1.0 / (ROPE_BASE ** jnp.linspace(0.0, 1.0, dk // 2, dtype=F32))
    ang = pos[:, None] * inv[None, :]

    q, k, v, gate = pl.pallas_call(
        functools.partial(_ret_proj_kernel, n_heads=n_heads, dk=dk, dv=dv, ctx_tiles=ctx_tiles),
        grid=(bsz, t_all // tm),
        in_specs=[
            pl.BlockSpec((None, tm, d), lambda b, i: (b, jnp.minimum(i, ctx_tiles - 1), 0)),
            pl.BlockSpec((None, tm, d), lambda b, i: (b, jnp.maximum(i - ctx_tiles, 0), 0)),
            _mod_spec(d, layer, lambda b, i: jnp.where(i < ctx_tiles, ctx_row, b)),
            _const_spec((1, d)),
            _const_spec((d, 2 * qk + 2 * vw)),
            pl.BlockSpec((tm, dk // 2), lambda b, i: (i, 0)),
            pl.BlockSpec((tm, dk // 2), lambda b, i: (i, 0)),
        ],
        out_specs=[
            pl.BlockSpec((None, tm, qk), lambda b, i: (b, i, 0)),
            pl.BlockSpec((None, tm, qk), lambda b, i: (b, i, 0)),
            pl.BlockSpec((None, tm, vw), lambda b, i: (b, i, 0)),
            pl.BlockSpec((None, tm, vw), lambda b, i: (b, i, 0)),
        ],
        out_shape=[
            jax.ShapeDtypeStruct((bsz, t_all, qk), BF16),
            jax.ShapeDtypeStruct((bsz, t_all, qk), BF16),
            jax.ShapeDtypeStruct((bsz, t_all, vw), BF16),
            jax.ShapeDtypeStruct((bsz, t_all, vw), BF16),
        ],
        compiler_params=_params(2),
        name="ret_proj",
    )(h_ctx, h_lat, mods, g.reshape(1, d), w_in, jnp.cos(ang), jnp.sin(ang))

    cs = SCAN_CHUNK
    assert t_ctx % cs == 0 and t_lat % cs == 0
    n_chunks = t_all // cs
    ctx_chunks = t_ctx // cs
    lg = jax.nn.log_sigmoid(decay.astype(F32))[:, :, None]
    n = jnp.arange(cs, dtype=F32)
    rel = n[:, None] - n[None, :]
    dm_f = jnp.where(rel >= 0, jnp.exp(lg[0][:, :, None] * jnp.maximum(rel, 0.0)), 0.0)
    dm_b = jnp.where(rel < 0, jnp.exp(lg[1][:, :, None] * jnp.maximum(-rel, 0.0)), 0.0)
    dmat = jnp.stack([dm_f, dm_b])
    qd = jnp.stack([jnp.exp(lg[0] * (n + 1.0)), jnp.exp(lg[1] * (cs - n))])
    kd = jnp.stack([jnp.exp(lg[0] * (cs - 1.0 - n)), jnp.exp(lg[1] * n)])
    cd = jnp.exp(lg * cs)
    qdec = jnp.broadcast_to(qd[..., None], (2, n_heads, cs, dv))
    kdec = jnp.broadcast_to(kd[..., None], (2, n_heads, cs, dk))
    cdec = jnp.broadcast_to(cd[..., None], (2, n_heads, 1, dv))

    def bwd_chunk(j):
        return jnp.where(j < ctx_chunks, ctx_chunks - 1 - j, n_chunks - 1 + ctx_chunks - j)

    def chunk(s):
        return jnp.where(s < n_chunks, s, bwd_chunk(jnp.maximum(s - n_chunks, 0)))

    def out_chunk(s):
        return bwd_chunk(jnp.maximum(s - n_chunks, 0))

    def table_spec(*shape):
        return pl.BlockSpec((None,) + shape, lambda b, s: (s // n_chunks, 0, 0, 0))

    y = pl.pallas_call(
        functools.partial(_ret_scan_kernel, n_heads=n_heads, dk=dk, dv=dv, n_chunks=n_chunks,
                          bwd_chunk=bwd_chunk),
        grid=(bsz, 2 * n_chunks),
        in_specs=[
            pl.BlockSpec((None, cs, qk), lambda b, s: (b, chunk(s), 0)),
            pl.BlockSpec((None, cs, qk), lambda b, s: (b, chunk(s), 0)),
            pl.BlockSpec((None, cs, vw), lambda b, s: (b, chunk(s), 0)),
            table_spec(n_heads, cs, cs),
            table_spec(n_heads, cs, dv),
            table_spec(n_heads, cs, dk),
            table_spec(n_heads, 1, dv),
        ],
        out_specs=pl.BlockSpec((None, cs, vw), lambda b, s: (b, out_chunk(s), 0)),
        out_shape=jax.ShapeDtypeStruct((bsz, t_all, vw), BF16),
        scratch_shapes=[pltpu.VMEM((n_heads, dk, dv), F32), pltpu.VMEM((t_all, vw), BF16)],
        compiler_params=_params(2),
        name="ret_scan",
    )(q, k, v, dmat, qdec, kdec, cdec)

    def readout(x, tile0, row_fn):
        return pl.pallas_call(
            _ret_out_kernel,
            grid=(bsz, x.shape[1] // tm),
            in_specs=[
                pl.BlockSpec((None, tm, d), lambda b, i: (b, i, 0)),
                _mod_spec(d, layer, row_fn),
                pl.BlockSpec((None, tm, vw), lambda b, i: (b, i + tile0, 0)),
                pl.BlockSpec((None, tm, vw), lambda b, i: (b, i + tile0, 0)),
                _const_spec((vw, d)),
            ],
            out_specs=pl.BlockSpec((None, tm, d), lambda b, i: (b, i, 0)),
            out_shape=jax.ShapeDtypeStruct(x.shape, F32),
            compiler_params=_params(2),
            name="ret_out",
        )(x, mods, y, gate, w_out)

    new_lat = readout(h_lat, ctx_tiles, lambda b, i: b)
    new_ctx = readout(h_ctx, 0, lambda b, i: ctx_row) if ctx_out else None
    return new_ctx, new_lat


def _pool_tables(seg_len):
    assert POOL_TILE % seg_len == 0
    r = np.arange(POOL_TILE)
    seg, pos = r // seg_len, r % seg_len
    mats, invs = [], []
    for win in POOL_WINDOWS:
        lo = np.clip(pos - win // 2, 0, seg_len)
        hi = np.clip(pos - win // 2 + win, 0, seg_len)
        col_seg, col_pos = seg[None, :], pos[None, :]
        member = (col_seg == seg[:, None]) & (col_pos >= lo[:, None]) & (col_pos < hi[:, None])
        mats.append(member.astype(np.float32))
        invs.append((1.0 / (hi - lo)).astype(np.float32))
    return np.stack(mats), np.stack(invs)


def _pool_kernel(x_ref, mod_ref, g_ref, s_ref, inv_ref, w_ref, b_ref, sc_ref, o_ref, *, n_groups, dg):
    x = x_ref[...]
    u = _adaln(x, g_ref[...], mod_ref[3:4, :], mod_ref[4:5, :])
    gate = mod_ref[5:6, :]
    for gi in range(n_groups):
        sl = slice(gi * dg, (gi + 1) * dg)
        ug = u[:, sl]
        hi = ug.astype(BF16)
        lo = (ug - hi.astype(F32)).astype(BF16)
        win_sum = _mm(s_ref[gi], hi) + _mm(s_ref[gi], lo)
        dlt = win_sum * inv_ref[gi] - ug
        yg = _mm(dlt.astype(BF16), w_ref[gi])
        o_ref[:, sl] = x[:, sl] + gate[:, sl] * ((yg + b_ref[:, sl]) * sc_ref[:, sl])


def _pool_layer(x, seg_len, mods, layer, row_fn, g, w_grp, b_grp, scale):
    bsz, t, d = x.shape
    n_groups = len(POOL_WINDOWS)
    dg = d // n_groups
    mats, invs = _pool_tables(seg_len)
    s_tab = jnp.asarray(mats, BF16)
    inv_tab = jnp.asarray(np.broadcast_to(invs[:, :, None], (n_groups, POOL_TILE, dg)).copy(), F32)
    return pl.pallas_call(
        functools.partial(_pool_kernel, n_groups=n_groups, dg=dg),
        grid=(bsz, t // POOL_TILE),
        in_specs=[
            pl.BlockSpec((None, POOL_TILE, d), lambda b, i: (b, i, 0)),
            _mod_spec(d, layer, row_fn),
            _const_spec((1, d)),
            _const_spec((n_groups, POOL_TILE, POOL_TILE)),
            _const_spec((n_groups, POOL_TILE, dg)),
            _const_spec((n_groups, dg, dg)),
            _const_spec((1, d)),
            _const_spec((1, d)),
        ],
        out_specs=pl.BlockSpec((None, POOL_TILE, d), lambda b, i: (b, i, 0)),
        out_shape=jax.ShapeDtypeStruct(x.shape, F32),
        compiler_params=_params(2),
        name="pool",
    )(x, mods, g.reshape(1, d), s_tab, inv_tab, w_grp.astype(BF16), b_grp.reshape(1, d), scale.reshape(1, d))


def _dft_tables(n_in):
    mod = 4 * n_in
    idx = jnp.arange(n_in, dtype=jnp.int32)[None, :]
    blk = jnp.arange(n_in // V7X_LANES, dtype=jnp.int32)[:, None]
    sub = jnp.arange(V7X_LANES, dtype=jnp.int32)[:, None]
    theta = 2.0 * math.pi / mod

    def cs(p):
        ph = (p % mod).astype(F32) * theta
        return jnp.cos(ph), jnp.sin(ph)

    fwd = cs(2 * V7X_LANES * blk * idx) + cs((2 * sub + 1) * idx)
    inv = cs(V7X_LANES * blk * (2 * idx + 1)) + cs(sub * (2 * idx + 1))
    return fwd, inv


def _dft_fwd_gen_kernel(pc_ref, ps_ref, qc_ref, qs_ref, o_ref, *, tf):
    ft = pl.program_id(0)
    blocks = tf // V7X_LANES
    qc = qc_ref[...]
    qs = qs_ref[...]
    for jb in range(blocks):
        kb = ft * blocks + jb
        pc = pc_ref[pl.ds(kb, 1), :]
        ps = ps_ref[pl.ds(kb, 1), :]
        o_ref[jb * V7X_LANES:(jb + 1) * V7X_LANES, :] = (pc * qc - ps * qs).astype(BF16)
        o_ref[tf + jb * V7X_LANES:tf + (jb + 1) * V7X_LANES, :] = (ps * qc + pc * qs).astype(BF16)


def _dft_inv_gen_kernel(pc_ref, ps_ref, qc_ref, qs_ref, o_ref, *, tf, n_in):
    mb = pl.program_id(0)
    pc = pc_ref[pl.ds(mb, 1), :]
    ps = ps_ref[pl.ds(mb, 1), :]
    qc = qc_ref[...]
    qs = qs_ref[...]
    cos_all = pc * qc - ps * qs
    sin_all = ps * qc + pc * qs
    for ft in range(n_in // tf):
        o_ref[:, 2 * ft * tf:(2 * ft + 1) * tf] = cos_all[:, ft * tf:(ft + 1) * tf].astype(BF16)
        o_ref[:, (2 * ft + 1) * tf:(2 * ft + 2) * tf] = sin_all[:, ft * tf:(ft + 1) * tf].astype(BF16)


def _dft_matrices(n_in, tf):
    fwd_tabs, inv_tabs = _dft_tables(n_in)
    nb = n_in // V7X_LANES
    tabs = [_const_spec((nb, n_in)), _const_spec((nb, n_in)),
            _const_spec((V7X_LANES, n_in)), _const_spec((V7X_LANES, n_in))]
    fwd = pl.pallas_call(
        functools.partial(_dft_fwd_gen_kernel, tf=tf),
        grid=(n_in // tf,),
        in_specs=tabs,
        out_specs=pl.BlockSpec((2 * tf, n_in), lambda i: (i, 0)),
        out_shape=jax.ShapeDtypeStruct((2 * n_in, n_in), BF16),
        compiler_params=_params(1),
        name="dft_fwd_gen",
    )(*fwd_tabs)
    inv = pl.pallas_call(
        functools.partial(_dft_inv_gen_kernel, tf=tf, n_in=n_in),
        grid=(nb,),
        in_specs=tabs,
        out_specs=pl.BlockSpec((V7X_LANES, 2 * n_in), lambda i: (i, 0)),
        out_shape=jax.ShapeDtypeStruct((n_in, 2 * n_in), BF16),
        compiler_params=_params(1),
        name="dft_inv_gen",
    )(*inv_tabs)
    return fwd, inv


def _hy_proj_kernel(x_ref, mod_ref, g_ref, perm_ref, w_ref, b_ref, o_ref):
    u = _adaln(x_ref[...], g_ref[...], mod_ref[3:4, :], mod_ref[4:5, :]).astype(BF16)
    u = _mm(perm_ref[...], u).astype(BF16)
    p = _mm(u, w_ref[...]) + b_ref[...]
    half, width = p.shape[0] // 2, p.shape[1]
    o_ref[:, :width] = p[:half]
    o_ref[:, width:] = p[half:]


def _hy_short_kernel(ev_ref, ov_ref, e1_ref, o1_ref, e2_ref, o2_ref, wv_ref, w1_ref, w2_ref,
                     bv_ref, b1_ref, b2_ref, vb_ref, v_ref, x1_ref, x2_ref):
    odd_out = pl.program_id(2) == 1

    def conv(e_ref, o_ref, w_ref, b_ref):
        e = e_ref[...]
        o = o_ref[...]
        n = e.shape[0]
        row = lax.broadcasted_iota(jnp.int32, e.shape, 0)
        o_before = jnp.where(row == 0, 0.0, pltpu.roll(o, 1, axis=0))
        e_after = jnp.where(row == n - 1, 0.0, pltpu.roll(e, n - 1, axis=0))
        prev = jnp.where(odd_out, e, o_before)
        cur = jnp.where(odd_out, o, e)
        nxt = jnp.where(odd_out, e_after, o)
        return prev * w_ref[0:1, :] + cur * w_ref[1:2, :] + nxt * w_ref[2:3, :] + b_ref[...]

    v = conv(ev_ref, ov_ref, wv_ref, bv_ref)
    v_ref[...] = v
    vb_ref[...] = v.astype(BF16)
    x1_ref[...] = conv(e1_ref, o1_ref, w1_ref, b1_ref)
    x2_ref[...] = conv(e2_ref, o2_ref, w2_ref, b2_ref)


def _hy_filter_kernel(feat_ref, wp_ref, bp_ref, wm_ref, bm_ref, fr_ref, wf_ref, dl_ref,
                      hf_ref, hb_ref, ss_ref, *, d, n_orders):
    start = (pl.program_id(0) == 0) & (pl.program_id(1) == 0)
    feat = feat_ref[...]
    fr = fr_ref[...]
    hdn = jnp.sin(fr * (_mm_f32(feat, wp_ref[...]) + bp_ref[...]))
    hdn = jnp.sin(fr * (_mm_f32(hdn, wm_ref[...]) + bm_ref[...]))
    h = _mm_f32(hdn, wf_ref[...])
    decay = jnp.exp(-feat[:, 0:1] * dl_ref[...])
    row = lax.broadcasted_iota(jnp.int32, decay.shape, 0)
    first = (row == 0) & start

    @pl.when(start)
    def _():
        ss_ref[...] = jnp.zeros_like(ss_ref)

    for o in range(n_orders):
        sl = slice(o * d, (o + 1) * d)
        hf = h[:, 2 * o * d:(2 * o + 1) * d] * decay
        hb = jnp.where(first, 0.0, h[:, (2 * o + 1) * d:(2 * o + 2) * d] * decay)
        hf_ref[:, sl] = hf.astype(BF16)
        hb_ref[:, sl] = hb.astype(BF16)
        ss_ref[:, sl] += jnp.sum(hf * hf + hb * hb, axis=0, keepdims=True)


def _butterfly(e_re, e_s, o_re, o_s, c, s):
    t_re = c * o_re - s * o_s
    t_s = c * o_s + s * o_re
    return e_re + t_re, e_s + t_s, e_re - t_re, e_s - t_s


def _hy_spec_kernel(f_ref, fe_ref, fo_ref, be_ref, bo_ref, c_ref, s_ref, ss_ref, o_ref, *, tp, scale):
    wgt = scale * lax.rsqrt(ss_ref[...] + NORM_EPS)
    for g in range(c_ref.shape[0] // tp):
        re, sn = slice(2 * g * tp, (2 * g + 1) * tp), slice((2 * g + 1) * tp, (2 * g + 2) * tp)
        dft = f_ref[2 * g * tp:(2 * g + 2) * tp, :]
        c = c_ref[g * tp:(g + 1) * tp, :]
        s = s_ref[g * tp:(g + 1) * tp, :]

        def blocks(even_ref, odd_ref):
            e = _mm(dft, even_ref[...])
            o = _mm(dft, odd_ref[...])
            return _butterfly(e[0:tp], e[tp:], o[0:tp], o[tp:], c, s)

        fa_re, fa_s, fb_re, fb_s = blocks(fe_ref, fo_ref)
        ba_re, ba_s, bb_re, bb_s = blocks(be_ref, bo_ref)
        o_ref[0, re, :] = (fa_re + ba_re) * wgt
        o_ref[0, sn, :] = (fa_s - ba_s) * wgt
        o_ref[1, re, :] = (fb_re + bb_re) * wgt
        o_ref[1, sn, :] = (fb_s - bb_s) * wgt


def _hy_fwd_kernel(f_ref, ue_ref, uo_ref, h_ref, c_ref, s_ref, y_ref, *, tp):
    for g in range(c_ref.shape[0] // tp):
        re, sn = slice(2 * g * tp, (2 * g + 1) * tp), slice((2 * g + 1) * tp, (2 * g + 2) * tp)
        dft = f_ref[2 * g * tp:(2 * g + 2) * tp, :]
        c = c_ref[g * tp:(g + 1) * tp, :]
        s = s_ref[g * tp:(g + 1) * tp, :]
        e = _mm(dft, ue_ref[...])
        o = _mm(dft, uo_ref[...])
        a_re, a_s, b_re, b_s = _butterfly(e[0:tp], e[tp:], o[0:tp], o[tp:], c, s)
        ha_re, ha_s = h_ref[0, re, :], h_ref[0, sn, :]
        hb_re, hb_s = h_ref[1, re, :], h_ref[1, sn, :]
        ya_re = a_re * ha_re - a_s * ha_s
        ya_s = a_re * ha_s + a_s * ha_re
        yb_re = b_re * hb_re - b_s * hb_s
        yb_s = b_re * hb_s + b_s * hb_re
        y_ref[0, re, :] = (ya_re + yb_re).astype(BF16)
        y_ref[0, sn, :] = (ya_s + yb_s).astype(BF16)
        d_re = ya_re - yb_re
        d_s = ya_s - yb_s
        y_ref[1, re, :] = (c * d_re + s * d_s).astype(BF16)
        y_ref[1, sn, :] = (c * d_s - s * d_re).astype(BF16)


def _hy_inv_kernel(f_ref, y_ref, a_ref, b_ref, fb_ref, *o_refs, rows):
    for r in range(f_ref.shape[0] // rows):
        sl = slice(r * rows, (r + 1) * rows)
        conv = _mm(f_ref[sl, :], y_ref[...])
        z = a_ref[sl, :] * (conv + b_ref[sl, :] * fb_ref[...])
        for o_ref in o_refs:
            o_ref[sl, :] = z.astype(o_ref.dtype)


def _hy_out_kernel(x_ref, mod_ref, z_ref, pe_ref, po_ref, w_ref, b_ref, o_ref):
    d = w_ref.shape[0]
    z = (_mm(pe_ref[...], z_ref[:, :d]) + _mm(po_ref[...], z_ref[:, d:])).astype(BF16)
    o_ref[...] = x_ref[...] + mod_ref[5:6, :] * (_mm(z, w_ref[...]) + b_ref[...])


def _hyena_layer(x, mods, layer, row_fn, g, w_in, b_in, w_short, b_short, w_pos, b_pos, w_mid, b_mid,
                 freq, w_filt, fbias, w_out, b_out):
    bsz, length, d = x.shape
    n_full = 2 * length
    tm = min(512, length)
    tc = DFT_COL_TILE
    n_ct = d // tc
    width = w_mid.shape[0]
    n_parts = 2 * HYENA_ORDER

    half = length // 2
    perm = np.zeros((tm, tm), np.float32)
    perm[np.arange(tm // 2), 2 * np.arange(tm // 2)] = 1.0
    perm[tm // 2 + np.arange(tm // 2), 2 * np.arange(tm // 2) + 1] = 1.0
    p = pl.pallas_call(
        _hy_proj_kernel,
        grid=(bsz, length // tm),
        in_specs=[
            pl.BlockSpec((None, tm, d), lambda b, i: (b, i, 0)),
            _mod_spec(d, layer, row_fn),
            _const_spec((1, d)),
            _const_spec((tm, tm)),
            _const_spec((d, 3 * d)),
            _const_spec((1, 3 * d)),
        ],
        out_specs=pl.BlockSpec((None, tm // 2, 6 * d), lambda b, i: (b, i, 0)),
        out_shape=jax.ShapeDtypeStruct((bsz, half, 6 * d), F32),
        compiler_params=_params(2),
        name="hy_proj",
    )(x, mods, g.reshape(1, d), jnp.asarray(perm, BF16), w_in, b_in.reshape(1, 3 * d))

    def stream_spec(part, odd):
        return pl.BlockSpec((None, half, tc), lambda b, j, par: (b, 0, (3 * odd + part) * n_ct + j))

    def wpart_spec(rows, part):
        return pl.BlockSpec((rows, tc), lambda b, j, par: (0, part * n_ct + j))

    pair_spec = pl.BlockSpec((None, half, tc), lambda b, j, par: (b, 0, par * n_ct + j))
    b_short2 = b_short.reshape(1, 3 * d)
    vb, v, x1, x2 = pl.pallas_call(
        _hy_short_kernel,
        grid=(bsz, n_ct, 2),
        in_specs=[stream_spec(0, 0), stream_spec(0, 1), stream_spec(1, 0), stream_spec(1, 1),
                  stream_spec(2, 0), stream_spec(2, 1),
                  wpart_spec(3, 0), wpart_spec(3, 1), wpart_spec(3, 2),
                  wpart_spec(1, 0), wpart_spec(1, 1), wpart_spec(1, 2)],
        out_specs=[pair_spec] * 4,
        out_shape=[jax.ShapeDtypeStruct((bsz, half, 2 * d), BF16)]
        + [jax.ShapeDtypeStruct((bsz, half, 2 * d), F32)] * 3,
        compiler_params=_params(3),
        name="hy_short",
    )(p, p, p, p, p, p, w_short, w_short, w_short, b_short2, b_short2, b_short2)

    t = jnp.linspace(0.0, 1.0, length, dtype=F32)[:, None]
    bands = jnp.linspace(1e-4, HYENA_BANDS - 1, HYENA_BANDS, dtype=F32)
    ang = (2.0 * math.pi / length) * jnp.arange(length, dtype=F32)[:, None] * bands[None, :]
    feat = jnp.concatenate([t, jnp.cos(ang), -jnp.sin(ang)], axis=-1)
    feat = jnp.pad(feat, ((0, 0), (0, V7X_LANES - HYENA_EMB)))
    w_pos_p = jnp.pad(w_pos.astype(F32), ((0, V7X_LANES - HYENA_EMB), (0, 0)))
    deltas = jnp.abs(jnp.linspace(HYENA_MIN_DECAY, HYENA_MAX_DECAY, d, dtype=F32)).reshape(1, d)
    feat = jnp.concatenate([feat[0::2], feat[1::2]], axis=0)
    tl = min(512, half)
    od = HYENA_ORDER * d
    tap_spec = pl.BlockSpec((tl, od), lambda par, i: (i, par))
    taps_f, taps_b, ss = pl.pallas_call(
        functools.partial(_hy_filter_kernel, d=d, n_orders=HYENA_ORDER),
        grid=(2, half // tl),
        in_specs=[
            pl.BlockSpec((tl, V7X_LANES), lambda par, i: (par * (half // tl) + i, 0)),
            _const_spec((V7X_LANES, width)),
            _const_spec((1, width)),
            _const_spec((width, width)),
            _const_spec((1, width)),
            _const_spec((1, width)),
            _const_spec((width, n_parts * d)),
            _const_spec((1, d)),
        ],
        out_specs=[tap_spec, tap_spec, pl.BlockSpec((1, od), lambda par, i: (0, 0))],
        out_shape=[jax.ShapeDtypeStruct((half, 2 * od), BF16), jax.ShapeDtypeStruct((half, 2 * od), BF16),
                   jax.ShapeDtypeStruct((1, od), F32)],
        compiler_params=_params(2),
        name="hy_filter",
    )(feat, w_pos_p, b_pos.reshape(1, width), w_mid, b_mid.reshape(1, width), freq.reshape(1, width),
      w_filt, deltas)

    tf = min(DFT_FREQ_TILE, half)
    tp = min(DFT_PACK, half)
    tn = min(DFT_TIME_TILE, half)
    dft_fwd, dft_inv = _dft_matrices(half, tp)
    tw = (math.pi / n_full) * (2.0 * jnp.arange(half, dtype=F32) + 1.0)
    tw_c = jnp.broadcast_to(jnp.cos(tw)[:, None], (half, tc))
    tw_s = jnp.broadcast_to(jnp.sin(tw)[:, None], (half, tc))

    n_oc = od // tc
    even_spec = pl.BlockSpec((half, tc), lambda f, o, j: (0, o * n_ct + j))
    odd_spec = pl.BlockSpec((half, tc), lambda f, o, j: (0, n_oc + o * n_ct + j))
    tw_spec = pl.BlockSpec((tf, tc), lambda f, *_: (f, 0))
    spec = pl.pallas_call(
        functools.partial(_hy_spec_kernel, tp=tp, scale=1.0 / length),
        grid=(half // tf, HYENA_ORDER, n_ct),
        in_specs=[pl.BlockSpec((2 * tf, half), lambda f, o, j: (f, 0)),
                  even_spec, odd_spec, even_spec, odd_spec, tw_spec, tw_spec,
                  pl.BlockSpec((1, tc), lambda f, o, j: (0, o * n_ct + j))],
        out_specs=pl.BlockSpec((None, 2, 2 * tf, tc), lambda f, o, j: (o, 0, f, j)),
        out_shape=jax.ShapeDtypeStruct((HYENA_ORDER, 2, length, d), F32),
        compiler_params=_params(3),
        name="hy_spec",
    )(dft_fwd, taps_f, taps_f, taps_b, taps_b, tw_c, tw_s, ss)

    def long_conv(order, u2, a, bsrc, want_f32):
        yspec = pl.pallas_call(
            functools.partial(_hy_fwd_kernel, tp=tp),
            grid=(half // tf, n_ct, bsz),
            in_specs=[
                pl.BlockSpec((2 * tf, half), lambda f, j, b: (f, 0)),
                pl.BlockSpec((None, half, tc), lambda f, j, b: (b, 0, j)),
                pl.BlockSpec((None, half, tc), lambda f, j, b: (b, 0, n_ct + j)),
                pl.BlockSpec((None, 2, 2 * tf, tc), lambda f, j, b: (order, 0, f, j)),
                tw_spec, tw_spec,
            ],
            out_specs=pl.BlockSpec((None, 2, 2 * tf, tc), lambda f, j, b: (b, 0, f, j)),
            out_shape=jax.ShapeDtypeStruct((bsz, 2, length, d), BF16),
            compiler_params=_params(3),
            name="hy_fwd",
        )(dft_fwd, u2, u2, spec, tw_c, tw_s)
        row_spec = pl.BlockSpec((None, tn, tc), lambda n, j, b, p: (b, n, p * n_ct + j))
        out_shape = [jax.ShapeDtypeStruct((bsz, half, 2 * d), BF16)]
        if want_f32:
            out_shape.append(jax.ShapeDtypeStruct((bsz, half, 2 * d), F32))
        outs = pl.pallas_call(
            functools.partial(_hy_inv_kernel, rows=min(DFT_ROW_GROUP, tn)),
            grid=(half // tn, n_ct, bsz, 2),
            in_specs=[
                pl.BlockSpec((tn, length), lambda n, j, b, p: (n, 0)),
                pl.BlockSpec((None, None, length, tc), lambda n, j, b, p: (b, p, 0, j)),
                row_spec,
                row_spec,
                pl.BlockSpec((None, 1, tc), lambda n, j, b, p: (order, 0, j)),
            ],
            out_specs=[row_spec] * len(out_shape),
            out_shape=out_shape,
            compiler_params=_params(4),
            name="hy_inv",
        )(dft_inv, yspec, a, bsrc, fbias.astype(F32).reshape(HYENA_ORDER, 1, d))
        return outs

    zb, z = long_conv(0, vb, x1, v, True)
    (zzb,) = long_conv(1, zb, x2, z, False)

    pick_even = np.zeros((tm, tm // 2), np.float32)
    pick_even[2 * np.arange(tm // 2), np.arange(tm // 2)] = 1.0
    pick_odd = np.zeros((tm, tm // 2), np.float32)
    pick_odd[2 * np.arange(tm // 2) + 1, np.arange(tm // 2)] = 1.0
    return pl.pallas_call(
        _hy_out_kernel,
        grid=(bsz, length // tm),
        in_specs=[
            pl.BlockSpec((None, tm, d), lambda b, i: (b, i, 0)),
            _mod_spec(d, layer, row_fn),
            pl.BlockSpec((None, tm // 2, 2 * d), lambda b, i: (b, i, 0)),
            _const_spec((tm, tm // 2)),
            _const_spec((tm, tm // 2)),
            _const_spec((d, d)),
            _const_spec((1, d)),
        ],
        out_specs=pl.BlockSpec((None, tm, d), lambda b, i: (b, i, 0)),
        out_shape=jax.ShapeDtypeStruct(x.shape, F32),
        compiler_params=_params(2),
        name="hy_out",
    )(x, mods, zzb, jnp.asarray(pick_even, BF16), jnp.asarray(pick_odd, BF16), w_out, b_out.reshape(1, d))


def kernel(x, c, ctx, c_ctx, ada_w, ada_b, norm_g, ffn_w1, ffn_w3, ffn_w2, ret_w_in, ret_w_out, ret_decay,
           pool_w, pool_b, pool_scale, hy_w_in, hy_b_in, hy_w_short, hy_b_short, hy_w_pos, hy_b_pos,
           hy_w_mid, hy_b_mid, hy_freq, hy_w_filt, hy_bias, hy_w_out, hy_b_out, final_g):
    bsz, seq, d = x.shape
    t_ctx = ctx.shape[1]
    depth = ada_w.shape[0]
    n_mixers = 3
    assert bsz + 1 <= MOD_ROWS
    ctx_row = bsz

    cond = jnp.concatenate([c, c_ctx[None, :], jnp.zeros((MOD_ROWS - bsz - 1, d), F32)], axis=0)
    mods = _ada_all(cond, ada_w, ada_b).reshape(depth, MOD_ROWS, ADA_CHUNKS, d)

    w1b, w3b, w2b = ffn_w1.astype(BF16), ffn_w3.astype(BF16), ffn_w2.astype(BF16)
    lat_row = lambda b, i: b
    ctx_flat_row = lambda b, i: ctx_row

    def ffn_lat(h, layer, half, final=False):
        return _half_ffn(h, mods, layer, half, lat_row, norm_g[layer, 2 * half], w1b, w3b, w2b, final_g,
                         final=final)

    def ffn_ctx(h, layer, half):
        flat = h.reshape(1, bsz * t_ctx, d)
        out = _half_ffn(flat, mods, layer, half, ctx_flat_row, norm_g[layer, 2 * half], w1b, w3b, w2b, final_g)
        return out.reshape(bsz, t_ctx, d)

    h_lat, h_ctx = x, ctx
    for layer in range(depth):
        kind = layer % n_mixers
        slot = layer // n_mixers
        last = layer == depth - 1
        ctx_out = not last
        ctx_live = ctx_out or kind == 0
        h_lat = ffn_lat(h_lat, layer, 0)
        if ctx_live:
            h_ctx = ffn_ctx(h_ctx, layer, 0)
        g_mix = norm_g[layer, 1]
        if kind == 0:
            h_ctx, h_lat = _retention_layer(h_ctx, h_lat, mods, layer, g_mix, ret_w_in[slot].astype(BF16),
                                            ret_w_out[slot].astype(BF16), ret_decay[slot], ctx_out)
        elif kind == 1:
            pp = (pool_w[slot], pool_b[slot], pool_scale[slot])
            h_lat = _pool_layer(h_lat, GRID_W, mods, layer, lat_row, g_mix, *pp)
            if ctx_out:
                h_ctx = _pool_layer(h_ctx, t_ctx, mods, layer, ctx_flat_row, g_mix, *pp)
        else:
            hp = (hy_w_in[slot].astype(BF16), hy_b_in[slot], hy_w_short[slot], hy_b_short[slot], hy_w_pos[slot],
                  hy_b_pos[slot], hy_w_mid[slot], hy_b_mid[slot], hy_freq[slot], hy_w_filt[slot],
                  hy_bias[slot], hy_w_out[slot].astype(BF16), hy_b_out[slot])
            h_lat = _hyena_layer(h_lat, mods, layer, lat_row, g_mix, *hp)
            if ctx_out:
                h_ctx = _hyena_layer(h_ctx, mods, layer, ctx_flat_row, g_mix, *hp)
        h_lat = ffn_lat(h_lat, layer, 1, final=last)
        if ctx_out:
            h_ctx = ffn_ctx(h_ctx, layer, 1)
    return h_lat
```

```python
import functools
import math

import numpy as np
import jax
import jax.numpy as jnp
from jax import lax
from jax.experimental import pallas as pl
from jax.experimental.pallas import tpu as pltpu

F32 = jnp.float32
BF16 = jnp.bfloat16

GRID_W = 64
ADA_CHUNKS = 9
NORM_EPS = 1e-6
RET_HEADS = 4
RET_CHUNK = 128
ROPE_BASE = 10000.0
POOL_WINDOWS = (2, 4, 8, 16)
HYENA_ORDER = 2
HYENA_EMB = 33
HYENA_BANDS = (HYENA_EMB - 1) // 2
HYENA_TARGET = 1e-2
HYENA_FAST = 0.3
HYENA_SLOW = 1.5
HYENA_MAX_DECAY = math.log(HYENA_TARGET) / HYENA_FAST
HYENA_MIN_DECAY = math.log(HYENA_TARGET) / HYENA_SLOW

V7X_LANES = 128
V7X_VMEM_LIMIT_BYTES = 56 * 1024 * 1024
MOD_ROWS = 8
POOL_TILE = 256
RET_TILE = 256
SCAN_CHUNK = 256
DFT_FREQ_TILE = 512
DFT_PACK = 256
DFT_ROW_GROUP = 512
DFT_TIME_TILE = 2048
DFT_COL_TILE = 256
HIGHEST = lax.Precision.HIGHEST


def _params(n_axes):
    return pltpu.CompilerParams(
        dimension_semantics=("arbitrary",) * n_axes,
        vmem_limit_bytes=V7X_VMEM_LIMIT_BYTES)


def _const_spec(shape):
    zeros = (0,) * len(shape)
    return pl.BlockSpec(shape, lambda *_: zeros, pipeline_mode=pl.Buffered(1))


def _mod_spec(d, layer, row_fn):
    return pl.BlockSpec((None, None, ADA_CHUNKS, d), lambda *idx: (layer, row_fn(*idx), 0, 0))


def _mm(a, b):
    return jnp.dot(a, b, preferred_element_type=F32)


def _mm_f32(a, b):
    return jnp.dot(a, b, preferred_element_type=F32, precision=HIGHEST)


def _silu(x):
    return x * jax.nn.sigmoid(x)


def _adaln(x, g, shift, scale):
    ms = jnp.mean(x * x, axis=-1, keepdims=True)
    y = x * lax.rsqrt(ms + NORM_EPS) * g
    return y * (1.0 + scale) + shift


def _ada_kernel(c_ref, w_ref, b_ref, o_ref):
    s = _silu(c_ref[...]).astype(BF16)
    o_ref[...] = _mm(s, w_ref[...].astype(BF16)) + b_ref[...]


def _ada_all(cond, ada_w, ada_b):
    depth, d, nd = ada_w.shape
    return pl.pallas_call(
        _ada_kernel,
        grid=(depth, nd // d),
        in_specs=[
            pl.BlockSpec((MOD_ROWS, d), lambda l, j: (0, 0)),
            pl.BlockSpec((None, d, d), lambda l, j: (l, 0, j)),
            pl.BlockSpec((None, 1, d), lambda l, j: (l, 0, j)),
        ],
        out_specs=pl.BlockSpec((None, MOD_ROWS, d), lambda l, j: (l, 0, j)),
        out_shape=jax.ShapeDtypeStruct((depth, MOD_ROWS, nd), F32),
        compiler_params=_params(2),
        name="ada",
    )(cond, ada_w, ada_b.reshape(depth, 1, nd))


def _ffn_kernel(xl_ref, xc_ref, mod_ref, g_ref, w1_ref, w3_ref, w2_ref, fg_ref, *o_refs, base, final, n_lat):
    is_ctx = pl.program_id(0) >= n_lat
    x = jnp.where(is_ctx, xc_ref[...], xl_ref[...]) if len(o_refs) == 2 else xl_ref[...]
    u = _adaln(x, g_ref[...], mod_ref[base:base + 1, :], mod_ref[base + 1:base + 2, :]).astype(BF16)
    h1 = _mm(u, w1_ref[...])
    h3 = _mm(u, w3_ref[...])
    a = (_silu(h1) * h3).astype(BF16)
    y = x + (0.5 * mod_ref[base + 2:base + 3, :]) * _mm(a, w2_ref[...])
    if final:
        ms = jnp.mean(y * y, axis=-1, keepdims=True)
        y = y * lax.rsqrt(ms + NORM_EPS) * fg_ref[...]
    if len(o_refs) == 2:
        @pl.when(jnp.logical_not(is_ctx))
        def _():
            o_refs[0][...] = y

        @pl.when(is_ctx)
        def _():
            o_refs[1][...] = y
    else:
        o_refs[0][...] = y


def _half_ffn(h_lat, h_ctx, mods, layer, half, g, w1, w3, w2, final_g, final=False, tm=512):
    bsz, t, d = h_lat.shape
    f = w1.shape[-1]
    tm = min(tm, t)
    base = 6 * half
    per_b = t // tm
    n_lat = bsz * per_b
    with_ctx = h_ctx is not None
    ctx_flat = h_ctx.reshape(1, -1, d) if with_ctx else h_lat[:1, :tm]
    n_ctx = ctx_flat.shape[1] // tm if with_ctx else 0
    assert t % tm == 0 and ctx_flat.shape[1] % tm == 0

    def lat_idx(s):
        s = jnp.minimum(s, n_lat - 1)
        return (s // per_b, s % per_b, 0)

    def ctx_idx(s):
        return (0, jnp.maximum(s - n_lat, 0), 0)

    def weight_spec(rows, cols):
        return pl.BlockSpec((None, None, rows, cols), lambda *_: (layer, half, 0, 0),
                            pipeline_mode=pl.Buffered(1))

    out_specs = [pl.BlockSpec((None, tm, d), lat_idx)]
    out_shape = [jax.ShapeDtypeStruct(h_lat.shape, F32)]
    if with_ctx:
        out_specs.append(pl.BlockSpec((None, tm, d), ctx_idx))
        out_shape.append(jax.ShapeDtypeStruct(ctx_flat.shape, F32))
    outs = pl.pallas_call(
        functools.partial(_ffn_kernel, base=base, final=final, n_lat=n_lat),
        grid=(n_lat + n_ctx,),
        in_specs=[
            pl.BlockSpec((None, tm, d), lat_idx),
            pl.BlockSpec((None, tm, d), ctx_idx),
            _mod_spec(d, layer, lambda s: jnp.where(s < n_lat, s // per_b, bsz)),
            _const_spec((1, d)),
            weight_spec(d, f),
            weight_spec(d, f),
            weight_spec(f, d),
            _const_spec((1, d)),
        ],
        out_specs=out_specs,
        out_shape=out_shape,
        compiler_params=_params(1),
        name="half_ffn",
    )(h_lat, ctx_flat, mods, g.reshape(1, d), w1, w3, w2, final_g.reshape(1, d))
    return outs[0], (outs[1].reshape(h_ctx.shape) if with_ctx else None)


def _ret_proj_kernel(xc_ref, xl_ref, mod_ref, g_ref, w_ref, cos_ref, sin_ref, q_ref, k_ref, v_ref, gate_ref,
                     *, n_heads, dk, dv, ctx_tiles):
    x = jnp.where(pl.program_id(1) < ctx_tiles, xc_ref[...], xl_ref[...])
    u = _adaln(x, g_ref[...], mod_ref[3:4, :], mod_ref[4:5, :]).astype(BF16)
    p = _mm(u, w_ref[...])
    cos = cos_ref[...]
    sin = sin_ref[...]
    half = dk // 2
    qk = n_heads * dk
    k_scale = dk ** -0.5

    def rot(a):
        a1, a2 = a[:, :half], a[:, half:]
        return jnp.concatenate([a1 * cos - a2 * sin, a2 * cos + a1 * sin], axis=-1)

    for h in range(n_heads):
        q_ref[:, h * dk:(h + 1) * dk] = rot(p[:, h * dk:(h + 1) * dk]).astype(BF16)
        k_ref[:, h * dk:(h + 1) * dk] = (rot(p[:, qk + h * dk:qk + (h + 1) * dk]) * k_scale).astype(BF16)
    v_ref[...] = p[:, 2 * qk:2 * qk + n_heads * dv].astype(BF16)
    gate_ref[...] = p[:, 2 * qk + n_heads * dv:].astype(BF16)


def _ret_scan_kernel(q_ref, k_ref, v_ref, dmat_ref, qdec_ref, kdec_ref, cdec_ref, o_ref,
                     state_ref, fwd_ref, *, n_heads, dk, dv, n_chunks, bwd_chunk):
    s = pl.program_id(1)
    cs = q_ref.shape[0]

    @pl.when((s == 0) | (s == n_chunks))
    def _():
        state_ref[...] = jnp.zeros_like(state_ref)

    chunk = jnp.where(s < n_chunks, s, bwd_chunk(jnp.maximum(s - n_chunks, 0)))
    rows = pl.ds(pl.multiple_of(chunk * cs, cs), cs)
    outs = []
    for h in range(n_heads):
        q = q_ref[:, h * dk:(h + 1) * dk]
        k = k_ref[:, h * dk:(h + 1) * dk]
        v = v_ref[:, h * dv:(h + 1) * dv]
        scores = lax.dot_general(q, k, (((1,), (1,)), ((), ())), preferred_element_type=F32) * dmat_ref[h]
        state = state_ref[h]
        outs.append(_mm(scores.astype(BF16), v) + _mm(q, state.astype(BF16)) * qdec_ref[h])
        kd = (k.astype(F32) * kdec_ref[h]).astype(BF16)
        state_ref[h] = state * cdec_ref[h] + lax.dot_general(
            kd, v, (((0,), (0,)), ((), ())), preferred_element_type=F32)

    @pl.when(s < n_chunks)
    def _():
        for h in range(n_heads):
            fwd_ref[rows, h * dv:(h + 1) * dv] = outs[h].astype(fwd_ref.dtype)

    @pl.when(s >= n_chunks)
    def _():
        for h in range(n_heads):
            o = outs[h] + fwd_ref[rows, h * dv:(h + 1) * dv].astype(F32)
            mu = jnp.mean(o, axis=-1, keepdims=True)
            oc = o - mu
            var = jnp.mean(oc * oc, axis=-1, keepdims=True)
            o_ref[:, h * dv:(h + 1) * dv] = (oc * lax.rsqrt(var + NORM_EPS)).astype(o_ref.dtype)


def _ret_out_kernel(x_ref, mod_ref, y_ref, gate_ref, w_ref, o_ref):
    z = (_silu(gate_ref[...].astype(F32)) * y_ref[...].astype(F32)).astype(BF16)
    o_ref[...] = x_ref[...] + mod_ref[5:6, :] * _mm(z, w_ref[...])


def _retention_layer(h_ctx, h_lat, mods, layer, g, w_in, w_out, decay, ctx_out):
    bsz, t_ctx, d = h_ctx.shape
    t_lat = h_lat.shape[1]
    t_all = t_ctx + t_lat
    n_heads = RET_HEADS
    dk = d // n_heads
    dv = 2 * d // n_heads
    qk = n_heads * dk
    vw = n_heads * dv
    tm = RET_TILE
    assert t_ctx % tm == 0 and t_lat % tm == 0
    ctx_tiles = t_ctx // tm
    ctx_row = bsz

    pos = jnp.arange(t_all, dtype=F32)
    inv = 1.0 / (ROPE_BASE ** jnp.linspace(0.0, 1.0, dk // 2, dtype=F32))
    ang = pos[:, None] * inv[None, :]

    q, k, v, gate = pl.pallas_call(
        functools.partial(_ret_proj_kernel, n_heads=n_heads, dk=dk, dv=dv, ctx_tiles=ctx_tiles),
        grid=(bsz, t_all // tm),
        in_specs=[
            pl.BlockSpec((None, tm, d), lambda b, i: (b, jnp.minimum(i, ctx_tiles - 1), 0)),
            pl.BlockSpec((None, tm, d), lambda b, i: (b, jnp.maximum(i - ctx_tiles, 0), 0)),
            _mod_spec(d, layer, lambda b, i: jnp.where(i < ctx_tiles, ctx_row, b)),
            _const_spec((1, d)),
            _const_spec((d, 2 * qk + 2 * vw)),
            pl.BlockSpec((tm, dk // 2), lambda b, i: (i, 0)),
            pl.BlockSpec((tm, dk // 2), lambda b, i: (i, 0)),
        ],
        out_specs=[
            pl.BlockSpec((None, tm, qk), lambda b, i: (b, i, 0)),
            pl.BlockSpec((None, tm, qk), lambda b, i: (b, i, 0)),
            pl.BlockSpec((None, tm, vw), lambda b, i: (b, i, 0)),
            pl.BlockSpec((None, tm, vw), lambda b, i: (b, i, 0)),
        ],
        out_shape=[
            jax.ShapeDtypeStruct((bsz, t_all, qk), BF16),
            jax.ShapeDtypeStruct((bsz, t_all, qk), BF16),
            jax.ShapeDtypeStruct((bsz, t_all, vw), BF16),
            jax.ShapeDtypeStruct((bsz, t_all, vw), BF16),
        ],
        compiler_params=_params(2),
        name="ret_proj",
    )(h_ctx, h_lat, mods, g.reshape(1, d), w_in, jnp.cos(ang), jnp.sin(ang))

    cs = SCAN_CHUNK
    assert t_ctx % cs == 0 and t_lat % cs == 0
    n_chunks = t_all // cs
    ctx_chunks = t_ctx // cs
    lg = jax.nn.log_sigmoid(decay.astype(F32))[:, :, None]
    n = jnp.arange(cs, dtype=F32)
    rel = n[:, None] - n[None, :]
    dm_f = jnp.where(rel >= 0, jnp.exp(lg[0][:, :, None] * jnp.maximum(rel, 0.0)), 0.0)
    dm_b = jnp.where(rel < 0, jnp.exp(lg[1][:, :, None] * jnp.maximum(-rel, 0.0)), 0.0)
    dmat = jnp.stack([dm_f, dm_b])
    qd = jnp.stack([jnp.exp(lg[0] * (n + 1.0)), jnp.exp(lg[1] * (cs - n))])
    kd = jnp.stack([jnp.exp(lg[0] * (cs - 1.0 - n)), jnp.exp(lg[1] * n)])
    cd = jnp.exp(lg * cs)
    qdec = jnp.broadcast_to(qd[..., None], (2, n_heads, cs, dv))
    kdec = jnp.broadcast_to(kd[..., None], (2, n_heads, cs, dk))
    cdec = jnp.broadcast_to(cd[..., None], (2, n_heads, 1, dv))

    def bwd_chunk(j):
        return jnp.where(j < ctx_chunks, ctx_chunks - 1 - j, n_chunks - 1 + ctx_chunks - j)

    def chunk(s):
        return jnp.where(s < n_chunks, s, bwd_chunk(jnp.maximum(s - n_chunks, 0)))

    def out_chunk(s):
        return bwd_chunk(jnp.maximum(s - n_chunks, 0))

    def table_spec(*shape):
        return pl.BlockSpec((None,) + shape, lambda b, s: (s // n_chunks, 0, 0, 0))

    y = pl.pallas_call(
        functools.partial(_ret_scan_kernel, n_heads=n_heads, dk=dk, dv=dv, n_chunks=n_chunks,
                          bwd_chunk=bwd_chunk),
        grid=(bsz, 2 * n_chunks),
        in_specs=[
            pl.BlockSpec((None, cs, qk), lambda b, s: (b, chunk(s), 0)),
            pl.BlockSpec((None, cs, qk), lambda b, s: (b, chunk(s), 0)),
            pl.BlockSpec((None, cs, vw), lambda b, s: (b, chunk(s), 0)),
            table_spec(n_heads, cs, cs),
            table_spec(n_heads, cs, dv),
            table_spec(n_heads, cs, dk),
            table_spec(n_heads, 1, dv),
        ],
        out_specs=pl.BlockSpec((None, cs, vw), lambda b, s: (b, out_chunk(s), 0)),
        out_shape=jax.ShapeDtypeStruct((bsz, t_all, vw), BF16),
        scratch_shapes=[pltpu.VMEM((n_heads, dk, dv), F32), pltpu.VMEM((t_all, vw), BF16)],
        compiler_params=_params(2),
        name="ret_scan",
    )(q, k, v, dmat, qdec, kdec, cdec)

    def readout(x, tile0, row_fn):
        return pl.pallas_call(
            _ret_out_kernel,
            grid=(bsz, x.shape[1] // tm),
            in_specs=[
                pl.BlockSpec((None, tm, d), lambda b, i: (b, i, 0)),
                _mod_spec(d, layer, row_fn),
                pl.BlockSpec((None, tm, vw), lambda b, i: (b, i + tile0, 0)),
                pl.BlockSpec((None, tm, vw), lambda b, i: (b, i + tile0, 0)),
                _const_spec((vw, d)),
            ],
            out_specs=pl.BlockSpec((None, tm, d), lambda b, i: (b, i, 0)),
            out_shape=jax.ShapeDtypeStruct(x.shape, F32),
            compiler_params=_params(2),
            name="ret_out",
        )(x, mods, y, gate, w_out)

    new_lat = readout(h_lat, ctx_tiles, lambda b, i: b)
    new_ctx = readout(h_ctx, 0, lambda b, i: ctx_row) if ctx_out else None
    return new_ctx, new_lat


def _pool_tables(seg_len):
    assert POOL_TILE % seg_len == 0
    r = np.arange(POOL_TILE)
    seg, pos = r // seg_len, r % seg_len
    mats, invs = [], []
    for win in POOL_WINDOWS:
        lo = np.clip(pos - win // 2, 0, seg_len)
        hi = np.clip(pos - win // 2 + win, 0, seg_len)
        col_seg, col_pos = seg[None, :], pos[None, :]
        member = (col_seg == seg[:, None]) & (col_pos >= lo[:, None]) & (col_pos < hi[:, None])
        mats.append(member.astype(np.float32))
        invs.append((1.0 / (hi - lo)).astype(np.float32))
    return np.stack(mats), np.stack(invs)


def _pool_kernel(x_ref, mod_ref, g_ref, s_ref, inv_ref, w_ref, b_ref, sc_ref, o_ref, *, n_groups, dg):
    x = x_ref[...]
    u = _adaln(x, g_ref[...], mod_ref[3:4, :], mod_ref[4:5, :])
    gate = mod_ref[5:6, :]
    for gi in range(n_groups):
        sl = slice(gi * dg, (gi + 1) * dg)
        ug = u[:, sl]
        hi = ug.astype(BF16)
        lo = (ug - hi.astype(F32)).astype(BF16)
        win_sum = _mm(s_ref[gi], hi) + _mm(s_ref[gi], lo)
        dlt = win_sum * inv_ref[gi] - ug
        yg = _mm(dlt.astype(BF16), w_ref[gi])
        o_ref[:, sl] = x[:, sl] + gate[:, sl] * ((yg + b_ref[:, sl]) * sc_ref[:, sl])


def _pool_layer(x, seg_len, mods, layer, row_fn, g, w_grp, b_grp, scale):
    bsz, t, d = x.shape
    n_groups = len(POOL_WINDOWS)
    dg = d // n_groups
    mats, invs = _pool_tables(seg_len)
    s_tab = jnp.asarray(mats, BF16)
    inv_tab = jnp.asarray(np.broadcast_to(invs[:, :, None], (n_groups, POOL_TILE, dg)).copy(), F32)
    return pl.pallas_call(
        functools.partial(_pool_kernel, n_groups=n_groups, dg=dg),
        grid=(bsz, t // POOL_TILE),
        in_specs=[
            pl.BlockSpec((None, POOL_TILE, d), lambda b, i: (b, i, 0)),
            _mod_spec(d, layer, row_fn),
            _const_spec((1, d)),
            _const_spec((n_groups, POOL_TILE, POOL_TILE)),
            _const_spec((n_groups, POOL_TILE, dg)),
            _const_spec((n_groups, dg, dg)),
            _const_spec((1, d)),
            _const_spec((1, d)),
        ],
        out_specs=pl.BlockSpec((None, POOL_TILE, d), lambda b, i: (b, i, 0)),
        out_shape=jax.ShapeDtypeStruct(x.shape, F32),
        compiler_params=_params(2),
        name="pool",
    )(x, mods, g.reshape(1, d), s_tab, inv_tab, w_grp.astype(BF16), b_grp.reshape(1, d), scale.reshape(1, d))


def _dft_tables(n_in):
    mod = 4 * n_in
    idx = jnp.arange(n_in, dtype=jnp.int32)[None, :]
    blk = jnp.arange(n_in // V7X_LANES, dtype=jnp.int32)[:, None]
    sub = jnp.arange(V7X_LANES, dtype=jnp.int32)[:, None]
    theta = 2.0 * math.pi / mod

    def cs(p):
        ph = (p % mod).astype(F32) * theta
        return jnp.cos(ph), jnp.sin(ph)

    fwd = cs(2 * V7X_LANES * blk * idx) + cs((2 * sub + 1) * idx)
    inv = cs(V7X_LANES * blk * (2 * idx + 1)) + cs(sub * (2 * idx + 1))
    return fwd, inv


def _dft_fwd_gen_kernel(pc_ref, ps_ref, qc_ref, qs_ref, o_ref, *, tf):
    ft = pl.program_id(0)
    blocks = tf // V7X_LANES
    qc = qc_ref[...]
    qs = qs_ref[...]
    for jb in range(blocks):
        kb = ft * blocks + jb
        pc = pc_ref[pl.ds(kb, 1), :]
        ps = ps_ref[pl.ds(kb, 1), :]
        o_ref[jb * V7X_LANES:(jb + 1) * V7X_LANES, :] = (pc * qc - ps * qs).astype(BF16)
        o_ref[tf + jb * V7X_LANES:tf + (jb + 1) * V7X_LANES, :] = (ps * qc + pc * qs).astype(BF16)


def _dft_inv_gen_kernel(pc_ref, ps_ref, qc_ref, qs_ref, o_ref, *, tf, n_in):
    mb = pl.program_id(0)
    pc = pc_ref[pl.ds(mb, 1), :]
    ps = ps_ref[pl.ds(mb, 1), :]
    qc = qc_ref[...]
    qs = qs_ref[...]
    cos_all = pc * qc - ps * qs
    sin_all = ps * qc + pc * qs
    for ft in range(n_in // tf):
        o_ref[:, 2 * ft * tf:(2 * ft + 1) * tf] = cos_all[:, ft * tf:(ft + 1) * tf].astype(BF16)
        o_ref[:, (2 * ft + 1) * tf:(2 * ft + 2) * tf] = sin_all[:, ft * tf:(ft + 1) * tf].astype(BF16)


def _dft_matrices(n_in, tf):
    fwd_tabs, inv_tabs = _dft_tables(n_in)
    nb = n_in // V7X_LANES
    tabs = [_const_spec((nb, n_in)), _const_spec((nb, n_in)),
            _const_spec((V7X_LANES, n_in)), _const_spec((V7X_LANES, n_in))]
    fwd = pl.pallas_call(
        functools.partial(_dft_fwd_gen_kernel, tf=tf),
        grid=(n_in // tf,),
        in_specs=tabs,
        out_specs=pl.BlockSpec((2 * tf, n_in), lambda i: (i, 0)),
        out_shape=jax.ShapeDtypeStruct((2 * n_in, n_in), BF16),
        compiler_params=_params(1),
        name="dft_fwd_gen",
    )(*fwd_tabs)
    inv = pl.pallas_call(
        functools.partial(_dft_inv_gen_kernel, tf=tf, n_in=n_in),
        grid=(nb,),
        in_specs=tabs,
        out_specs=pl.BlockSpec((V7X_LANES, 2 * n_in), lambda i: (i, 0)),
        out_shape=jax.ShapeDtypeStruct((n_in, 2 * n_in), BF16),
        compiler_params=_params(1),
        name="dft_inv_gen",
    )(*inv_tabs)
    return fwd, inv


def _hy_proj_kernel(x_ref, xp_ref, xn_ref, mod_ref, g_ref, perm_ref, w_ref, b_ref, ws_ref, bs_ref,
                    vb_ref, v_ref, x1_ref, x2_ref):
    i = pl.program_id(1)
    tm, d = x_ref.shape
    h2 = tm // 2
    g = g_ref[...]
    shift, scale = mod_ref[3:4, :], mod_ref[4:5, :]
    u = _adaln(x_ref[...], g, shift, scale).astype(BF16)
    u = _mm(perm_ref[...], u).astype(BF16)
    halo = jnp.concatenate([xp_ref[...], xn_ref[...]], axis=0)
    uh = _adaln(halo, g, shift, scale).astype(BF16)
    p = _mm(jnp.concatenate([u, uh], axis=0), w_ref[...]) + b_ref[...]
    pe, po = p[:h2], p[h2:tm]
    n_halo = xp_ref.shape[0]
    before = jnp.where(i == 0, 0.0, p[tm + n_halo - 1:tm + n_halo])
    after = jnp.where(i == pl.num_programs(1) - 1, 0.0, p[tm + n_halo:tm + n_halo + 1])
    row = lax.broadcasted_iota(jnp.int32, pe.shape, 0)
    po_prev = jnp.where(row == 0, before, pltpu.roll(po, 1, axis=0))
    pe_next = jnp.where(row == h2 - 1, after, pltpu.roll(pe, h2 - 1, axis=0))
    w0, w1, w2, bs = ws_ref[0:1, :], ws_ref[1:2, :], ws_ref[2:3, :], bs_ref[...]
    conv = (po_prev * w0 + pe * w1 + po * w2 + bs, pe * w0 + po * w1 + pe_next * w2 + bs)
    for par in range(2):
        v = conv[par][:, :d]
        v_ref[par] = v
        vb_ref[par] = v.astype(BF16)
        x1_ref[par] = conv[par][:, d:2 * d]
        x2_ref[par] = conv[par][:, 2 * d:]


def _hy_filter_kernel(feat_ref, wp_ref, bp_ref, wm_ref, bm_ref, fr_ref, wf_ref, dl_ref,
                      hf_ref, hb_ref, ss_ref, *, d, n_orders):
    start = (pl.program_id(0) == 0) & (pl.program_id(1) == 0)
    feat = feat_ref[...]
    fr = fr_ref[...]
    hdn = jnp.sin(fr * (_mm_f32(feat, wp_ref[...]) + bp_ref[...]))
    hdn = jnp.sin(fr * (_mm_f32(hdn, wm_ref[...]) + bm_ref[...]))
    h = _mm(hdn.astype(BF16), wf_ref[...])
    decay = jnp.exp(-feat[:, 0:1] * dl_ref[...])
    row = lax.broadcasted_iota(jnp.int32, decay.shape, 0)
    first = (row == 0) & start

    @pl.when(start)
    def _():
        ss_ref[...] = jnp.zeros_like(ss_ref)

    for o in range(n_orders):
        sl = slice(o * d, (o + 1) * d)
        hf = h[:, 2 * o * d:(2 * o + 1) * d] * decay
        hb = jnp.where(first, 0.0, h[:, (2 * o + 1) * d:(2 * o + 2) * d] * decay)
        hf_ref[:, sl] = hf.astype(BF16)
        hb_ref[:, sl] = hb.astype(BF16)
        ss_ref[:, sl] += jnp.sum(hf * hf + hb * hb, axis=0, keepdims=True)


def _butterfly(e_re, e_s, o_re, o_s, c, s):
    t_re = c * o_re - s * o_s
    t_s = c * o_s + s * o_re
    return e_re + t_re, e_s + t_s, e_re - t_re, e_s - t_s


def _hy_spec_kernel(f_ref, fe_ref, fo_ref, be_ref, bo_ref, c_ref, s_ref, ss_ref, o_ref, *, tp, scale):
    wgt = scale * lax.rsqrt(ss_ref[...] + NORM_EPS)
    for g in range(c_ref.shape[0] // tp):
        re, sn = slice(2 * g * tp, (2 * g + 1) * tp), slice((2 * g + 1) * tp, (2 * g + 2) * tp)
        dft = f_ref[2 * g * tp:(2 * g + 2) * tp, :]
        c = c_ref[g * tp:(g + 1) * tp, :]
        s = s_ref[g * tp:(g + 1) * tp, :]

        def blocks(even_ref, odd_ref):
            e = _mm(dft, even_ref[...])
            o = _mm(dft, odd_ref[...])
            return _butterfly(e[0:tp], e[tp:], o[0:tp], o[tp:], c, s)

        fa_re, fa_s, fb_re, fb_s = blocks(fe_ref, fo_ref)
        ba_re, ba_s, bb_re, bb_s = blocks(be_ref, bo_ref)
        o_ref[0, re, :] = (fa_re + ba_re) * wgt
        o_ref[0, sn, :] = (fa_s - ba_s) * wgt
        o_ref[1, re, :] = (fb_re + bb_re) * wgt
        o_ref[1, sn, :] = (fb_s - bb_s) * wgt


def _hy_fwd_kernel(f_ref, ue_ref, uo_ref, h_ref, c_ref, s_ref, y_ref, *, tp):
    for g in range(c_ref.shape[0] // tp):
        re, sn = slice(2 * g * tp, (2 * g + 1) * tp), slice((2 * g + 1) * tp, (2 * g + 2) * tp)
        dft = f_ref[2 * g * tp:(2 * g + 2) * tp, :]
        c = c_ref[g * tp:(g + 1) * tp, :]
        s = s_ref[g * tp:(g + 1) * tp, :]
        e = _mm(dft, ue_ref[...])
        o = _mm(dft, uo_ref[...])
        a_re, a_s, b_re, b_s = _butterfly(e[0:tp], e[tp:], o[0:tp], o[tp:], c, s)
        ha_re, ha_s = h_ref[0, re, :], h_ref[0, sn, :]
        hb_re, hb_s = h_ref[1, re, :], h_ref[1, sn, :]
        ya_re = a_re * ha_re - a_s * ha_s
        ya_s = a_re * ha_s + a_s * ha_re
        yb_re = b_re * hb_re - b_s * hb_s
        yb_s = b_re * hb_s + b_s * hb_re
        y_ref[0, re, :] = (ya_re + yb_re).astype(BF16)
        y_ref[0, sn, :] = (ya_s + yb_s).astype(BF16)
        d_re = ya_re - yb_re
        d_s = ya_s - yb_s
        y_ref[1, re, :] = (c * d_re + s * d_s).astype(BF16)
        y_ref[1, sn, :] = (c * d_s - s * d_re).astype(BF16)


def _hy_inv_kernel(f_ref, y_ref, a_ref, b_ref, fb_ref, *o_refs, rows):
    for r in range(f_ref.shape[0] // rows):
        sl = slice(r * rows, (r + 1) * rows)
        conv = _mm(f_ref[sl, :], y_ref[...])
        z = a_ref[sl, :] * (conv + b_ref[sl, :] * fb_ref[...])
        for o_ref in o_refs:
            o_ref[sl, :] = z.astype(o_ref.dtype)


def _hy_out_kernel(x_ref, mod_ref, z_ref, pe_ref, po_ref, w_ref, b_ref, o_ref):
    z = (_mm(pe_ref[...], z_ref[0]) + _mm(po_ref[...], z_ref[1])).astype(BF16)
    o_ref[...] = x_ref[...] + mod_ref[5:6, :] * (_mm(z, w_ref[...]) + b_ref[...])


def _hyena_layer(x, mods, layer, row_fn, g, w_in, b_in, w_short, b_short, w_pos, b_pos, w_mid, b_mid,
                 freq, w_filt, fbias, w_out, b_out):
    bsz, length, d = x.shape
    n_full = 2 * length
    tm = min(512, length)
    tc = DFT_COL_TILE
    n_ct = d // tc
    width = w_mid.shape[0]
    n_parts = 2 * HYENA_ORDER

    half = length // 2
    halo = 8
    perm = np.zeros((tm, tm), np.float32)
    perm[np.arange(tm // 2), 2 * np.arange(tm // 2)] = 1.0
    perm[tm // 2 + np.arange(tm // 2), 2 * np.arange(tm // 2) + 1] = 1.0
    par_spec = pl.BlockSpec((None, 2, tm // 2, d), lambda b, i: (b, 0, i, 0))
    vb, v, x1, x2 = pl.pallas_call(
        _hy_proj_kernel,
        grid=(bsz, length // tm),
        in_specs=[
            pl.BlockSpec((None, tm, d), lambda b, i: (b, i, 0)),
            pl.BlockSpec((None, halo, d), lambda b, i: (b, jnp.maximum(i * (tm // halo) - 1, 0), 0)),
            pl.BlockSpec((None, halo, d),
                         lambda b, i: (b, jnp.minimum((i + 1) * (tm // halo), length // halo - 1), 0)),
            _mod_spec(d, layer, row_fn),
            _const_spec((1, d)),
            _const_spec((tm, tm)),
            _const_spec((d, 3 * d)),
            _const_spec((1, 3 * d)),
            _const_spec((3, 3 * d)),
            _const_spec((1, 3 * d)),
        ],
        out_specs=[par_spec] * 4,
        out_shape=[jax.ShapeDtypeStruct((bsz, 2, half, d), BF16)]
        + [jax.ShapeDtypeStruct((bsz, 2, half, d), F32)] * 3,
        compiler_params=_params(2),
        name="hy_proj",
    )(x, x, x, mods, g.reshape(1, d), jnp.asarray(perm, BF16), w_in, b_in.reshape(1, 3 * d),
      w_short, b_short.reshape(1, 3 * d))

    t = jnp.linspace(0.0, 1.0, length, dtype=F32)[:, None]
    bands = jnp.linspace(1e-4, HYENA_BANDS - 1, HYENA_BANDS, dtype=F32)
    ang = (2.0 * math.pi / length) * jnp.arange(length, dtype=F32)[:, None] * bands[None, :]
    feat = jnp.concatenate([t, jnp.cos(ang), -jnp.sin(ang)], axis=-1)
    feat = jnp.pad(feat, ((0, 0), (0, V7X_LANES - HYENA_EMB)))
    w_pos_p = jnp.pad(w_pos.astype(F32), ((0, V7X_LANES - HYENA_EMB), (0, 0)))
    deltas = jnp.abs(jnp.linspace(HYENA_MIN_DECAY, HYENA_MAX_DECAY, d, dtype=F32)).reshape(1, d)
    feat = jnp.concatenate([feat[0::2], feat[1::2]], axis=0)
    tl = min(512, half)
    od = HYENA_ORDER * d
    tap_spec = pl.BlockSpec((tl, od), lambda par, i: (i, par))
    taps_f, taps_b, ss = pl.pallas_call(
        functools.partial(_hy_filter_kernel, d=d, n_orders=HYENA_ORDER),
        grid=(2, half // tl),
        in_specs=[
            pl.BlockSpec((tl, V7X_LANES), lambda par, i: (par * (half // tl) + i, 0)),
            _const_spec((V7X_LANES, width)),
            _const_spec((1, width)),
            _const_spec((width, width)),
            _const_spec((1, width)),
            _const_spec((1, width)),
            _const_spec((width, n_parts * d)),
            _const_spec((1, d)),
        ],
        out_specs=[tap_spec, tap_spec, pl.BlockSpec((1, od), lambda par, i: (0, 0))],
        out_shape=[jax.ShapeDtypeStruct((half, 2 * od), BF16), jax.ShapeDtypeStruct((half, 2 * od), BF16),
                   jax.ShapeDtypeStruct((1, od), F32)],
        compiler_params=_params(2),
        name="hy_filter",
    )(feat, w_pos_p, b_pos.reshape(1, width), w_mid, b_mid.reshape(1, width), freq.reshape(1, width),
      w_filt.astype(BF16), deltas)

    tf = min(DFT_FREQ_TILE, half)
    tp = min(DFT_PACK, half)
    tn = min(DFT_TIME_TILE, half)
    dft_fwd, dft_inv = _dft_matrices(half, tp)
    tw = (math.pi / n_full) * (2.0 * jnp.arange(half, dtype=F32) + 1.0)
    tw_c = jnp.broadcast_to(jnp.cos(tw)[:, None], (half, tc))
    tw_s = jnp.broadcast_to(jnp.sin(tw)[:, None], (half, tc))

    n_oc = od // tc
    even_spec = pl.BlockSpec((half, tc), lambda f, o, j: (0, o * n_ct + j))
    odd_spec = pl.BlockSpec((half, tc), lambda f, o, j: (0, n_oc + o * n_ct + j))
    tw_spec = pl.BlockSpec((tf, tc), lambda f, *_: (f, 0))
    spec = pl.pallas_call(
        functools.partial(_hy_spec_kernel, tp=tp, scale=1.0 / length),
        grid=(half // tf, HYENA_ORDER, n_ct),
        in_specs=[pl.BlockSpec((2 * tf, half), lambda f, o, j: (f, 0)),
                  even_spec, odd_spec, even_spec, odd_spec, tw_spec, tw_spec,
                  pl.BlockSpec((1, tc), lambda f, o, j: (0, o * n_ct + j))],
        out_specs=pl.BlockSpec((None, 2, 2 * tf, tc), lambda f, o, j: (o, 0, f, j)),
        out_shape=jax.ShapeDtypeStruct((HYENA_ORDER, 2, length, d), F32),
        compiler_params=_params(3),
        name="hy_spec",
    )(dft_fwd, taps_f, taps_f, taps_b, taps_b, tw_c, tw_s, ss)

    def long_conv(order, u2, a, bsrc, want_f32):
        yspec = pl.pallas_call(
            functools.partial(_hy_fwd_kernel, tp=tp),
            grid=(half // tf, n_ct, bsz),
            in_specs=[
                pl.BlockSpec((2 * tf, half), lambda f, j, b: (f, 0)),
                pl.BlockSpec((None, None, half, tc), lambda f, j, b: (b, 0, 0, j)),
                pl.BlockSpec((None, None, half, tc), lambda f, j, b: (b, 1, 0, j)),
                pl.BlockSpec((None, 2, 2 * tf, tc), lambda f, j, b: (order, 0, f, j)),
                tw_spec, tw_spec,
            ],
            out_specs=pl.BlockSpec((None, 2, 2 * tf, tc), lambda f, j, b: (b, 0, f, j)),
            out_shape=jax.ShapeDtypeStruct((bsz, 2, length, d), BF16),
            compiler_params=_params(3),
            name="hy_fwd",
        )(dft_fwd, u2, u2, spec, tw_c, tw_s)
        row_spec = pl.BlockSpec((None, None, tn, tc), lambda n, j, b, p: (b, p, n, j))
        out_shape = [jax.ShapeDtypeStruct((bsz, 2, half, d), BF16)]
        if want_f32:
            out_shape.append(jax.ShapeDtypeStruct((bsz, 2, half, d), F32))
        outs = pl.pallas_call(
            functools.partial(_hy_inv_kernel, rows=min(DFT_ROW_GROUP, tn)),
            grid=(half // tn, n_ct, bsz, 2),
            in_specs=[
                pl.BlockSpec((tn, length), lambda n, j, b, p: (n, 0)),
                pl.BlockSpec((None, None, length, tc), lambda n, j, b, p: (b, p, 0, j)),
                row_spec,
                row_spec,
                pl.BlockSpec((None, 1, tc), lambda n, j, b, p: (order, 0, j)),
            ],
            out_specs=[row_spec] * len(out_shape),
            out_shape=out_shape,
            compiler_params=_params(4),
            name="hy_inv",
        )(dft_inv, yspec, a, bsrc, fbias.astype(F32).reshape(HYENA_ORDER, 1, d))
        return outs

    zb, z = long_conv(0, vb, x1, v, True)
    (zzb,) = long_conv(1, zb, x2, z, False)

    pick_even = np.zeros((tm, tm // 2), np.float32)
    pick_even[2 * np.arange(tm // 2), np.arange(tm // 2)] = 1.0
    pick_odd = np.zeros((tm, tm // 2), np.float32)
    pick_odd[2 * np.arange(tm // 2) + 1, np.arange(tm // 2)] = 1.0
    return pl.pallas_call(
        _hy_out_kernel,
        grid=(bsz, length // tm),
        in_specs=[
            pl.BlockSpec((None, tm, d), lambda b, i: (b, i, 0)),
            _mod_spec(d, layer, row_fn),
            pl.BlockSpec((None, 2, tm // 2, d), lambda b, i: (b, 0, i, 0)),
            _const_spec((tm, tm // 2)),
            _const_spec((tm, tm // 2)),
            _const_spec((d, d)),
            _const_spec((1, d)),
        ],
        out_specs=pl.BlockSpec((None, tm, d), lambda b, i: (b, i, 0)),
        out_shape=jax.ShapeDtypeStruct(x.shape, F32),
        compiler_params=_params(2),
        name="hy_out",
    )(x, mods, zzb, jnp.asarray(pick_even, BF16), jnp.asarray(pick_odd, BF16), w_out, b_out.reshape(1, d))


def kernel(x, c, ctx, c_ctx, ada_w, ada_b, norm_g, ffn_w1, ffn_w3, ffn_w2, ret_w_in, ret_w_out, ret_decay,
           pool_w, pool_b, pool_scale, hy_w_in, hy_b_in, hy_w_short, hy_b_short, hy_w_pos, hy_b_pos,
           hy_w_mid, hy_b_mid, hy_freq, hy_w_filt, hy_bias, hy_w_out, hy_b_out, final_g):
    bsz, seq, d = x.shape
    t_ctx = ctx.shape[1]
    depth = ada_w.shape[0]
    n_mixers = 3
    assert bsz + 1 <= MOD_ROWS
    ctx_row = bsz

    cond = jnp.concatenate([c, c_ctx[None, :], jnp.zeros((MOD_ROWS - bsz - 1, d), F32)], axis=0)
    mods = _ada_all(cond, ada_w, ada_b).reshape(depth, MOD_ROWS, ADA_CHUNKS, d)

    w1b, w3b, w2b = ffn_w1.astype(BF16), ffn_w3.astype(BF16), ffn_w2.astype(BF16)
    lat_row = lambda b, i: b
    ctx_flat_row = lambda b, i: ctx_row

    def ffn(h_lat, h_ctx, layer, half, final=False):
        return _half_ffn(h_lat, h_ctx, mods, layer, half, norm_g[layer, 2 * half], w1b, w3b, w2b, final_g,
                         final=final)

    h_lat, h_ctx = x, ctx
    for layer in range(depth):
        kind = layer % n_mixers
        slot = layer // n_mixers
        last = layer == depth - 1
        ctx_out = not last
        ctx_live = ctx_out or kind == 0
        h_lat, h_ctx = ffn(h_lat, h_ctx if ctx_live else None, layer, 0)
        g_mix = norm_g[layer, 1]
        if kind == 0:
            h_ctx, h_lat = _retention_layer(h_ctx, h_lat, mods, layer, g_mix, ret_w_in[slot].astype(BF16),
                                            ret_w_out[slot].astype(BF16), ret_decay[slot], ctx_out)
        elif kind == 1:
            pp = (pool_w[slot], pool_b[slot], pool_scale[slot])
            h_lat = _pool_layer(h_lat, GRID_W, mods, layer, lat_row, g_mix, *pp)
            if ctx_out:
                h_ctx = _pool_layer(h_ctx, t_ctx, mods, layer, ctx_flat_row, g_mix, *pp)
        else:
            hp = (hy_w_in[slot].astype(BF16), hy_b_in[slot], hy_w_short[slot], hy_b_short[slot], hy_w_pos[slot],
                  hy_b_pos[slot], hy_w_mid[slot], hy_b_mid[slot], hy_freq[slot], hy_w_filt[slot],
                  hy_bias[slot], hy_w_out[slot].astype(BF16), hy_b_out[slot])
            h_lat = _hyena_layer(h_lat, mods, layer, lat_row, g_mix, *hp)
            if ctx_out:
                h_ctx = _hyena_layer(h_ctx, mods, layer, ctx_flat_row, g_mix, *hp)
        h_lat, h_ctx = ffn(h_lat, h_ctx if ctx_out else None, layer, 1, final=last)
    return h_lat
```

```python
import functools
import math

import numpy as np
import jax
import jax.numpy as jnp
from jax import lax
from jax.experimental import pallas as pl
from jax.experimental.pallas import tpu as pltpu

F32 = jnp.float32
BF16 = jnp.bfloat16

GRID_W = 64
ADA_CHUNKS = 9
NORM_EPS = 1e-6
RET_HEADS = 4
RET_CHUNK = 128
ROPE_BASE = 10000.0
POOL_WINDOWS = (2, 4, 8, 16)
HYENA_ORDER = 2
HYENA_EMB = 33
HYENA_BANDS = (HYENA_EMB - 1) // 2
HYENA_TARGET = 1e-2
HYENA_FAST = 0.3
HYENA_SLOW = 1.5
HYENA_MAX_DECAY = math.log(HYENA_TARGET) / HYENA_FAST
HYENA_MIN_DECAY = math.log(HYENA_TARGET) / HYENA_SLOW

V7X_LANES = 128
V7X_VMEM_LIMIT_BYTES = 56 * 1024 * 1024
MOD_ROWS = 8
POOL_TILE = 256
RET_TILE = 256
SCAN_CHUNK = 256
DFT_RADIX = 4
DFT_FREQ_TILE = 512
DFT_PACK = 256
DFT_ROW_GROUP = 512
DFT_TIME_TILE = 2048
DFT_COL_TILE = 256
HIGHEST = lax.Precision.HIGHEST


def _params(n_axes):
    return pltpu.CompilerParams(
        dimension_semantics=("arbitrary",) * n_axes,
        vmem_limit_bytes=V7X_VMEM_LIMIT_BYTES)


def _const_spec(shape):
    zeros = (0,) * len(shape)
    return pl.BlockSpec(shape, lambda *_: zeros, pipeline_mode=pl.Buffered(1))


def _mod_spec(d, layer, row_fn):
    return pl.BlockSpec((None, None, ADA_CHUNKS, d), lambda *idx: (layer, row_fn(*idx), 0, 0))


def _mm(a, b):
    return jnp.dot(a, b, preferred_element_type=F32)


def _mm_f32(a, b):
    return jnp.dot(a, b, preferred_element_type=F32, precision=HIGHEST)


def _silu(x):
    return x * jax.nn.sigmoid(x)


def _adaln(x, g, shift, scale):
    ms = jnp.mean(x * x, axis=-1, keepdims=True)
    y = x * lax.rsqrt(ms + NORM_EPS) * g
    return y * (1.0 + scale) + shift


def _ada_kernel(c_ref, w_ref, b_ref, o_ref):
    s = _silu(c_ref[...]).astype(BF16)
    o_ref[...] = _mm(s, w_ref[...].astype(BF16)) + b_ref[...]


def _ada_all(cond, ada_w, ada_b):
    depth, d, nd = ada_w.shape
    return pl.pallas_call(
        _ada_kernel,
        grid=(depth, nd // d),
        in_specs=[
            pl.BlockSpec((MOD_ROWS, d), lambda l, j: (0, 0)),
            pl.BlockSpec((None, d, d), lambda l, j: (l, 0, j)),
            pl.BlockSpec((None, 1, d), lambda l, j: (l, 0, j)),
        ],
        out_specs=pl.BlockSpec((None, MOD_ROWS, d), lambda l, j: (l, 0, j)),
        out_shape=jax.ShapeDtypeStruct((depth, MOD_ROWS, nd), F32),
        compiler_params=_params(2),
        name="ada",
    )(cond, ada_w, ada_b.reshape(depth, 1, nd))


def _ffn_kernel(xl_ref, xc_ref, mod_ref, g_ref, w1_ref, w3_ref, w2_ref, fg_ref, *o_refs, base, final, n_lat):
    is_ctx = pl.program_id(0) >= n_lat
    x = jnp.where(is_ctx, xc_ref[...], xl_ref[...]) if len(o_refs) == 2 else xl_ref[...]
    u = _adaln(x, g_ref[...], mod_ref[base:base + 1, :], mod_ref[base + 1:base + 2, :]).astype(BF16)
    h1 = _mm(u, w1_ref[...])
    h3 = _mm(u, w3_ref[...])
    a = (_silu(h1) * h3).astype(BF16)
    y = x + (0.5 * mod_ref[base + 2:base + 3, :]) * _mm(a, w2_ref[...])
    if final:
        ms = jnp.mean(y * y, axis=-1, keepdims=True)
        y = y * lax.rsqrt(ms + NORM_EPS) * fg_ref[...]
    if len(o_refs) == 2:
        @pl.when(jnp.logical_not(is_ctx))
        def _():
            o_refs[0][...] = y

        @pl.when(is_ctx)
        def _():
            o_refs[1][...] = y
    else:
        o_refs[0][...] = y


def _half_ffn(h_lat, h_ctx, mods, layer, half, g, w1, w3, w2, final_g, final=False, tm=512):
    bsz, t, d = h_lat.shape
    f = w1.shape[-1]
    tm = min(tm, t)
    base = 6 * half
    per_b = t // tm
    n_lat = bsz * per_b
    with_ctx = h_ctx is not None
    ctx_flat = h_ctx.reshape(1, -1, d) if with_ctx else h_lat[:1, :tm]
    n_ctx = ctx_flat.shape[1] // tm if with_ctx else 0
    assert t % tm == 0 and ctx_flat.shape[1] % tm == 0

    def lat_idx(s):
        s = jnp.minimum(s, n_lat - 1)
        return (s // per_b, s % per_b, 0)

    def ctx_idx(s):
        return (0, jnp.maximum(s - n_lat, 0), 0)

    def weight_spec(rows, cols):
        return pl.BlockSpec((None, None, rows, cols), lambda *_: (layer, half, 0, 0),
                            pipeline_mode=pl.Buffered(1))

    out_specs = [pl.BlockSpec((None, tm, d), lat_idx)]
    out_shape = [jax.ShapeDtypeStruct(h_lat.shape, F32)]
    if with_ctx:
        out_specs.append(pl.BlockSpec((None, tm, d), ctx_idx))
        out_shape.append(jax.ShapeDtypeStruct(ctx_flat.shape, F32))
    outs = pl.pallas_call(
        functools.partial(_ffn_kernel, base=base, final=final, n_lat=n_lat),
        grid=(n_lat + n_ctx,),
        in_specs=[
            pl.BlockSpec((None, tm, d), lat_idx),
            pl.BlockSpec((None, tm, d), ctx_idx),
            _mod_spec(d, layer, lambda s: jnp.where(s < n_lat, s // per_b, bsz)),
            _const_spec((1, d)),
            weight_spec(d, f),
            weight_spec(d, f),
            weight_spec(f, d),
            _const_spec((1, d)),
        ],
        out_specs=out_specs,
        out_shape=out_shape,
        compiler_params=_params(1),
        name="half_ffn",
    )(h_lat, ctx_flat, mods, g.reshape(1, d), w1, w3, w2, final_g.reshape(1, d))
    return outs[0], (outs[1].reshape(h_ctx.shape) if with_ctx else None)


def _ret_proj_kernel(xc_ref, xl_ref, mod_ref, g_ref, w_ref, cos_ref, sin_ref, q_ref, k_ref, v_ref, gate_ref,
                     *, n_heads, dk, dv, ctx_tiles):
    x = jnp.where(pl.program_id(1) < ctx_tiles, xc_ref[...], xl_ref[...])
    u = _adaln(x, g_ref[...], mod_ref[3:4, :], mod_ref[4:5, :]).astype(BF16)
    p = _mm(u, w_ref[...])
    cos = cos_ref[...]
    sin = sin_ref[...]
    half = dk // 2
    qk = n_heads * dk
    k_scale = dk ** -0.5

    def rot(a):
        a1, a2 = a[:, :half], a[:, half:]
        return jnp.concatenate([a1 * cos - a2 * sin, a2 * cos + a1 * sin], axis=-1)

    for h in range(n_heads):
        q_ref[:, h * dk:(h + 1) * dk] = rot(p[:, h * dk:(h + 1) * dk]).astype(BF16)
        k_ref[:, h * dk:(h + 1) * dk] = (rot(p[:, qk + h * dk:qk + (h + 1) * dk]) * k_scale).astype(BF16)
    v_ref[...] = p[:, 2 * qk:2 * qk + n_heads * dv].astype(BF16)
    gate_ref[...] = p[:, 2 * qk + n_heads * dv:].astype(BF16)


def _ret_scan_kernel(q_ref, k_ref, v_ref, dmat_ref, qdec_ref, kdec_ref, cdec_ref, o_ref,
                     state_ref, fwd_ref, *, n_heads, dk, dv, n_chunks, bwd_chunk):
    s = pl.program_id(1)
    cs = q_ref.shape[0]

    @pl.when((s == 0) | (s == n_chunks))
    def _():
        state_ref[...] = jnp.zeros_like(state_ref)

    chunk = jnp.where(s < n_chunks, s, bwd_chunk(jnp.maximum(s - n_chunks, 0)))
    rows = pl.ds(pl.multiple_of(chunk * cs, cs), cs)
    outs = []
    for h in range(n_heads):
        q = q_ref[:, h * dk:(h + 1) * dk]
        k = k_ref[:, h * dk:(h + 1) * dk]
        v = v_ref[:, h * dv:(h + 1) * dv]
        scores = lax.dot_general(q, k, (((1,), (1,)), ((), ())), preferred_element_type=F32) * dmat_ref[h]
        state = state_ref[h]
        outs.append(_mm(scores.astype(BF16), v) + _mm(q, state.astype(BF16)) * qdec_ref[h])
        kd = (k.astype(F32) * kdec_ref[h]).astype(BF16)
        state_ref[h] = state * cdec_ref[h] + lax.dot_general(
            kd, v, (((0,), (0,)), ((), ())), preferred_element_type=F32)

    @pl.when(s < n_chunks)
    def _():
        for h in range(n_heads):
            fwd_ref[rows, h * dv:(h + 1) * dv] = outs[h].astype(fwd_ref.dtype)

    @pl.when(s >= n_chunks)
    def _():
        for h in range(n_heads):
            o = outs[h] + fwd_ref[rows, h * dv:(h + 1) * dv].astype(F32)
            mu = jnp.mean(o, axis=-1, keepdims=True)
            oc = o - mu
            var = jnp.mean(oc * oc, axis=-1, keepdims=True)
            o_ref[:, h * dv:(h + 1) * dv] = (oc * lax.rsqrt(var + NORM_EPS)).astype(o_ref.dtype)


def _ret_out_kernel(x_ref, mod_ref, y_ref, gate_ref, w_ref, o_ref):
    z = (_silu(gate_ref[...].astype(F32)) * y_ref[...].astype(F32)).astype(BF16)
    o_ref[...] = x_ref[...] + mod_ref[5:6, :] * _mm(z, w_ref[...])


def _retention_layer(h_ctx, h_lat, mods, layer, g, w_in, w_out, decay, ctx_out):
    bsz, t_ctx, d = h_ctx.shape
    t_lat = h_lat.shape[1]
    t_all = t_ctx + t_lat
    n_heads = RET_HEADS
    dk = d // n_heads
    dv = 2 * d // n_heads
    qk = n_heads * dk
    vw = n_heads * dv
    tm = RET_TILE
    assert t_ctx % tm == 0 and t_lat % tm == 0
    ctx_tiles = t_ctx // tm
    ctx_row = bsz

    pos = jnp.arange(t_all, dtype=F32)
    inv = 1.0 / (ROPE_BASE ** jnp.linspace(0.0, 1.0, dk // 2, dtype=F32))
    ang = pos[:, None] * inv[None, :]

    q, k, v, gate = pl.pallas_call(
        functools.partial(_ret_proj_kernel, n_heads=n_heads, dk=dk, dv=dv, ctx_tiles=ctx_tiles),
        grid=(bsz, t_all // tm),
        in_specs=[
            pl.BlockSpec((None, tm, d), lambda b, i: (b, jnp.minimum(i, ctx_tiles - 1), 0)),
            pl.BlockSpec((None, tm, d), lambda b, i: (b, jnp.maximum(i - ctx_tiles, 0), 0)),
            _mod_spec(d, layer, lambda b, i: jnp.where(i < ctx_tiles, ctx_row, b)),
            _const_spec((1, d)),
            _const_spec((d, 2 * qk + 2 * vw)),
            pl.BlockSpec((tm, dk // 2), lambda b, i: (i, 0)),
            pl.BlockSpec((tm, dk // 2), lambda b, i: (i, 0)),
        ],
        out_specs=[
            pl.BlockSpec((None, tm, qk), lambda b, i: (b, i, 0)),
            pl.BlockSpec((None, tm, qk), lambda b, i: (b, i, 0)),
            pl.BlockSpec((None, tm, vw), lambda b, i: (b, i, 0)),
            pl.BlockSpec((None, tm, vw), lambda b, i: (b, i, 0)),
        ],
        out_shape=[
            jax.ShapeDtypeStruct((bsz, t_all, qk), BF16),
            jax.ShapeDtypeStruct((bsz, t_all, qk), BF16),
            jax.ShapeDtypeStruct((bsz, t_all, vw), BF16),
            jax.ShapeDtypeStruct((bsz, t_all, vw), BF16),
        ],
        compiler_params=_params(2),
        name="ret_proj",
    )(h_ctx, h_lat, mods, g.reshape(1, d), w_in, jnp.cos(ang), jnp.sin(ang))

    cs = SCAN_CHUNK
    assert t_ctx % cs == 0 and t_lat % cs == 0
    n_chunks = t_all // cs
    ctx_chunks = t_ctx // cs
    lg = jax.nn.log_sigmoid(decay.astype(F32))[:, :, None]
    n = jnp.arange(cs, dtype=F32)
    rel = n[:, None] - n[None, :]
    dm_f = jnp.where(rel >= 0, jnp.exp(lg[0][:, :, None] * jnp.maximum(rel, 0.0)), 0.0)
    dm_b = jnp.where(rel < 0, jnp.exp(lg[1][:, :, None] * jnp.maximum(-rel, 0.0)), 0.0)
    dmat = jnp.stack([dm_f, dm_b])
    qd = jnp.stack([jnp.exp(lg[0] * (n + 1.0)), jnp.exp(lg[1] * (cs - n))])
    kd = jnp.stack([jnp.exp(lg[0] * (cs - 1.0 - n)), jnp.exp(lg[1] * n)])
    cd = jnp.exp(lg * cs)
    qdec = jnp.broadcast_to(qd[..., None], (2, n_heads, cs, dv))
    kdec = jnp.broadcast_to(kd[..., None], (2, n_heads, cs, dk))
    cdec = jnp.broadcast_to(cd[..., None], (2, n_heads, 1, dv))

    def bwd_chunk(j):
        return jnp.where(j < ctx_chunks, ctx_chunks - 1 - j, n_chunks - 1 + ctx_chunks - j)

    def chunk(s):
        return jnp.where(s < n_chunks, s, bwd_chunk(jnp.maximum(s - n_chunks, 0)))

    def out_chunk(s):
        return bwd_chunk(jnp.maximum(s - n_chunks, 0))

    def table_spec(*shape):
        return pl.BlockSpec((None,) + shape, lambda b, s: (s // n_chunks, 0, 0, 0))

    y = pl.pallas_call(
        functools.partial(_ret_scan_kernel, n_heads=n_heads, dk=dk, dv=dv, n_chunks=n_chunks,
                          bwd_chunk=bwd_chunk),
        grid=(bsz, 2 * n_chunks),
        in_specs=[
            pl.BlockSpec((None, cs, qk), lambda b, s: (b, chunk(s), 0)),
            pl.BlockSpec((None, cs, qk), lambda b, s: (b, chunk(s), 0)),
            pl.BlockSpec((None, cs, vw), lambda b, s: (b, chunk(s), 0)),
            table_spec(n_heads, cs, cs),
            table_spec(n_heads, cs, dv),
            table_spec(n_heads, cs, dk),
            table_spec(n_heads, 1, dv),
        ],
        out_specs=pl.BlockSpec((None, cs, vw), lambda b, s: (b, out_chunk(s), 0)),
        out_shape=jax.ShapeDtypeStruct((bsz, t_all, vw), BF16),
        scratch_shapes=[pltpu.VMEM((n_heads, dk, dv), F32), pltpu.VMEM((t_all, vw), BF16)],
        compiler_params=_params(2),
        name="ret_scan",
    )(q, k, v, dmat, qdec, kdec, cdec)

    def readout(x, tile0, row_fn):
        return pl.pallas_call(
            _ret_out_kernel,
            grid=(bsz, x.shape[1] // tm),
            in_specs=[
                pl.BlockSpec((None, tm, d), lambda b, i: (b, i, 0)),
                _mod_spec(d, layer, row_fn),
                pl.BlockSpec((None, tm, vw), lambda b, i: (b, i + tile0, 0)),
                pl.BlockSpec((None, tm, vw), lambda b, i: (b, i + tile0, 0)),
                _const_spec((vw, d)),
            ],
            out_specs=pl.BlockSpec((None, tm, d), lambda b, i: (b, i, 0)),
            out_shape=jax.ShapeDtypeStruct(x.shape, F32),
            compiler_params=_params(2),
            name="ret_out",
        )(x, mods, y, gate, w_out)

    new_lat = readout(h_lat, ctx_tiles, lambda b, i: b)
    new_ctx = readout(h_ctx, 0, lambda b, i: ctx_row) if ctx_out else None
    return new_ctx, new_lat


def _pool_tables(seg_len):
    assert POOL_TILE % seg_len == 0
    r = np.arange(POOL_TILE)
    seg, pos = r // seg_len, r % seg_len
    mats, invs = [], []
    for win in POOL_WINDOWS:
        lo = np.clip(pos - win // 2, 0, seg_len)
        hi = np.clip(pos - win // 2 + win, 0, seg_len)
        col_seg, col_pos = seg[None, :], pos[None, :]
        member = (col_seg == seg[:, None]) & (col_pos >= lo[:, None]) & (col_pos < hi[:, None])
        mats.append(member.astype(np.float32))
        invs.append((1.0 / (hi - lo)).astype(np.float32))
    return np.stack(mats), np.stack(invs)


def _pool_kernel(x_ref, mod_ref, g_ref, s_ref, inv_ref, w_ref, b_ref, sc_ref, o_ref, *, n_groups, dg):
    x = x_ref[...]
    u = _adaln(x, g_ref[...], mod_ref[3:4, :], mod_ref[4:5, :])
    gate = mod_ref[5:6, :]
    for gi in range(n_groups):
        sl = slice(gi * dg, (gi + 1) * dg)
        ug = u[:, sl]
        hi = ug.astype(BF16)
        lo = (ug - hi.astype(F32)).astype(BF16)
        win_sum = _mm(s_ref[gi], hi) + _mm(s_ref[gi], lo)
        dlt = win_sum * inv_ref[gi] - ug
        yg = _mm(dlt.astype(BF16), w_ref[gi])
        o_ref[:, sl] = x[:, sl] + gate[:, sl] * ((yg + b_ref[:, sl]) * sc_ref[:, sl])


def _pool_layer(x, seg_len, mods, layer, row_fn, g, w_grp, b_grp, scale):
    bsz, t, d = x.shape
    n_groups = len(POOL_WINDOWS)
    dg = d // n_groups
    mats, invs = _pool_tables(seg_len)
    s_tab = jnp.asarray(mats, BF16)
    inv_tab = jnp.asarray(np.broadcast_to(invs[:, :, None], (n_groups, POOL_TILE, dg)).copy(), F32)
    return pl.pallas_call(
        functools.partial(_pool_kernel, n_groups=n_groups, dg=dg),
        grid=(bsz, t // POOL_TILE),
        in_specs=[
            pl.BlockSpec((None, POOL_TILE, d), lambda b, i: (b, i, 0)),
            _mod_spec(d, layer, row_fn),
            _const_spec((1, d)),
            _const_spec((n_groups, POOL_TILE, POOL_TILE)),
            _const_spec((n_groups, POOL_TILE, dg)),
            _const_spec((n_groups, dg, dg)),
            _const_spec((1, d)),
            _const_spec((1, d)),
        ],
        out_specs=pl.BlockSpec((None, POOL_TILE, d), lambda b, i: (b, i, 0)),
        out_shape=jax.ShapeDtypeStruct(x.shape, F32),
        compiler_params=_params(2),
        name="pool",
    )(x, mods, g.reshape(1, d), s_tab, inv_tab, w_grp.astype(BF16), b_grp.reshape(1, d), scale.reshape(1, d))


def _dft_tables(n_in):
    mod = 4 * n_in
    sub_n = min(V7X_LANES, n_in)
    idx = jnp.arange(n_in, dtype=jnp.int32)[None, :]
    blk = jnp.arange(n_in // sub_n, dtype=jnp.int32)[:, None]
    sub = jnp.arange(sub_n, dtype=jnp.int32)[:, None]
    theta = 2.0 * math.pi / mod

    def cs(p):
        ph = (p % mod).astype(F32) * theta
        return jnp.cos(ph), jnp.sin(ph)

    fwd = cs(2 * sub_n * blk * idx) + cs((2 * sub + 1) * idx)
    inv = cs(sub_n * blk * (2 * idx + 1)) + cs(sub * (2 * idx + 1))
    return fwd, inv


def _dft_fwd_gen_kernel(pc_ref, ps_ref, qc_ref, qs_ref, o_ref, *, tf):
    ft = pl.program_id(0)
    sub_n = qc_ref.shape[0]
    blocks = tf // sub_n
    qc = qc_ref[...]
    qs = qs_ref[...]
    for jb in range(blocks):
        kb = ft * blocks + jb
        pc = pc_ref[pl.ds(kb, 1), :]
        ps = ps_ref[pl.ds(kb, 1), :]
        o_ref[jb * sub_n:(jb + 1) * sub_n, :] = (pc * qc - ps * qs).astype(BF16)
        o_ref[tf + jb * sub_n:tf + (jb + 1) * sub_n, :] = (ps * qc + pc * qs).astype(BF16)


def _dft_inv_gen_kernel(pc_ref, ps_ref, qc_ref, qs_ref, o_ref, *, tf, n_in):
    mb = pl.program_id(0)
    pc = pc_ref[pl.ds(mb, 1), :]
    ps = ps_ref[pl.ds(mb, 1), :]
    qc = qc_ref[...]
    qs = qs_ref[...]
    cos_all = pc * qc - ps * qs
    sin_all = ps * qc + pc * qs
    for ft in range(n_in // tf):
        o_ref[:, 2 * ft * tf:(2 * ft + 1) * tf] = cos_all[:, ft * tf:(ft + 1) * tf].astype(BF16)
        o_ref[:, (2 * ft + 1) * tf:(2 * ft + 2) * tf] = sin_all[:, ft * tf:(ft + 1) * tf].astype(BF16)


def _dft_matrices(n_in, tf):
    fwd_tabs, inv_tabs = _dft_tables(n_in)
    sub_n = min(V7X_LANES, n_in)
    nb = n_in // sub_n
    tabs = [_const_spec((nb, n_in)), _const_spec((nb, n_in)),
            _const_spec((sub_n, n_in)), _const_spec((sub_n, n_in))]
    fwd = pl.pallas_call(
        functools.partial(_dft_fwd_gen_kernel, tf=tf),
        grid=(n_in // tf,),
        in_specs=tabs,
        out_specs=pl.BlockSpec((2 * tf, n_in), lambda i: (i, 0)),
        out_shape=jax.ShapeDtypeStruct((2 * n_in, n_in), BF16),
        compiler_params=_params(1),
        name="dft_fwd_gen",
    )(*fwd_tabs)
    inv = pl.pallas_call(
        functools.partial(_dft_inv_gen_kernel, tf=tf, n_in=n_in),
        grid=(nb,),
        in_specs=tabs,
        out_specs=pl.BlockSpec((sub_n, 2 * n_in), lambda i: (i, 0)),
        out_shape=jax.ShapeDtypeStruct((n_in, 2 * n_in), BF16),
        compiler_params=_params(1),
        name="dft_inv_gen",
    )(*inv_tabs)
    return fwd, inv


def _hy_proj_kernel(x_ref, xp_ref, xn_ref, mod_ref, g_ref, perm_ref, w_ref, b_ref, ws_ref, bs_ref,
                    vb_ref, v_ref, x1_ref, x2_ref):
    i = pl.program_id(1)
    tm, d = x_ref.shape
    n_ph = v_ref.shape[0]
    hp = tm // n_ph
    g = g_ref[...]
    shift, scale = mod_ref[3:4, :], mod_ref[4:5, :]
    u = _adaln(x_ref[...], g, shift, scale).astype(BF16)
    u = _mm(perm_ref[...], u).astype(BF16)
    halo = jnp.concatenate([xp_ref[...], xn_ref[...]], axis=0)
    uh = _adaln(halo, g, shift, scale).astype(BF16)
    p = _mm(jnp.concatenate([u, uh], axis=0), w_ref[...]) + b_ref[...]
    ph = [p[r * hp:(r + 1) * hp] for r in range(n_ph)]
    n_halo = xp_ref.shape[0]
    before = jnp.where(i == 0, 0.0, p[tm + n_halo - 1:tm + n_halo])
    after = jnp.where(i == pl.num_programs(1) - 1, 0.0, p[tm + n_halo:tm + n_halo + 1])
    row = lax.broadcasted_iota(jnp.int32, ph[0].shape, 0)
    last_prev = jnp.where(row == 0, before, pltpu.roll(ph[-1], 1, axis=0))
    first_next = jnp.where(row == hp - 1, after, pltpu.roll(ph[0], hp - 1, axis=0))
    w0, w1, w2, bs = ws_ref[0:1, :], ws_ref[1:2, :], ws_ref[2:3, :], bs_ref[...]
    for r in range(n_ph):
        prev = ph[r - 1] if r > 0 else last_prev
        nxt = ph[r + 1] if r < n_ph - 1 else first_next
        conv = prev * w0 + ph[r] * w1 + nxt * w2 + bs
        v = conv[:, :d]
        v_ref[r] = v
        vb_ref[r] = v.astype(BF16)
        x1_ref[r] = conv[:, d:2 * d]
        x2_ref[r] = conv[:, 2 * d:]


def _hy_filter_kernel(feat_ref, wp_ref, bp_ref, wm_ref, bm_ref, fr_ref, wf_ref, dl_ref,
                      hf_ref, hb_ref, ss_ref, *, d, n_orders):
    start = (pl.program_id(0) == 0) & (pl.program_id(1) == 0)
    feat = feat_ref[...]
    fr = fr_ref[...]
    hdn = jnp.sin(fr * (_mm_f32(feat, wp_ref[...]) + bp_ref[...]))
    hdn = jnp.sin(fr * (_mm_f32(hdn, wm_ref[...]) + bm_ref[...]))
    h = _mm(hdn.astype(BF16), wf_ref[...])
    decay = jnp.exp(-feat[:, 0:1] * dl_ref[...])
    row = lax.broadcasted_iota(jnp.int32, decay.shape, 0)
    first = (row == 0) & start

    @pl.when(start)
    def _():
        ss_ref[...] = jnp.zeros_like(ss_ref)

    for o in range(n_orders):
        sl = slice(o * d, (o + 1) * d)
        hf = h[:, 2 * o * d:(2 * o + 1) * d] * decay
        hb = jnp.where(first, 0.0, h[:, (2 * o + 1) * d:(2 * o + 2) * d] * decay)
        hf_ref[:, sl] = hf.astype(BF16)
        hb_ref[:, sl] = hb.astype(BF16)
        ss_ref[:, sl] += jnp.sum(hf * hf + hb * hb, axis=0, keepdims=True)


def _twiddle(a, c, s, conj=False):
    a_re, a_s = a
    if conj:
        return a_re * c + a_s * s, a_s * c - a_re * s
    return a_re * c - a_s * s, a_s * c + a_re * s


def _add(a, b):
    return a[0] + b[0], a[1] + b[1]


def _sub(a, b):
    return a[0] - b[0], a[1] - b[1]


def _dit_blocks(dft, phase_refs, tw, tp):
    ph = []
    for ref in phase_refs:
        t = _mm(dft, ref[...])
        ph.append((t[0:tp], t[tp:]))
    c1, s1, c2, s2, c3, s3 = tw
    t2, t3 = _twiddle(ph[2], c1, s1), _twiddle(ph[3], c1, s1)
    e_a, e_b = _add(ph[0], t2), _sub(ph[0], t2)
    o_a, o_b = _add(ph[1], t3), _sub(ph[1], t3)
    ta, tb = _twiddle(o_a, c2, s2), _twiddle(o_b, c3, s3, conj=True)
    return [_add(e_a, ta), _sub(e_a, ta), _add(e_b, tb), _sub(e_b, tb)]


def _bin_group(f_ref, tw_ref, g, tp):
    re, sn = slice(2 * g * tp, (2 * g + 1) * tp), slice((2 * g + 1) * tp, (2 * g + 2) * tp)
    dft = f_ref[2 * g * tp:(2 * g + 2) * tp, :]
    tw = [tw_ref[i, g * tp:(g + 1) * tp, :] for i in range(6)]
    return re, sn, dft, tw


def _hy_spec_kernel(f_ref, f0, f1, f2, f3, b0, b1, b2, b3, tw_ref, ss_ref, o_ref, *, tp, scale):
    wgt = scale * lax.rsqrt(ss_ref[...] + NORM_EPS)
    for g in range(tw_ref.shape[1] // tp):
        re, sn, dft, tw = _bin_group(f_ref, tw_ref, g, tp)
        fwd = _dit_blocks(dft, (f0, f1, f2, f3), tw, tp)
        bwd = _dit_blocks(dft, (b0, b1, b2, b3), tw, tp)
        for blk in range(4):
            o_ref[blk, re, :] = (fwd[blk][0] + bwd[blk][0]) * wgt
            o_ref[blk, sn, :] = (fwd[blk][1] - bwd[blk][1]) * wgt


def _hy_fwd_kernel(f_ref, u0, u1, u2, u3, h_ref, tw_ref, y_ref, *, tp):
    for g in range(tw_ref.shape[1] // tp):
        re, sn, dft, tw = _bin_group(f_ref, tw_ref, g, tp)
        c1, s1, c2, s2, c3, s3 = tw
        x = _dit_blocks(dft, (u0, u1, u2, u3), tw, tp)
        y = []
        for blk in range(4):
            h_re, h_s = h_ref[blk, re, :], h_ref[blk, sn, :]
            x_re, x_s = x[blk]
            y.append((x_re * h_re - x_s * h_s, x_re * h_s + x_s * h_re))
        e_a, o_a = _add(y[0], y[1]), _twiddle(_sub(y[0], y[1]), c2, s2, conj=True)
        e_b, o_b = _add(y[2], y[3]), _twiddle(_sub(y[2], y[3]), c3, s3)
        out = [_add(e_a, e_b), _add(o_a, o_b),
               _twiddle(_sub(e_a, e_b), c1, s1, conj=True), _twiddle(_sub(o_a, o_b), c1, s1, conj=True)]
        for r in range(4):
            y_ref[r, re, :] = out[r][0].astype(BF16)
            y_ref[r, sn, :] = out[r][1].astype(BF16)


def _hy_inv_kernel(f_ref, y_ref, a_ref, b_ref, fb_ref, *o_refs, rows):
    for r in range(f_ref.shape[0] // rows):
        sl = slice(r * rows, (r + 1) * rows)
        conv = _mm(f_ref[sl, :], y_ref[...])
        z = a_ref[sl, :] * (conv + b_ref[sl, :] * fb_ref[...])
        for o_ref in o_refs:
            o_ref[sl, :] = z.astype(o_ref.dtype)


def _hy_out_kernel(x_ref, mod_ref, z_ref, pick_ref, w_ref, b_ref, o_ref):
    z = _mm(pick_ref[0], z_ref[0])
    for r in range(1, z_ref.shape[0]):
        z = z + _mm(pick_ref[r], z_ref[r])
    o_ref[...] = x_ref[...] + mod_ref[5:6, :] * (_mm(z.astype(BF16), w_ref[...]) + b_ref[...])


def _hyena_layer(x, mods, layer, row_fn, g, w_in, b_in, w_short, b_short, w_pos, b_pos, w_mid, b_mid,
                 freq, w_filt, fbias, w_out, b_out):
    bsz, length, d = x.shape
    n_full = 2 * length
    n_ph = DFT_RADIX
    q = length // n_ph
    tm = min(512, length)
    hp = tm // n_ph
    tc = DFT_COL_TILE
    n_ct = d // tc
    width = w_mid.shape[0]
    n_parts = 2 * HYENA_ORDER

    halo = 8
    m_idx = np.arange(hp)
    perm = np.zeros((tm, tm), np.float32)
    pick = np.zeros((n_ph, tm, hp), np.float32)
    for r in range(n_ph):
        perm[r * hp + m_idx, n_ph * m_idx + r] = 1.0
        pick[r, n_ph * m_idx + r, m_idx] = 1.0
    ph_spec = pl.BlockSpec((None, n_ph, hp, d), lambda b, i: (b, 0, i, 0))
    vb, v, x1, x2 = pl.pallas_call(
        _hy_proj_kernel,
        grid=(bsz, length // tm),
        in_specs=[
            pl.BlockSpec((None, tm, d), lambda b, i: (b, i, 0)),
            pl.BlockSpec((None, halo, d), lambda b, i: (b, jnp.maximum(i * (tm // halo) - 1, 0), 0)),
            pl.BlockSpec((None, halo, d),
                         lambda b, i: (b, jnp.minimum((i + 1) * (tm // halo), length // halo - 1), 0)),
            _mod_spec(d, layer, row_fn),
            _const_spec((1, d)),
            _const_spec((tm, tm)),
            _const_spec((d, 3 * d)),
            _const_spec((1, 3 * d)),
            _const_spec((3, 3 * d)),
            _const_spec((1, 3 * d)),
        ],
        out_specs=[ph_spec] * 4,
        out_shape=[jax.ShapeDtypeStruct((bsz, n_ph, q, d), BF16)]
        + [jax.ShapeDtypeStruct((bsz, n_ph, q, d), F32)] * 3,
        compiler_params=_params(2),
        name="hy_proj",
    )(x, x, x, mods, g.reshape(1, d), jnp.asarray(perm, BF16), w_in, b_in.reshape(1, 3 * d),
      w_short, b_short.reshape(1, 3 * d))

    t = jnp.linspace(0.0, 1.0, length, dtype=F32)[:, None]
    bands = jnp.linspace(1e-4, HYENA_BANDS - 1, HYENA_BANDS, dtype=F32)
    ang = (2.0 * math.pi / length) * jnp.arange(length, dtype=F32)[:, None] * bands[None, :]
    feat = jnp.concatenate([t, jnp.cos(ang), -jnp.sin(ang)], axis=-1)
    feat = jnp.pad(feat, ((0, 0), (0, V7X_LANES - HYENA_EMB)))
    w_pos_p = jnp.pad(w_pos.astype(F32), ((0, V7X_LANES - HYENA_EMB), (0, 0)))
    deltas = jnp.abs(jnp.linspace(HYENA_MIN_DECAY, HYENA_MAX_DECAY, d, dtype=F32)).reshape(1, d)
    feat = jnp.concatenate([feat[r::n_ph] for r in range(n_ph)], axis=0)
    tl = min(512, q)
    od = HYENA_ORDER * d
    tap_spec = pl.BlockSpec((tl, od), lambda r, i: (i, r))
    taps_f, taps_b, ss = pl.pallas_call(
        functools.partial(_hy_filter_kernel, d=d, n_orders=HYENA_ORDER),
        grid=(n_ph, q // tl),
        in_specs=[
            pl.BlockSpec((tl, V7X_LANES), lambda r, i: (r * (q // tl) + i, 0)),
            _const_spec((V7X_LANES, width)),
            _const_spec((1, width)),
            _const_spec((width, width)),
            _const_spec((1, width)),
            _const_spec((1, width)),
            _const_spec((width, n_parts * d)),
            _const_spec((1, d)),
        ],
        out_specs=[tap_spec, tap_spec, pl.BlockSpec((1, od), lambda r, i: (0, 0))],
        out_shape=[jax.ShapeDtypeStruct((q, n_ph * od), BF16), jax.ShapeDtypeStruct((q, n_ph * od), BF16),
                   jax.ShapeDtypeStruct((1, od), F32)],
        compiler_params=_params(2),
        name="hy_filter",
    )(feat, w_pos_p, b_pos.reshape(1, width), w_mid, b_mid.reshape(1, width), freq.reshape(1, width),
      w_filt.astype(BF16), deltas)

    tf = min(DFT_FREQ_TILE, q)
    tp = min(DFT_PACK, q)
    tn = min(DFT_TIME_TILE, q)
    dft_fwd, dft_inv = _dft_matrices(q, tp)
    odd = 2.0 * jnp.arange(q, dtype=F32) + 1.0
    angles = jnp.stack([(2.0 * math.pi / n_full) * odd, (math.pi / n_full) * odd,
                        (math.pi / n_full) * (length - odd)])
    tw = jnp.stack([jnp.cos(angles), jnp.sin(angles)], axis=1).reshape(6, q)
    tw = jnp.broadcast_to(tw[:, :, None], (6, q, tc))
    tw_spec = pl.BlockSpec((6, tf, tc), lambda f, *_: (0, f, 0))

    n_oc = od // tc

    def tap_phase(r):
        return pl.BlockSpec((q, tc), lambda f, o, j: (0, r * n_oc + o * n_ct + j))

    spec = pl.pallas_call(
        functools.partial(_hy_spec_kernel, tp=tp, scale=1.0 / length),
        grid=(q // tf, HYENA_ORDER, n_ct),
        in_specs=[pl.BlockSpec((2 * tf, q), lambda f, o, j: (f, 0))]
        + [tap_phase(r) for r in range(n_ph)] * 2
        + [tw_spec, pl.BlockSpec((1, tc), lambda f, o, j: (0, o * n_ct + j))],
        out_specs=pl.BlockSpec((None, n_ph, 2 * tf, tc), lambda f, o, j: (o, 0, f, j)),
        out_shape=jax.ShapeDtypeStruct((HYENA_ORDER, n_ph, 2 * q, d), F32),
        compiler_params=_params(3),
        name="hy_spec",
    )(dft_fwd, *([taps_f] * n_ph), *([taps_b] * n_ph), tw, ss)

    def long_conv(order, u, a, bsrc, want_f32):
        yspec = pl.pallas_call(
            functools.partial(_hy_fwd_kernel, tp=tp),
            grid=(q // tf, n_ct, bsz),
            in_specs=[pl.BlockSpec((2 * tf, q), lambda f, j, b: (f, 0))]
            + [pl.BlockSpec((None, None, q, tc), functools.partial(lambda r, f, j, b: (b, r, 0, j), r))
               for r in range(n_ph)]
            + [pl.BlockSpec((None, n_ph, 2 * tf, tc), lambda f, j, b: (order, 0, f, j)), tw_spec],
            out_specs=pl.BlockSpec((None, n_ph, 2 * tf, tc), lambda f, j, b: (b, 0, f, j)),
            out_shape=jax.ShapeDtypeStruct((bsz, n_ph, 2 * q, d), BF16),
            compiler_params=_params(3),
            name="hy_fwd",
        )(dft_fwd, *([u] * n_ph), spec, tw)
        row_spec = pl.BlockSpec((None, None, tn, tc), lambda n, j, b, r: (b, r, n, j))
        out_shape = [jax.ShapeDtypeStruct((bsz, n_ph, q, d), BF16)]
        if want_f32:
            out_shape.append(jax.ShapeDtypeStruct((bsz, n_ph, q, d), F32))
        outs = pl.pallas_call(
            functools.partial(_hy_inv_kernel, rows=min(DFT_ROW_GROUP, tn)),
            grid=(q // tn, n_ct, bsz, n_ph),
            in_specs=[
                pl.BlockSpec((tn, 2 * q), lambda n, j, b, r: (n, 0)),
                pl.BlockSpec((None, None, 2 * q, tc), lambda n, j, b, r: (b, r, 0, j)),
                row_spec,
                row_spec,
                pl.BlockSpec((None, 1, tc), lambda n, j, b, r: (order, 0, j)),
            ],
            out_specs=[row_spec] * len(out_shape),
            out_shape=out_shape,
            compiler_params=_params(4),
            name="hy_inv",
        )(dft_inv, yspec, a, bsrc, fbias.astype(F32).reshape(HYENA_ORDER, 1, d))
        return outs

    zb, z = long_conv(0, vb, x1, v, True)
    (zzb,) = long_conv(1, zb, x2, z, False)

    return pl.pallas_call(
        _hy_out_kernel,
        grid=(bsz, length // tm),
        in_specs=[
            pl.BlockSpec((None, tm, d), lambda b, i: (b, i, 0)),
            _mod_spec(d, layer, row_fn),
            pl.BlockSpec((None, n_ph, hp, d), lambda b, i: (b, 0, i, 0)),
            _const_spec((n_ph, tm, hp)),
            _const_spec((d, d)),
            _const_spec((1, d)),
        ],
        out_specs=pl.BlockSpec((None, tm, d), lambda b, i: (b, i, 0)),
        out_shape=jax.ShapeDtypeStruct(x.shape, F32),
        compiler_params=_params(2),
        name="hy_out",
    )(x, mods, zzb, jnp.asarray(pick, BF16), w_out, b_out.reshape(1, d))


def kernel(x, c, ctx, c_ctx, ada_w, ada_b, norm_g, ffn_w1, ffn_w3, ffn_w2, ret_w_in, ret_w_out, ret_decay,
           pool_w, pool_b, pool_scale, hy_w_in, hy_b_in, hy_w_short, hy_b_short, hy_w_pos, hy_b_pos,
           hy_w_mid, hy_b_mid, hy_freq, hy_w_filt, hy_bias, hy_w_out, hy_b_out, final_g):
    bsz, seq, d = x.shape
    t_ctx = ctx.shape[1]
    depth = ada_w.shape[0]
    n_mixers = 3
    assert bsz + 1 <= MOD_ROWS
    ctx_row = bsz

    cond = jnp.concatenate([c, c_ctx[None, :], jnp.zeros((MOD_ROWS - bsz - 1, d), F32)], axis=0)
    mods = _ada_all(cond, ada_w, ada_b).reshape(depth, MOD_ROWS, ADA_CHUNKS, d)

    w1b, w3b, w2b = ffn_w1.astype(BF16), ffn_w3.astype(BF16), ffn_w2.astype(BF16)
    lat_row = lambda b, i: b
    ctx_flat_row = lambda b, i: ctx_row

    def ffn(h_lat, h_ctx, layer, half, final=False):
        return _half_ffn(h_lat, h_ctx, mods, layer, half, norm_g[layer, 2 * half], w1b, w3b, w2b, final_g,
                         final=final)

    h_lat, h_ctx = x, ctx
    for layer in range(depth):
        kind = layer % n_mixers
        slot = layer // n_mixers
        last = layer == depth - 1
        ctx_out = not last
        ctx_live = ctx_out or kind == 0
        h_lat, h_ctx = ffn(h_lat, h_ctx if ctx_live else None, layer, 0)
        g_mix = norm_g[layer, 1]
        if kind == 0:
            h_ctx, h_lat = _retention_layer(h_ctx, h_lat, mods, layer, g_mix, ret_w_in[slot].astype(BF16),
                                            ret_w_out[slot].astype(BF16), ret_decay[slot], ctx_out)
        elif kind == 1:
            pp = (pool_w[slot], pool_b[slot], pool_scale[slot])
            h_lat = _pool_layer(h_lat, GRID_W, mods, layer, lat_row, g_mix, *pp)
            if ctx_out:
                h_ctx = _pool_layer(h_ctx, t_ctx, mods, layer, ctx_flat_row, g_mix, *pp)
        else:
            hp = (hy_w_in[slot].astype(BF16), hy_b_in[slot], hy_w_short[slot], hy_b_short[slot], hy_w_pos[slot],
                  hy_b_pos[slot], hy_w_mid[slot], hy_b_mid[slot], hy_freq[slot], hy_w_filt[slot],
                  hy_bias[slot], hy_w_out[slot].astype(BF16), hy_b_out[slot])
            h_lat = _hyena_layer(h_lat, mods, layer, lat_row, g_mix, *hp)
            if ctx_out:
                h_ctx = _hyena_layer(h_ctx, mods, layer, ctx_flat_row, g_mix, *hp)
        h_lat, h_ctx = ffn(h_lat, h_ctx if ctx_out else None, layer, 1, final=last)
    return h_lat
```

```python
import functools
import math

import numpy as np
import jax
import jax.numpy as jnp
from jax import lax
from jax.experimental import pallas as pl
from jax.experimental.pallas import tpu as pltpu

F32 = jnp.float32
BF16 = jnp.bfloat16

GRID_W = 64
ADA_CHUNKS = 9
NORM_EPS = 1e-6
RET_HEADS = 4
RET_CHUNK = 128
ROPE_BASE = 10000.0
POOL_WINDOWS = (2, 4, 8, 16)
HYENA_ORDER = 2
HYENA_EMB = 33
HYENA_BANDS = (HYENA_EMB - 1) // 2
HYENA_TARGET = 1e-2
HYENA_FAST = 0.3
HYENA_SLOW = 1.5
HYENA_MAX_DECAY = math.log(HYENA_TARGET) / HYENA_FAST
HYENA_MIN_DECAY = math.log(HYENA_TARGET) / HYENA_SLOW

V7X_LANES = 128
V7X_VMEM_LIMIT_BYTES = 56 * 1024 * 1024
MOD_ROWS = 8
POOL_TILE = 256
RET_TILE = 256
SCAN_CHUNK = 256
DFT_RADIX = 4
DFT_FREQ_TILE = 512
DFT_PACK = 256
DFT_ROW_GROUP = 512
DFT_TIME_TILE = 2048
DFT_COL_TILE = 256
HIGHEST = lax.Precision.HIGHEST


def _params(n_axes):
    return pltpu.CompilerParams(
        dimension_semantics=("arbitrary",) * n_axes,
        vmem_limit_bytes=V7X_VMEM_LIMIT_BYTES)


def _const_spec(shape):
    zeros = (0,) * len(shape)
    return pl.BlockSpec(shape, lambda *_: zeros, pipeline_mode=pl.Buffered(1))


def _mod_spec(d, layer, row_fn):
    return pl.BlockSpec((None, None, ADA_CHUNKS, d), lambda *idx: (layer, row_fn(*idx), 0, 0))


def _mm(a, b):
    return jnp.dot(a, b, preferred_element_type=F32)


def _mm_f32(a, b):
    return jnp.dot(a, b, preferred_element_type=F32, precision=HIGHEST)


def _silu(x):
    return x * jax.nn.sigmoid(x)


def _adaln(x, g, shift, scale):
    ms = jnp.mean(x * x, axis=-1, keepdims=True)
    y = x * lax.rsqrt(ms + NORM_EPS) * g
    return y * (1.0 + scale) + shift


def _ada_kernel(c_ref, w_ref, b_ref, o_ref):
    s = _silu(c_ref[...]).astype(BF16)
    o_ref[...] = _mm(s, w_ref[...].astype(BF16)) + b_ref[...]


def _ada_all(cond, ada_w, ada_b):
    depth, d, nd = ada_w.shape
    return pl.pallas_call(
        _ada_kernel,
        grid=(depth, nd // d),
        in_specs=[
            pl.BlockSpec((MOD_ROWS, d), lambda l, j: (0, 0)),
            pl.BlockSpec((None, d, d), lambda l, j: (l, 0, j)),
            pl.BlockSpec((None, 1, d), lambda l, j: (l, 0, j)),
        ],
        out_specs=pl.BlockSpec((None, MOD_ROWS, d), lambda l, j: (l, 0, j)),
        out_shape=jax.ShapeDtypeStruct((depth, MOD_ROWS, nd), F32),
        compiler_params=_params(2),
        name="ada",
    )(cond, ada_w, ada_b.reshape(depth, 1, nd))


def _ffn_kernel(xl_ref, xc_ref, mod_ref, g_ref, w1_ref, w3_ref, w2_ref, fg_ref, *o_refs, base, final, n_lat):
    is_ctx = pl.program_id(0) >= n_lat
    x = jnp.where(is_ctx, xc_ref[...], xl_ref[...]) if len(o_refs) == 2 else xl_ref[...]
    u = _adaln(x, g_ref[...], mod_ref[base:base + 1, :], mod_ref[base + 1:base + 2, :]).astype(BF16)
    h1 = _mm(u, w1_ref[...])
    h3 = _mm(u, w3_ref[...])
    a = (_silu(h1) * h3).astype(BF16)
    y = x + (0.5 * mod_ref[base + 2:base + 3, :]) * _mm(a, w2_ref[...])
    if final:
        ms = jnp.mean(y * y, axis=-1, keepdims=True)
        y = y * lax.rsqrt(ms + NORM_EPS) * fg_ref[...]
    if len(o_refs) == 2:
        @pl.when(jnp.logical_not(is_ctx))
        def _():
            o_refs[0][...] = y

        @pl.when(is_ctx)
        def _():
            o_refs[1][...] = y
    else:
        o_refs[0][...] = y


def _half_ffn(h_lat, h_ctx, mods, layer, half, g, w1, w3, w2, final_g, final=False, tm=512):
    bsz, t, d = h_lat.shape
    f = w1.shape[-1]
    tm = min(tm, t)
    base = 6 * half
    per_b = t // tm
    n_lat = bsz * per_b
    with_ctx = h_ctx is not None
    ctx_flat = h_ctx.reshape(1, -1, d) if with_ctx else h_lat[:1, :tm]
    n_ctx = ctx_flat.shape[1] // tm if with_ctx else 0
    assert t % tm == 0 and ctx_flat.shape[1] % tm == 0

    def lat_idx(s):
        s = jnp.minimum(s, n_lat - 1)
        return (s // per_b, s % per_b, 0)

    def ctx_idx(s):
        return (0, jnp.maximum(s - n_lat, 0), 0)

    def weight_spec(rows, cols):
        return pl.BlockSpec((None, None, rows, cols), lambda *_: (layer, half, 0, 0),
                            pipeline_mode=pl.Buffered(1))

    out_specs = [pl.BlockSpec((None, tm, d), lat_idx)]
    out_shape = [jax.ShapeDtypeStruct(h_lat.shape, F32)]
    if with_ctx:
        out_specs.append(pl.BlockSpec((None, tm, d), ctx_idx))
        out_shape.append(jax.ShapeDtypeStruct(ctx_flat.shape, F32))
    outs = pl.pallas_call(
        functools.partial(_ffn_kernel, base=base, final=final, n_lat=n_lat),
        grid=(n_lat + n_ctx,),
        in_specs=[
            pl.BlockSpec((None, tm, d), lat_idx),
            pl.BlockSpec((None, tm, d), ctx_idx),
            _mod_spec(d, layer, lambda s: jnp.where(s < n_lat, s // per_b, bsz)),
            _const_spec((1, d)),
            weight_spec(d, f),
            weight_spec(d, f),
            weight_spec(f, d),
            _const_spec((1, d)),
        ],
        out_specs=out_specs,
        out_shape=out_shape,
        compiler_params=_params(1),
        name="half_ffn",
    )(h_lat, ctx_flat, mods, g.reshape(1, d), w1, w3, w2, final_g.reshape(1, d))
    return outs[0], (outs[1].reshape(h_ctx.shape) if with_ctx else None)


def _ret_proj_kernel(xc_ref, xl_ref, mod_ref, g_ref, w_ref, cos_ref, sin_ref, q_ref, k_ref, v_ref, gate_ref,
                     *, n_heads, dk, dv, ctx_tiles):
    x = jnp.where(pl.program_id(1) < ctx_tiles, xc_ref[...], xl_ref[...])
    u = _adaln(x, g_ref[...], mod_ref[3:4, :], mod_ref[4:5, :]).astype(BF16)
    p = _mm(u, w_ref[...])
    cos = cos_ref[...]
    sin = sin_ref[...]
    half = dk // 2
    qk = n_heads * dk
    k_scale = dk ** -0.5

    def rot(a):
        a1, a2 = a[:, :half], a[:, half:]
        return jnp.concatenate([a1 * cos - a2 * sin, a2 * cos + a1 * sin], axis=-1)

    for h in range(n_heads):
        q_ref[:, h * dk:(h + 1) * dk] = rot(p[:, h * dk:(h + 1) * dk]).astype(BF16)
        k_ref[:, h * dk:(h + 1) * dk] = (rot(p[:, qk + h * dk:qk + (h + 1) * dk]) * k_scale).astype(BF16)
    v_ref[...] = p[:, 2 * qk:2 * qk + n_heads * dv].astype(BF16)
    gate_ref[...] = p[:, 2 * qk + n_heads * dv:].astype(BF16)


def _ret_scan_kernel(q_ref, k_ref, v_ref, dmat_ref, qdec_ref, kdec_ref, cdec_ref, o_ref,
                     state_ref, fwd_ref, *, n_heads, dk, dv, n_chunks, bwd_chunk):
    s = pl.program_id(1)
    cs = q_ref.shape[0]

    @pl.when((s == 0) | (s == n_chunks))
    def _():
        state_ref[...] = jnp.zeros_like(state_ref)

    chunk = jnp.where(s < n_chunks, s, bwd_chunk(jnp.maximum(s - n_chunks, 0)))
    rows = pl.ds(pl.multiple_of(chunk * cs, cs), cs)
    outs = []
    for h in range(n_heads):
        q = q_ref[:, h * dk:(h + 1) * dk]
        k = k_ref[:, h * dk:(h + 1) * dk]
        v = v_ref[:, h * dv:(h + 1) * dv]
        scores = lax.dot_general(q, k, (((1,), (1,)), ((), ())), preferred_element_type=F32) * dmat_ref[h]
        state = state_ref[h]
        outs.append(_mm(scores.astype(BF16), v) + _mm(q, state.astype(BF16)) * qdec_ref[h])
        kd = (k.astype(F32) * kdec_ref[h]).astype(BF16)
        state_ref[h] = state * cdec_ref[h] + lax.dot_general(
            kd, v, (((0,), (0,)), ((), ())), preferred_element_type=F32)

    @pl.when(s < n_chunks)
    def _():
        for h in range(n_heads):
            fwd_ref[rows, h * dv:(h + 1) * dv] = outs[h].astype(fwd_ref.dtype)

    @pl.when(s >= n_chunks)
    def _():
        for h in range(n_heads):
            o = outs[h] + fwd_ref[rows, h * dv:(h + 1) * dv].astype(F32)
            mu = jnp.mean(o, axis=-1, keepdims=True)
            oc = o - mu
            var = jnp.mean(oc * oc, axis=-1, keepdims=True)
            o_ref[:, h * dv:(h + 1) * dv] = (oc * lax.rsqrt(var + NORM_EPS)).astype(o_ref.dtype)


def _ret_out_kernel(x_ref, mod_ref, y_ref, gate_ref, w_ref, o_ref):
    z = (_silu(gate_ref[...].astype(F32)) * y_ref[...].astype(F32)).astype(BF16)
    o_ref[...] = x_ref[...] + mod_ref[5:6, :] * _mm(z, w_ref[...])


def _retention_layer(h_ctx, h_lat, mods, layer, g, w_in, w_out, decay, ctx_out):
    bsz, t_ctx, d = h_ctx.shape
    t_lat = h_lat.shape[1]
    t_all = t_ctx + t_lat
    n_heads = RET_HEADS
    dk = d // n_heads
    dv = 2 * d // n_heads
    qk = n_heads * dk
    vw = n_heads * dv
    tm = RET_TILE
    assert t_ctx % tm == 0 and t_lat % tm == 0
    ctx_tiles = t_ctx // tm
    ctx_row = bsz

    pos = jnp.arange(t_all, dtype=F32)
    inv = 1.0 / (ROPE_BASE ** jnp.linspace(0.0, 1.0, dk // 2, dtype=F32))
    ang = pos[:, None] * inv[None, :]

    q, k, v, gate = pl.pallas_call(
        functools.partial(_ret_proj_kernel, n_heads=n_heads, dk=dk, dv=dv, ctx_tiles=ctx_tiles),
        grid=(bsz, t_all // tm),
        in_specs=[
            pl.BlockSpec((None, tm, d), lambda b, i: (b, jnp.minimum(i, ctx_tiles - 1), 0)),
            pl.BlockSpec((None, tm, d), lambda b, i: (b, jnp.maximum(i - ctx_tiles, 0), 0)),
            _mod_spec(d, layer, lambda b, i: jnp.where(i < ctx_tiles, ctx_row, b)),
            _const_spec((1, d)),
            _const_spec((d, 2 * qk + 2 * vw)),
            pl.BlockSpec((tm, dk // 2), lambda b, i: (i, 0)),
            pl.BlockSpec((tm, dk // 2), lambda b, i: (i, 0)),
        ],
        out_specs=[
            pl.BlockSpec((None, tm, qk), lambda b, i: (b, i, 0)),
            pl.BlockSpec((None, tm, qk), lambda b, i: (b, i, 0)),
            pl.BlockSpec((None, tm, vw), lambda b, i: (b, i, 0)),
            pl.BlockSpec((None, tm, vw), lambda b, i: (b, i, 0)),
        ],
        out_shape=[
            jax.ShapeDtypeStruct((bsz, t_all, qk), BF16),
            jax.ShapeDtypeStruct((bsz, t_all, qk), BF16),
            jax.ShapeDtypeStruct((bsz, t_all, vw), BF16),
            jax.ShapeDtypeStruct((bsz, t_all, vw), BF16),
        ],
        compiler_params=_params(2),
        name="ret_proj",
    )(h_ctx, h_lat, mods, g.reshape(1, d), w_in, jnp.cos(ang), jnp.sin(ang))

    cs = SCAN_CHUNK
    assert t_ctx % cs == 0 and t_lat % cs == 0
    n_chunks = t_all // cs
    ctx_chunks = t_ctx // cs
    lg = jax.nn.log_sigmoid(decay.astype(F32))[:, :, None]
    n = jnp.arange(cs, dtype=F32)
    rel = n[:, None] - n[None, :]
    dm_f = jnp.where(rel >= 0, jnp.exp(lg[0][:, :, None] * jnp.maximum(rel, 0.0)), 0.0)
    dm_b = jnp.where(rel < 0, jnp.exp(lg[1][:, :, None] * jnp.maximum(-rel, 0.0)), 0.0)
    dmat = jnp.stack([dm_f, dm_b])
    qd = jnp.stack([jnp.exp(lg[0] * (n + 1.0)), jnp.exp(lg[1] * (cs - n))])
    kd = jnp.stack([jnp.exp(lg[0] * (cs - 1.0 - n)), jnp.exp(lg[1] * n)])
    cd = jnp.exp(lg * cs)
    qdec = jnp.broadcast_to(qd[..., None], (2, n_heads, cs, dv))
    kdec = jnp.broadcast_to(kd[..., None], (2, n_heads, cs, dk))
    cdec = jnp.broadcast_to(cd[..., None], (2, n_heads, 1, dv))

    def bwd_chunk(j):
        return jnp.where(j < ctx_chunks, ctx_chunks - 1 - j, n_chunks - 1 + ctx_chunks - j)

    def chunk(s):
        return jnp.where(s < n_chunks, s, bwd_chunk(jnp.maximum(s - n_chunks, 0)))

    def out_chunk(s):
        return bwd_chunk(jnp.maximum(s - n_chunks, 0))

    def table_spec(*shape):
        return pl.BlockSpec((None,) + shape, lambda b, s: (s // n_chunks, 0, 0, 0))

    y = pl.pallas_call(
        functools.partial(_ret_scan_kernel, n_heads=n_heads, dk=dk, dv=dv, n_chunks=n_chunks,
                          bwd_chunk=bwd_chunk),
        grid=(bsz, 2 * n_chunks),
        in_specs=[
            pl.BlockSpec((None, cs, qk), lambda b, s: (b, chunk(s), 0)),
            pl.BlockSpec((None, cs, qk), lambda b, s: (b, chunk(s), 0)),
            pl.BlockSpec((None, cs, vw), lambda b, s: (b, chunk(s), 0)),
            table_spec(n_heads, cs, cs),
            table_spec(n_heads, cs, dv),
            table_spec(n_heads, cs, dk),
            table_spec(n_heads, 1, dv),
        ],
        out_specs=pl.BlockSpec((None, cs, vw), lambda b, s: (b, out_chunk(s), 0)),
        out_shape=jax.ShapeDtypeStruct((bsz, t_all, vw), BF16),
        scratch_shapes=[pltpu.VMEM((n_heads, dk, dv), F32), pltpu.VMEM((t_all, vw), BF16)],
        compiler_params=_params(2),
        name="ret_scan",
    )(q, k, v, dmat, qdec, kdec, cdec)

    def readout(x, tile0, row_fn):
        return pl.pallas_call(
            _ret_out_kernel,
            grid=(bsz, x.shape[1] // tm),
            in_specs=[
                pl.BlockSpec((None, tm, d), lambda b, i: (b, i, 0)),
                _mod_spec(d, layer, row_fn),
                pl.BlockSpec((None, tm, vw), lambda b, i: (b, i + tile0, 0)),
                pl.BlockSpec((None, tm, vw), lambda b, i: (b, i + tile0, 0)),
                _const_spec((vw, d)),
            ],
            out_specs=pl.BlockSpec((None, tm, d), lambda b, i: (b, i, 0)),
            out_shape=jax.ShapeDtypeStruct(x.shape, F32),
            compiler_params=_params(2),
            name="ret_out",
        )(x, mods, y, gate, w_out)

    new_lat = readout(h_lat, ctx_tiles, lambda b, i: b)
    new_ctx = readout(h_ctx, 0, lambda b, i: ctx_row) if ctx_out else None
    return new_ctx, new_lat


def _pool_tables(seg_len):
    assert POOL_TILE % seg_len == 0
    r = np.arange(POOL_TILE)
    seg, pos = r // seg_len, r % seg_len
    mats, invs = [], []
    for win in POOL_WINDOWS:
        lo = np.clip(pos - win // 2, 0, seg_len)
        hi = np.clip(pos - win // 2 + win, 0, seg_len)
        col_seg, col_pos = seg[None, :], pos[None, :]
        member = (col_seg == seg[:, None]) & (col_pos >= lo[:, None]) & (col_pos < hi[:, None])
        mats.append(member.astype(np.float32))
        invs.append((1.0 / (hi - lo)).astype(np.float32))
    return np.stack(mats), np.stack(invs)


def _pool_kernel(x_ref, mod_ref, g_ref, s_ref, inv_ref, w_ref, b_ref, sc_ref, o_ref, *, n_groups, dg):
    x = x_ref[...]
    u = _adaln(x, g_ref[...], mod_ref[3:4, :], mod_ref[4:5, :])
    gate = mod_ref[5:6, :]
    for gi in range(n_groups):
        sl = slice(gi * dg, (gi + 1) * dg)
        ug = u[:, sl]
        hi = ug.astype(BF16)
        lo = (ug - hi.astype(F32)).astype(BF16)
        win_sum = _mm(s_ref[gi], hi) + _mm(s_ref[gi], lo)
        dlt = win_sum * inv_ref[gi] - ug
        yg = _mm(dlt.astype(BF16), w_ref[gi])
        o_ref[:, sl] = x[:, sl] + gate[:, sl] * ((yg + b_ref[:, sl]) * sc_ref[:, sl])


def _pool_layer(x, seg_len, mods, layer, row_fn, g, w_grp, b_grp, scale):
    bsz, t, d = x.shape
    n_groups = len(POOL_WINDOWS)
    dg = d // n_groups
    mats, invs = _pool_tables(seg_len)
    s_tab = jnp.asarray(mats, BF16)
    inv_tab = jnp.asarray(np.broadcast_to(invs[:, :, None], (n_groups, POOL_TILE, dg)).copy(), F32)
    return pl.pallas_call(
        functools.partial(_pool_kernel, n_groups=n_groups, dg=dg),
        grid=(bsz, t // POOL_TILE),
        in_specs=[
            pl.BlockSpec((None, POOL_TILE, d), lambda b, i: (b, i, 0)),
            _mod_spec(d, layer, row_fn),
            _const_spec((1, d)),
            _const_spec((n_groups, POOL_TILE, POOL_TILE)),
            _const_spec((n_groups, POOL_TILE, dg)),
            _const_spec((n_groups, dg, dg)),
            _const_spec((1, d)),
            _const_spec((1, d)),
        ],
        out_specs=pl.BlockSpec((None, POOL_TILE, d), lambda b, i: (b, i, 0)),
        out_shape=jax.ShapeDtypeStruct(x.shape, F32),
        compiler_params=_params(2),
        name="pool",
    )(x, mods, g.reshape(1, d), s_tab, inv_tab, w_grp.astype(BF16), b_grp.reshape(1, d), scale.reshape(1, d))


def _dft_tables(n_in):
    mod = 4 * n_in
    sub_n = min(V7X_LANES, n_in)
    idx = jnp.arange(n_in, dtype=jnp.int32)[None, :]
    blk = jnp.arange(n_in // sub_n, dtype=jnp.int32)[:, None]
    sub = jnp.arange(sub_n, dtype=jnp.int32)[:, None]
    theta = 2.0 * math.pi / mod

    def cs(p):
        ph = (p % mod).astype(F32) * theta
        return jnp.cos(ph), jnp.sin(ph)

    fwd = cs(2 * sub_n * blk * idx) + cs((2 * sub + 1) * idx)
    inv = cs(sub_n * blk * (2 * idx + 1)) + cs(sub * (2 * idx + 1))
    return fwd, inv


def _dft_fwd_gen_kernel(pc_ref, ps_ref, qc_ref, qs_ref, o_ref, *, tf):
    ft = pl.program_id(0)
    sub_n = qc_ref.shape[0]
    blocks = tf // sub_n
    qc = qc_ref[...]
    qs = qs_ref[...]
    for jb in range(blocks):
        kb = ft * blocks + jb
        pc = pc_ref[pl.ds(kb, 1), :]
        ps = ps_ref[pl.ds(kb, 1), :]
        o_ref[jb * sub_n:(jb + 1) * sub_n, :] = (pc * qc - ps * qs).astype(BF16)
        o_ref[tf + jb * sub_n:tf + (jb + 1) * sub_n, :] = (ps * qc + pc * qs).astype(BF16)


def _dft_inv_gen_kernel(pc_ref, ps_ref, qc_ref, qs_ref, o_ref, *, tf, n_in):
    mb = pl.program_id(0)
    pc = pc_ref[pl.ds(mb, 1), :]
    ps = ps_ref[pl.ds(mb, 1), :]
    qc = qc_ref[...]
    qs = qs_ref[...]
    cos_all = pc * qc - ps * qs
    sin_all = ps * qc + pc * qs
    for ft in range(n_in // tf):
        o_ref[:, 2 * ft * tf:(2 * ft + 1) * tf] = cos_all[:, ft * tf:(ft + 1) * tf].astype(BF16)
        o_ref[:, (2 * ft + 1) * tf:(2 * ft + 2) * tf] = sin_all[:, ft * tf:(ft + 1) * tf].astype(BF16)


def _dft_matrices(n_in, tf):
    fwd_tabs, inv_tabs = _dft_tables(n_in)
    sub_n = min(V7X_LANES, n_in)
    nb = n_in // sub_n
    tabs = [_const_spec((nb, n_in)), _const_spec((nb, n_in)),
            _const_spec((sub_n, n_in)), _const_spec((sub_n, n_in))]
    fwd = pl.pallas_call(
        functools.partial(_dft_fwd_gen_kernel, tf=tf),
        grid=(n_in // tf,),
        in_specs=tabs,
        out_specs=pl.BlockSpec((2 * tf, n_in), lambda i: (i, 0)),
        out_shape=jax.ShapeDtypeStruct((2 * n_in, n_in), BF16),
        compiler_params=_params(1),
        name="dft_fwd_gen",
    )(*fwd_tabs)
    inv = pl.pallas_call(
        functools.partial(_dft_inv_gen_kernel, tf=tf, n_in=n_in),
        grid=(nb,),
        in_specs=tabs,
        out_specs=pl.BlockSpec((sub_n, 2 * n_in), lambda i: (i, 0)),
        out_shape=jax.ShapeDtypeStruct((n_in, 2 * n_in), BF16),
        compiler_params=_params(1),
        name="dft_inv_gen",
    )(*inv_tabs)
    return fwd, inv


def _hy_proj_kernel(x_ref, xp_ref, xn_ref, mod_ref, g_ref, perm_ref, w_ref, b_ref, ws_ref, bs_ref,
                    v_ref, x1_ref, x2_ref):
    i = pl.program_id(1)
    tm, d = x_ref.shape
    n_ph = v_ref.shape[0]
    hp = tm // n_ph
    g = g_ref[...]
    shift, scale = mod_ref[3:4, :], mod_ref[4:5, :]
    u = _adaln(x_ref[...], g, shift, scale).astype(BF16)
    u = _mm(perm_ref[...], u).astype(BF16)
    halo = jnp.concatenate([xp_ref[...], xn_ref[...]], axis=0)
    uh = _adaln(halo, g, shift, scale).astype(BF16)
    p = _mm(jnp.concatenate([u, uh], axis=0), w_ref[...]) + b_ref[...]
    ph = [p[r * hp:(r + 1) * hp] for r in range(n_ph)]
    n_halo = xp_ref.shape[0]
    before = jnp.where(i == 0, 0.0, p[tm + n_halo - 1:tm + n_halo])
    after = jnp.where(i == pl.num_programs(1) - 1, 0.0, p[tm + n_halo:tm + n_halo + 1])
    row = lax.broadcasted_iota(jnp.int32, ph[0].shape, 0)
    last_prev = jnp.where(row == 0, before, pltpu.roll(ph[-1], 1, axis=0))
    first_next = jnp.where(row == hp - 1, after, pltpu.roll(ph[0], hp - 1, axis=0))
    w0, w1, w2, bs = ws_ref[0:1, :], ws_ref[1:2, :], ws_ref[2:3, :], bs_ref[...]
    for r in range(n_ph):
        prev = ph[r - 1] if r > 0 else last_prev
        nxt = ph[r + 1] if r < n_ph - 1 else first_next
        conv = prev * w0 + ph[r] * w1 + nxt * w2 + bs
        v_ref[r] = conv[:, :d].astype(BF16)
        x1_ref[r] = conv[:, d:2 * d].astype(BF16)
        x2_ref[r] = conv[:, 2 * d:].astype(BF16)


def _hy_filter_kernel(feat_ref, wp_ref, bp_ref, wm_ref, bm_ref, fr_ref, wf_ref, dl_ref,
                      hf_ref, hb_ref, ss_ref, *, d, n_orders):
    start = (pl.program_id(0) == 0) & (pl.program_id(1) == 0)
    feat = feat_ref[...]
    fr = fr_ref[...]
    hdn = jnp.sin(fr * (_mm_f32(feat, wp_ref[...]) + bp_ref[...]))
    hdn = jnp.sin(fr * (_mm_f32(hdn, wm_ref[...]) + bm_ref[...]))
    h = _mm(hdn.astype(BF16), wf_ref[...])
    decay = jnp.exp(-feat[:, 0:1] * dl_ref[...])
    row = lax.broadcasted_iota(jnp.int32, decay.shape, 0)
    first = (row == 0) & start

    @pl.when(start)
    def _():
        ss_ref[...] = jnp.zeros_like(ss_ref)

    for o in range(n_orders):
        sl = slice(o * d, (o + 1) * d)
        hf = h[:, 2 * o * d:(2 * o + 1) * d] * decay
        hb = jnp.where(first, 0.0, h[:, (2 * o + 1) * d:(2 * o + 2) * d] * decay)
        hf_ref[:, sl] = hf.astype(BF16)
        hb_ref[:, sl] = hb.astype(BF16)
        ss_ref[:, sl] += jnp.sum(hf * hf + hb * hb, axis=0, keepdims=True)


def _twiddle(a, c, s, conj=False):
    a_re, a_s = a
    if conj:
        return a_re * c + a_s * s, a_s * c - a_re * s
    return a_re * c - a_s * s, a_s * c + a_re * s


def _add(a, b):
    return a[0] + b[0], a[1] + b[1]


def _sub(a, b):
    return a[0] - b[0], a[1] - b[1]


def _dit_blocks(dft, phase_refs, tw, tp):
    ph = []
    for ref in phase_refs:
        t = _mm(dft, ref[...])
        ph.append((t[0:tp], t[tp:]))
    c1, s1, c2, s2, c3, s3 = tw
    t2, t3 = _twiddle(ph[2], c1, s1), _twiddle(ph[3], c1, s1)
    e_a, e_b = _add(ph[0], t2), _sub(ph[0], t2)
    o_a, o_b = _add(ph[1], t3), _sub(ph[1], t3)
    ta, tb = _twiddle(o_a, c2, s2), _twiddle(o_b, c3, s3, conj=True)
    return [_add(e_a, ta), _sub(e_a, ta), _add(e_b, tb), _sub(e_b, tb)]


def _bin_group(f_ref, tw_ref, g, tp):
    re, sn = slice(2 * g * tp, (2 * g + 1) * tp), slice((2 * g + 1) * tp, (2 * g + 2) * tp)
    dft = f_ref[2 * g * tp:(2 * g + 2) * tp, :]
    tw = [tw_ref[i, g * tp:(g + 1) * tp, :] for i in range(6)]
    return re, sn, dft, tw


def _hy_spec_kernel(f_ref, f0, f1, f2, f3, b0, b1, b2, b3, tw_ref, ss_ref, o_ref, *, tp, scale):
    wgt = scale * lax.rsqrt(ss_ref[...] + NORM_EPS)
    for g in range(tw_ref.shape[1] // tp):
        re, sn, dft, tw = _bin_group(f_ref, tw_ref, g, tp)
        fwd = _dit_blocks(dft, (f0, f1, f2, f3), tw, tp)
        bwd = _dit_blocks(dft, (b0, b1, b2, b3), tw, tp)
        for blk in range(4):
            o_ref[blk, re, :] = (fwd[blk][0] + bwd[blk][0]) * wgt
            o_ref[blk, sn, :] = (fwd[blk][1] - bwd[blk][1]) * wgt


def _hy_fwd_kernel(f_ref, u0, u1, u2, u3, h_ref, tw_ref, y_ref, *, tp):
    for g in range(tw_ref.shape[1] // tp):
        re, sn, dft, tw = _bin_group(f_ref, tw_ref, g, tp)
        c1, s1, c2, s2, c3, s3 = tw
        x = _dit_blocks(dft, (u0, u1, u2, u3), tw, tp)
        y = []
        for blk in range(4):
            h_re, h_s = h_ref[blk, re, :], h_ref[blk, sn, :]
            x_re, x_s = x[blk]
            y.append((x_re * h_re - x_s * h_s, x_re * h_s + x_s * h_re))
        e_a, o_a = _add(y[0], y[1]), _twiddle(_sub(y[0], y[1]), c2, s2, conj=True)
        e_b, o_b = _add(y[2], y[3]), _twiddle(_sub(y[2], y[3]), c3, s3)
        out = [_add(e_a, e_b), _add(o_a, o_b),
               _twiddle(_sub(e_a, e_b), c1, s1, conj=True), _twiddle(_sub(o_a, o_b), c1, s1, conj=True)]
        for r in range(4):
            y_ref[r, re, :] = out[r][0].astype(BF16)
            y_ref[r, sn, :] = out[r][1].astype(BF16)


def _hy_inv_kernel(f_ref, y_ref, a_ref, u_ref, fb_ref, o_ref, *, rows):
    for r in range(f_ref.shape[0] // rows):
        sl = slice(r * rows, (r + 1) * rows)
        conv = _mm(f_ref[sl, :], y_ref[...])
        z = a_ref[sl, :].astype(F32) * (conv + u_ref[sl, :].astype(F32) * fb_ref[...])
        o_ref[sl, :] = z.astype(o_ref.dtype)


def _hy_out_kernel(x_ref, mod_ref, z_ref, pick_ref, w_ref, b_ref, o_ref):
    z = _mm(pick_ref[0], z_ref[0])
    for r in range(1, z_ref.shape[0]):
        z = z + _mm(pick_ref[r], z_ref[r])
    o_ref[...] = x_ref[...] + mod_ref[5:6, :] * (_mm(z.astype(BF16), w_ref[...]) + b_ref[...])


def _hyena_layer(x, mods, layer, row_fn, g, w_in, b_in, w_short, b_short, w_pos, b_pos, w_mid, b_mid,
                 freq, w_filt, fbias, w_out, b_out):
    bsz, length, d = x.shape
    n_full = 2 * length
    n_ph = DFT_RADIX
    q = length // n_ph
    tm = min(512, length)
    hp = tm // n_ph
    tc = DFT_COL_TILE
    n_ct = d // tc
    width = w_mid.shape[0]
    n_parts = 2 * HYENA_ORDER

    halo = 8
    m_idx = np.arange(hp)
    perm = np.zeros((tm, tm), np.float32)
    pick = np.zeros((n_ph, tm, hp), np.float32)
    for r in range(n_ph):
        perm[r * hp + m_idx, n_ph * m_idx + r] = 1.0
        pick[r, n_ph * m_idx + r, m_idx] = 1.0
    ph_spec = pl.BlockSpec((None, n_ph, hp, d), lambda b, i: (b, 0, i, 0))
    v, x1, x2 = pl.pallas_call(
        _hy_proj_kernel,
        grid=(bsz, length // tm),
        in_specs=[
            pl.BlockSpec((None, tm, d), lambda b, i: (b, i, 0)),
            pl.BlockSpec((None, halo, d), lambda b, i: (b, jnp.maximum(i * (tm // halo) - 1, 0), 0)),
            pl.BlockSpec((None, halo, d),
                         lambda b, i: (b, jnp.minimum((i + 1) * (tm // halo), length // halo - 1), 0)),
            _mod_spec(d, layer, row_fn),
            _const_spec((1, d)),
            _const_spec((tm, tm)),
            _const_spec((d, 3 * d)),
            _const_spec((1, 3 * d)),
            _const_spec((3, 3 * d)),
            _const_spec((1, 3 * d)),
        ],
        out_specs=[ph_spec] * 3,
        out_shape=[jax.ShapeDtypeStruct((bsz, n_ph, q, d), BF16)] * 3,
        compiler_params=_params(2),
        name="hy_proj",
    )(x, x, x, mods, g.reshape(1, d), jnp.asarray(perm, BF16), w_in, b_in.reshape(1, 3 * d),
      w_short, b_short.reshape(1, 3 * d))

    t = jnp.linspace(0.0, 1.0, length, dtype=F32)[:, None]
    bands = jnp.linspace(1e-4, HYENA_BANDS - 1, HYENA_BANDS, dtype=F32)
    ang = (2.0 * math.pi / length) * jnp.arange(length, dtype=F32)[:, None] * bands[None, :]
    feat = jnp.concatenate([t, jnp.cos(ang), -jnp.sin(ang)], axis=-1)
    feat = jnp.pad(feat, ((0, 0), (0, V7X_LANES - HYENA_EMB)))
    w_pos_p = jnp.pad(w_pos.astype(F32), ((0, V7X_LANES - HYENA_EMB), (0, 0)))
    deltas = jnp.abs(jnp.linspace(HYENA_MIN_DECAY, HYENA_MAX_DECAY, d, dtype=F32)).reshape(1, d)
    feat = jnp.concatenate([feat[r::n_ph] for r in range(n_ph)], axis=0)
    tl = min(512, q)
    od = HYENA_ORDER * d
    tap_spec = pl.BlockSpec((tl, od), lambda r, i: (i, r))
    taps_f, taps_b, ss = pl.pallas_call(
        functools.partial(_hy_filter_kernel, d=d, n_orders=HYENA_ORDER),
        grid=(n_ph, q // tl),
        in_specs=[
            pl.BlockSpec((tl, V7X_LANES), lambda r, i: (r * (q // tl) + i, 0)),
            _const_spec((V7X_LANES, width)),
            _const_spec((1, width)),
            _const_spec((width, width)),
            _const_spec((1, width)),
            _const_spec((1, width)),
            _const_spec((width, n_parts * d)),
            _const_spec((1, d)),
        ],
        out_specs=[tap_spec, tap_spec, pl.BlockSpec((1, od), lambda r, i: (0, 0))],
        out_shape=[jax.ShapeDtypeStruct((q, n_ph * od), BF16), jax.ShapeDtypeStruct((q, n_ph * od), BF16),
                   jax.ShapeDtypeStruct((1, od), F32)],
        compiler_params=_params(2),
        name="hy_filter",
    )(feat, w_pos_p, b_pos.reshape(1, width), w_mid, b_mid.reshape(1, width), freq.reshape(1, width),
      w_filt.astype(BF16), deltas)

    tf = min(DFT_FREQ_TILE, q)
    tp = min(DFT_PACK, q)
    tn = min(DFT_TIME_TILE, q)
    dft_fwd, dft_inv = _dft_matrices(q, tp)
    odd = 2.0 * jnp.arange(q, dtype=F32) + 1.0
    angles = jnp.stack([(2.0 * math.pi / n_full) * odd, (math.pi / n_full) * odd,
                        (math.pi / n_full) * (length - odd)])
    tw = jnp.stack([jnp.cos(angles), jnp.sin(angles)], axis=1).reshape(6, q)
    tw = jnp.broadcast_to(tw[:, :, None], (6, q, tc))
    tw_spec = pl.BlockSpec((6, tf, tc), lambda f, *_: (0, f, 0))

    n_oc = od // tc

    def tap_phase(r):
        return pl.BlockSpec((q, tc), lambda f, o, j: (0, r * n_oc + o * n_ct + j))

    spec = pl.pallas_call(
        functools.partial(_hy_spec_kernel, tp=tp, scale=1.0 / length),
        grid=(q // tf, HYENA_ORDER, n_ct),
        in_specs=[pl.BlockSpec((2 * tf, q), lambda f, o, j: (f, 0))]
        + [tap_phase(r) for r in range(n_ph)] * 2
        + [tw_spec, pl.BlockSpec((1, tc), lambda f, o, j: (0, o * n_ct + j))],
        out_specs=pl.BlockSpec((None, n_ph, 2 * tf, tc), lambda f, o, j: (o, 0, f, j)),
        out_shape=jax.ShapeDtypeStruct((HYENA_ORDER, n_ph, 2 * q, d), F32),
        compiler_params=_params(3),
        name="hy_spec",
    )(dft_fwd, *([taps_f] * n_ph), *([taps_b] * n_ph), tw, ss)

    tci = d if q < DFT_FREQ_TILE else 2 * tc
    n_ci = d // tci

    def long_conv(order, u, a):
        yspec = pl.pallas_call(
            functools.partial(_hy_fwd_kernel, tp=tp),
            grid=(q // tf, n_ct, bsz),
            in_specs=[pl.BlockSpec((2 * tf, q), lambda f, j, b: (f, 0))]
            + [pl.BlockSpec((None, None, q, tc), functools.partial(lambda r, f, j, b: (b, r, 0, j), r))
               for r in range(n_ph)]
            + [pl.BlockSpec((None, n_ph, 2 * tf, tc), lambda f, j, b: (order, 0, f, j)), tw_spec],
            out_specs=pl.BlockSpec((None, n_ph, 2 * tf, tc), lambda f, j, b: (b, 0, f, j)),
            out_shape=jax.ShapeDtypeStruct((bsz, n_ph, 2 * q, d), BF16),
            compiler_params=_params(3),
            name="hy_fwd",
        )(dft_fwd, *([u] * n_ph), spec, tw)
        row_spec = pl.BlockSpec((None, None, tn, tci), lambda n, j, b, r: (b, r, n, j))
        return pl.pallas_call(
            functools.partial(_hy_inv_kernel, rows=min(DFT_ROW_GROUP, tn)),
            grid=(q // tn, n_ci, bsz, n_ph),
            in_specs=[
                pl.BlockSpec((tn, 2 * q), lambda n, j, b, r: (n, 0)),
                pl.BlockSpec((None, None, 2 * q, tci), lambda n, j, b, r: (b, r, 0, j)),
                row_spec,
                row_spec,
                pl.BlockSpec((None, 1, tci), lambda n, j, b, r: (order, 0, j)),
            ],
            out_specs=row_spec,
            out_shape=jax.ShapeDtypeStruct((bsz, n_ph, q, d), BF16),
            compiler_params=_params(4),
            name="hy_inv",
        )(dft_inv, yspec, a, u, fbias.astype(F32).reshape(HYENA_ORDER, 1, d))

    z = long_conv(0, v, x1)
    zz = long_conv(1, z, x2)

    return pl.pallas_call(
        _hy_out_kernel,
        grid=(bsz, length // tm),
        in_specs=[
            pl.BlockSpec((None, tm, d), lambda b, i: (b, i, 0)),
            _mod_spec(d, layer, row_fn),
            pl.BlockSpec((None, n_ph, hp, d), lambda b, i: (b, 0, i, 0)),
            _const_spec((n_ph, tm, hp)),
            _const_spec((d, d)),
            _const_spec((1, d)),
        ],
        out_specs=pl.BlockSpec((None, tm, d), lambda b, i: (b, i, 0)),
        out_shape=jax.ShapeDtypeStruct(x.shape, F32),
        compiler_params=_params(2),
        name="hy_out",
    )(x, mods, zz, jnp.asarray(pick, BF16), w_out, b_out.reshape(1, d))


def kernel(x, c, ctx, c_ctx, ada_w, ada_b, norm_g, ffn_w1, ffn_w3, ffn_w2, ret_w_in, ret_w_out, ret_decay,
           pool_w, pool_b, pool_scale, hy_w_in, hy_b_in, hy_w_short, hy_b_short, hy_w_pos, hy_b_pos,
           hy_w_mid, hy_b_mid, hy_freq, hy_w_filt, hy_bias, hy_w_out, hy_b_out, final_g):
    bsz, seq, d = x.shape
    t_ctx = ctx.shape[1]
    depth = ada_w.shape[0]
    n_mixers = 3
    assert bsz + 1 <= MOD_ROWS
    ctx_row = bsz

    cond = jnp.concatenate([c, c_ctx[None, :], jnp.zeros((MOD_ROWS - bsz - 1, d), F32)], axis=0)
    mods = _ada_all(cond, ada_w, ada_b).reshape(depth, MOD_ROWS, ADA_CHUNKS, d)

    w1b, w3b, w2b = ffn_w1.astype(BF16), ffn_w3.astype(BF16), ffn_w2.astype(BF16)
    lat_row = lambda b, i: b
    ctx_flat_row = lambda b, i: ctx_row

    def ffn(h_lat, h_ctx, layer, half, final=False):
        return _half_ffn(h_lat, h_ctx, mods, layer, half, norm_g[layer, 2 * half], w1b, w3b, w2b, final_g,
                         final=final)

    h_lat, h_ctx = x, ctx
    for layer in range(depth):
        kind = layer % n_mixers
        slot = layer // n_mixers
        last = layer == depth - 1
        ctx_out = not last
        ctx_live = ctx_out or kind == 0
        h_lat, h_ctx = ffn(h_lat, h_ctx if ctx_live else None, layer, 0)
        g_mix = norm_g[layer, 1]
        if kind == 0:
            h_ctx, h_lat = _retention_layer(h_ctx, h_lat, mods, layer, g_mix, ret_w_in[slot].astype(BF16),
                                            ret_w_out[slot].astype(BF16), ret_decay[slot], ctx_out)
        elif kind == 1:
            pp = (pool_w[slot], pool_b[slot], pool_scale[slot])
            h_lat = _pool_layer(h_lat, GRID_W, mods, layer, lat_row, g_mix, *pp)
            if ctx_out:
                h_ctx = _pool_layer(h_ctx, t_ctx, mods, layer, ctx_flat_row, g_mix, *pp)
        else:
            hp = (hy_w_in[slot].astype(BF16), hy_b_in[slot], hy_w_short[slot], hy_b_short[slot], hy_w_pos[slot],
                  hy_b_pos[slot], hy_w_mid[slot], hy_b_mid[slot], hy_freq[slot], hy_w_filt[slot],
                  hy_bias[slot], hy_w_out[slot].astype(BF16), hy_b_out[slot])
            h_lat = _hyena_layer(h_lat, mods, layer, lat_row, g_mix, *hp)
            if ctx_out:
                h_ctx = _hyena_layer(h_ctx, mods, layer, ctx_flat_row, g_mix, *hp)
        h_lat, h_ctx = ffn(h_lat, h_ctx if ctx_out else None, layer, 1, final=last)
    return h_lat
```

```python
import functools
import math
from typing import NamedTuple

import numpy as np
import jax
import jax.numpy as jnp
from jax import lax
from jax.experimental import pallas as pl
from jax.experimental.pallas import tpu as pltpu

F32 = jnp.float32
BF16 = jnp.bfloat16

GRID_W = 64
ADA_CHUNKS = 9
NORM_EPS = 1e-6
RET_HEADS = 4
RET_CHUNK = 128
ROPE_BASE = 10000.0
POOL_WINDOWS = (2, 4, 8, 16)
HYENA_ORDER = 2
HYENA_EMB = 33
HYENA_BANDS = (HYENA_EMB - 1) // 2
HYENA_TARGET = 1e-2
HYENA_FAST = 0.3
HYENA_SLOW = 1.5
HYENA_MAX_DECAY = math.log(HYENA_TARGET) / HYENA_FAST
HYENA_MIN_DECAY = math.log(HYENA_TARGET) / HYENA_SLOW

V7X_LANES = 128
V7X_VMEM_LIMIT_BYTES = 56 * 1024 * 1024
MOD_ROWS = 8
POOL_TILE = 256
RET_TILE = 256
SCAN_CHUNK = 256
DFT_RADIX = 4
DFT_FREQ_TILE = 512
DFT_PACK = 256
DFT_ROW_GROUP = 512
DFT_TIME_TILE = 2048
DFT_COL_TILE = 256
HIGHEST = lax.Precision.HIGHEST


def _params(n_axes):
    return pltpu.CompilerParams(
        dimension_semantics=("arbitrary",) * n_axes,
        vmem_limit_bytes=V7X_VMEM_LIMIT_BYTES)


def _const_spec(shape):
    zeros = (0,) * len(shape)
    return pl.BlockSpec(shape, lambda *_: zeros, pipeline_mode=pl.Buffered(1))


def _mod_spec(d, layer, row_fn):
    return pl.BlockSpec((None, None, ADA_CHUNKS, d), lambda *idx: (layer, row_fn(*idx), 0, 0))


class _Rows(NamedTuple):
    bsz: int
    t_lat: int
    t_ctx: int

    @property
    def n_lat(self):
        return self.bsz * self.t_lat

    @property
    def n_all(self):
        return self.bsz * (self.t_lat + self.t_ctx)

    def length(self, part):
        return self.t_lat if part == "lat" else self.t_ctx

    def block(self, part, tile):
        per = self.length(part) // tile
        base = 0 if part == "lat" else self.n_lat // tile
        assert self.length(part) % tile == 0 and self.n_lat % tile == 0
        return lambda b, i: base + b * per + i


def _mm(a, b):
    return jnp.dot(a, b, preferred_element_type=F32)


def _mm_f32(a, b):
    return jnp.dot(a, b, preferred_element_type=F32, precision=HIGHEST)


def _silu(x):
    return x * jax.nn.sigmoid(x)


def _adaln(x, g, shift, scale):
    ms = jnp.mean(x * x, axis=-1, keepdims=True)
    y = x * lax.rsqrt(ms + NORM_EPS) * g
    return y * (1.0 + scale) + shift


def _ada_kernel(c_ref, w_ref, b_ref, o_ref):
    s = _silu(c_ref[...]).astype(BF16)
    o_ref[...] = _mm(s, w_ref[...].astype(BF16)) + b_ref[...]


def _ada_all(cond, ada_w, ada_b):
    depth, d, nd = ada_w.shape
    return pl.pallas_call(
        _ada_kernel,
        grid=(depth, nd // d),
        in_specs=[
            pl.BlockSpec((MOD_ROWS, d), lambda l, j: (0, 0)),
            pl.BlockSpec((None, d, d), lambda l, j: (l, 0, j)),
            pl.BlockSpec((None, 1, d), lambda l, j: (l, 0, j)),
        ],
        out_specs=pl.BlockSpec((None, MOD_ROWS, d), lambda l, j: (l, 0, j)),
        out_shape=jax.ShapeDtypeStruct((depth, MOD_ROWS, nd), F32),
        compiler_params=_params(2),
        name="ada",
    )(cond, ada_w, ada_b.reshape(depth, 1, nd))


def _ffn_kernel(*refs, base, final, n_lat, split_input):
    if split_input:
        xl_ref, xc_ref, mod_ref, g_ref, w1_ref, w3_ref, w2_ref, fg_ref, o_ref = refs
        x = jnp.where(pl.program_id(0) >= n_lat, xc_ref[...], xl_ref[...])
    else:
        x_ref, mod_ref, g_ref, w1_ref, w3_ref, w2_ref, fg_ref, o_ref = refs
        x = x_ref[...]
    u = _adaln(x, g_ref[...], mod_ref[base:base + 1, :], mod_ref[base + 1:base + 2, :]).astype(BF16)
    h1 = _mm(u, w1_ref[...])
    h3 = _mm(u, w3_ref[...])
    a = (_silu(h1) * h3).astype(BF16)
    y = x + (0.5 * mod_ref[base + 2:base + 3, :]) * _mm(a, w2_ref[...])
    if final:
        ms = jnp.mean(y * y, axis=-1, keepdims=True)
        y = y * lax.rsqrt(ms + NORM_EPS) * fg_ref[...]
    o_ref[...] = y


def _half_ffn(h, rows, mods, layer, half, g, w1, w3, w2, final_g, ctx_live, final=False, tm=512):
    d = w1.shape[-2]
    f = w1.shape[-1]
    base = 6 * half
    per_b = rows.t_lat // tm
    n_lat = rows.n_lat // tm
    n_ctx = rows.bsz * rows.t_ctx // tm if ctx_live else 0
    assert rows.t_lat % tm == 0 and (rows.bsz * rows.t_ctx) % tm == 0
    split_input = isinstance(h, tuple)

    def weight_spec(nrows, ncols):
        return pl.BlockSpec((None, None, nrows, ncols), lambda *_: (layer, half, 0, 0),
                            pipeline_mode=pl.Buffered(1))

    if split_input:
        x_specs = [pl.BlockSpec((tm, d), lambda s: (jnp.minimum(s, n_lat - 1), 0)),
                   pl.BlockSpec((tm, d), lambda s: (jnp.maximum(s - n_lat, 0), 0))]
        x_args = list(h)
    else:
        x_specs = [pl.BlockSpec((tm, d), lambda s: (s, 0))]
        x_args = [h]
    return pl.pallas_call(
        functools.partial(_ffn_kernel, base=base, final=final, n_lat=n_lat, split_input=split_input),
        grid=(n_lat + n_ctx,),
        in_specs=x_specs + [
            _mod_spec(d, layer, lambda s: jnp.where(s < n_lat, s // per_b, rows.bsz)),
            _const_spec((1, d)),
            weight_spec(d, f),
            weight_spec(d, f),
            weight_spec(f, d),
            _const_spec((1, d)),
        ],
        out_specs=pl.BlockSpec((tm, d), lambda s: (s, 0)),
        out_shape=jax.ShapeDtypeStruct((rows.n_lat if final else rows.n_all, d), F32),
        compiler_params=_params(1),
        name="half_ffn",
    )(*x_args, mods, g.reshape(1, d), w1, w3, w2, final_g.reshape(1, d))


def _ret_proj_kernel(xc_ref, xl_ref, mod_ref, g_ref, w_ref, cos_ref, sin_ref, q_ref, k_ref, v_ref, gate_ref,
                     *, n_heads, dk, dv, ctx_tiles):
    x = jnp.where(pl.program_id(1) < ctx_tiles, xc_ref[...], xl_ref[...])
    u = _adaln(x, g_ref[...], mod_ref[3:4, :], mod_ref[4:5, :]).astype(BF16)
    p = _mm(u, w_ref[...])
    cos = cos_ref[...]
    sin = sin_ref[...]
    half = dk // 2
    qk = n_heads * dk
    k_scale = dk ** -0.5

    def rot(a):
        a1, a2 = a[:, :half], a[:, half:]
        return jnp.concatenate([a1 * cos - a2 * sin, a2 * cos + a1 * sin], axis=-1)

    for h in range(n_heads):
        q_ref[:, h * dk:(h + 1) * dk] = rot(p[:, h * dk:(h + 1) * dk]).astype(BF16)
        k_ref[:, h * dk:(h + 1) * dk] = (rot(p[:, qk + h * dk:qk + (h + 1) * dk]) * k_scale).astype(BF16)
    v_ref[...] = p[:, 2 * qk:2 * qk + n_heads * dv].astype(BF16)
    gate_ref[...] = p[:, 2 * qk + n_heads * dv:].astype(BF16)


def _ret_scan_kernel(q_ref, k_ref, v_ref, dmat_ref, qdec_ref, kdec_ref, cdec_ref, o_ref,
                     state_ref, fwd_ref, *, n_heads, dk, dv, n_chunks, bwd_chunk):
    s = pl.program_id(1)
    cs = q_ref.shape[0]

    @pl.when((s == 0) | (s == n_chunks))
    def _():
        state_ref[...] = jnp.zeros_like(state_ref)

    chunk = jnp.where(s < n_chunks, s, bwd_chunk(jnp.maximum(s - n_chunks, 0)))
    rows = pl.ds(pl.multiple_of(chunk * cs, cs), cs)
    outs = []
    for h in range(n_heads):
        q = q_ref[:, h * dk:(h + 1) * dk]
        k = k_ref[:, h * dk:(h + 1) * dk]
        v = v_ref[:, h * dv:(h + 1) * dv]
        scores = lax.dot_general(q, k, (((1,), (1,)), ((), ())), preferred_element_type=F32) * dmat_ref[h]
        state = state_ref[h]
        outs.append(_mm(scores.astype(BF16), v) + _mm(q, state.astype(BF16)) * qdec_ref[h])
        kd = (k.astype(F32) * kdec_ref[h]).astype(BF16)
        state_ref[h] = state * cdec_ref[h] + lax.dot_general(
            kd, v, (((0,), (0,)), ((), ())), preferred_element_type=F32)

    @pl.when(s < n_chunks)
    def _():
        for h in range(n_heads):
            fwd_ref[rows, h * dv:(h + 1) * dv] = outs[h].astype(fwd_ref.dtype)

    @pl.when(s >= n_chunks)
    def _():
        for h in range(n_heads):
            o = outs[h] + fwd_ref[rows, h * dv:(h + 1) * dv].astype(F32)
            mu = jnp.mean(o, axis=-1, keepdims=True)
            oc = o - mu
            var = jnp.mean(oc * oc, axis=-1, keepdims=True)
            o_ref[:, h * dv:(h + 1) * dv] = (oc * lax.rsqrt(var + NORM_EPS)).astype(o_ref.dtype)


def _ret_out_kernel(x_ref, mod_ref, y_ref, gate_ref, w_ref, o_ref):
    z = (_silu(gate_ref[...].astype(F32)) * y_ref[...].astype(F32)).astype(BF16)
    o_ref[...] = x_ref[...] + mod_ref[5:6, :] * _mm(z, w_ref[...])


def _retention_layer(h, rows, mods, layer, g, w_in, w_out, decay, ctx_out):
    bsz, t_lat, t_ctx = rows
    d = h.shape[1]
    t_all = t_ctx + t_lat
    n_heads = RET_HEADS
    dk = d // n_heads
    dv = 2 * d // n_heads
    qk = n_heads * dk
    vw = n_heads * dv
    tm = RET_TILE
    assert t_ctx % tm == 0 and t_lat % tm == 0
    ctx_tiles = t_ctx // tm
    ctx_row = bsz
    lat_blk, ctx_blk = rows.block("lat", tm), rows.block("ctx", tm)

    pos = jnp.arange(t_all, dtype=F32)
    inv = 1.0 / (ROPE_BASE ** jnp.linspace(0.0, 1.0, dk // 2, dtype=F32))
    ang = pos[:, None] * inv[None, :]

    q, k, v, gate = pl.pallas_call(
        functools.partial(_ret_proj_kernel, n_heads=n_heads, dk=dk, dv=dv, ctx_tiles=ctx_tiles),
        grid=(bsz, t_all // tm),
        in_specs=[
            pl.BlockSpec((tm, d), lambda b, i: (ctx_blk(b, jnp.minimum(i, ctx_tiles - 1)), 0)),
            pl.BlockSpec((tm, d), lambda b, i: (lat_blk(b, jnp.maximum(i - ctx_tiles, 0)), 0)),
            _mod_spec(d, layer, lambda b, i: jnp.where(i < ctx_tiles, ctx_row, b)),
            _const_spec((1, d)),
            _const_spec((d, 2 * qk + 2 * vw)),
            pl.BlockSpec((tm, dk // 2), lambda b, i: (i, 0)),
            pl.BlockSpec((tm, dk // 2), lambda b, i: (i, 0)),
        ],
        out_specs=[
            pl.BlockSpec((None, tm, qk), lambda b, i: (b, i, 0)),
            pl.BlockSpec((None, tm, qk), lambda b, i: (b, i, 0)),
            pl.BlockSpec((None, tm, vw), lambda b, i: (b, i, 0)),
            pl.BlockSpec((None, tm, vw), lambda b, i: (b, i, 0)),
        ],
        out_shape=[
            jax.ShapeDtypeStruct((bsz, t_all, qk), BF16),
            jax.ShapeDtypeStruct((bsz, t_all, qk), BF16),
            jax.ShapeDtypeStruct((bsz, t_all, vw), BF16),
            jax.ShapeDtypeStruct((bsz, t_all, vw), BF16),
        ],
        compiler_params=_params(2),
        name="ret_proj",
    )(h, h, mods, g.reshape(1, d), w_in, jnp.cos(ang), jnp.sin(ang))

    cs = SCAN_CHUNK
    assert t_ctx % cs == 0 and t_lat % cs == 0
    n_chunks = t_all // cs
    ctx_chunks = t_ctx // cs
    lg = jax.nn.log_sigmoid(decay.astype(F32))[:, :, None]
    n = jnp.arange(cs, dtype=F32)
    rel = n[:, None] - n[None, :]
    dm_f = jnp.where(rel >= 0, jnp.exp(lg[0][:, :, None] * jnp.maximum(rel, 0.0)), 0.0)
    dm_b = jnp.where(rel < 0, jnp.exp(lg[1][:, :, None] * jnp.maximum(-rel, 0.0)), 0.0)
    dmat = jnp.stack([dm_f, dm_b])
    qd = jnp.stack([jnp.exp(lg[0] * (n + 1.0)), jnp.exp(lg[1] * (cs - n))])
    kd = jnp.stack([jnp.exp(lg[0] * (cs - 1.0 - n)), jnp.exp(lg[1] * n)])
    cd = jnp.exp(lg * cs)
    qdec = jnp.broadcast_to(qd[..., None], (2, n_heads, cs, dv))
    kdec = jnp.broadcast_to(kd[..., None], (2, n_heads, cs, dk))
    cdec = jnp.broadcast_to(cd[..., None], (2, n_heads, 1, dv))

    def bwd_chunk(j):
        return jnp.where(j < ctx_chunks, ctx_chunks - 1 - j, n_chunks - 1 + ctx_chunks - j)

    def chunk(s):
        return jnp.where(s < n_chunks, s, bwd_chunk(jnp.maximum(s - n_chunks, 0)))

    def out_chunk(s):
        return bwd_chunk(jnp.maximum(s - n_chunks, 0))

    def table_spec(*shape):
        return pl.BlockSpec((None,) + shape, lambda b, s: (s // n_chunks, 0, 0, 0))

    y = pl.pallas_call(
        functools.partial(_ret_scan_kernel, n_heads=n_heads, dk=dk, dv=dv, n_chunks=n_chunks,
                          bwd_chunk=bwd_chunk),
        grid=(bsz, 2 * n_chunks),
        in_specs=[
            pl.BlockSpec((None, cs, qk), lambda b, s: (b, chunk(s), 0)),
            pl.BlockSpec((None, cs, qk), lambda b, s: (b, chunk(s), 0)),
            pl.BlockSpec((None, cs, vw), lambda b, s: (b, chunk(s), 0)),
            table_spec(n_heads, cs, cs),
            table_spec(n_heads, cs, dv),
            table_spec(n_heads, cs, dk),
            table_spec(n_heads, 1, dv),
        ],
        out_specs=pl.BlockSpec((None, cs, vw), lambda b, s: (b, out_chunk(s), 0)),
        out_shape=jax.ShapeDtypeStruct((bsz, t_all, vw), BF16),
        scratch_shapes=[pltpu.VMEM((n_heads, dk, dv), F32), pltpu.VMEM((t_all, vw), BF16)],
        compiler_params=_params(2),
        name="ret_scan",
    )(q, k, v, dmat, qdec, kdec, cdec)

    first = 0 if ctx_out else ctx_tiles

    def res_blk(b, i):
        j = i + first
        return jnp.where(j < ctx_tiles, ctx_blk(b, jnp.minimum(j, ctx_tiles - 1)),
                         lat_blk(b, jnp.maximum(j - ctx_tiles, 0)))

    return pl.pallas_call(
        _ret_out_kernel,
        grid=(bsz, t_all // tm - first),
        in_specs=[
            pl.BlockSpec((tm, d), lambda b, i: (res_blk(b, i), 0)),
            _mod_spec(d, layer, lambda b, i: jnp.where(i + first < ctx_tiles, ctx_row, b)),
            pl.BlockSpec((None, tm, vw), lambda b, i: (b, i + first, 0)),
            pl.BlockSpec((None, tm, vw), lambda b, i: (b, i + first, 0)),
            _const_spec((vw, d)),
        ],
        out_specs=pl.BlockSpec((tm, d), lambda b, i: (res_blk(b, i), 0)),
        out_shape=jax.ShapeDtypeStruct(h.shape, F32),
        compiler_params=_params(2),
        name="ret_out",
    )(h, mods, y, gate, w_out)


def _pool_tables(seg_len):
    assert POOL_TILE % seg_len == 0
    r = np.arange(POOL_TILE)
    seg, pos = r // seg_len, r % seg_len
    mats, invs = [], []
    for win in POOL_WINDOWS:
        lo = np.clip(pos - win // 2, 0, seg_len)
        hi = np.clip(pos - win // 2 + win, 0, seg_len)
        col_seg, col_pos = seg[None, :], pos[None, :]
        member = (col_seg == seg[:, None]) & (col_pos >= lo[:, None]) & (col_pos < hi[:, None])
        mats.append(member.astype(np.float32))
        invs.append((1.0 / (hi - lo)).astype(np.float32))
    return np.stack(mats), np.stack(invs)


def _pool_kernel(x_ref, mod_ref, g_ref, s_ref, inv_ref, w_ref, b_ref, sc_ref, o_ref, *, n_groups, dg):
    x = x_ref[...]
    u = _adaln(x, g_ref[...], mod_ref[3:4, :], mod_ref[4:5, :])
    gate = mod_ref[5:6, :]
    for gi in range(n_groups):
        sl = slice(gi * dg, (gi + 1) * dg)
        ug = u[:, sl]
        hi = ug.astype(BF16)
        lo = (ug - hi.astype(F32)).astype(BF16)
        win_sum = _mm(s_ref[gi], hi) + _mm(s_ref[gi], lo)
        dlt = win_sum * inv_ref[gi] - ug
        yg = _mm(dlt.astype(BF16), w_ref[gi])
        o_ref[:, sl] = x[:, sl] + gate[:, sl] * ((yg + b_ref[:, sl]) * sc_ref[:, sl])


def _pool_layer(h, rows, lat_seg, mods, layer, g, w_grp, b_grp, scale, ctx_out):
    bsz, t_lat, t_ctx = rows
    d = h.shape[1]
    n_groups = len(POOL_WINDOWS)
    dg = d // n_groups
    tabs = [_pool_tables(lat_seg), _pool_tables(t_ctx)]
    s_tab = jnp.asarray(np.stack([m for m, _ in tabs]), BF16)
    inv_tab = jnp.asarray(np.stack([np.broadcast_to(v[:, :, None], (n_groups, POOL_TILE, dg)) for _, v in tabs]), F32)
    lat_tiles = t_lat // POOL_TILE
    ctx_tiles = t_ctx // POOL_TILE if ctx_out else 0
    lat_blk, ctx_blk = rows.block("lat", POOL_TILE), rows.block("ctx", POOL_TILE)

    def res_blk(b, i):
        return jnp.where(i < lat_tiles, lat_blk(b, jnp.minimum(i, lat_tiles - 1)),
                         ctx_blk(b, jnp.maximum(i - lat_tiles, 0)))

    def table_spec(*shape):
        return pl.BlockSpec((None,) + shape, lambda b, i: (jnp.where(i < lat_tiles, 0, 1), 0, 0, 0))

    return pl.pallas_call(
        functools.partial(_pool_kernel, n_groups=n_groups, dg=dg),
        grid=(bsz, lat_tiles + ctx_tiles),
        in_specs=[
            pl.BlockSpec((POOL_TILE, d), lambda b, i: (res_blk(b, i), 0)),
            _mod_spec(d, layer, lambda b, i: jnp.where(i < lat_tiles, b, bsz)),
            _const_spec((1, d)),
            table_spec(n_groups, POOL_TILE, POOL_TILE),
            table_spec(n_groups, POOL_TILE, dg),
            _const_spec((n_groups, dg, dg)),
            _const_spec((1, d)),
            _const_spec((1, d)),
        ],
        out_specs=pl.BlockSpec((POOL_TILE, d), lambda b, i: (res_blk(b, i), 0)),
        out_shape=jax.ShapeDtypeStruct(h.shape, F32),
        compiler_params=_params(2),
        name="pool",
    )(h, mods, g.reshape(1, d), s_tab, inv_tab, w_grp.astype(BF16), b_grp.reshape(1, d), scale.reshape(1, d))


def _dft_tables(n_in):
    mod = 4 * n_in
    sub_n = min(V7X_LANES, n_in)
    idx = jnp.arange(n_in, dtype=jnp.int32)[None, :]
    blk = jnp.arange(n_in // sub_n, dtype=jnp.int32)[:, None]
    sub = jnp.arange(sub_n, dtype=jnp.int32)[:, None]
    theta = 2.0 * math.pi / mod

    def cs(p):
        ph = (p % mod).astype(F32) * theta
        return jnp.cos(ph), jnp.sin(ph)

    fwd = cs(2 * sub_n * blk * idx) + cs((2 * sub + 1) * idx)
    inv = cs(sub_n * blk * (2 * idx + 1)) + cs(sub * (2 * idx + 1))
    return fwd, inv


def _dft_fwd_gen_kernel(pc_ref, ps_ref, qc_ref, qs_ref, o_ref, *, tf):
    ft = pl.program_id(0)
    sub_n = qc_ref.shape[0]
    blocks = tf // sub_n
    qc = qc_ref[...]
    qs = qs_ref[...]
    for jb in range(blocks):
        kb = ft * blocks + jb
        pc = pc_ref[pl.ds(kb, 1), :]
        ps = ps_ref[pl.ds(kb, 1), :]
        o_ref[jb * sub_n:(jb + 1) * sub_n, :] = (pc * qc - ps * qs).astype(BF16)
        o_ref[tf + jb * sub_n:tf + (jb + 1) * sub_n, :] = (ps * qc + pc * qs).astype(BF16)


def _dft_inv_gen_kernel(pc_ref, ps_ref, qc_ref, qs_ref, o_ref, *, tf, n_in):
    mb = pl.program_id(0)
    pc = pc_ref[pl.ds(mb, 1), :]
    ps = ps_ref[pl.ds(mb, 1), :]
    qc = qc_ref[...]
    qs = qs_ref[...]
    cos_all = pc * qc - ps * qs
    sin_all = ps * qc + pc * qs
    for ft in range(n_in // tf):
        o_ref[:, 2 * ft * tf:(2 * ft + 1) * tf] = cos_all[:, ft * tf:(ft + 1) * tf].astype(BF16)
        o_ref[:, (2 * ft + 1) * tf:(2 * ft + 2) * tf] = sin_all[:, ft * tf:(ft + 1) * tf].astype(BF16)


def _dft_matrices(n_in, tf):
    fwd_tabs, inv_tabs = _dft_tables(n_in)
    sub_n = min(V7X_LANES, n_in)
    nb = n_in // sub_n
    tabs = [_const_spec((nb, n_in)), _const_spec((nb, n_in)),
            _const_spec((sub_n, n_in)), _const_spec((sub_n, n_in))]
    fwd = pl.pallas_call(
        functools.partial(_dft_fwd_gen_kernel, tf=tf),
        grid=(n_in // tf,),
        in_specs=tabs,
        out_specs=pl.BlockSpec((2 * tf, n_in), lambda i: (i, 0)),
        out_shape=jax.ShapeDtypeStruct((2 * n_in, n_in), BF16),
        compiler_params=_params(1),
        name="dft_fwd_gen",
    )(*fwd_tabs)
    inv = pl.pallas_call(
        functools.partial(_dft_inv_gen_kernel, tf=tf, n_in=n_in),
        grid=(nb,),
        in_specs=tabs,
        out_specs=pl.BlockSpec((sub_n, 2 * n_in), lambda i: (i, 0)),
        out_shape=jax.ShapeDtypeStruct((n_in, 2 * n_in), BF16),
        compiler_params=_params(1),
        name="dft_inv_gen",
    )(*inv_tabs)
    return fwd, inv


def _hy_proj_kernel(x_ref, xp_ref, xn_ref, mod_ref, g_ref, perm_ref, w_ref, b_ref, ws_ref, bs_ref,
                    v_ref, x1_ref, x2_ref):
    i = pl.program_id(1)
    tm, d = x_ref.shape
    n_ph = v_ref.shape[0]
    hp = tm // n_ph
    g = g_ref[...]
    shift, scale = mod_ref[3:4, :], mod_ref[4:5, :]
    u = _adaln(x_ref[...], g, shift, scale).astype(BF16)
    u = _mm(perm_ref[...], u).astype(BF16)
    halo = jnp.concatenate([xp_ref[...], xn_ref[...]], axis=0)
    uh = _adaln(halo, g, shift, scale).astype(BF16)
    p = _mm(jnp.concatenate([u, uh], axis=0), w_ref[...]) + b_ref[...]
    ph = [p[r * hp:(r + 1) * hp] for r in range(n_ph)]
    n_halo = xp_ref.shape[0]
    before = jnp.where(i == 0, 0.0, p[tm + n_halo - 1:tm + n_halo])
    after = jnp.where(i == pl.num_programs(1) - 1, 0.0, p[tm + n_halo:tm + n_halo + 1])
    row = lax.broadcasted_iota(jnp.int32, ph[0].shape, 0)
    last_prev = jnp.where(row == 0, before, pltpu.roll(ph[-1], 1, axis=0))
    first_next = jnp.where(row == hp - 1, after, pltpu.roll(ph[0], hp - 1, axis=0))
    w0, w1, w2, bs = ws_ref[0:1, :], ws_ref[1:2, :], ws_ref[2:3, :], bs_ref[...]
    for r in range(n_ph):
        prev = ph[r - 1] if r > 0 else last_prev
        nxt = ph[r + 1] if r < n_ph - 1 else first_next
        conv = prev * w0 + ph[r] * w1 + nxt * w2 + bs
        v_ref[r] = conv[:, :d].astype(BF16)
        x1_ref[r] = conv[:, d:2 * d].astype(BF16)
        x2_ref[r] = conv[:, 2 * d:].astype(BF16)


def _hy_filter_kernel(feat_ref, wp_ref, bp_ref, wm_ref, bm_ref, fr_ref, wf_ref, dl_ref,
                      hf_ref, hb_ref, ss_ref, *, d, n_orders):
    start = (pl.program_id(0) == 0) & (pl.program_id(1) == 0)
    feat = feat_ref[...]
    fr = fr_ref[...]
    hdn = jnp.sin(fr * (_mm_f32(feat, wp_ref[...]) + bp_ref[...]))
    hdn = jnp.sin(fr * (_mm_f32(hdn, wm_ref[...]) + bm_ref[...]))
    h = _mm(hdn.astype(BF16), wf_ref[...])
    decay = jnp.exp(-feat[:, 0:1] * dl_ref[...])
    row = lax.broadcasted_iota(jnp.int32, decay.shape, 0)
    first = (row == 0) & start

    @pl.when(start)
    def _():
        ss_ref[...] = jnp.zeros_like(ss_ref)

    for o in range(n_orders):
        sl = slice(o * d, (o + 1) * d)
        hf = h[:, 2 * o * d:(2 * o + 1) * d] * decay
        hb = jnp.where(first, 0.0, h[:, (2 * o + 1) * d:(2 * o + 2) * d] * decay)
        hf_ref[:, sl] = hf.astype(BF16)
        hb_ref[:, sl] = hb.astype(BF16)
        ss_ref[:, sl] += jnp.sum(hf * hf + hb * hb, axis=0, keepdims=True)


def _twiddle(a, c, s, conj=False):
    a_re, a_s = a
    if conj:
        return a_re * c + a_s * s, a_s * c - a_re * s
    return a_re * c - a_s * s, a_s * c + a_re * s


def _add(a, b):
    return a[0] + b[0], a[1] + b[1]


def _sub(a, b):
    return a[0] - b[0], a[1] - b[1]


def _dit_blocks(dft, phase_refs, tw, tp):
    ph = []
    for ref in phase_refs:
        t = _mm(dft, ref[...])
        ph.append((t[0:tp], t[tp:]))
    c1, s1, c2, s2, c3, s3 = tw
    t2, t3 = _twiddle(ph[2], c1, s1), _twiddle(ph[3], c1, s1)
    e_a, e_b = _add(ph[0], t2), _sub(ph[0], t2)
    o_a, o_b = _add(ph[1], t3), _sub(ph[1], t3)
    ta, tb = _twiddle(o_a, c2, s2), _twiddle(o_b, c3, s3, conj=True)
    return [_add(e_a, ta), _sub(e_a, ta), _add(e_b, tb), _sub(e_b, tb)]


def _bin_group(f_ref, tw_ref, g, tp):
    re, sn = slice(2 * g * tp, (2 * g + 1) * tp), slice((2 * g + 1) * tp, (2 * g + 2) * tp)
    dft = f_ref[2 * g * tp:(2 * g + 2) * tp, :]
    tw = [tw_ref[i, g * tp:(g + 1) * tp, :] for i in range(6)]
    return re, sn, dft, tw


def _hy_spec_kernel(f_ref, f0, f1, f2, f3, b0, b1, b2, b3, tw_ref, ss_ref, o_ref, *, tp, scale):
    wgt = scale * lax.rsqrt(ss_ref[...] + NORM_EPS)
    for g in range(tw_ref.shape[1] // tp):
        re, sn, dft, tw = _bin_group(f_ref, tw_ref, g, tp)
        fwd = _dit_blocks(dft, (f0, f1, f2, f3), tw, tp)
        bwd = _dit_blocks(dft, (b0, b1, b2, b3), tw, tp)
        for blk in range(4):
            o_ref[blk, re, :] = (fwd[blk][0] + bwd[blk][0]) * wgt
            o_ref[blk, sn, :] = (fwd[blk][1] - bwd[blk][1]) * wgt


def _hy_fwd_kernel(f_ref, u0, u1, u2, u3, h_ref, tw_ref, y_ref, *, tp):
    for g in range(tw_ref.shape[1] // tp):
        re, sn, dft, tw = _bin_group(f_ref, tw_ref, g, tp)
        c1, s1, c2, s2, c3, s3 = tw
        x = _dit_blocks(dft, (u0, u1, u2, u3), tw, tp)
        y = []
        for blk in range(4):
            h_re, h_s = h_ref[blk, re, :], h_ref[blk, sn, :]
            x_re, x_s = x[blk]
            y.append((x_re * h_re - x_s * h_s, x_re * h_s + x_s * h_re))
        e_a, o_a = _add(y[0], y[1]), _twiddle(_sub(y[0], y[1]), c2, s2, conj=True)
        e_b, o_b = _add(y[2], y[3]), _twiddle(_sub(y[2], y[3]), c3, s3)
        out = [_add(e_a, e_b), _add(o_a, o_b),
               _twiddle(_sub(e_a, e_b), c1, s1, conj=True), _twiddle(_sub(o_a, o_b), c1, s1, conj=True)]
        for r in range(4):
            y_ref[r, re, :] = out[r][0].astype(BF16)
            y_ref[r, sn, :] = out[r][1].astype(BF16)


def _hy_inv_kernel(f_ref, y_ref, a_ref, u_ref, fb_ref, o_ref, *, rows):
    for r in range(f_ref.shape[0] // rows):
        sl = slice(r * rows, (r + 1) * rows)
        conv = _mm(f_ref[sl, :], y_ref[...])
        z = a_ref[sl, :].astype(F32) * (conv + u_ref[sl, :].astype(F32) * fb_ref[...])
        o_ref[sl, :] = z.astype(o_ref.dtype)


def _hy_out_kernel(x_ref, mod_ref, z_ref, pick_ref, w_ref, b_ref, *refs):
    o_ref = refs[-1]
    z = _mm(pick_ref[0], z_ref[0])
    for r in range(1, z_ref.shape[0]):
        z = z + _mm(pick_ref[r], z_ref[r])
    o_ref[...] = x_ref[...] + mod_ref[5:6, :] * (_mm(z.astype(BF16), w_ref[...]) + b_ref[...])


def _hyena_layer(h, rows, part, out_buf, mods, layer, g, w_in, b_in, w_short, b_short, w_pos, b_pos, w_mid,
                 b_mid, freq, w_filt, fbias, w_out, b_out):
    bsz = rows.bsz
    length = rows.length(part)
    d = h.shape[1]
    mod_row = (lambda b, i: b) if part == "lat" else (lambda b, i: bsz)
    n_full = 2 * length
    n_ph = DFT_RADIX
    q = length // n_ph
    tm = min(512, length)
    hp = tm // n_ph
    tc = DFT_COL_TILE
    n_ct = d // tc
    width = w_mid.shape[0]
    n_parts = 2 * HYENA_ORDER

    halo = 8
    m_idx = np.arange(hp)
    perm = np.zeros((tm, tm), np.float32)
    pick = np.zeros((n_ph, tm, hp), np.float32)
    for r in range(n_ph):
        perm[r * hp + m_idx, n_ph * m_idx + r] = 1.0
        pick[r, n_ph * m_idx + r, m_idx] = 1.0
    ph_spec = pl.BlockSpec((None, n_ph, hp, d), lambda b, i: (b, 0, i, 0))
    tile_blk, halo_blk = rows.block(part, tm), rows.block(part, halo)
    v, x1, x2 = pl.pallas_call(
        _hy_proj_kernel,
        grid=(bsz, length // tm),
        in_specs=[
            pl.BlockSpec((tm, d), lambda b, i: (tile_blk(b, i), 0)),
            pl.BlockSpec((halo, d), lambda b, i: (halo_blk(b, jnp.maximum(i * (tm // halo) - 1, 0)), 0)),
            pl.BlockSpec((halo, d),
                         lambda b, i: (halo_blk(b, jnp.minimum((i + 1) * (tm // halo), length // halo - 1)), 0)),
            _mod_spec(d, layer, mod_row),
            _const_spec((1, d)),
            _const_spec((tm, tm)),
            _const_spec((d, 3 * d)),
            _const_spec((1, 3 * d)),
            _const_spec((3, 3 * d)),
            _const_spec((1, 3 * d)),
        ],
        out_specs=[ph_spec] * 3,
        out_shape=[jax.ShapeDtypeStruct((bsz, n_ph, q, d), BF16)] * 3,
        compiler_params=_params(2),
        name="hy_proj",
    )(h, h, h, mods, g.reshape(1, d), jnp.asarray(perm, BF16), w_in, b_in.reshape(1, 3 * d),
      w_short, b_short.reshape(1, 3 * d))

    t = jnp.linspace(0.0, 1.0, length, dtype=F32)[:, None]
    bands = jnp.linspace(1e-4, HYENA_BANDS - 1, HYENA_BANDS, dtype=F32)
    ang = (2.0 * math.pi / length) * jnp.arange(length, dtype=F32)[:, None] * bands[None, :]
    feat = jnp.concatenate([t, jnp.cos(ang), -jnp.sin(ang)], axis=-1)
    feat = jnp.pad(feat, ((0, 0), (0, V7X_LANES - HYENA_EMB)))
    w_pos_p = jnp.pad(w_pos.astype(F32), ((0, V7X_LANES - HYENA_EMB), (0, 0)))
    deltas = jnp.abs(jnp.linspace(HYENA_MIN_DECAY, HYENA_MAX_DECAY, d, dtype=F32)).reshape(1, d)
    feat = jnp.concatenate([feat[r::n_ph] for r in range(n_ph)], axis=0)
    tl = min(512, q)
    od = HYENA_ORDER * d
    tap_spec = pl.BlockSpec((tl, od), lambda r, i: (i, r))
    taps_f, taps_b, ss = pl.pallas_call(
        functools.partial(_hy_filter_kernel, d=d, n_orders=HYENA_ORDER),
        grid=(n_ph, q // tl),
        in_specs=[
            pl.BlockSpec((tl, V7X_LANES), lambda r, i: (r * (q // tl) + i, 0)),
            _const_spec((V7X_LANES, width)),
            _const_spec((1, width)),
            _const_spec((width, width)),
            _const_spec((1, width)),
            _const_spec((1, width)),
            _const_spec((width, n_parts * d)),
            _const_spec((1, d)),
        ],
        out_specs=[tap_spec, tap_spec, pl.BlockSpec((1, od), lambda r, i: (0, 0))],
        out_shape=[jax.ShapeDtypeStruct((q, n_ph * od), BF16), jax.ShapeDtypeStruct((q, n_ph * od), BF16),
                   jax.ShapeDtypeStruct((1, od), F32)],
        compiler_params=_params(2),
        name="hy_filter",
    )(feat, w_pos_p, b_pos.reshape(1, width), w_mid, b_mid.reshape(1, width), freq.reshape(1, width),
      w_filt.astype(BF16), deltas)

    tf = min(DFT_FREQ_TILE, q)
    tp = min(DFT_PACK, q)
    tn = min(DFT_TIME_TILE, q)
    dft_fwd, dft_inv = _dft_matrices(q, tp)
    odd = 2.0 * jnp.arange(q, dtype=F32) + 1.0
    angles = jnp.stack([(2.0 * math.pi / n_full) * odd, (math.pi / n_full) * odd,
                        (math.pi / n_full) * (length - odd)])
    tw = jnp.stack([jnp.cos(angles), jnp.sin(angles)], axis=1).reshape(6, q)
    tw = jnp.broadcast_to(tw[:, :, None], (6, q, tc))
    tw_spec = pl.BlockSpec((6, tf, tc), lambda f, *_: (0, f, 0))

    n_oc = od // tc

    def tap_phase(r):
        return pl.BlockSpec((q, tc), lambda f, o, j: (0, r * n_oc + o * n_ct + j))

    spec = pl.pallas_call(
        functools.partial(_hy_spec_kernel, tp=tp, scale=1.0 / length),
        grid=(q // tf, HYENA_ORDER, n_ct),
        in_specs=[pl.BlockSpec((2 * tf, q), lambda f, o, j: (f, 0))]
        + [tap_phase(r) for r in range(n_ph)] * 2
        + [tw_spec, pl.BlockSpec((1, tc), lambda f, o, j: (0, o * n_ct + j))],
        out_specs=pl.BlockSpec((None, n_ph, 2 * tf, tc), lambda f, o, j: (o, 0, f, j)),
        out_shape=jax.ShapeDtypeStruct((HYENA_ORDER, n_ph, 2 * q, d), F32),
        compiler_params=_params(3),
        name="hy_spec",
    )(dft_fwd, *([taps_f] * n_ph), *([taps_b] * n_ph), tw, ss)

    tci = d if q < DFT_FREQ_TILE else 2 * tc
    n_ci = d // tci

    def long_conv(order, u, a):
        yspec = pl.pallas_call(
            functools.partial(_hy_fwd_kernel, tp=tp),
            grid=(q // tf, n_ct, bsz),
            in_specs=[pl.BlockSpec((2 * tf, q), lambda f, j, b: (f, 0))]
            + [pl.BlockSpec((None, None, q, tc), functools.partial(lambda r, f, j, b: (b, r, 0, j), r))
               for r in range(n_ph)]
            + [pl.BlockSpec((None, n_ph, 2 * tf, tc), lambda f, j, b: (order, 0, f, j)), tw_spec],
            out_specs=pl.BlockSpec((None, n_ph, 2 * tf, tc), lambda f, j, b: (b, 0, f, j)),
            out_shape=jax.ShapeDtypeStruct((bsz, n_ph, 2 * q, d), BF16),
            compiler_params=_params(3),
            name="hy_fwd",
        )(dft_fwd, *([u] * n_ph), spec, tw)
        row_spec = pl.BlockSpec((None, None, tn, tci), lambda n, j, b, r: (b, r, n, j))
        return pl.pallas_call(
            functools.partial(_hy_inv_kernel, rows=min(DFT_ROW_GROUP, tn)),
            grid=(q // tn, n_ci, bsz, n_ph),
            in_specs=[
                pl.BlockSpec((tn, 2 * q), lambda n, j, b, r: (n, 0)),
                pl.BlockSpec((None, None, 2 * q, tci), lambda n, j, b, r: (b, r, 0, j)),
                row_spec,
                row_spec,
                pl.BlockSpec((None, 1, tci), lambda n, j, b, r: (order, 0, j)),
            ],
            out_specs=row_spec,
            out_shape=jax.ShapeDtypeStruct((bsz, n_ph, q, d), BF16),
            compiler_params=_params(4),
            name="hy_inv",
        )(dft_inv, yspec, a, u, fbias.astype(F32).reshape(HYENA_ORDER, 1, d))

    z = long_conv(0, v, x1)
    zz = long_conv(1, z, x2)

    in_specs = [
        pl.BlockSpec((tm, d), lambda b, i: (tile_blk(b, i), 0)),
        _mod_spec(d, layer, mod_row),
        pl.BlockSpec((None, n_ph, hp, d), lambda b, i: (b, 0, i, 0)),
        _const_spec((n_ph, tm, hp)),
        _const_spec((d, d)),
        _const_spec((1, d)),
    ]
    args = [h, mods, zz, jnp.asarray(pick, BF16), w_out, b_out.reshape(1, d)]
    aliases = {}
    if out_buf is not None:
        in_specs.append(pl.BlockSpec(memory_space=pl.ANY))
        args.append(out_buf)
        aliases = {len(args) - 1: 0}
    return pl.pallas_call(
        _hy_out_kernel,
        grid=(bsz, length // tm),
        in_specs=in_specs,
        out_specs=pl.BlockSpec((tm, d), lambda b, i: (tile_blk(b, i), 0)),
        out_shape=jax.ShapeDtypeStruct(h.shape, F32),
        input_output_aliases=aliases,
        compiler_params=_params(2),
        name="hy_out",
    )(*args)


def kernel(x, c, ctx, c_ctx, ada_w, ada_b, norm_g, ffn_w1, ffn_w3, ffn_w2, ret_w_in, ret_w_out, ret_decay,
           pool_w, pool_b, pool_scale, hy_w_in, hy_b_in, hy_w_short, hy_b_short, hy_w_pos, hy_b_pos,
           hy_w_mid, hy_b_mid, hy_freq, hy_w_filt, hy_bias, hy_w_out, hy_b_out, final_g):
    bsz, seq, d = x.shape
    t_ctx = ctx.shape[1]
    depth = ada_w.shape[0]
    n_mixers = 3
    assert bsz + 1 <= MOD_ROWS
    ctx_row = bsz

    cond = jnp.concatenate([c, c_ctx[None, :], jnp.zeros((MOD_ROWS - bsz - 1, d), F32)], axis=0)
    mods = _ada_all(cond, ada_w, ada_b).reshape(depth, MOD_ROWS, ADA_CHUNKS, d)

    w1b, w3b, w2b = ffn_w1.astype(BF16), ffn_w3.astype(BF16), ffn_w2.astype(BF16)
    rows = _Rows(bsz, seq, t_ctx)

    def ffn(h, layer, half, ctx_live, final=False):
        return _half_ffn(h, rows, mods, layer, half, norm_g[layer, 2 * half], w1b, w3b, w2b, final_g,
                         ctx_live, final=final)

    h = (x.reshape(bsz * seq, d), ctx.reshape(bsz * t_ctx, d))
    for layer in range(depth):
        kind = layer % n_mixers
        slot = layer // n_mixers
        last = layer == depth - 1
        ctx_out = not last
        ctx_live = ctx_out or kind == 0
        h = ffn(h, layer, 0, ctx_live)
        g_mix = norm_g[layer, 1]
        if kind == 0:
            h = _retention_layer(h, rows, mods, layer, g_mix, ret_w_in[slot].astype(BF16),
                                 ret_w_out[slot].astype(BF16), ret_decay[slot], ctx_out)
        elif kind == 1:
            h = _pool_layer(h, rows, GRID_W, mods, layer, g_mix, pool_w[slot], pool_b[slot], pool_scale[slot],
                            ctx_out)
        else:
            hp = (hy_w_in[slot].astype(BF16), hy_b_in[slot], hy_w_short[slot], hy_b_short[slot], hy_w_pos[slot],
                  hy_b_pos[slot], hy_w_mid[slot], hy_b_mid[slot], hy_freq[slot], hy_w_filt[slot],
                  hy_bias[slot], hy_w_out[slot].astype(BF16), hy_b_out[slot])
            new_h = _hyena_layer(h, rows, "lat", None, mods, layer, g_mix, *hp)
            if ctx_out:
                new_h = _hyena_layer(h, rows, "ctx", new_h, mods, layer, g_mix, *hp)
            h = new_h
        h = ffn(h, layer, 1, ctx_out, final=last)
    return h.reshape(bsz, seq, d)
```

```python
import functools
import math
from typing import NamedTuple

import numpy as np
import jax
import jax.numpy as jnp
from jax import lax
from jax.experimental import pallas as pl
from jax.experimental.pallas import tpu as pltpu

F32 = jnp.float32
BF16 = jnp.bfloat16

GRID_W = 64
ADA_CHUNKS = 9
NORM_EPS = 1e-6
RET_HEADS = 4
RET_CHUNK = 128
ROPE_BASE = 10000.0
POOL_WINDOWS = (2, 4, 8, 16)
HYENA_ORDER = 2
HYENA_EMB = 33
HYENA_BANDS = (HYENA_EMB - 1) // 2
HYENA_TARGET = 1e-2
HYENA_FAST = 0.3
HYENA_SLOW = 1.5
HYENA_MAX_DECAY = math.log(HYENA_TARGET) / HYENA_FAST
HYENA_MIN_DECAY = math.log(HYENA_TARGET) / HYENA_SLOW

V7X_LANES = 128
V7X_VMEM_LIMIT_BYTES = 56 * 1024 * 1024
MOD_ROWS = 8
FFN_CAST_SLABS = 16
POOL_TILE = 256
RET_TILE = 256
SCAN_CHUNK = 256
DFT_RADIX = 4
DFT_FREQ_TILE = 512
DFT_PACK = 256
DFT_ROW_GROUP = 512
DFT_TIME_TILE = 2048
DFT_COL_TILE = 256
HIGHEST = lax.Precision.HIGHEST


def _params(n_axes):
    return pltpu.CompilerParams(
        dimension_semantics=("arbitrary",) * n_axes,
        vmem_limit_bytes=V7X_VMEM_LIMIT_BYTES)


def _const_spec(shape):
    zeros = (0,) * len(shape)
    return pl.BlockSpec(shape, lambda *_: zeros, pipeline_mode=pl.Buffered(1))


def _mod_spec(d, layer, row_fn):
    return pl.BlockSpec((None, None, ADA_CHUNKS, d), lambda *idx: (layer, row_fn(*idx), 0, 0))


class _Rows(NamedTuple):
    bsz: int
    t_lat: int
    t_ctx: int

    @property
    def n_lat(self):
        return self.bsz * self.t_lat

    @property
    def n_all(self):
        return self.bsz * (self.t_lat + self.t_ctx)

    def length(self, part):
        return self.t_lat if part == "lat" else self.t_ctx

    def block(self, part, tile):
        per = self.length(part) // tile
        base = 0 if part == "lat" else self.n_lat // tile
        assert self.length(part) % tile == 0 and self.n_lat % tile == 0
        return lambda b, i: base + b * per + i


def _mm(a, b):
    return jnp.dot(a, b, preferred_element_type=F32)


def _mm_f32(a, b):
    return jnp.dot(a, b, preferred_element_type=F32, precision=HIGHEST)


def _silu(x):
    return x * jax.nn.sigmoid(x)


def _adaln(x, g, shift, scale):
    ms = jnp.mean(x * x, axis=-1, keepdims=True)
    y = x * lax.rsqrt(ms + NORM_EPS) * g
    return y * (1.0 + scale) + shift


def _ada_kernel(c_ref, w_ref, b_ref, o_ref):
    s = _silu(c_ref[...]).astype(BF16)
    o_ref[...] = _mm(s, w_ref[...].astype(BF16)) + b_ref[...]


def _ada_all(cond, ada_w, ada_b):
    depth, d, nd = ada_w.shape
    return pl.pallas_call(
        _ada_kernel,
        grid=(depth, nd // d),
        in_specs=[
            pl.BlockSpec((MOD_ROWS, d), lambda l, j: (0, 0)),
            pl.BlockSpec((None, d, d), lambda l, j: (l, 0, j)),
            pl.BlockSpec((None, 1, d), lambda l, j: (l, 0, j)),
        ],
        out_specs=pl.BlockSpec((None, MOD_ROWS, d), lambda l, j: (l, 0, j)),
        out_shape=jax.ShapeDtypeStruct((depth, MOD_ROWS, nd), F32),
        compiler_params=_params(2),
        name="ada",
    )(cond, ada_w, ada_b.reshape(depth, 1, nd))


def _ffn_kernel(*refs, base, final, n_lat, split_input, n_cast):
    n_x = 2 if split_input else 1
    x_refs, refs = refs[:n_x], refs[n_x:]
    mod_ref, g_ref, w1_ref, w3_ref, w2_ref, fg_ref = refs[:6]
    rest = refs[6:]
    if n_cast:
        src_refs, o_ref, dst_refs = rest[:3], rest[3], rest[4:]

        @pl.when(pl.program_id(0) < n_cast)
        def _():
            for src, dst in zip(src_refs, dst_refs):
                dst[...] = src[...].astype(BF16)
    else:
        (o_ref,) = rest
    if split_input:
        x = jnp.where(pl.program_id(0) >= n_lat, x_refs[1][...], x_refs[0][...])
    else:
        x = x_refs[0][...]
    u = _adaln(x, g_ref[...], mod_ref[base:base + 1, :], mod_ref[base + 1:base + 2, :]).astype(BF16)
    h1 = _mm(u, w1_ref[...])
    h3 = _mm(u, w3_ref[...])
    a = (_silu(h1) * h3).astype(BF16)
    y = x + (0.5 * mod_ref[base + 2:base + 3, :]) * _mm(a, w2_ref[...])
    if final:
        ms = jnp.mean(y * y, axis=-1, keepdims=True)
        y = y * lax.rsqrt(ms + NORM_EPS) * fg_ref[...]
    o_ref[...] = y


def _half_ffn(h, rows, mods, layer, half, g, weights, final_g, ctx_live, next_weights=None, final=False, tm=512):
    w1, w3, w2 = weights
    d, f = w1.shape
    base = 6 * half
    per_b = rows.t_lat // tm
    n_lat = rows.n_lat // tm
    n_ctx = rows.bsz * rows.t_ctx // tm if ctx_live else 0
    assert rows.t_lat % tm == 0 and (rows.bsz * rows.t_ctx) % tm == 0
    split_input = isinstance(h, tuple)

    if split_input:
        x_specs = [pl.BlockSpec((tm, d), lambda s: (jnp.minimum(s, n_lat - 1), 0)),
                   pl.BlockSpec((tm, d), lambda s: (jnp.maximum(s - n_lat, 0), 0))]
        x_args = list(h)
    else:
        x_specs = [pl.BlockSpec((tm, d), lambda s: (s, 0))]
        x_args = [h]
    in_specs = x_specs + [
        _mod_spec(d, layer, lambda s: jnp.where(s < n_lat, s // per_b, rows.bsz)),
        _const_spec((1, d)),
        _const_spec((d, f)),
        _const_spec((d, f)),
        _const_spec((f, d)),
        _const_spec((1, d)),
    ]
    args = x_args + [mods, g.reshape(1, d), w1, w3, w2, final_g.reshape(1, d)]
    out_specs = [pl.BlockSpec((tm, d), lambda s: (s, 0))]
    out_shape = [jax.ShapeDtypeStruct((rows.n_lat if final else rows.n_all, d), F32)]
    n_cast = 0
    if next_weights is not None:
        n_cast = min(FFN_CAST_SLABS, n_lat)
        *stacks, nl, nh = next_weights
        for w in stacks:
            r, c = w.shape[-2:]
            assert r % (16 * n_cast) == 0
            in_specs.append(pl.BlockSpec((None, None, r // n_cast, c),
                                         lambda s: (nl, nh, jnp.minimum(s, n_cast - 1), 0)))
            out_specs.append(pl.BlockSpec((r // n_cast, c), lambda s: (jnp.minimum(s, n_cast - 1), 0)))
            out_shape.append(jax.ShapeDtypeStruct((r, c), BF16))
        args += stacks
    outs = pl.pallas_call(
        functools.partial(_ffn_kernel, base=base, final=final, n_lat=n_lat, split_input=split_input,
                          n_cast=n_cast),
        grid=(n_lat + n_ctx,),
        in_specs=in_specs,
        out_specs=out_specs,
        out_shape=out_shape,
        compiler_params=_params(1),
        name="half_ffn",
    )(*args)
    return outs[0], tuple(outs[1:])


def _ret_proj_kernel(xc_ref, xl_ref, mod_ref, g_ref, w_ref, cos_ref, sin_ref, q_ref, k_ref, v_ref, gate_ref,
                     *, n_heads, dk, dv, ctx_tiles):
    x = jnp.where(pl.program_id(1) < ctx_tiles, xc_ref[...], xl_ref[...])
    u = _adaln(x, g_ref[...], mod_ref[3:4, :], mod_ref[4:5, :]).astype(BF16)
    p = _mm(u, w_ref[...])
    cos = cos_ref[...]
    sin = sin_ref[...]
    half = dk // 2
    qk = n_heads * dk
    k_scale = dk ** -0.5

    def rot(a):
        a1, a2 = a[:, :half], a[:, half:]
        return jnp.concatenate([a1 * cos - a2 * sin, a2 * cos + a1 * sin], axis=-1)

    for h in range(n_heads):
        q_ref[:, h * dk:(h + 1) * dk] = rot(p[:, h * dk:(h + 1) * dk]).astype(BF16)
        k_ref[:, h * dk:(h + 1) * dk] = (rot(p[:, qk + h * dk:qk + (h + 1) * dk]) * k_scale).astype(BF16)
    v_ref[...] = p[:, 2 * qk:2 * qk + n_heads * dv].astype(BF16)
    gate_ref[...] = p[:, 2 * qk + n_heads * dv:].astype(BF16)


def _ret_scan_kernel(q_ref, k_ref, v_ref, dmat_ref, qdec_ref, kdec_ref, cdec_ref, o_ref,
                     state_ref, fwd_ref, *, n_heads, dk, dv, n_chunks, bwd_chunk):
    s = pl.program_id(1)
    cs = q_ref.shape[0]

    @pl.when((s == 0) | (s == n_chunks))
    def _():
        state_ref[...] = jnp.zeros_like(state_ref)

    chunk = jnp.where(s < n_chunks, s, bwd_chunk(jnp.maximum(s - n_chunks, 0)))
    rows = pl.ds(pl.multiple_of(chunk * cs, cs), cs)
    outs = []
    for h in range(n_heads):
        q = q_ref[:, h * dk:(h + 1) * dk]
        k = k_ref[:, h * dk:(h + 1) * dk]
        v = v_ref[:, h * dv:(h + 1) * dv]
        scores = lax.dot_general(q, k, (((1,), (1,)), ((), ())), preferred_element_type=F32) * dmat_ref[h]
        state = state_ref[h]
        outs.append(_mm(scores.astype(BF16), v) + _mm(q, state.astype(BF16)) * qdec_ref[h])
        kd = (k.astype(F32) * kdec_ref[h]).astype(BF16)
        state_ref[h] = state * cdec_ref[h] + lax.dot_general(
            kd, v, (((0,), (0,)), ((), ())), preferred_element_type=F32)

    @pl.when(s < n_chunks)
    def _():
        for h in range(n_heads):
            fwd_ref[rows, h * dv:(h + 1) * dv] = outs[h].astype(fwd_ref.dtype)

    @pl.when(s >= n_chunks)
    def _():
        for h in range(n_heads):
            o = outs[h] + fwd_ref[rows, h * dv:(h + 1) * dv].astype(F32)
            mu = jnp.mean(o, axis=-1, keepdims=True)
            oc = o - mu
            var = jnp.mean(oc * oc, axis=-1, keepdims=True)
            o_ref[:, h * dv:(h + 1) * dv] = (oc * lax.rsqrt(var + NORM_EPS)).astype(o_ref.dtype)


def _ret_out_kernel(x_ref, mod_ref, y_ref, gate_ref, w_ref, o_ref):
    z = (_silu(gate_ref[...].astype(F32)) * y_ref[...].astype(F32)).astype(BF16)
    o_ref[...] = x_ref[...] + mod_ref[5:6, :] * _mm(z, w_ref[...])


def _retention_layer(h, rows, mods, layer, g, w_in, w_out, decay, ctx_out):
    bsz, t_lat, t_ctx = rows
    d = h.shape[1]
    t_all = t_ctx + t_lat
    n_heads = RET_HEADS
    dk = d // n_heads
    dv = 2 * d // n_heads
    qk = n_heads * dk
    vw = n_heads * dv
    tm = RET_TILE
    assert t_ctx % tm == 0 and t_lat % tm == 0
    ctx_tiles = t_ctx // tm
    ctx_row = bsz
    lat_blk, ctx_blk = rows.block("lat", tm), rows.block("ctx", tm)

    pos = jnp.arange(t_all, dtype=F32)
    inv = 1.0 / (ROPE_BASE ** jnp.linspace(0.0, 1.0, dk // 2, dtype=F32))
    ang = pos[:, None] * inv[None, :]

    q, k, v, gate = pl.pallas_call(
        functools.partial(_ret_proj_kernel, n_heads=n_heads, dk=dk, dv=dv, ctx_tiles=ctx_tiles),
        grid=(bsz, t_all // tm),
        in_specs=[
            pl.BlockSpec((tm, d), lambda b, i: (ctx_blk(b, jnp.minimum(i, ctx_tiles - 1)), 0)),
            pl.BlockSpec((tm, d), lambda b, i: (lat_blk(b, jnp.maximum(i - ctx_tiles, 0)), 0)),
            _mod_spec(d, layer, lambda b, i: jnp.where(i < ctx_tiles, ctx_row, b)),
            _const_spec((1, d)),
            _const_spec((d, 2 * qk + 2 * vw)),
            pl.BlockSpec((tm, dk // 2), lambda b, i: (i, 0)),
            pl.BlockSpec((tm, dk // 2), lambda b, i: (i, 0)),
        ],
        out_specs=[
            pl.BlockSpec((None, tm, qk), lambda b, i: (b, i, 0)),
            pl.BlockSpec((None, tm, qk), lambda b, i: (b, i, 0)),
            pl.BlockSpec((None, tm, vw), lambda b, i: (b, i, 0)),
            pl.BlockSpec((None, tm, vw), lambda b, i: (b, i, 0)),
        ],
        out_shape=[
            jax.ShapeDtypeStruct((bsz, t_all, qk), BF16),
            jax.ShapeDtypeStruct((bsz, t_all, qk), BF16),
            jax.ShapeDtypeStruct((bsz, t_all, vw), BF16),
            jax.ShapeDtypeStruct((bsz, t_all, vw), BF16),
        ],
        compiler_params=_params(2),
        name="ret_proj",
    )(h, h, mods, g.reshape(1, d), w_in, jnp.cos(ang), jnp.sin(ang))

    cs = SCAN_CHUNK
    assert t_ctx % cs == 0 and t_lat % cs == 0
    n_chunks = t_all // cs
    ctx_chunks = t_ctx // cs
    lg = jax.nn.log_sigmoid(decay.astype(F32))[:, :, None]
    n = jnp.arange(cs, dtype=F32)
    rel = n[:, None] - n[None, :]
    dm_f = jnp.where(rel >= 0, jnp.exp(lg[0][:, :, None] * jnp.maximum(rel, 0.0)), 0.0)
    dm_b = jnp.where(rel < 0, jnp.exp(lg[1][:, :, None] * jnp.maximum(-rel, 0.0)), 0.0)
    dmat = jnp.stack([dm_f, dm_b])
    qd = jnp.stack([jnp.exp(lg[0] * (n + 1.0)), jnp.exp(lg[1] * (cs - n))])
    kd = jnp.stack([jnp.exp(lg[0] * (cs - 1.0 - n)), jnp.exp(lg[1] * n)])
    cd = jnp.exp(lg * cs)
    qdec = jnp.broadcast_to(qd[..., None], (2, n_heads, cs, dv))
    kdec = jnp.broadcast_to(kd[..., None], (2, n_heads, cs, dk))
    cdec = jnp.broadcast_to(cd[..., None], (2, n_heads, 1, dv))

    def bwd_chunk(j):
        return jnp.where(j < ctx_chunks, ctx_chunks - 1 - j, n_chunks - 1 + ctx_chunks - j)

    def chunk(s):
        return jnp.where(s < n_chunks, s, bwd_chunk(jnp.maximum(s - n_chunks, 0)))

    def out_chunk(s):
        return bwd_chunk(jnp.maximum(s - n_chunks, 0))

    def table_spec(*shape):
        return pl.BlockSpec((None,) + shape, lambda b, s: (s // n_chunks, 0, 0, 0))

    y = pl.pallas_call(
        functools.partial(_ret_scan_kernel, n_heads=n_heads, dk=dk, dv=dv, n_chunks=n_chunks,
                          bwd_chunk=bwd_chunk),
        grid=(bsz, 2 * n_chunks),
        in_specs=[
            pl.BlockSpec((None, cs, qk), lambda b, s: (b, chunk(s), 0)),
            pl.BlockSpec((None, cs, qk), lambda b, s: (b, chunk(s), 0)),
            pl.BlockSpec((None, cs, vw), lambda b, s: (b, chunk(s), 0)),
            table_spec(n_heads, cs, cs),
            table_spec(n_heads, cs, dv),
            table_spec(n_heads, cs, dk),
            table_spec(n_heads, 1, dv),
        ],
        out_specs=pl.BlockSpec((None, cs, vw), lambda b, s: (b, out_chunk(s), 0)),
        out_shape=jax.ShapeDtypeStruct((bsz, t_all, vw), BF16),
        scratch_shapes=[pltpu.VMEM((n_heads, dk, dv), F32), pltpu.VMEM((t_all, vw), BF16)],
        compiler_params=_params(2),
        name="ret_scan",
    )(q, k, v, dmat, qdec, kdec, cdec)

    first = 0 if ctx_out else ctx_tiles

    def res_blk(b, i):
        j = i + first
        return jnp.where(j < ctx_tiles, ctx_blk(b, jnp.minimum(j, ctx_tiles - 1)),
                         lat_blk(b, jnp.maximum(j - ctx_tiles, 0)))

    return pl.pallas_call(
        _ret_out_kernel,
        grid=(bsz, t_all // tm - first),
        in_specs=[
            pl.BlockSpec((tm, d), lambda b, i: (res_blk(b, i), 0)),
            _mod_spec(d, layer, lambda b, i: jnp.where(i + first < ctx_tiles, ctx_row, b)),
            pl.BlockSpec((None, tm, vw), lambda b, i: (b, i + first, 0)),
            pl.BlockSpec((None, tm, vw), lambda b, i: (b, i + first, 0)),
            _const_spec((vw, d)),
        ],
        out_specs=pl.BlockSpec((tm, d), lambda b, i: (res_blk(b, i), 0)),
        out_shape=jax.ShapeDtypeStruct(h.shape, F32),
        compiler_params=_params(2),
        name="ret_out",
    )(h, mods, y, gate, w_out)


def _pool_tables(seg_len):
    assert POOL_TILE % seg_len == 0
    r = np.arange(POOL_TILE)
    seg, pos = r // seg_len, r % seg_len
    mats, invs = [], []
    for win in POOL_WINDOWS:
        lo = np.clip(pos - win // 2, 0, seg_len)
        hi = np.clip(pos - win // 2 + win, 0, seg_len)
        col_seg, col_pos = seg[None, :], pos[None, :]
        member = (col_seg == seg[:, None]) & (col_pos >= lo[:, None]) & (col_pos < hi[:, None])
        mats.append(member.astype(np.float32))
        invs.append((1.0 / (hi - lo)).astype(np.float32))
    return np.stack(mats), np.stack(invs)


def _pool_kernel(x_ref, mod_ref, g_ref, s_ref, inv_ref, w_ref, b_ref, sc_ref, o_ref, *, n_groups, dg):
    x = x_ref[...]
    u = _adaln(x, g_ref[...], mod_ref[3:4, :], mod_ref[4:5, :])
    gate = mod_ref[5:6, :]
    for gi in range(n_groups):
        sl = slice(gi * dg, (gi + 1) * dg)
        ug = u[:, sl]
        hi = ug.astype(BF16)
        lo = (ug - hi.astype(F32)).astype(BF16)
        win_sum = _mm(s_ref[gi], hi) + _mm(s_ref[gi], lo)
        dlt = win_sum * inv_ref[gi] - ug
        yg = _mm(dlt.astype(BF16), w_ref[gi])
        o_ref[:, sl] = x[:, sl] + gate[:, sl] * ((yg + b_ref[:, sl]) * sc_ref[:, sl])


def _pool_layer(h, rows, lat_seg, mods, layer, g, w_grp, b_grp, scale, ctx_out):
    bsz, t_lat, t_ctx = rows
    d = h.shape[1]
    n_groups = len(POOL_WINDOWS)
    dg = d // n_groups
    tabs = [_pool_tables(lat_seg), _pool_tables(t_ctx)]
    s_tab = jnp.asarray(np.stack([m for m, _ in tabs]), BF16)
    inv_tab = jnp.asarray(np.stack([np.broadcast_to(v[:, :, None], (n_groups, POOL_TILE, dg)) for _, v in tabs]), F32)
    lat_tiles = t_lat // POOL_TILE
    ctx_tiles = t_ctx // POOL_TILE if ctx_out else 0
    lat_blk, ctx_blk = rows.block("lat", POOL_TILE), rows.block("ctx", POOL_TILE)

    def res_blk(b, i):
        return jnp.where(i < lat_tiles, lat_blk(b, jnp.minimum(i, lat_tiles - 1)),
                         ctx_blk(b, jnp.maximum(i - lat_tiles, 0)))

    def table_spec(*shape):
        return pl.BlockSpec((None,) + shape, lambda b, i: (jnp.where(i < lat_tiles, 0, 1), 0, 0, 0))

    return pl.pallas_call(
        functools.partial(_pool_kernel, n_groups=n_groups, dg=dg),
        grid=(bsz, lat_tiles + ctx_tiles),
        in_specs=[
            pl.BlockSpec((POOL_TILE, d), lambda b, i: (res_blk(b, i), 0)),
            _mod_spec(d, layer, lambda b, i: jnp.where(i < lat_tiles, b, bsz)),
            _const_spec((1, d)),
            table_spec(n_groups, POOL_TILE, POOL_TILE),
            table_spec(n_groups, POOL_TILE, dg),
            _const_spec((n_groups, dg, dg)),
            _const_spec((1, d)),
            _const_spec((1, d)),
        ],
        out_specs=pl.BlockSpec((POOL_TILE, d), lambda b, i: (res_blk(b, i), 0)),
        out_shape=jax.ShapeDtypeStruct(h.shape, F32),
        compiler_params=_params(2),
        name="pool",
    )(h, mods, g.reshape(1, d), s_tab, inv_tab, w_grp.astype(BF16), b_grp.reshape(1, d), scale.reshape(1, d))


def _dft_tables(n_in):
    mod = 4 * n_in
    sub_n = min(V7X_LANES, n_in)
    idx = jnp.arange(n_in, dtype=jnp.int32)[None, :]
    blk = jnp.arange(n_in // sub_n, dtype=jnp.int32)[:, None]
    sub = jnp.arange(sub_n, dtype=jnp.int32)[:, None]
    theta = 2.0 * math.pi / mod

    def cs(p):
        ph = (p % mod).astype(F32) * theta
        return jnp.cos(ph), jnp.sin(ph)

    fwd = cs(2 * sub_n * blk * idx) + cs((2 * sub + 1) * idx)
    inv = cs(sub_n * blk * (2 * idx + 1)) + cs(sub * (2 * idx + 1))
    return fwd, inv


def _dft_fwd_gen_kernel(pc_ref, ps_ref, qc_ref, qs_ref, o_ref, *, tf):
    ft = pl.program_id(0)
    sub_n = qc_ref.shape[0]
    blocks = tf // sub_n
    qc = qc_ref[...]
    qs = qs_ref[...]
    for jb in range(blocks):
        kb = ft * blocks + jb
        pc = pc_ref[pl.ds(kb, 1), :]
        ps = ps_ref[pl.ds(kb, 1), :]
        o_ref[jb * sub_n:(jb + 1) * sub_n, :] = (pc * qc - ps * qs).astype(BF16)
        o_ref[tf + jb * sub_n:tf + (jb + 1) * sub_n, :] = (ps * qc + pc * qs).astype(BF16)


def _dft_inv_gen_kernel(pc_ref, ps_ref, qc_ref, qs_ref, o_ref, *, tf, n_in):
    mb = pl.program_id(0)
    pc = pc_ref[pl.ds(mb, 1), :]
    ps = ps_ref[pl.ds(mb, 1), :]
    qc = qc_ref[...]
    qs = qs_ref[...]
    cos_all = pc * qc - ps * qs
    sin_all = ps * qc + pc * qs
    for ft in range(n_in // tf):
        o_ref[:, 2 * ft * tf:(2 * ft + 1) * tf] = cos_all[:, ft * tf:(ft + 1) * tf].astype(BF16)
        o_ref[:, (2 * ft + 1) * tf:(2 * ft + 2) * tf] = sin_all[:, ft * tf:(ft + 1) * tf].astype(BF16)


def _dft_matrices(n_in, tf):
    fwd_tabs, inv_tabs = _dft_tables(n_in)
    sub_n = min(V7X_LANES, n_in)
    nb = n_in // sub_n
    tabs = [_const_spec((nb, n_in)), _const_spec((nb, n_in)),
            _const_spec((sub_n, n_in)), _const_spec((sub_n, n_in))]
    fwd = pl.pallas_call(
        functools.partial(_dft_fwd_gen_kernel, tf=tf),
        grid=(n_in // tf,),
        in_specs=tabs,
        out_specs=pl.BlockSpec((2 * tf, n_in), lambda i: (i, 0)),
        out_shape=jax.ShapeDtypeStruct((2 * n_in, n_in), BF16),
        compiler_params=_params(1),
        name="dft_fwd_gen",
    )(*fwd_tabs)
    inv = pl.pallas_call(
        functools.partial(_dft_inv_gen_kernel, tf=tf, n_in=n_in),
        grid=(nb,),
        in_specs=tabs,
        out_specs=pl.BlockSpec((sub_n, 2 * n_in), lambda i: (i, 0)),
        out_shape=jax.ShapeDtypeStruct((n_in, 2 * n_in), BF16),
        compiler_params=_params(1),
        name="dft_inv_gen",
    )(*inv_tabs)
    return fwd, inv


def _hy_proj_kernel(x_ref, xp_ref, xn_ref, mod_ref, g_ref, perm_ref, w_ref, b_ref, ws_ref, bs_ref,
                    v_ref, x1_ref, x2_ref):
    i = pl.program_id(1)
    tm, d = x_ref.shape
    n_ph = v_ref.shape[0]
    hp = tm // n_ph
    g = g_ref[...]
    shift, scale = mod_ref[3:4, :], mod_ref[4:5, :]
    u = _adaln(x_ref[...], g, shift, scale).astype(BF16)
    u = _mm(perm_ref[...], u).astype(BF16)
    halo = jnp.concatenate([xp_ref[...], xn_ref[...]], axis=0)
    uh = _adaln(halo, g, shift, scale).astype(BF16)
    p = _mm(jnp.concatenate([u, uh], axis=0), w_ref[...]) + b_ref[...]
    ph = [p[r * hp:(r + 1) * hp] for r in range(n_ph)]
    n_halo = xp_ref.shape[0]
    before = jnp.where(i == 0, 0.0, p[tm + n_halo - 1:tm + n_halo])
    after = jnp.where(i == pl.num_programs(1) - 1, 0.0, p[tm + n_halo:tm + n_halo + 1])
    row = lax.broadcasted_iota(jnp.int32, ph[0].shape, 0)
    last_prev = jnp.where(row == 0, before, pltpu.roll(ph[-1], 1, axis=0))
    first_next = jnp.where(row == hp - 1, after, pltpu.roll(ph[0], hp - 1, axis=0))
    w0, w1, w2, bs = ws_ref[0:1, :], ws_ref[1:2, :], ws_ref[2:3, :], bs_ref[...]
    for r in range(n_ph):
        prev = ph[r - 1] if r > 0 else last_prev
        nxt = ph[r + 1] if r < n_ph - 1 else first_next
        conv = prev * w0 + ph[r] * w1 + nxt * w2 + bs
        v_ref[r] = conv[:, :d].astype(BF16)
        x1_ref[r] = conv[:, d:2 * d].astype(BF16)
        x2_ref[r] = conv[:, 2 * d:].astype(BF16)


def _hy_filter_kernel(feat_ref, wp_ref, bp_ref, wm_ref, bm_ref, fr_ref, wf_ref, dl_ref,
                      hf_ref, hb_ref, ss_ref, *, d, n_orders):
    start = (pl.program_id(0) == 0) & (pl.program_id(1) == 0)
    feat = feat_ref[...]
    fr = fr_ref[...]
    hdn = jnp.sin(fr * (_mm_f32(feat, wp_ref[...]) + bp_ref[...]))
    hdn = jnp.sin(fr * (_mm_f32(hdn, wm_ref[...]) + bm_ref[...]))
    h = _mm(hdn.astype(BF16), wf_ref[...])
    decay = jnp.exp(-feat[:, 0:1] * dl_ref[...])
    row = lax.broadcasted_iota(jnp.int32, decay.shape, 0)
    first = (row == 0) & start

    @pl.when(start)
    def _():
        ss_ref[...] = jnp.zeros_like(ss_ref)

    for o in range(n_orders):
        sl = slice(o * d, (o + 1) * d)
        hf = h[:, 2 * o * d:(2 * o + 1) * d] * decay
        hb = jnp.where(first, 0.0, h[:, (2 * o + 1) * d:(2 * o + 2) * d] * decay)
        hf_ref[:, sl] = hf.astype(BF16)
        hb_ref[:, sl] = hb.astype(BF16)
        ss_ref[:, sl] += jnp.sum(hf * hf + hb * hb, axis=0, keepdims=True)


def _twiddle(a, c, s, conj=False):
    a_re, a_s = a
    if conj:
        return a_re * c + a_s * s, a_s * c - a_re * s
    return a_re * c - a_s * s, a_s * c + a_re * s


def _add(a, b):
    return a[0] + b[0], a[1] + b[1]


def _sub(a, b):
    return a[0] - b[0], a[1] - b[1]


def _dit_blocks(dft, phase_refs, tw, tp):
    ph = []
    for ref in phase_refs:
        t = _mm(dft, ref[...])
        ph.append((t[0:tp], t[tp:]))
    c1, s1, c2, s2, c3, s3 = tw
    t2, t3 = _twiddle(ph[2], c1, s1), _twiddle(ph[3], c1, s1)
    e_a, e_b = _add(ph[0], t2), _sub(ph[0], t2)
    o_a, o_b = _add(ph[1], t3), _sub(ph[1], t3)
    ta, tb = _twiddle(o_a, c2, s2), _twiddle(o_b, c3, s3, conj=True)
    return [_add(e_a, ta), _sub(e_a, ta), _add(e_b, tb), _sub(e_b, tb)]


def _bin_group(f_ref, tw_ref, g, tp):
    re, sn = slice(2 * g * tp, (2 * g + 1) * tp), slice((2 * g + 1) * tp, (2 * g + 2) * tp)
    dft = f_ref[2 * g * tp:(2 * g + 2) * tp, :]
    tw = [tw_ref[i, g * tp:(g + 1) * tp, :] for i in range(6)]
    return re, sn, dft, tw


def _hy_spec_kernel(f_ref, f0, f1, f2, f3, b0, b1, b2, b3, tw_ref, ss_ref, o_ref, *, tp, scale):
    wgt = scale * lax.rsqrt(ss_ref[...] + NORM_EPS)
    for g in range(tw_ref.shape[1] // tp):
        re, sn, dft, tw = _bin_group(f_ref, tw_ref, g, tp)
        fwd = _dit_blocks(dft, (f0, f1, f2, f3), tw, tp)
        bwd = _dit_blocks(dft, (b0, b1, b2, b3), tw, tp)
        for blk in range(4):
            o_ref[blk, re, :] = (fwd[blk][0] + bwd[blk][0]) * wgt
            o_ref[blk, sn, :] = (fwd[blk][1] - bwd[blk][1]) * wgt


def _hy_fwd_kernel(f_ref, u0, u1, u2, u3, h_ref, tw_ref, y_ref, *, tp):
    for g in range(tw_ref.shape[1] // tp):
        re, sn, dft, tw = _bin_group(f_ref, tw_ref, g, tp)
        c1, s1, c2, s2, c3, s3 = tw
        x = _dit_blocks(dft, (u0, u1, u2, u3), tw, tp)
        y = []
        for blk in range(4):
            h_re, h_s = h_ref[blk, re, :], h_ref[blk, sn, :]
            x_re, x_s = x[blk]
            y.append((x_re * h_re - x_s * h_s, x_re * h_s + x_s * h_re))
        e_a, o_a = _add(y[0], y[1]), _twiddle(_sub(y[0], y[1]), c2, s2, conj=True)
        e_b, o_b = _add(y[2], y[3]), _twiddle(_sub(y[2], y[3]), c3, s3)
        out = [_add(e_a, e_b), _add(o_a, o_b),
               _twiddle(_sub(e_a, e_b), c1, s1, conj=True), _twiddle(_sub(o_a, o_b), c1, s1, conj=True)]
        for r in range(4):
            y_ref[r, re, :] = out[r][0].astype(BF16)
            y_ref[r, sn, :] = out[r][1].astype(BF16)


def _hy_inv_kernel(f_ref, y_ref, a_ref, u_ref, fb_ref, o_ref, *, rows):
    for r in range(f_ref.shape[0] // rows):
        sl = slice(r * rows, (r + 1) * rows)
        conv = _mm(f_ref[sl, :], y_ref[...])
        z = a_ref[sl, :].astype(F32) * (conv + u_ref[sl, :].astype(F32) * fb_ref[...])
        o_ref[sl, :] = z.astype(o_ref.dtype)


def _hy_out_kernel(x_ref, mod_ref, z_ref, pick_ref, w_ref, b_ref, *refs):
    o_ref = refs[-1]
    z = _mm(pick_ref[0], z_ref[0])
    for r in range(1, z_ref.shape[0]):
        z = z + _mm(pick_ref[r], z_ref[r])
    o_ref[...] = x_ref[...] + mod_ref[5:6, :] * (_mm(z.astype(BF16), w_ref[...]) + b_ref[...])


def _hyena_layer(h, rows, part, out_buf, mods, layer, g, w_in, b_in, w_short, b_short, w_pos, b_pos, w_mid,
                 b_mid, freq, w_filt, fbias, w_out, b_out):
    bsz = rows.bsz
    length = rows.length(part)
    d = h.shape[1]
    mod_row = (lambda b, i: b) if part == "lat" else (lambda b, i: bsz)
    n_full = 2 * length
    n_ph = DFT_RADIX
    q = length // n_ph
    tm = min(512, length)
    hp = tm // n_ph
    tc = DFT_COL_TILE
    n_ct = d // tc
    width = w_mid.shape[0]
    n_parts = 2 * HYENA_ORDER

    halo = 8
    m_idx = np.arange(hp)
    perm = np.zeros((tm, tm), np.float32)
    pick = np.zeros((n_ph, tm, hp), np.float32)
    for r in range(n_ph):
        perm[r * hp + m_idx, n_ph * m_idx + r] = 1.0
        pick[r, n_ph * m_idx + r, m_idx] = 1.0
    ph_spec = pl.BlockSpec((None, n_ph, hp, d), lambda b, i: (b, 0, i, 0))
    tile_blk, halo_blk = rows.block(part, tm), rows.block(part, halo)
    v, x1, x2 = pl.pallas_call(
        _hy_proj_kernel,
        grid=(bsz, length // tm),
        in_specs=[
            pl.BlockSpec((tm, d), lambda b, i: (tile_blk(b, i), 0)),
            pl.BlockSpec((halo, d), lambda b, i: (halo_blk(b, jnp.maximum(i * (tm // halo) - 1, 0)), 0)),
            pl.BlockSpec((halo, d),
                         lambda b, i: (halo_blk(b, jnp.minimum((i + 1) * (tm // halo), length // halo - 1)), 0)),
            _mod_spec(d, layer, mod_row),
            _const_spec((1, d)),
            _const_spec((tm, tm)),
            _const_spec((d, 3 * d)),
            _const_spec((1, 3 * d)),
            _const_spec((3, 3 * d)),
            _const_spec((1, 3 * d)),
        ],
        out_specs=[ph_spec] * 3,
        out_shape=[jax.ShapeDtypeStruct((bsz, n_ph, q, d), BF16)] * 3,
        compiler_params=_params(2),
        name="hy_proj",
    )(h, h, h, mods, g.reshape(1, d), jnp.asarray(perm, BF16), w_in, b_in.reshape(1, 3 * d),
      w_short, b_short.reshape(1, 3 * d))

    t = jnp.linspace(0.0, 1.0, length, dtype=F32)[:, None]
    bands = jnp.linspace(1e-4, HYENA_BANDS - 1, HYENA_BANDS, dtype=F32)
    ang = (2.0 * math.pi / length) * jnp.arange(length, dtype=F32)[:, None] * bands[None, :]
    feat = jnp.concatenate([t, jnp.cos(ang), -jnp.sin(ang)], axis=-1)
    feat = jnp.pad(feat, ((0, 0), (0, V7X_LANES - HYENA_EMB)))
    w_pos_p = jnp.pad(w_pos.astype(F32), ((0, V7X_LANES - HYENA_EMB), (0, 0)))
    deltas = jnp.abs(jnp.linspace(HYENA_MIN_DECAY, HYENA_MAX_DECAY, d, dtype=F32)).reshape(1, d)
    feat = jnp.concatenate([feat[r::n_ph] for r in range(n_ph)], axis=0)
    tl = min(512, q)
    od = HYENA_ORDER * d
    tap_spec = pl.BlockSpec((tl, od), lambda r, i: (i, r))
    taps_f, taps_b, ss = pl.pallas_call(
        functools.partial(_hy_filter_kernel, d=d, n_orders=HYENA_ORDER),
        grid=(n_ph, q // tl),
        in_specs=[
            pl.BlockSpec((tl, V7X_LANES), lambda r, i: (r * (q // tl) + i, 0)),
            _const_spec((V7X_LANES, width)),
            _const_spec((1, width)),
            _const_spec((width, width)),
            _const_spec((1, width)),
            _const_spec((1, width)),
            _const_spec((width, n_parts * d)),
            _const_spec((1, d)),
        ],
        out_specs=[tap_spec, tap_spec, pl.BlockSpec((1, od), lambda r, i: (0, 0))],
        out_shape=[jax.ShapeDtypeStruct((q, n_ph * od), BF16), jax.ShapeDtypeStruct((q, n_ph * od), BF16),
                   jax.ShapeDtypeStruct((1, od), F32)],
        compiler_params=_params(2),
        name="hy_filter",
    )(feat, w_pos_p, b_pos.reshape(1, width), w_mid, b_mid.reshape(1, width), freq.reshape(1, width),
      w_filt.astype(BF16), deltas)

    tf = min(DFT_FREQ_TILE, q)
    tp = min(DFT_PACK, q)
    tn = min(DFT_TIME_TILE, q)
    dft_fwd, dft_inv = _dft_matrices(q, tp)
    odd = 2.0 * jnp.arange(q, dtype=F32) + 1.0
    angles = jnp.stack([(2.0 * math.pi / n_full) * odd, (math.pi / n_full) * odd,
                        (math.pi / n_full) * (length - odd)])
    tw = jnp.stack([jnp.cos(angles), jnp.sin(angles)], axis=1).reshape(6, q)
    tw = jnp.broadcast_to(tw[:, :, None], (6, q, tc))
    tw_spec = pl.BlockSpec((6, tf, tc), lambda f, *_: (0, f, 0))

    n_oc = od // tc

    def tap_phase(r):
        return pl.BlockSpec((q, tc), lambda f, o, j: (0, r * n_oc + o * n_ct + j))

    spec = pl.pallas_call(
        functools.partial(_hy_spec_kernel, tp=tp, scale=1.0 / length),
        grid=(q // tf, HYENA_ORDER, n_ct),
        in_specs=[pl.BlockSpec((2 * tf, q), lambda f, o, j: (f, 0))]
        + [tap_phase(r) for r in range(n_ph)] * 2
        + [tw_spec, pl.BlockSpec((1, tc), lambda f, o, j: (0, o * n_ct + j))],
        out_specs=pl.BlockSpec((None, n_ph, 2 * tf, tc), lambda f, o, j: (o, 0, f, j)),
        out_shape=jax.ShapeDtypeStruct((HYENA_ORDER, n_ph, 2 * q, d), F32),
        compiler_params=_params(3),
        name="hy_spec",
    )(dft_fwd, *([taps_f] * n_ph), *([taps_b] * n_ph), tw, ss)

    tci = d if q < DFT_FREQ_TILE else 2 * tc
    n_ci = d // tci

    def long_conv(order, u, a):
        yspec = pl.pallas_call(
            functools.partial(_hy_fwd_kernel, tp=tp),
            grid=(q // tf, n_ct, bsz),
            in_specs=[pl.BlockSpec((2 * tf, q), lambda f, j, b: (f, 0))]
            + [pl.BlockSpec((None, None, q, tc), functools.partial(lambda r, f, j, b: (b, r, 0, j), r))
               for r in range(n_ph)]
            + [pl.BlockSpec((None, n_ph, 2 * tf, tc), lambda f, j, b: (order, 0, f, j)), tw_spec],
            out_specs=pl.BlockSpec((None, n_ph, 2 * tf, tc), lambda f, j, b: (b, 0, f, j)),
            out_shape=jax.ShapeDtypeStruct((bsz, n_ph, 2 * q, d), BF16),
            compiler_params=_params(3),
            name="hy_fwd",
        )(dft_fwd, *([u] * n_ph), spec, tw)
        row_spec = pl.BlockSpec((None, None, tn, tci), lambda n, j, b, r: (b, r, n, j))
        return pl.pallas_call(
            functools.partial(_hy_inv_kernel, rows=min(DFT_ROW_GROUP, tn)),
            grid=(q // tn, n_ci, bsz, n_ph),
            in_specs=[
                pl.BlockSpec((tn, 2 * q), lambda n, j, b, r: (n, 0)),
                pl.BlockSpec((None, None, 2 * q, tci), lambda n, j, b, r: (b, r, 0, j)),
                row_spec,
                row_spec,
                pl.BlockSpec((None, 1, tci), lambda n, j, b, r: (order, 0, j)),
            ],
            out_specs=row_spec,
            out_shape=jax.ShapeDtypeStruct((bsz, n_ph, q, d), BF16),
            compiler_params=_params(4),
            name="hy_inv",
        )(dft_inv, yspec, a, u, fbias.astype(F32).reshape(HYENA_ORDER, 1, d))

    z = long_conv(0, v, x1)
    zz = long_conv(1, z, x2)

    in_specs = [
        pl.BlockSpec((tm, d), lambda b, i: (tile_blk(b, i), 0)),
        _mod_spec(d, layer, mod_row),
        pl.BlockSpec((None, n_ph, hp, d), lambda b, i: (b, 0, i, 0)),
        _const_spec((n_ph, tm, hp)),
        _const_spec((d, d)),
        _const_spec((1, d)),
    ]
    args = [h, mods, zz, jnp.asarray(pick, BF16), w_out, b_out.reshape(1, d)]
    aliases = {}
    if out_buf is not None:
        in_specs.append(pl.BlockSpec(memory_space=pl.ANY))
        args.append(out_buf)
        aliases = {len(args) - 1: 0}
    return pl.pallas_call(
        _hy_out_kernel,
        grid=(bsz, length // tm),
        in_specs=in_specs,
        out_specs=pl.BlockSpec((tm, d), lambda b, i: (tile_blk(b, i), 0)),
        out_shape=jax.ShapeDtypeStruct(h.shape, F32),
        input_output_aliases=aliases,
        compiler_params=_params(2),
        name="hy_out",
    )(*args)


def kernel(x, c, ctx, c_ctx, ada_w, ada_b, norm_g, ffn_w1, ffn_w3, ffn_w2, ret_w_in, ret_w_out, ret_decay,
           pool_w, pool_b, pool_scale, hy_w_in, hy_b_in, hy_w_short, hy_b_short, hy_w_pos, hy_b_pos,
           hy_w_mid, hy_b_mid, hy_freq, hy_w_filt, hy_bias, hy_w_out, hy_b_out, final_g):
    bsz, seq, d = x.shape
    t_ctx = ctx.shape[1]
    depth = ada_w.shape[0]
    n_mixers = 3
    assert bsz + 1 <= MOD_ROWS
    ctx_row = bsz

    cond = jnp.concatenate([c, c_ctx[None, :], jnp.zeros((MOD_ROWS - bsz - 1, d), F32)], axis=0)
    mods = _ada_all(cond, ada_w, ada_b).reshape(depth, MOD_ROWS, ADA_CHUNKS, d)

    rows = _Rows(bsz, seq, t_ctx)
    ffn_stacks = (ffn_w1, ffn_w3, ffn_w2)
    ffn_state = {"weights": tuple(w[0, 0].astype(BF16) for w in ffn_stacks)}

    def ffn(h, layer, half, ctx_live, final=False):
        nxt = None if final else ffn_stacks + ((layer, 1) if half == 0 else (layer + 1, 0))
        h, cast = _half_ffn(h, rows, mods, layer, half, norm_g[layer, 2 * half], ffn_state["weights"], final_g,
                            ctx_live, next_weights=nxt, final=final)
        ffn_state["weights"] = cast
        return h

    h = (x.reshape(bsz * seq, d), ctx.reshape(bsz * t_ctx, d))
    for layer in range(depth):
        kind = layer % n_mixers
        slot = layer // n_mixers
        last = layer == depth - 1
        ctx_out = not last
        ctx_live = ctx_out or kind == 0
        h = ffn(h, layer, 0, ctx_live)
        g_mix = norm_g[layer, 1]
        if kind == 0:
            h = _retention_layer(h, rows, mods, layer, g_mix, ret_w_in[slot].astype(BF16),
                                 ret_w_out[slot].astype(BF16), ret_decay[slot], ctx_out)
        elif kind == 1:
            h = _pool_layer(h, rows, GRID_W, mods, layer, g_mix, pool_w[slot], pool_b[slot], pool_scale[slot],
                            ctx_out)
        else:
            hp = (hy_w_in[slot].astype(BF16), hy_b_in[slot], hy_w_short[slot], hy_b_short[slot], hy_w_pos[slot],
                  hy_b_pos[slot], hy_w_mid[slot], hy_b_mid[slot], hy_freq[slot], hy_w_filt[slot],
                  hy_bias[slot], hy_w_out[slot].astype(BF16), hy_b_out[slot])
            new_h = _hyena_layer(h, rows, "lat", None, mods, layer, g_mix, *hp)
            if ctx_out:
                new_h = _hyena_layer(h, rows, "ctx", new_h, mods, layer, g_mix, *hp)
            h = new_h
        h = ffn(h, layer, 1, ctx_out, final=last)
    return h.reshape(bsz, seq, d)
```

```python
import functools
import math
from typing import NamedTuple

import numpy as np
import jax
import jax.numpy as jnp
from jax import lax
from jax.experimental import pallas as pl
from jax.experimental.pallas import tpu as pltpu

F32 = jnp.float32
BF16 = jnp.bfloat16

GRID_W = 64
ADA_CHUNKS = 9
NORM_EPS = 1e-6
RET_HEADS = 4
RET_CHUNK = 128
ROPE_BASE = 10000.0
POOL_WINDOWS = (2, 4, 8, 16)
HYENA_ORDER = 2
HYENA_EMB = 33
HYENA_BANDS = (HYENA_EMB - 1) // 2
HYENA_TARGET = 1e-2
HYENA_FAST = 0.3
HYENA_SLOW = 1.5
HYENA_MAX_DECAY = math.log(HYENA_TARGET) / HYENA_FAST
HYENA_MIN_DECAY = math.log(HYENA_TARGET) / HYENA_SLOW

V7X_LANES = 128
V7X_VMEM_LIMIT_BYTES = 56 * 1024 * 1024
MOD_ROWS = 8
FFN_CAST_SLABS = 16
POOL_TILE = 256
RET_TILE = 256
SCAN_CHUNK = 256
DFT_RADIX = 4
DFT_FREQ_TILE = 512
DFT_PACK = 256
DFT_ROW_GROUP = 512
DFT_TIME_TILE = 2048
DFT_COL_TILE = 256
HIGHEST = lax.Precision.HIGHEST


def _params(n_axes):
    return pltpu.CompilerParams(
        dimension_semantics=("arbitrary",) * n_axes,
        vmem_limit_bytes=V7X_VMEM_LIMIT_BYTES)


def _const_spec(shape):
    zeros = (0,) * len(shape)
    return pl.BlockSpec(shape, lambda *_: zeros, pipeline_mode=pl.Buffered(1))


def _mod_spec(d, layer, row_fn):
    return pl.BlockSpec((None, None, ADA_CHUNKS, d), lambda *idx: (layer, row_fn(*idx), 0, 0))


class _Rows(NamedTuple):
    bsz: int
    t_lat: int
    t_ctx: int

    @property
    def n_lat(self):
        return self.bsz * self.t_lat

    @property
    def n_all(self):
        return self.bsz * (self.t_lat + self.t_ctx)

    def length(self, part):
        return self.t_lat if part == "lat" else self.t_ctx

    def block(self, part, tile):
        per = self.length(part) // tile
        base = 0 if part == "lat" else self.n_lat // tile
        assert self.length(part) % tile == 0 and self.n_lat % tile == 0
        return lambda b, i: base + b * per + i


def _mm(a, b):
    return jnp.dot(a, b, preferred_element_type=F32)


def _mm_f32(a, b):
    return jnp.dot(a, b, preferred_element_type=F32, precision=HIGHEST)


def _silu(x):
    return x * jax.nn.sigmoid(x)


def _adaln(x, g, shift, scale):
    ms = jnp.mean(x * x, axis=-1, keepdims=True)
    y = x * lax.rsqrt(ms + NORM_EPS) * g
    return y * (1.0 + scale) + shift


def _ada_kernel(c_ref, w_ref, b_ref, o_ref):
    s = _silu(c_ref[...]).astype(BF16)
    o_ref[...] = _mm(s, w_ref[...].astype(BF16)) + b_ref[...]


def _ada_all(cond, ada_w, ada_b):
    depth, d, nd = ada_w.shape
    return pl.pallas_call(
        _ada_kernel,
        grid=(depth, nd // d),
        in_specs=[
            pl.BlockSpec((MOD_ROWS, d), lambda l, j: (0, 0)),
            pl.BlockSpec((None, d, d), lambda l, j: (l, 0, j)),
            pl.BlockSpec((None, 1, d), lambda l, j: (l, 0, j)),
        ],
        out_specs=pl.BlockSpec((None, MOD_ROWS, d), lambda l, j: (l, 0, j)),
        out_shape=jax.ShapeDtypeStruct((depth, MOD_ROWS, nd), F32),
        compiler_params=_params(2),
        name="ada",
    )(cond, ada_w, ada_b.reshape(depth, 1, nd))


def _ffn_kernel(*refs, base, final, n_lat, split_input, n_cast):
    n_x = 2 if split_input else 1
    x_refs, refs = refs[:n_x], refs[n_x:]
    mod_ref, g_ref, w1_ref, w3_ref, w2_ref, fg_ref = refs[:6]
    rest = refs[6:]
    if n_cast:
        n_jobs = (len(rest) - 1) // 2
        src_refs, o_ref, dst_refs = rest[:n_jobs], rest[n_jobs], rest[n_jobs + 1:]

        @pl.when(pl.program_id(0) < n_cast)
        def _():
            for src, dst in zip(src_refs, dst_refs):
                dst[...] = src[...].astype(BF16)
    else:
        (o_ref,) = rest
    if split_input:
        x = jnp.where(pl.program_id(0) >= n_lat, x_refs[1][...], x_refs[0][...])
    else:
        x = x_refs[0][...]
    u = _adaln(x, g_ref[...], mod_ref[base:base + 1, :], mod_ref[base + 1:base + 2, :]).astype(BF16)
    h1 = _mm(u, w1_ref[...])
    h3 = _mm(u, w3_ref[...])
    a = (_silu(h1) * h3).astype(BF16)
    y = x + (0.5 * mod_ref[base + 2:base + 3, :]) * _mm(a, w2_ref[...])
    if final:
        ms = jnp.mean(y * y, axis=-1, keepdims=True)
        y = y * lax.rsqrt(ms + NORM_EPS) * fg_ref[...]
    o_ref[...] = y


def _half_ffn(h, rows, mods, layer, half, g, weights, final_g, ctx_live, cast_jobs=(), final=False, tm=512):
    w1, w3, w2 = weights
    d, f = w1.shape
    base = 6 * half
    per_b = rows.t_lat // tm
    n_lat = rows.n_lat // tm
    n_ctx = rows.bsz * rows.t_ctx // tm if ctx_live else 0
    assert rows.t_lat % tm == 0 and (rows.bsz * rows.t_ctx) % tm == 0
    split_input = isinstance(h, tuple)

    if split_input:
        x_specs = [pl.BlockSpec((tm, d), lambda s: (jnp.minimum(s, n_lat - 1), 0)),
                   pl.BlockSpec((tm, d), lambda s: (jnp.maximum(s - n_lat, 0), 0))]
        x_args = list(h)
    else:
        x_specs = [pl.BlockSpec((tm, d), lambda s: (s, 0))]
        x_args = [h]
    in_specs = x_specs + [
        _mod_spec(d, layer, lambda s: jnp.where(s < n_lat, s // per_b, rows.bsz)),
        _const_spec((1, d)),
        _const_spec((d, f)),
        _const_spec((d, f)),
        _const_spec((f, d)),
        _const_spec((1, d)),
    ]
    args = x_args + [mods, g.reshape(1, d), w1, w3, w2, final_g.reshape(1, d)]
    out_specs = [pl.BlockSpec((tm, d), lambda s: (s, 0))]
    out_shape = [jax.ShapeDtypeStruct((rows.n_lat if final else rows.n_all, d), F32)]
    n_cast = min(FFN_CAST_SLABS, n_lat) if cast_jobs else 0
    for w, lead in cast_jobs:
        r, c = w.shape[-2:]
        assert r % (16 * n_cast) == 0 and len(lead) == w.ndim - 2
        in_specs.append(pl.BlockSpec((None,) * len(lead) + (r // n_cast, c),
                                     functools.partial(lambda lead, s: lead + (jnp.minimum(s, n_cast - 1), 0), lead)))
        out_specs.append(pl.BlockSpec((r // n_cast, c), lambda s: (jnp.minimum(s, n_cast - 1), 0)))
        out_shape.append(jax.ShapeDtypeStruct((r, c), BF16))
        args.append(w)
    outs = pl.pallas_call(
        functools.partial(_ffn_kernel, base=base, final=final, n_lat=n_lat, split_input=split_input,
                          n_cast=n_cast),
        grid=(n_lat + n_ctx,),
        in_specs=in_specs,
        out_specs=out_specs,
        out_shape=out_shape,
        compiler_params=_params(1),
        name="half_ffn",
    )(*args)
    return outs[0], tuple(outs[1:])


def _ret_proj_kernel(xc_ref, xl_ref, mod_ref, g_ref, w_ref, cos_ref, sin_ref, q_ref, k_ref, v_ref, gate_ref,
                     *, n_heads, dk, dv, ctx_tiles):
    x = jnp.where(pl.program_id(1) < ctx_tiles, xc_ref[...], xl_ref[...])
    u = _adaln(x, g_ref[...], mod_ref[3:4, :], mod_ref[4:5, :]).astype(BF16)
    p = _mm(u, w_ref[...])
    cos = cos_ref[...]
    sin = sin_ref[...]
    half = dk // 2
    qk = n_heads * dk
    k_scale = dk ** -0.5

    def rot(a):
        a1, a2 = a[:, :half], a[:, half:]
        return jnp.concatenate([a1 * cos - a2 * sin, a2 * cos + a1 * sin], axis=-1)

    for h in range(n_heads):
        q_ref[:, h * dk:(h + 1) * dk] = rot(p[:, h * dk:(h + 1) * dk]).astype(BF16)
        k_ref[:, h * dk:(h + 1) * dk] = (rot(p[:, qk + h * dk:qk + (h + 1) * dk]) * k_scale).astype(BF16)
    v_ref[...] = p[:, 2 * qk:2 * qk + n_heads * dv].astype(BF16)
    gate_ref[...] = p[:, 2 * qk + n_heads * dv:].astype(BF16)


def _ret_scan_kernel(q_ref, k_ref, v_ref, dmat_ref, qdec_ref, kdec_ref, cdec_ref, o_ref,
                     state_ref, fwd_ref, *, n_heads, dk, dv, n_chunks, bwd_chunk):
    s = pl.program_id(1)
    cs = q_ref.shape[0]

    @pl.when((s == 0) | (s == n_chunks))
    def _():
        state_ref[...] = jnp.zeros_like(state_ref)

    chunk = jnp.where(s < n_chunks, s, bwd_chunk(jnp.maximum(s - n_chunks, 0)))
    rows = pl.ds(pl.multiple_of(chunk * cs, cs), cs)
    outs = []
    for h in range(n_heads):
        q = q_ref[:, h * dk:(h + 1) * dk]
        k = k_ref[:, h * dk:(h + 1) * dk]
        v = v_ref[:, h * dv:(h + 1) * dv]
        scores = lax.dot_general(q, k, (((1,), (1,)), ((), ())), preferred_element_type=F32) * dmat_ref[h]
        state = state_ref[h]
        outs.append(_mm(scores.astype(BF16), v) + _mm(q, state.astype(BF16)) * qdec_ref[h])
        kd = (k.astype(F32) * kdec_ref[h]).astype(BF16)
        state_ref[h] = state * cdec_ref[h] + lax.dot_general(
            kd, v, (((0,), (0,)), ((), ())), preferred_element_type=F32)

    @pl.when(s < n_chunks)
    def _():
        for h in range(n_heads):
            fwd_ref[rows, h * dv:(h + 1) * dv] = outs[h].astype(fwd_ref.dtype)

    @pl.when(s >= n_chunks)
    def _():
        for h in range(n_heads):
            o = outs[h] + fwd_ref[rows, h * dv:(h + 1) * dv].astype(F32)
            mu = jnp.mean(o, axis=-1, keepdims=True)
            oc = o - mu
            var = jnp.mean(oc * oc, axis=-1, keepdims=True)
            o_ref[:, h * dv:(h + 1) * dv] = (oc * lax.rsqrt(var + NORM_EPS)).astype(o_ref.dtype)


def _ret_out_kernel(x_ref, mod_ref, y_ref, gate_ref, w_ref, o_ref):
    z = (_silu(gate_ref[...].astype(F32)) * y_ref[...].astype(F32)).astype(BF16)
    o_ref[...] = x_ref[...] + mod_ref[5:6, :] * _mm(z, w_ref[...])


def _retention_layer(h, rows, mods, layer, g, w_in, w_out, decay, ctx_out):
    bsz, t_lat, t_ctx = rows
    d = h.shape[1]
    t_all = t_ctx + t_lat
    n_heads = RET_HEADS
    dk = d // n_heads
    dv = 2 * d // n_heads
    qk = n_heads * dk
    vw = n_heads * dv
    tm = RET_TILE
    assert t_ctx % tm == 0 and t_lat % tm == 0
    ctx_tiles = t_ctx // tm
    ctx_row = bsz
    lat_blk, ctx_blk = rows.block("lat", tm), rows.block("ctx", tm)

    pos = jnp.arange(t_all, dtype=F32)
    inv = 1.0 / (ROPE_BASE ** jnp.linspace(0.0, 1.0, dk // 2, dtype=F32))
    ang = pos[:, None] * inv[None, :]

    q, k, v, gate = pl.pallas_call(
        functools.partial(_ret_proj_kernel, n_heads=n_heads, dk=dk, dv=dv, ctx_tiles=ctx_tiles),
        grid=(bsz, t_all // tm),
        in_specs=[
            pl.BlockSpec((tm, d), lambda b, i: (ctx_blk(b, jnp.minimum(i, ctx_tiles - 1)), 0)),
            pl.BlockSpec((tm, d), lambda b, i: (lat_blk(b, jnp.maximum(i - ctx_tiles, 0)), 0)),
            _mod_spec(d, layer, lambda b, i: jnp.where(i < ctx_tiles, ctx_row, b)),
            _const_spec((1, d)),
            _const_spec((d, 2 * qk + 2 * vw)),
            pl.BlockSpec((tm, dk // 2), lambda b, i: (i, 0)),
            pl.BlockSpec((tm, dk // 2), lambda b, i: (i, 0)),
        ],
        out_specs=[
            pl.BlockSpec((None, tm, qk), lambda b, i: (b, i, 0)),
            pl.BlockSpec((None, tm, qk), lambda b, i: (b, i, 0)),
            pl.BlockSpec((None, tm, vw), lambda b, i: (b, i, 0)),
            pl.BlockSpec((None, tm, vw), lambda b, i: (b, i, 0)),
        ],
        out_shape=[
            jax.ShapeDtypeStruct((bsz, t_all, qk), BF16),
            jax.ShapeDtypeStruct((bsz, t_all, qk), BF16),
            jax.ShapeDtypeStruct((bsz, t_all, vw), BF16),
            jax.ShapeDtypeStruct((bsz, t_all, vw), BF16),
        ],
        compiler_params=_params(2),
        name="ret_proj",
    )(h, h, mods, g.reshape(1, d), w_in, jnp.cos(ang), jnp.sin(ang))

    cs = SCAN_CHUNK
    assert t_ctx % cs == 0 and t_lat % cs == 0
    n_chunks = t_all // cs
    ctx_chunks = t_ctx // cs
    lg = jax.nn.log_sigmoid(decay.astype(F32))[:, :, None]
    n = jnp.arange(cs, dtype=F32)
    rel = n[:, None] - n[None, :]
    dm_f = jnp.where(rel >= 0, jnp.exp(lg[0][:, :, None] * jnp.maximum(rel, 0.0)), 0.0)
    dm_b = jnp.where(rel < 0, jnp.exp(lg[1][:, :, None] * jnp.maximum(-rel, 0.0)), 0.0)
    dmat = jnp.stack([dm_f, dm_b])
    qd = jnp.stack([jnp.exp(lg[0] * (n + 1.0)), jnp.exp(lg[1] * (cs - n))])
    kd = jnp.stack([jnp.exp(lg[0] * (cs - 1.0 - n)), jnp.exp(lg[1] * n)])
    cd = jnp.exp(lg * cs)
    qdec = jnp.broadcast_to(qd[..., None], (2, n_heads, cs, dv))
    kdec = jnp.broadcast_to(kd[..., None], (2, n_heads, cs, dk))
    cdec = jnp.broadcast_to(cd[..., None], (2, n_heads, 1, dv))

    def bwd_chunk(j):
        return jnp.where(j < ctx_chunks, ctx_chunks - 1 - j, n_chunks - 1 + ctx_chunks - j)

    def chunk(s):
        return jnp.where(s < n_chunks, s, bwd_chunk(jnp.maximum(s - n_chunks, 0)))

    def out_chunk(s):
        return bwd_chunk(jnp.maximum(s - n_chunks, 0))

    def table_spec(*shape):
        return pl.BlockSpec((None,) + shape, lambda b, s: (s // n_chunks, 0, 0, 0))

    y = pl.pallas_call(
        functools.partial(_ret_scan_kernel, n_heads=n_heads, dk=dk, dv=dv, n_chunks=n_chunks,
                          bwd_chunk=bwd_chunk),
        grid=(bsz, 2 * n_chunks),
        in_specs=[
            pl.BlockSpec((None, cs, qk), lambda b, s: (b, chunk(s), 0)),
            pl.BlockSpec((None, cs, qk), lambda b, s: (b, chunk(s), 0)),
            pl.BlockSpec((None, cs, vw), lambda b, s: (b, chunk(s), 0)),
            table_spec(n_heads, cs, cs),
            table_spec(n_heads, cs, dv),
            table_spec(n_heads, cs, dk),
            table_spec(n_heads, 1, dv),
        ],
        out_specs=pl.BlockSpec((None, cs, vw), lambda b, s: (b, out_chunk(s), 0)),
        out_shape=jax.ShapeDtypeStruct((bsz, t_all, vw), BF16),
        scratch_shapes=[pltpu.VMEM((n_heads, dk, dv), F32), pltpu.VMEM((t_all, vw), BF16)],
        compiler_params=_params(2),
        name="ret_scan",
    )(q, k, v, dmat, qdec, kdec, cdec)

    first = 0 if ctx_out else ctx_tiles

    def res_blk(b, i):
        j = i + first
        return jnp.where(j < ctx_tiles, ctx_blk(b, jnp.minimum(j, ctx_tiles - 1)),
                         lat_blk(b, jnp.maximum(j - ctx_tiles, 0)))

    return pl.pallas_call(
        _ret_out_kernel,
        grid=(bsz, t_all // tm - first),
        in_specs=[
            pl.BlockSpec((tm, d), lambda b, i: (res_blk(b, i), 0)),
            _mod_spec(d, layer, lambda b, i: jnp.where(i + first < ctx_tiles, ctx_row, b)),
            pl.BlockSpec((None, tm, vw), lambda b, i: (b, i + first, 0)),
            pl.BlockSpec((None, tm, vw), lambda b, i: (b, i + first, 0)),
            _const_spec((vw, d)),
        ],
        out_specs=pl.BlockSpec((tm, d), lambda b, i: (res_blk(b, i), 0)),
        out_shape=jax.ShapeDtypeStruct(h.shape, F32),
        compiler_params=_params(2),
        name="ret_out",
    )(h, mods, y, gate, w_out)


def _pool_tables(seg_len):
    assert POOL_TILE % seg_len == 0
    r = np.arange(POOL_TILE)
    seg, pos = r // seg_len, r % seg_len
    mats, invs = [], []
    for win in POOL_WINDOWS:
        lo = np.clip(pos - win // 2, 0, seg_len)
        hi = np.clip(pos - win // 2 + win, 0, seg_len)
        col_seg, col_pos = seg[None, :], pos[None, :]
        member = (col_seg == seg[:, None]) & (col_pos >= lo[:, None]) & (col_pos < hi[:, None])
        mats.append(member.astype(np.float32))
        invs.append((1.0 / (hi - lo)).astype(np.float32))
    return np.stack(mats), np.stack(invs)


def _pool_kernel(x_ref, mod_ref, g_ref, s_ref, inv_ref, w_ref, b_ref, sc_ref, o_ref, *, n_groups, dg):
    x = x_ref[...]
    u = _adaln(x, g_ref[...], mod_ref[3:4, :], mod_ref[4:5, :])
    gate = mod_ref[5:6, :]
    for gi in range(n_groups):
        sl = slice(gi * dg, (gi + 1) * dg)
        ug = u[:, sl]
        hi = ug.astype(BF16)
        lo = (ug - hi.astype(F32)).astype(BF16)
        win_sum = _mm(s_ref[gi], hi) + _mm(s_ref[gi], lo)
        dlt = win_sum * inv_ref[gi] - ug
        yg = _mm(dlt.astype(BF16), w_ref[gi])
        o_ref[:, sl] = x[:, sl] + gate[:, sl] * ((yg + b_ref[:, sl]) * sc_ref[:, sl])


def _pool_layer(h, rows, lat_seg, mods, layer, g, w_grp, b_grp, scale, ctx_out):
    bsz, t_lat, t_ctx = rows
    d = h.shape[1]
    n_groups = len(POOL_WINDOWS)
    dg = d // n_groups
    tabs = [_pool_tables(lat_seg), _pool_tables(t_ctx)]
    s_tab = jnp.asarray(np.stack([m for m, _ in tabs]), BF16)
    inv_tab = jnp.asarray(np.stack([np.broadcast_to(v[:, :, None], (n_groups, POOL_TILE, dg)) for _, v in tabs]), F32)
    lat_tiles = t_lat // POOL_TILE
    ctx_tiles = t_ctx // POOL_TILE if ctx_out else 0
    lat_blk, ctx_blk = rows.block("lat", POOL_TILE), rows.block("ctx", POOL_TILE)

    def res_blk(b, i):
        return jnp.where(i < lat_tiles, lat_blk(b, jnp.minimum(i, lat_tiles - 1)),
                         ctx_blk(b, jnp.maximum(i - lat_tiles, 0)))

    def table_spec(*shape):
        return pl.BlockSpec((None,) + shape, lambda b, i: (jnp.where(i < lat_tiles, 0, 1), 0, 0, 0))

    return pl.pallas_call(
        functools.partial(_pool_kernel, n_groups=n_groups, dg=dg),
        grid=(bsz, lat_tiles + ctx_tiles),
        in_specs=[
            pl.BlockSpec((POOL_TILE, d), lambda b, i: (res_blk(b, i), 0)),
            _mod_spec(d, layer, lambda b, i: jnp.where(i < lat_tiles, b, bsz)),
            _const_spec((1, d)),
            table_spec(n_groups, POOL_TILE, POOL_TILE),
            table_spec(n_groups, POOL_TILE, dg),
            _const_spec((n_groups, dg, dg)),
            _const_spec((1, d)),
            _const_spec((1, d)),
        ],
        out_specs=pl.BlockSpec((POOL_TILE, d), lambda b, i: (res_blk(b, i), 0)),
        out_shape=jax.ShapeDtypeStruct(h.shape, F32),
        compiler_params=_params(2),
        name="pool",
    )(h, mods, g.reshape(1, d), s_tab, inv_tab, w_grp.astype(BF16), b_grp.reshape(1, d), scale.reshape(1, d))


def _dft_tables(n_in):
    mod = 4 * n_in
    sub_n = min(V7X_LANES, n_in)
    idx = jnp.arange(n_in, dtype=jnp.int32)[None, :]
    blk = jnp.arange(n_in // sub_n, dtype=jnp.int32)[:, None]
    sub = jnp.arange(sub_n, dtype=jnp.int32)[:, None]
    theta = 2.0 * math.pi / mod

    def cs(p):
        ph = (p % mod).astype(F32) * theta
        return jnp.cos(ph), jnp.sin(ph)

    fwd = cs(2 * sub_n * blk * idx) + cs((2 * sub + 1) * idx)
    inv = cs(sub_n * blk * (2 * idx + 1)) + cs(sub * (2 * idx + 1))
    return fwd, inv


def _dft_fwd_gen_kernel(pc_ref, ps_ref, qc_ref, qs_ref, o_ref, *, tf):
    ft = pl.program_id(0)
    sub_n = qc_ref.shape[0]
    blocks = tf // sub_n
    qc = qc_ref[...]
    qs = qs_ref[...]
    for jb in range(blocks):
        kb = ft * blocks + jb
        pc = pc_ref[pl.ds(kb, 1), :]
        ps = ps_ref[pl.ds(kb, 1), :]
        o_ref[jb * sub_n:(jb + 1) * sub_n, :] = (pc * qc - ps * qs).astype(BF16)
        o_ref[tf + jb * sub_n:tf + (jb + 1) * sub_n, :] = (ps * qc + pc * qs).astype(BF16)


def _dft_inv_gen_kernel(pc_ref, ps_ref, qc_ref, qs_ref, o_ref, *, tf, n_in):
    mb = pl.program_id(0)
    pc = pc_ref[pl.ds(mb, 1), :]
    ps = ps_ref[pl.ds(mb, 1), :]
    qc = qc_ref[...]
    qs = qs_ref[...]
    cos_all = pc * qc - ps * qs
    sin_all = ps * qc + pc * qs
    for ft in range(n_in // tf):
        o_ref[:, 2 * ft * tf:(2 * ft + 1) * tf] = cos_all[:, ft * tf:(ft + 1) * tf].astype(BF16)
        o_ref[:, (2 * ft + 1) * tf:(2 * ft + 2) * tf] = sin_all[:, ft * tf:(ft + 1) * tf].astype(BF16)


def _dft_matrices(n_in, tf):
    fwd_tabs, inv_tabs = _dft_tables(n_in)
    sub_n = min(V7X_LANES, n_in)
    nb = n_in // sub_n
    tabs = [_const_spec((nb, n_in)), _const_spec((nb, n_in)),
            _const_spec((sub_n, n_in)), _const_spec((sub_n, n_in))]
    fwd = pl.pallas_call(
        functools.partial(_dft_fwd_gen_kernel, tf=tf),
        grid=(n_in // tf,),
        in_specs=tabs,
        out_specs=pl.BlockSpec((2 * tf, n_in), lambda i: (i, 0)),
        out_shape=jax.ShapeDtypeStruct((2 * n_in, n_in), BF16),
        compiler_params=_params(1),
        name="dft_fwd_gen",
    )(*fwd_tabs)
    inv = pl.pallas_call(
        functools.partial(_dft_inv_gen_kernel, tf=tf, n_in=n_in),
        grid=(nb,),
        in_specs=tabs,
        out_specs=pl.BlockSpec((sub_n, 2 * n_in), lambda i: (i, 0)),
        out_shape=jax.ShapeDtypeStruct((n_in, 2 * n_in), BF16),
        compiler_params=_params(1),
        name="dft_inv_gen",
    )(*inv_tabs)
    return fwd, inv


def _hy_proj_kernel(x_ref, xp_ref, xn_ref, mod_ref, g_ref, perm_ref, w_ref, b_ref, ws_ref, bs_ref,
                    v_ref, x1_ref, x2_ref):
    i = pl.program_id(1)
    tm, d = x_ref.shape
    n_ph = v_ref.shape[0]
    hp = tm // n_ph
    g = g_ref[...]
    shift, scale = mod_ref[3:4, :], mod_ref[4:5, :]
    u = _adaln(x_ref[...], g, shift, scale).astype(BF16)
    u = _mm(perm_ref[...], u).astype(BF16)
    halo = jnp.concatenate([xp_ref[...], xn_ref[...]], axis=0)
    uh = _adaln(halo, g, shift, scale).astype(BF16)
    p = _mm(jnp.concatenate([u, uh], axis=0), w_ref[...]) + b_ref[...]
    ph = [p[r * hp:(r + 1) * hp] for r in range(n_ph)]
    n_halo = xp_ref.shape[0]
    before = jnp.where(i == 0, 0.0, p[tm + n_halo - 1:tm + n_halo])
    after = jnp.where(i == pl.num_programs(1) - 1, 0.0, p[tm + n_halo:tm + n_halo + 1])
    row = lax.broadcasted_iota(jnp.int32, ph[0].shape, 0)
    last_prev = jnp.where(row == 0, before, pltpu.roll(ph[-1], 1, axis=0))
    first_next = jnp.where(row == hp - 1, after, pltpu.roll(ph[0], hp - 1, axis=0))
    w0, w1, w2, bs = ws_ref[0:1, :], ws_ref[1:2, :], ws_ref[2:3, :], bs_ref[...]
    for r in range(n_ph):
        prev = ph[r - 1] if r > 0 else last_prev
        nxt = ph[r + 1] if r < n_ph - 1 else first_next
        conv = prev * w0 + ph[r] * w1 + nxt * w2 + bs
        v_ref[r] = conv[:, :d].astype(BF16)
        x1_ref[r] = conv[:, d:2 * d].astype(BF16)
        x2_ref[r] = conv[:, 2 * d:].astype(BF16)


def _hy_filter_kernel(feat_ref, wp_ref, bp_ref, wm_ref, bm_ref, fr_ref, wf_ref, dl_ref,
                      hf_ref, hb_ref, ss_ref, *, d, n_orders):
    start = (pl.program_id(0) == 0) & (pl.program_id(1) == 0)
    feat = feat_ref[...]
    fr = fr_ref[...]
    hdn = jnp.sin(fr * (_mm_f32(feat, wp_ref[...]) + bp_ref[...]))
    hdn = jnp.sin(fr * (_mm_f32(hdn, wm_ref[...]) + bm_ref[...]))
    h = _mm(hdn.astype(BF16), wf_ref[...])
    decay = jnp.exp(-feat[:, 0:1] * dl_ref[...])
    row = lax.broadcasted_iota(jnp.int32, decay.shape, 0)
    first = (row == 0) & start

    @pl.when(start)
    def _():
        ss_ref[...] = jnp.zeros_like(ss_ref)

    for o in range(n_orders):
        sl = slice(o * d, (o + 1) * d)
        hf = h[:, 2 * o * d:(2 * o + 1) * d] * decay
        hb = jnp.where(first, 0.0, h[:, (2 * o + 1) * d:(2 * o + 2) * d] * decay)
        hf_ref[:, sl] = hf.astype(BF16)
        hb_ref[:, sl] = hb.astype(BF16)
        ss_ref[:, sl] += jnp.sum(hf * hf + hb * hb, axis=0, keepdims=True)


def _twiddle(a, c, s, conj=False):
    a_re, a_s = a
    if conj:
        return a_re * c + a_s * s, a_s * c - a_re * s
    return a_re * c - a_s * s, a_s * c + a_re * s


def _add(a, b):
    return a[0] + b[0], a[1] + b[1]


def _sub(a, b):
    return a[0] - b[0], a[1] - b[1]


def _dit_blocks(dft, phase_refs, tw, tp):
    ph = []
    for ref in phase_refs:
        t = _mm(dft, ref[...])
        ph.append((t[0:tp], t[tp:]))
    c1, s1, c2, s2, c3, s3 = tw
    t2, t3 = _twiddle(ph[2], c1, s1), _twiddle(ph[3], c1, s1)
    e_a, e_b = _add(ph[0], t2), _sub(ph[0], t2)
    o_a, o_b = _add(ph[1], t3), _sub(ph[1], t3)
    ta, tb = _twiddle(o_a, c2, s2), _twiddle(o_b, c3, s3, conj=True)
    return [_add(e_a, ta), _sub(e_a, ta), _add(e_b, tb), _sub(e_b, tb)]


def _bin_group(f_ref, tw_ref, g, tp):
    re, sn = slice(2 * g * tp, (2 * g + 1) * tp), slice((2 * g + 1) * tp, (2 * g + 2) * tp)
    dft = f_ref[2 * g * tp:(2 * g + 2) * tp, :]
    tw = [tw_ref[i, g * tp:(g + 1) * tp, :] for i in range(6)]
    return re, sn, dft, tw


def _hy_spec_kernel(f_ref, f0, f1, f2, f3, b0, b1, b2, b3, tw_ref, ss_ref, o_ref, *, tp, scale):
    wgt = scale * lax.rsqrt(ss_ref[...] + NORM_EPS)
    for g in range(tw_ref.shape[1] // tp):
        re, sn, dft, tw = _bin_group(f_ref, tw_ref, g, tp)
        fwd = _dit_blocks(dft, (f0, f1, f2, f3), tw, tp)
        bwd = _dit_blocks(dft, (b0, b1, b2, b3), tw, tp)
        for blk in range(4):
            o_ref[blk, re, :] = (fwd[blk][0] + bwd[blk][0]) * wgt
            o_ref[blk, sn, :] = (fwd[blk][1] - bwd[blk][1]) * wgt


def _hy_fwd_kernel(f_ref, u0, u1, u2, u3, h_ref, tw_ref, y_ref, *, tp):
    for g in range(tw_ref.shape[1] // tp):
        re, sn, dft, tw = _bin_group(f_ref, tw_ref, g, tp)
        c1, s1, c2, s2, c3, s3 = tw
        x = _dit_blocks(dft, (u0, u1, u2, u3), tw, tp)
        y = []
        for blk in range(4):
            h_re, h_s = h_ref[blk, re, :], h_ref[blk, sn, :]
            x_re, x_s = x[blk]
            y.append((x_re * h_re - x_s * h_s, x_re * h_s + x_s * h_re))
        e_a, o_a = _add(y[0], y[1]), _twiddle(_sub(y[0], y[1]), c2, s2, conj=True)
        e_b, o_b = _add(y[2], y[3]), _twiddle(_sub(y[2], y[3]), c3, s3)
        out = [_add(e_a, e_b), _add(o_a, o_b),
               _twiddle(_sub(e_a, e_b), c1, s1, conj=True), _twiddle(_sub(o_a, o_b), c1, s1, conj=True)]
        for r in range(4):
            y_ref[r, re, :] = out[r][0].astype(BF16)
            y_ref[r, sn, :] = out[r][1].astype(BF16)


def _hy_inv_kernel(f_ref, y_ref, a_ref, u_ref, fb_ref, o_ref, *, rows):
    for r in range(f_ref.shape[0] // rows):
        sl = slice(r * rows, (r + 1) * rows)
        conv = _mm(f_ref[sl, :], y_ref[...])
        z = a_ref[sl, :].astype(F32) * (conv + u_ref[sl, :].astype(F32) * fb_ref[...])
        o_ref[sl, :] = z.astype(o_ref.dtype)


def _hy_out_kernel(x_ref, mod_ref, z_ref, pick_ref, w_ref, b_ref, *refs):
    o_ref = refs[-1]
    z = _mm(pick_ref[0], z_ref[0])
    for r in range(1, z_ref.shape[0]):
        z = z + _mm(pick_ref[r], z_ref[r])
    o_ref[...] = x_ref[...] + mod_ref[5:6, :] * (_mm(z.astype(BF16), w_ref[...]) + b_ref[...])


def _hyena_layer(h, rows, part, out_buf, mods, layer, g, w_in, b_in, w_short, b_short, w_pos, b_pos, w_mid,
                 b_mid, freq, w_filt, fbias, w_out, b_out):
    bsz = rows.bsz
    length = rows.length(part)
    d = h.shape[1]
    mod_row = (lambda b, i: b) if part == "lat" else (lambda b, i: bsz)
    n_full = 2 * length
    n_ph = DFT_RADIX
    q = length // n_ph
    tm = min(512, length)
    hp = tm // n_ph
    tc = DFT_COL_TILE
    n_ct = d // tc
    width = w_mid.shape[0]
    n_parts = 2 * HYENA_ORDER

    halo = 8
    m_idx = np.arange(hp)
    perm = np.zeros((tm, tm), np.float32)
    pick = np.zeros((n_ph, tm, hp), np.float32)
    for r in range(n_ph):
        perm[r * hp + m_idx, n_ph * m_idx + r] = 1.0
        pick[r, n_ph * m_idx + r, m_idx] = 1.0
    ph_spec = pl.BlockSpec((None, n_ph, hp, d), lambda b, i: (b, 0, i, 0))
    tile_blk, halo_blk = rows.block(part, tm), rows.block(part, halo)
    v, x1, x2 = pl.pallas_call(
        _hy_proj_kernel,
        grid=(bsz, length // tm),
        in_specs=[
            pl.BlockSpec((tm, d), lambda b, i: (tile_blk(b, i), 0)),
            pl.BlockSpec((halo, d), lambda b, i: (halo_blk(b, jnp.maximum(i * (tm // halo) - 1, 0)), 0)),
            pl.BlockSpec((halo, d),
                         lambda b, i: (halo_blk(b, jnp.minimum((i + 1) * (tm // halo), length // halo - 1)), 0)),
            _mod_spec(d, layer, mod_row),
            _const_spec((1, d)),
            _const_spec((tm, tm)),
            _const_spec((d, 3 * d)),
            _const_spec((1, 3 * d)),
            _const_spec((3, 3 * d)),
            _const_spec((1, 3 * d)),
        ],
        out_specs=[ph_spec] * 3,
        out_shape=[jax.ShapeDtypeStruct((bsz, n_ph, q, d), BF16)] * 3,
        compiler_params=_params(2),
        name="hy_proj",
    )(h, h, h, mods, g.reshape(1, d), jnp.asarray(perm, BF16), w_in, b_in.reshape(1, 3 * d),
      w_short, b_short.reshape(1, 3 * d))

    t = jnp.linspace(0.0, 1.0, length, dtype=F32)[:, None]
    bands = jnp.linspace(1e-4, HYENA_BANDS - 1, HYENA_BANDS, dtype=F32)
    ang = (2.0 * math.pi / length) * jnp.arange(length, dtype=F32)[:, None] * bands[None, :]
    feat = jnp.concatenate([t, jnp.cos(ang), -jnp.sin(ang)], axis=-1)
    feat = jnp.pad(feat, ((0, 0), (0, V7X_LANES - HYENA_EMB)))
    w_pos_p = jnp.pad(w_pos.astype(F32), ((0, V7X_LANES - HYENA_EMB), (0, 0)))
    deltas = jnp.abs(jnp.linspace(HYENA_MIN_DECAY, HYENA_MAX_DECAY, d, dtype=F32)).reshape(1, d)
    feat = jnp.concatenate([feat[r::n_ph] for r in range(n_ph)], axis=0)
    tl = min(512, q)
    od = HYENA_ORDER * d
    tap_spec = pl.BlockSpec((tl, od), lambda r, i: (i, r))
    taps_f, taps_b, ss = pl.pallas_call(
        functools.partial(_hy_filter_kernel, d=d, n_orders=HYENA_ORDER),
        grid=(n_ph, q // tl),
        in_specs=[
            pl.BlockSpec((tl, V7X_LANES), lambda r, i: (r * (q // tl) + i, 0)),
            _const_spec((V7X_LANES, width)),
            _const_spec((1, width)),
            _const_spec((width, width)),
            _const_spec((1, width)),
            _const_spec((1, width)),
            _const_spec((width, n_parts * d)),
            _const_spec((1, d)),
        ],
        out_specs=[tap_spec, tap_spec, pl.BlockSpec((1, od), lambda r, i: (0, 0))],
        out_shape=[jax.ShapeDtypeStruct((q, n_ph * od), BF16), jax.ShapeDtypeStruct((q, n_ph * od), BF16),
                   jax.ShapeDtypeStruct((1, od), F32)],
        compiler_params=_params(2),
        name="hy_filter",
    )(feat, w_pos_p, b_pos.reshape(1, width), w_mid, b_mid.reshape(1, width), freq.reshape(1, width),
      w_filt.astype(BF16), deltas)

    tf = min(DFT_FREQ_TILE, q)
    tp = min(DFT_PACK, q)
    tn = min(DFT_TIME_TILE, q)
    dft_fwd, dft_inv = _dft_matrices(q, tp)
    odd = 2.0 * jnp.arange(q, dtype=F32) + 1.0
    angles = jnp.stack([(2.0 * math.pi / n_full) * odd, (math.pi / n_full) * odd,
                        (math.pi / n_full) * (length - odd)])
    tw = jnp.stack([jnp.cos(angles), jnp.sin(angles)], axis=1).reshape(6, q)
    tw = jnp.broadcast_to(tw[:, :, None], (6, q, tc))
    tw_spec = pl.BlockSpec((6, tf, tc), lambda f, *_: (0, f, 0))

    n_oc = od // tc

    def tap_phase(r):
        return pl.BlockSpec((q, tc), lambda f, o, j: (0, r * n_oc + o * n_ct + j))

    spec = pl.pallas_call(
        functools.partial(_hy_spec_kernel, tp=tp, scale=1.0 / length),
        grid=(q // tf, HYENA_ORDER, n_ct),
        in_specs=[pl.BlockSpec((2 * tf, q), lambda f, o, j: (f, 0))]
        + [tap_phase(r) for r in range(n_ph)] * 2
        + [tw_spec, pl.BlockSpec((1, tc), lambda f, o, j: (0, o * n_ct + j))],
        out_specs=pl.BlockSpec((None, n_ph, 2 * tf, tc), lambda f, o, j: (o, 0, f, j)),
        out_shape=jax.ShapeDtypeStruct((HYENA_ORDER, n_ph, 2 * q, d), F32),
        compiler_params=_params(3),
        name="hy_spec",
    )(dft_fwd, *([taps_f] * n_ph), *([taps_b] * n_ph), tw, ss)

    tci = d if q < DFT_FREQ_TILE else 2 * tc
    n_ci = d // tci

    def long_conv(order, u, a):
        yspec = pl.pallas_call(
            functools.partial(_hy_fwd_kernel, tp=tp),
            grid=(q // tf, n_ct, bsz),
            in_specs=[pl.BlockSpec((2 * tf, q), lambda f, j, b: (f, 0))]
            + [pl.BlockSpec((None, None, q, tc), functools.partial(lambda r, f, j, b: (b, r, 0, j), r))
               for r in range(n_ph)]
            + [pl.BlockSpec((None, n_ph, 2 * tf, tc), lambda f, j, b: (order, 0, f, j)), tw_spec],
            out_specs=pl.BlockSpec((None, n_ph, 2 * tf, tc), lambda f, j, b: (b, 0, f, j)),
            out_shape=jax.ShapeDtypeStruct((bsz, n_ph, 2 * q, d), BF16),
            compiler_params=_params(3),
            name="hy_fwd",
        )(dft_fwd, *([u] * n_ph), spec, tw)
        row_spec = pl.BlockSpec((None, None, tn, tci), lambda n, j, b, r: (b, r, n, j))
        return pl.pallas_call(
            functools.partial(_hy_inv_kernel, rows=min(DFT_ROW_GROUP, tn)),
            grid=(q // tn, n_ci, bsz, n_ph),
            in_specs=[
                pl.BlockSpec((tn, 2 * q), lambda n, j, b, r: (n, 0)),
                pl.BlockSpec((None, None, 2 * q, tci), lambda n, j, b, r: (b, r, 0, j)),
                row_spec,
                row_spec,
                pl.BlockSpec((None, 1, tci), lambda n, j, b, r: (order, 0, j)),
            ],
            out_specs=row_spec,
            out_shape=jax.ShapeDtypeStruct((bsz, n_ph, q, d), BF16),
            compiler_params=_params(4),
            name="hy_inv",
        )(dft_inv, yspec, a, u, fbias.astype(F32).reshape(HYENA_ORDER, 1, d))

    z = long_conv(0, v, x1)
    zz = long_conv(1, z, x2)

    in_specs = [
        pl.BlockSpec((tm, d), lambda b, i: (tile_blk(b, i), 0)),
        _mod_spec(d, layer, mod_row),
        pl.BlockSpec((None, n_ph, hp, d), lambda b, i: (b, 0, i, 0)),
        _const_spec((n_ph, tm, hp)),
        _const_spec((d, d)),
        _const_spec((1, d)),
    ]
    args = [h, mods, zz, jnp.asarray(pick, BF16), w_out, b_out.reshape(1, d)]
    aliases = {}
    if out_buf is not None:
        in_specs.append(pl.BlockSpec(memory_space=pl.ANY))
        args.append(out_buf)
        aliases = {len(args) - 1: 0}
    return pl.pallas_call(
        _hy_out_kernel,
        grid=(bsz, length // tm),
        in_specs=in_specs,
        out_specs=pl.BlockSpec((tm, d), lambda b, i: (tile_blk(b, i), 0)),
        out_shape=jax.ShapeDtypeStruct(h.shape, F32),
        input_output_aliases=aliases,
        compiler_params=_params(2),
        name="hy_out",
    )(*args)


def kernel(x, c, ctx, c_ctx, ada_w, ada_b, norm_g, ffn_w1, ffn_w3, ffn_w2, ret_w_in, ret_w_out, ret_decay,
           pool_w, pool_b, pool_scale, hy_w_in, hy_b_in, hy_w_short, hy_b_short, hy_w_pos, hy_b_pos,
           hy_w_mid, hy_b_mid, hy_freq, hy_w_filt, hy_bias, hy_w_out, hy_b_out, final_g):
    bsz, seq, d = x.shape
    t_ctx = ctx.shape[1]
    depth = ada_w.shape[0]
    n_mixers = 3
    assert bsz + 1 <= MOD_ROWS

    cond = jnp.concatenate([c, c_ctx[None, :], jnp.zeros((MOD_ROWS - bsz - 1, d), F32)], axis=0)
    mods = _ada_all(cond, ada_w, ada_b).reshape(depth, MOD_ROWS, ADA_CHUNKS, d)

    rows = _Rows(bsz, seq, t_ctx)
    ffn_stacks = (ffn_w1, ffn_w3, ffn_w2)
    ffn_weights = [tuple(w[0, 0].astype(BF16) for w in ffn_stacks)]

    def ffn(h, layer, half, ctx_live, mixer_jobs=(), final=False):
        jobs = [] if final else [(w, (layer, 1) if half == 0 else (layer + 1, 0)) for w in ffn_stacks]
        h, cast = _half_ffn(h, rows, mods, layer, half, norm_g[layer, 2 * half], ffn_weights[0], final_g,
                            ctx_live, cast_jobs=jobs + list(mixer_jobs), final=final)
        ffn_weights[0] = cast[:len(ffn_stacks)]
        return h, cast[len(ffn_stacks):]

    h = (x.reshape(bsz * seq, d), ctx.reshape(bsz * t_ctx, d))
    for layer in range(depth):
        kind = layer % n_mixers
        slot = layer // n_mixers
        last = layer == depth - 1
        ctx_out = not last
        ctx_live = ctx_out or kind == 0
        g_mix = norm_g[layer, 1]
        if kind == 0:
            h, (w_in, w_out) = ffn(h, layer, 0, ctx_live, [(ret_w_in, (slot,)), (ret_w_out, (slot,))])
            h = _retention_layer(h, rows, mods, layer, g_mix, w_in, w_out, ret_decay[slot], ctx_out)
        elif kind == 1:
            h, _ = ffn(h, layer, 0, ctx_live)
            h = _pool_layer(h, rows, GRID_W, mods, layer, g_mix, pool_w[slot], pool_b[slot], pool_scale[slot],
                            ctx_out)
        else:
            h, (w_in, w_out) = ffn(h, layer, 0, ctx_live, [(hy_w_in, (slot,)), (hy_w_out, (slot,))])
            hp = (w_in, hy_b_in[slot], hy_w_short[slot], hy_b_short[slot], hy_w_pos[slot],
                  hy_b_pos[slot], hy_w_mid[slot], hy_b_mid[slot], hy_freq[slot], hy_w_filt[slot],
                  hy_bias[slot], w_out, hy_b_out[slot])
            new_h = _hyena_layer(h, rows, "lat", None, mods, layer, g_mix, *hp)
            if ctx_out:
                new_h = _hyena_layer(h, rows, "ctx", new_h, mods, layer, g_mix, *hp)
            h = new_h
        h, _ = ffn(h, layer, 1, ctx_out, final=last)
    return h.reshape(bsz, seq, d)
```

```python
import functools
import math
from typing import NamedTuple

import numpy as np
import jax
import jax.numpy as jnp
from jax import lax
from jax.experimental import pallas as pl
from jax.experimental.pallas import tpu as pltpu

F32 = jnp.float32
BF16 = jnp.bfloat16

GRID_W = 64
ADA_CHUNKS = 9
NORM_EPS = 1e-6
RET_HEADS = 4
RET_CHUNK = 128
ROPE_BASE = 10000.0
POOL_WINDOWS = (2, 4, 8, 16)
HYENA_ORDER = 2
HYENA_EMB = 33
HYENA_BANDS = (HYENA_EMB - 1) // 2
HYENA_TARGET = 1e-2
HYENA_FAST = 0.3
HYENA_SLOW = 1.5
HYENA_MAX_DECAY = math.log(HYENA_TARGET) / HYENA_FAST
HYENA_MIN_DECAY = math.log(HYENA_TARGET) / HYENA_SLOW

V7X_LANES = 128
V7X_VMEM_LIMIT_BYTES = 56 * 1024 * 1024
MOD_ROWS = 8
FFN_CAST_SLABS = 16
POOL_TILE = 512
RET_TILE = 256
SCAN_CHUNK = 256
DFT_RADIX = 4
DFT_FREQ_TILE = 512
DFT_PACK = 256
DFT_ROW_GROUP = 512
DFT_TIME_TILE = 2048
DFT_COL_TILE = 256
HIGHEST = lax.Precision.HIGHEST


def _params(n_axes):
    return pltpu.CompilerParams(
        dimension_semantics=("arbitrary",) * n_axes,
        vmem_limit_bytes=V7X_VMEM_LIMIT_BYTES)


def _const_spec(shape):
    zeros = (0,) * len(shape)
    return pl.BlockSpec(shape, lambda *_: zeros, pipeline_mode=pl.Buffered(1))


def _mod_spec(d, layer, row_fn):
    return pl.BlockSpec((None, None, ADA_CHUNKS, d), lambda *idx: (layer, row_fn(*idx), 0, 0))


class _Rows(NamedTuple):
    bsz: int
    t_lat: int
    t_ctx: int

    @property
    def n_lat(self):
        return self.bsz * self.t_lat

    @property
    def n_all(self):
        return self.bsz * (self.t_lat + self.t_ctx)

    def length(self, part):
        return self.t_lat if part == "lat" else self.t_ctx

    def block(self, part, tile):
        per = self.length(part) // tile
        base = 0 if part == "lat" else self.n_lat // tile
        assert self.length(part) % tile == 0 and self.n_lat % tile == 0
        return lambda b, i: base + b * per + i


def _mm(a, b):
    return jnp.dot(a, b, preferred_element_type=F32)


def _mm_f32(a, b):
    return jnp.dot(a, b, preferred_element_type=F32, precision=HIGHEST)


def _silu(x):
    return x * jax.nn.sigmoid(x)


def _adaln(x, g, shift, scale):
    ms = jnp.mean(x * x, axis=-1, keepdims=True)
    y = x * lax.rsqrt(ms + NORM_EPS) * g
    return y * (1.0 + scale) + shift


def _ada_kernel(c_ref, w_ref, b_ref, o_ref):
    s = _silu(c_ref[...]).astype(BF16)
    o_ref[...] = _mm(s, w_ref[...].astype(BF16)) + b_ref[...]


def _ada_all(cond, ada_w, ada_b):
    depth, d, nd = ada_w.shape
    return pl.pallas_call(
        _ada_kernel,
        grid=(depth, nd // d),
        in_specs=[
            pl.BlockSpec((MOD_ROWS, d), lambda l, j: (0, 0)),
            pl.BlockSpec((None, d, d), lambda l, j: (l, 0, j)),
            pl.BlockSpec((None, 1, d), lambda l, j: (l, 0, j)),
        ],
        out_specs=pl.BlockSpec((None, MOD_ROWS, d), lambda l, j: (l, 0, j)),
        out_shape=jax.ShapeDtypeStruct((depth, MOD_ROWS, nd), F32),
        compiler_params=_params(2),
        name="ada",
    )(cond, ada_w, ada_b.reshape(depth, 1, nd))


def _ffn_kernel(*refs, base, final, n_lat, split_input, n_cast):
    n_x = 2 if split_input else 1
    x_refs, refs = refs[:n_x], refs[n_x:]
    mod_ref, g_ref, w1_ref, w3_ref, w2_ref, fg_ref = refs[:6]
    rest = refs[6:]
    if n_cast:
        n_jobs = (len(rest) - 1) // 2
        src_refs, o_ref, dst_refs = rest[:n_jobs], rest[n_jobs], rest[n_jobs + 1:]

        @pl.when(pl.program_id(0) < n_cast)
        def _():
            for src, dst in zip(src_refs, dst_refs):
                dst[...] = src[...].astype(BF16)
    else:
        (o_ref,) = rest
    if split_input:
        x = jnp.where(pl.program_id(0) >= n_lat, x_refs[1][...], x_refs[0][...])
    else:
        x = x_refs[0][...]
    u = _adaln(x, g_ref[...], mod_ref[base:base + 1, :], mod_ref[base + 1:base + 2, :]).astype(BF16)
    h1 = _mm(u, w1_ref[...])
    h3 = _mm(u, w3_ref[...])
    a = (_silu(h1) * h3).astype(BF16)
    y = x + (0.5 * mod_ref[base + 2:base + 3, :]) * _mm(a, w2_ref[...])
    if final:
        ms = jnp.mean(y * y, axis=-1, keepdims=True)
        y = y * lax.rsqrt(ms + NORM_EPS) * fg_ref[...]
    o_ref[...] = y


def _half_ffn(h, rows, mods, layer, half, g, weights, final_g, ctx_live, cast_jobs=(), final=False, tm=512):
    w1, w3, w2 = weights
    d, f = w1.shape
    base = 6 * half
    per_b = rows.t_lat // tm
    n_lat = rows.n_lat // tm
    n_ctx = rows.bsz * rows.t_ctx // tm if ctx_live else 0
    assert rows.t_lat % tm == 0 and (rows.bsz * rows.t_ctx) % tm == 0
    split_input = isinstance(h, tuple)

    if split_input:
        x_specs = [pl.BlockSpec((tm, d), lambda s: (jnp.minimum(s, n_lat - 1), 0)),
                   pl.BlockSpec((tm, d), lambda s: (jnp.maximum(s - n_lat, 0), 0))]
        x_args = list(h)
    else:
        x_specs = [pl.BlockSpec((tm, d), lambda s: (s, 0))]
        x_args = [h]
    in_specs = x_specs + [
        _mod_spec(d, layer, lambda s: jnp.where(s < n_lat, s // per_b, rows.bsz)),
        _const_spec((1, d)),
        _const_spec((d, f)),
        _const_spec((d, f)),
        _const_spec((f, d)),
        _const_spec((1, d)),
    ]
    args = x_args + [mods, g.reshape(1, d), w1, w3, w2, final_g.reshape(1, d)]
    out_specs = [pl.BlockSpec((tm, d), lambda s: (s, 0))]
    out_shape = [jax.ShapeDtypeStruct((rows.n_lat if final else rows.n_all, d), F32)]
    n_cast = min(FFN_CAST_SLABS, n_lat) if cast_jobs else 0
    for w, lead in cast_jobs:
        r, c = w.shape[-2:]
        assert r % (16 * n_cast) == 0 and len(lead) == w.ndim - 2
        in_specs.append(pl.BlockSpec((None,) * len(lead) + (r // n_cast, c),
                                     functools.partial(lambda lead, s: lead + (jnp.minimum(s, n_cast - 1), 0), lead)))
        out_specs.append(pl.BlockSpec((r // n_cast, c), lambda s: (jnp.minimum(s, n_cast - 1), 0)))
        out_shape.append(jax.ShapeDtypeStruct((r, c), BF16))
        args.append(w)
    outs = pl.pallas_call(
        functools.partial(_ffn_kernel, base=base, final=final, n_lat=n_lat, split_input=split_input,
                          n_cast=n_cast),
        grid=(n_lat + n_ctx,),
        in_specs=in_specs,
        out_specs=out_specs,
        out_shape=out_shape,
        compiler_params=_params(1),
        name="half_ffn",
    )(*args)
    return outs[0], tuple(outs[1:])


def _ret_proj_kernel(xc_ref, xl_ref, mod_ref, g_ref, w_ref, cos_ref, sin_ref, q_ref, k_ref, v_ref, gate_ref,
                     *, n_heads, dk, dv, ctx_tiles):
    x = jnp.where(pl.program_id(1) < ctx_tiles, xc_ref[...], xl_ref[...])
    u = _adaln(x, g_ref[...], mod_ref[3:4, :], mod_ref[4:5, :]).astype(BF16)
    p = _mm(u, w_ref[...])
    cos = cos_ref[...]
    sin = sin_ref[...]
    half = dk // 2
    qk = n_heads * dk
    k_scale = dk ** -0.5

    def rot(a):
        a1, a2 = a[:, :half], a[:, half:]
        return jnp.concatenate([a1 * cos - a2 * sin, a2 * cos + a1 * sin], axis=-1)

    for h in range(n_heads):
        q_ref[:, h * dk:(h + 1) * dk] = rot(p[:, h * dk:(h + 1) * dk]).astype(BF16)
        k_ref[:, h * dk:(h + 1) * dk] = (rot(p[:, qk + h * dk:qk + (h + 1) * dk]) * k_scale).astype(BF16)
    v_ref[...] = p[:, 2 * qk:2 * qk + n_heads * dv].astype(BF16)
    gate_ref[...] = p[:, 2 * qk + n_heads * dv:].astype(BF16)


def _ret_scan_kernel(q_ref, k_ref, v_ref, dmat_ref, qdec_ref, kdec_ref, cdec_ref, o_ref,
                     state_ref, fwd_ref, *, n_heads, dk, dv, n_chunks, bwd_chunk):
    s = pl.program_id(1)
    cs = q_ref.shape[0]

    @pl.when((s == 0) | (s == n_chunks))
    def _():
        state_ref[...] = jnp.zeros_like(state_ref)

    chunk = jnp.where(s < n_chunks, s, bwd_chunk(jnp.maximum(s - n_chunks, 0)))
    rows = pl.ds(pl.multiple_of(chunk * cs, cs), cs)
    outs = []
    for h in range(n_heads):
        q = q_ref[:, h * dk:(h + 1) * dk]
        k = k_ref[:, h * dk:(h + 1) * dk]
        v = v_ref[:, h * dv:(h + 1) * dv]
        scores = lax.dot_general(q, k, (((1,), (1,)), ((), ())), preferred_element_type=F32) * dmat_ref[h]
        state = state_ref[h]
        outs.append(_mm(scores.astype(BF16), v) + _mm(q, state.astype(BF16)) * qdec_ref[h])
        kd = (k.astype(F32) * kdec_ref[h]).astype(BF16)
        state_ref[h] = state * cdec_ref[h] + lax.dot_general(
            kd, v, (((0,), (0,)), ((), ())), preferred_element_type=F32)

    @pl.when(s < n_chunks)
    def _():
        for h in range(n_heads):
            fwd_ref[rows, h * dv:(h + 1) * dv] = outs[h].astype(fwd_ref.dtype)

    @pl.when(s >= n_chunks)
    def _():
        for h in range(n_heads):
            o = outs[h] + fwd_ref[rows, h * dv:(h + 1) * dv].astype(F32)
            mu = jnp.mean(o, axis=-1, keepdims=True)
            oc = o - mu
            var = jnp.mean(oc * oc, axis=-1, keepdims=True)
            o_ref[:, h * dv:(h + 1) * dv] = (oc * lax.rsqrt(var + NORM_EPS)).astype(o_ref.dtype)


def _ret_out_kernel(x_ref, mod_ref, y_ref, gate_ref, w_ref, o_ref):
    z = (_silu(gate_ref[...].astype(F32)) * y_ref[...].astype(F32)).astype(BF16)
    o_ref[...] = x_ref[...] + mod_ref[5:6, :] * _mm(z, w_ref[...])


def _retention_layer(h, rows, mods, layer, g, w_in, w_out, decay, ctx_out):
    bsz, t_lat, t_ctx = rows
    d = h.shape[1]
    t_all = t_ctx + t_lat
    n_heads = RET_HEADS
    dk = d // n_heads
    dv = 2 * d // n_heads
    qk = n_heads * dk
    vw = n_heads * dv
    tm = RET_TILE
    assert t_ctx % tm == 0 and t_lat % tm == 0
    ctx_tiles = t_ctx // tm
    ctx_row = bsz
    lat_blk, ctx_blk = rows.block("lat", tm), rows.block("ctx", tm)

    pos = jnp.arange(t_all, dtype=F32)
    inv = 1.0 / (ROPE_BASE ** jnp.linspace(0.0, 1.0, dk // 2, dtype=F32))
    ang = pos[:, None] * inv[None, :]

    q, k, v, gate = pl.pallas_call(
        functools.partial(_ret_proj_kernel, n_heads=n_heads, dk=dk, dv=dv, ctx_tiles=ctx_tiles),
        grid=(bsz, t_all // tm),
        in_specs=[
            pl.BlockSpec((tm, d), lambda b, i: (ctx_blk(b, jnp.minimum(i, ctx_tiles - 1)), 0)),
            pl.BlockSpec((tm, d), lambda b, i: (lat_blk(b, jnp.maximum(i - ctx_tiles, 0)), 0)),
            _mod_spec(d, layer, lambda b, i: jnp.where(i < ctx_tiles, ctx_row, b)),
            _const_spec((1, d)),
            _const_spec((d, 2 * qk + 2 * vw)),
            pl.BlockSpec((tm, dk // 2), lambda b, i: (i, 0)),
            pl.BlockSpec((tm, dk // 2), lambda b, i: (i, 0)),
        ],
        out_specs=[
            pl.BlockSpec((None, tm, qk), lambda b, i: (b, i, 0)),
            pl.BlockSpec((None, tm, qk), lambda b, i: (b, i, 0)),
            pl.BlockSpec((None, tm, vw), lambda b, i: (b, i, 0)),
            pl.BlockSpec((None, tm, vw), lambda b, i: (b, i, 0)),
        ],
        out_shape=[
            jax.ShapeDtypeStruct((bsz, t_all, qk), BF16),
            jax.ShapeDtypeStruct((bsz, t_all, qk), BF16),
            jax.ShapeDtypeStruct((bsz, t_all, vw), BF16),
            jax.ShapeDtypeStruct((bsz, t_all, vw), BF16),
        ],
        compiler_params=_params(2),
        name="ret_proj",
    )(h, h, mods, g.reshape(1, d), w_in, jnp.cos(ang), jnp.sin(ang))

    cs = SCAN_CHUNK
    assert t_ctx % cs == 0 and t_lat % cs == 0
    n_chunks = t_all // cs
    ctx_chunks = t_ctx // cs
    lg = jax.nn.log_sigmoid(decay.astype(F32))[:, :, None]
    n = jnp.arange(cs, dtype=F32)
    rel = n[:, None] - n[None, :]
    dm_f = jnp.where(rel >= 0, jnp.exp(lg[0][:, :, None] * jnp.maximum(rel, 0.0)), 0.0)
    dm_b = jnp.where(rel < 0, jnp.exp(lg[1][:, :, None] * jnp.maximum(-rel, 0.0)), 0.0)
    dmat = jnp.stack([dm_f, dm_b])
    qd = jnp.stack([jnp.exp(lg[0] * (n + 1.0)), jnp.exp(lg[1] * (cs - n))])
    kd = jnp.stack([jnp.exp(lg[0] * (cs - 1.0 - n)), jnp.exp(lg[1] * n)])
    cd = jnp.exp(lg * cs)
    qdec = jnp.broadcast_to(qd[..., None], (2, n_heads, cs, dv))
    kdec = jnp.broadcast_to(kd[..., None], (2, n_heads, cs, dk))
    cdec = jnp.broadcast_to(cd[..., None], (2, n_heads, 1, dv))

    def bwd_chunk(j):
        return jnp.where(j < ctx_chunks, ctx_chunks - 1 - j, n_chunks - 1 + ctx_chunks - j)

    def chunk(s):
        return jnp.where(s < n_chunks, s, bwd_chunk(jnp.maximum(s - n_chunks, 0)))

    def out_chunk(s):
        return bwd_chunk(jnp.maximum(s - n_chunks, 0))

    def table_spec(*shape):
        return pl.BlockSpec((None,) + shape, lambda b, s: (s // n_chunks, 0, 0, 0))

    y = pl.pallas_call(
        functools.partial(_ret_scan_kernel, n_heads=n_heads, dk=dk, dv=dv, n_chunks=n_chunks,
                          bwd_chunk=bwd_chunk),
        grid=(bsz, 2 * n_chunks),
        in_specs=[
            pl.BlockSpec((None, cs, qk), lambda b, s: (b, chunk(s), 0)),
            pl.BlockSpec((None, cs, qk), lambda b, s: (b, chunk(s), 0)),
            pl.BlockSpec((None, cs, vw), lambda b, s: (b, chunk(s), 0)),
            table_spec(n_heads, cs, cs),
            table_spec(n_heads, cs, dv),
            table_spec(n_heads, cs, dk),
            table_spec(n_heads, 1, dv),
        ],
        out_specs=pl.BlockSpec((None, cs, vw), lambda b, s: (b, out_chunk(s), 0)),
        out_shape=jax.ShapeDtypeStruct((bsz, t_all, vw), BF16),
        scratch_shapes=[pltpu.VMEM((n_heads, dk, dv), F32), pltpu.VMEM((t_all, vw), BF16)],
        compiler_params=_params(2),
        name="ret_scan",
    )(q, k, v, dmat, qdec, kdec, cdec)

    first = 0 if ctx_out else ctx_tiles

    def res_blk(b, i):
        j = i + first
        return jnp.where(j < ctx_tiles, ctx_blk(b, jnp.minimum(j, ctx_tiles - 1)),
                         lat_blk(b, jnp.maximum(j - ctx_tiles, 0)))

    return pl.pallas_call(
        _ret_out_kernel,
        grid=(bsz, t_all // tm - first),
        in_specs=[
            pl.BlockSpec((tm, d), lambda b, i: (res_blk(b, i), 0)),
            _mod_spec(d, layer, lambda b, i: jnp.where(i + first < ctx_tiles, ctx_row, b)),
            pl.BlockSpec((None, tm, vw), lambda b, i: (b, i + first, 0)),
            pl.BlockSpec((None, tm, vw), lambda b, i: (b, i + first, 0)),
            _const_spec((vw, d)),
        ],
        out_specs=pl.BlockSpec((tm, d), lambda b, i: (res_blk(b, i), 0)),
        out_shape=jax.ShapeDtypeStruct(h.shape, F32),
        compiler_params=_params(2),
        name="ret_out",
    )(h, mods, y, gate, w_out)


def _pool_tables(seg_len):
    assert POOL_TILE % seg_len == 0
    r = np.arange(POOL_TILE)
    seg, pos = r // seg_len, r % seg_len
    mats, invs = [], []
    for win in POOL_WINDOWS:
        lo = np.clip(pos - win // 2, 0, seg_len)
        hi = np.clip(pos - win // 2 + win, 0, seg_len)
        col_seg, col_pos = seg[None, :], pos[None, :]
        member = (col_seg == seg[:, None]) & (col_pos >= lo[:, None]) & (col_pos < hi[:, None])
        mats.append(member.astype(np.float32))
        invs.append((1.0 / (hi - lo)).astype(np.float32))
    return np.stack(mats), np.stack(invs)


def _pool_kernel(x_ref, mod_ref, g_ref, s_ref, inv_ref, w_ref, b_ref, sc_ref, o_ref, *, n_groups, dg):
    x = x_ref[...]
    u = _adaln(x, g_ref[...], mod_ref[3:4, :], mod_ref[4:5, :])
    gate = mod_ref[5:6, :]
    for gi in range(n_groups):
        sl = slice(gi * dg, (gi + 1) * dg)
        ug = u[:, sl]
        hi = ug.astype(BF16)
        lo = (ug - hi.astype(F32)).astype(BF16)
        win_sum = _mm(s_ref[gi], hi) + _mm(s_ref[gi], lo)
        dlt = win_sum * inv_ref[gi] - ug
        yg = _mm(dlt.astype(BF16), w_ref[gi])
        o_ref[:, sl] = x[:, sl] + gate[:, sl] * ((yg + b_ref[:, sl]) * sc_ref[:, sl])


def _pool_layer(h, rows, lat_seg, mods, layer, g, w_grp, b_grp, scale, ctx_out):
    bsz, t_lat, t_ctx = rows
    d = h.shape[1]
    n_groups = len(POOL_WINDOWS)
    dg = d // n_groups
    tabs = [_pool_tables(lat_seg), _pool_tables(t_ctx)]
    s_tab = jnp.asarray(np.stack([m for m, _ in tabs]), BF16)
    inv_tab = jnp.asarray(np.stack([np.broadcast_to(v[:, :, None], (n_groups, POOL_TILE, dg)) for _, v in tabs]), F32)
    assert t_lat % POOL_TILE == 0 and (bsz * t_ctx) % POOL_TILE == 0
    per_b = t_lat // POOL_TILE
    n_lat = rows.n_lat // POOL_TILE
    n_ctx = bsz * t_ctx // POOL_TILE if ctx_out else 0

    def table_spec(*shape):
        return pl.BlockSpec((None,) + shape, lambda s: (jnp.where(s < n_lat, 0, 1), 0, 0, 0))

    return pl.pallas_call(
        functools.partial(_pool_kernel, n_groups=n_groups, dg=dg),
        grid=(n_lat + n_ctx,),
        in_specs=[
            pl.BlockSpec((POOL_TILE, d), lambda s: (s, 0)),
            _mod_spec(d, layer, lambda s: jnp.where(s < n_lat, s // per_b, bsz)),
            _const_spec((1, d)),
            table_spec(n_groups, POOL_TILE, POOL_TILE),
            table_spec(n_groups, POOL_TILE, dg),
            _const_spec((n_groups, dg, dg)),
            _const_spec((1, d)),
            _const_spec((1, d)),
        ],
        out_specs=pl.BlockSpec((POOL_TILE, d), lambda s: (s, 0)),
        out_shape=jax.ShapeDtypeStruct(h.shape, F32),
        compiler_params=_params(1),
        name="pool",
    )(h, mods, g.reshape(1, d), s_tab, inv_tab, w_grp.astype(BF16), b_grp.reshape(1, d), scale.reshape(1, d))


def _dft_tables(n_in):
    mod = 4 * n_in
    sub_n = min(V7X_LANES, n_in)
    idx = jnp.arange(n_in, dtype=jnp.int32)[None, :]
    blk = jnp.arange(n_in // sub_n, dtype=jnp.int32)[:, None]
    sub = jnp.arange(sub_n, dtype=jnp.int32)[:, None]
    theta = 2.0 * math.pi / mod

    def cs(p):
        ph = (p % mod).astype(F32) * theta
        return jnp.cos(ph), jnp.sin(ph)

    fwd = cs(2 * sub_n * blk * idx) + cs((2 * sub + 1) * idx)
    inv = cs(sub_n * blk * (2 * idx + 1)) + cs(sub * (2 * idx + 1))
    return fwd, inv


def _dft_fwd_gen_kernel(pc_ref, ps_ref, qc_ref, qs_ref, o_ref, *, tf):
    ft = pl.program_id(0)
    sub_n = qc_ref.shape[0]
    blocks = tf // sub_n
    qc = qc_ref[...]
    qs = qs_ref[...]
    for jb in range(blocks):
        kb = ft * blocks + jb
        pc = pc_ref[pl.ds(kb, 1), :]
        ps = ps_ref[pl.ds(kb, 1), :]
        o_ref[jb * sub_n:(jb + 1) * sub_n, :] = (pc * qc - ps * qs).astype(BF16)
        o_ref[tf + jb * sub_n:tf + (jb + 1) * sub_n, :] = (ps * qc + pc * qs).astype(BF16)


def _dft_inv_gen_kernel(pc_ref, ps_ref, qc_ref, qs_ref, o_ref, *, tf, n_in):
    mb = pl.program_id(0)
    pc = pc_ref[pl.ds(mb, 1), :]
    ps = ps_ref[pl.ds(mb, 1), :]
    qc = qc_ref[...]
    qs = qs_ref[...]
    cos_all = pc * qc - ps * qs
    sin_all = ps * qc + pc * qs
    for ft in range(n_in // tf):
        o_ref[:, 2 * ft * tf:(2 * ft + 1) * tf] = cos_all[:, ft * tf:(ft + 1) * tf].astype(BF16)
        o_ref[:, (2 * ft + 1) * tf:(2 * ft + 2) * tf] = sin_all[:, ft * tf:(ft + 1) * tf].astype(BF16)


def _dft_matrices(n_in, tf):
    fwd_tabs, inv_tabs = _dft_tables(n_in)
    sub_n = min(V7X_LANES, n_in)
    nb = n_in // sub_n
    tabs = [_const_spec((nb, n_in)), _const_spec((nb, n_in)),
            _const_spec((sub_n, n_in)), _const_spec((sub_n, n_in))]
    fwd = pl.pallas_call(
        functools.partial(_dft_fwd_gen_kernel, tf=tf),
        grid=(n_in // tf,),
        in_specs=tabs,
        out_specs=pl.BlockSpec((2 * tf, n_in), lambda i: (i, 0)),
        out_shape=jax.ShapeDtypeStruct((2 * n_in, n_in), BF16),
        compiler_params=_params(1),
        name="dft_fwd_gen",
    )(*fwd_tabs)
    inv = pl.pallas_call(
        functools.partial(_dft_inv_gen_kernel, tf=tf, n_in=n_in),
        grid=(nb,),
        in_specs=tabs,
        out_specs=pl.BlockSpec((sub_n, 2 * n_in), lambda i: (i, 0)),
        out_shape=jax.ShapeDtypeStruct((n_in, 2 * n_in), BF16),
        compiler_params=_params(1),
        name="dft_inv_gen",
    )(*inv_tabs)
    return fwd, inv


def _hy_proj_kernel(x_ref, xp_ref, xn_ref, mod_ref, g_ref, perm_ref, w_ref, b_ref, ws_ref, bs_ref,
                    v_ref, x1_ref, x2_ref):
    i = pl.program_id(1)
    tm, d = x_ref.shape
    n_ph = v_ref.shape[0]
    hp = tm // n_ph
    g = g_ref[...]
    shift, scale = mod_ref[3:4, :], mod_ref[4:5, :]
    u = _adaln(x_ref[...], g, shift, scale).astype(BF16)
    u = _mm(perm_ref[...], u).astype(BF16)
    halo = jnp.concatenate([xp_ref[...], xn_ref[...]], axis=0)
    uh = _adaln(halo, g, shift, scale).astype(BF16)
    p = _mm(jnp.concatenate([u, uh], axis=0), w_ref[...]) + b_ref[...]
    ph = [p[r * hp:(r + 1) * hp] for r in range(n_ph)]
    n_halo = xp_ref.shape[0]
    before = jnp.where(i == 0, 0.0, p[tm + n_halo - 1:tm + n_halo])
    after = jnp.where(i == pl.num_programs(1) - 1, 0.0, p[tm + n_halo:tm + n_halo + 1])
    row = lax.broadcasted_iota(jnp.int32, ph[0].shape, 0)
    last_prev = jnp.where(row == 0, before, pltpu.roll(ph[-1], 1, axis=0))
    first_next = jnp.where(row == hp - 1, after, pltpu.roll(ph[0], hp - 1, axis=0))
    w0, w1, w2, bs = ws_ref[0:1, :], ws_ref[1:2, :], ws_ref[2:3, :], bs_ref[...]
    for r in range(n_ph):
        prev = ph[r - 1] if r > 0 else last_prev
        nxt = ph[r + 1] if r < n_ph - 1 else first_next
        conv = prev * w0 + ph[r] * w1 + nxt * w2 + bs
        v_ref[r] = conv[:, :d].astype(BF16)
        x1_ref[r] = conv[:, d:2 * d].astype(BF16)
        x2_ref[r] = conv[:, 2 * d:].astype(BF16)


def _hy_filter_kernel(feat_ref, wp_ref, bp_ref, wm_ref, bm_ref, fr_ref, wf_ref, dl_ref,
                      hf_ref, hb_ref, ss_ref, *, d, n_orders):
    start = (pl.program_id(0) == 0) & (pl.program_id(1) == 0)
    feat = feat_ref[...]
    fr = fr_ref[...]
    hdn = jnp.sin(fr * (_mm_f32(feat, wp_ref[...]) + bp_ref[...]))
    hdn = jnp.sin(fr * (_mm_f32(hdn, wm_ref[...]) + bm_ref[...]))
    h = _mm(hdn.astype(BF16), wf_ref[...])
    decay = jnp.exp(-feat[:, 0:1] * dl_ref[...])
    row = lax.broadcasted_iota(jnp.int32, decay.shape, 0)
    first = (row == 0) & start

    @pl.when(start)
    def _():
        ss_ref[...] = jnp.zeros_like(ss_ref)

    for o in range(n_orders):
        sl = slice(o * d, (o + 1) * d)
        hf = h[:, 2 * o * d:(2 * o + 1) * d] * decay
        hb = jnp.where(first, 0.0, h[:, (2 * o + 1) * d:(2 * o + 2) * d] * decay)
        hf_ref[:, sl] = hf.astype(BF16)
        hb_ref[:, sl] = hb.astype(BF16)
        ss_ref[:, sl] += jnp.sum(hf * hf + hb * hb, axis=0, keepdims=True)


def _twiddle(a, c, s, conj=False):
    a_re, a_s = a
    if conj:
        return a_re * c + a_s * s, a_s * c - a_re * s
    return a_re * c - a_s * s, a_s * c + a_re * s


def _add(a, b):
    return a[0] + b[0], a[1] + b[1]


def _sub(a, b):
    return a[0] - b[0], a[1] - b[1]


def _dit_blocks(dft, phase_refs, tw, tp):
    ph = []
    for ref in phase_refs:
        t = _mm(dft, ref[...])
        ph.append((t[0:tp], t[tp:]))
    c1, s1, c2, s2, c3, s3 = tw
    t2, t3 = _twiddle(ph[2], c1, s1), _twiddle(ph[3], c1, s1)
    e_a, e_b = _add(ph[0], t2), _sub(ph[0], t2)
    o_a, o_b = _add(ph[1], t3), _sub(ph[1], t3)
    ta, tb = _twiddle(o_a, c2, s2), _twiddle(o_b, c3, s3, conj=True)
    return [_add(e_a, ta), _sub(e_a, ta), _add(e_b, tb), _sub(e_b, tb)]


def _bin_group(f_ref, tw_ref, g, tp):
    re, sn = slice(2 * g * tp, (2 * g + 1) * tp), slice((2 * g + 1) * tp, (2 * g + 2) * tp)
    dft = f_ref[2 * g * tp:(2 * g + 2) * tp, :]
    tw = [tw_ref[i, g * tp:(g + 1) * tp, :] for i in range(6)]
    return re, sn, dft, tw


def _hy_spec_kernel(f_ref, f0, f1, f2, f3, b0, b1, b2, b3, tw_ref, ss_ref, o_ref, *, tp, scale):
    wgt = scale * lax.rsqrt(ss_ref[...] + NORM_EPS)
    for g in range(tw_ref.shape[1] // tp):
        re, sn, dft, tw = _bin_group(f_ref, tw_ref, g, tp)
        fwd = _dit_blocks(dft, (f0, f1, f2, f3), tw, tp)
        bwd = _dit_blocks(dft, (b0, b1, b2, b3), tw, tp)
        for blk in range(4):
            o_ref[blk, re, :] = (fwd[blk][0] + bwd[blk][0]) * wgt
            o_ref[blk, sn, :] = (fwd[blk][1] - bwd[blk][1]) * wgt


def _hy_fwd_kernel(f_ref, u0, u1, u2, u3, h_ref, tw_ref, y_ref, *, tp):
    for g in range(tw_ref.shape[1] // tp):
        re, sn, dft, tw = _bin_group(f_ref, tw_ref, g, tp)
        c1, s1, c2, s2, c3, s3 = tw
        x = _dit_blocks(dft, (u0, u1, u2, u3), tw, tp)
        y = []
        for blk in range(4):
            h_re, h_s = h_ref[blk, re, :], h_ref[blk, sn, :]
            x_re, x_s = x[blk]
            y.append((x_re * h_re - x_s * h_s, x_re * h_s + x_s * h_re))
        e_a, o_a = _add(y[0], y[1]), _twiddle(_sub(y[0], y[1]), c2, s2, conj=True)
        e_b, o_b = _add(y[2], y[3]), _twiddle(_sub(y[2], y[3]), c3, s3)
        out = [_add(e_a, e_b), _add(o_a, o_b),
               _twiddle(_sub(e_a, e_b), c1, s1, conj=True), _twiddle(_sub(o_a, o_b), c1, s1, conj=True)]
        for r in range(4):
            y_ref[r, re, :] = out[r][0].astype(BF16)
            y_ref[r, sn, :] = out[r][1].astype(BF16)


def _hy_inv_kernel(f_ref, y_ref, a_ref, u_ref, fb_ref, o_ref, *, rows):
    for r in range(f_ref.shape[0] // rows):
        sl = slice(r * rows, (r + 1) * rows)
        conv = _mm(f_ref[sl, :], y_ref[...])
        z = a_ref[sl, :].astype(F32) * (conv + u_ref[sl, :].astype(F32) * fb_ref[...])
        o_ref[sl, :] = z.astype(o_ref.dtype)


def _hy_out_kernel(x_ref, mod_ref, z_ref, pick_ref, w_ref, b_ref, *refs):
    o_ref = refs[-1]
    z = _mm(pick_ref[0], z_ref[0])
    for r in range(1, z_ref.shape[0]):
        z = z + _mm(pick_ref[r], z_ref[r])
    o_ref[...] = x_ref[...] + mod_ref[5:6, :] * (_mm(z.astype(BF16), w_ref[...]) + b_ref[...])


def _hyena_layer(h, rows, part, out_buf, mods, layer, g, w_in, b_in, w_short, b_short, w_pos, b_pos, w_mid,
                 b_mid, freq, w_filt, fbias, w_out, b_out):
    bsz = rows.bsz
    length = rows.length(part)
    d = h.shape[1]
    mod_row = (lambda b, i: b) if part == "lat" else (lambda b, i: bsz)
    n_full = 2 * length
    n_ph = DFT_RADIX
    q = length // n_ph
    tm = min(512, length)
    hp = tm // n_ph
    tc = DFT_COL_TILE
    n_ct = d // tc
    width = w_mid.shape[0]
    n_parts = 2 * HYENA_ORDER

    halo = 8
    m_idx = np.arange(hp)
    perm = np.zeros((tm, tm), np.float32)
    pick = np.zeros((n_ph, tm, hp), np.float32)
    for r in range(n_ph):
        perm[r * hp + m_idx, n_ph * m_idx + r] = 1.0
        pick[r, n_ph * m_idx + r, m_idx] = 1.0
    ph_spec = pl.BlockSpec((None, n_ph, hp, d), lambda b, i: (b, 0, i, 0))
    tile_blk, halo_blk = rows.block(part, tm), rows.block(part, halo)
    v, x1, x2 = pl.pallas_call(
        _hy_proj_kernel,
        grid=(bsz, length // tm),
        in_specs=[
            pl.BlockSpec((tm, d), lambda b, i: (tile_blk(b, i), 0)),
            pl.BlockSpec((halo, d), lambda b, i: (halo_blk(b, jnp.maximum(i * (tm // halo) - 1, 0)), 0)),
            pl.BlockSpec((halo, d),
                         lambda b, i: (halo_blk(b, jnp.minimum((i + 1) * (tm // halo), length // halo - 1)), 0)),
            _mod_spec(d, layer, mod_row),
            _const_spec((1, d)),
            _const_spec((tm, tm)),
            _const_spec((d, 3 * d)),
            _const_spec((1, 3 * d)),
            _const_spec((3, 3 * d)),
            _const_spec((1, 3 * d)),
        ],
        out_specs=[ph_spec] * 3,
        out_shape=[jax.ShapeDtypeStruct((bsz, n_ph, q, d), BF16)] * 3,
        compiler_params=_params(2),
        name="hy_proj",
    )(h, h, h, mods, g.reshape(1, d), jnp.asarray(perm, BF16), w_in, b_in.reshape(1, 3 * d),
      w_short, b_short.reshape(1, 3 * d))

    t = jnp.linspace(0.0, 1.0, length, dtype=F32)[:, None]
    bands = jnp.linspace(1e-4, HYENA_BANDS - 1, HYENA_BANDS, dtype=F32)
    ang = (2.0 * math.pi / length) * jnp.arange(length, dtype=F32)[:, None] * bands[None, :]
    feat = jnp.concatenate([t, jnp.cos(ang), -jnp.sin(ang)], axis=-1)
    feat = jnp.pad(feat, ((0, 0), (0, V7X_LANES - HYENA_EMB)))
    w_pos_p = jnp.pad(w_pos.astype(F32), ((0, V7X_LANES - HYENA_EMB), (0, 0)))
    deltas = jnp.abs(jnp.linspace(HYENA_MIN_DECAY, HYENA_MAX_DECAY, d, dtype=F32)).reshape(1, d)
    feat = jnp.concatenate([feat[r::n_ph] for r in range(n_ph)], axis=0)
    tl = min(512, q)
    od = HYENA_ORDER * d
    tap_spec = pl.BlockSpec((tl, od), lambda r, i: (i, r))
    taps_f, taps_b, ss = pl.pallas_call(
        functools.partial(_hy_filter_kernel, d=d, n_orders=HYENA_ORDER),
        grid=(n_ph, q // tl),
        in_specs=[
            pl.BlockSpec((tl, V7X_LANES), lambda r, i: (r * (q // tl) + i, 0)),
            _const_spec((V7X_LANES, width)),
            _const_spec((1, width)),
            _const_spec((width, width)),
            _const_spec((1, width)),
            _const_spec((1, width)),
            _const_spec((width, n_parts * d)),
            _const_spec((1, d)),
        ],
        out_specs=[tap_spec, tap_spec, pl.BlockSpec((1, od), lambda r, i: (0, 0))],
        out_shape=[jax.ShapeDtypeStruct((q, n_ph * od), BF16), jax.ShapeDtypeStruct((q, n_ph * od), BF16),
                   jax.ShapeDtypeStruct((1, od), F32)],
        compiler_params=_params(2),
        name="hy_filter",
    )(feat, w_pos_p, b_pos.reshape(1, width), w_mid, b_mid.reshape(1, width), freq.reshape(1, width),
      w_filt.astype(BF16), deltas)

    tf = min(DFT_FREQ_TILE, q)
    tp = min(DFT_PACK, q)
    tn = min(DFT_TIME_TILE, q)
    dft_fwd, dft_inv = _dft_matrices(q, tp)
    odd = 2.0 * jnp.arange(q, dtype=F32) + 1.0
    angles = jnp.stack([(2.0 * math.pi / n_full) * odd, (math.pi / n_full) * odd,
                        (math.pi / n_full) * (length - odd)])
    tw = jnp.stack([jnp.cos(angles), jnp.sin(angles)], axis=1).reshape(6, q)
    tw = jnp.broadcast_to(tw[:, :, None], (6, q, tc))
    tw_spec = pl.BlockSpec((6, tf, tc), lambda f, *_: (0, f, 0))

    n_oc = od // tc

    def tap_phase(r):
        return pl.BlockSpec((q, tc), lambda f, o, j: (0, r * n_oc + o * n_ct + j))

    spec = pl.pallas_call(
        functools.partial(_hy_spec_kernel, tp=tp, scale=1.0 / length),
        grid=(q // tf, HYENA_ORDER, n_ct),
        in_specs=[pl.BlockSpec((2 * tf, q), lambda f, o, j: (f, 0))]
        + [tap_phase(r) for r in range(n_ph)] * 2
        + [tw_spec, pl.BlockSpec((1, tc), lambda f, o, j: (0, o * n_ct + j))],
        out_specs=pl.BlockSpec((None, n_ph, 2 * tf, tc), lambda f, o, j: (o, 0, f, j)),
        out_shape=jax.ShapeDtypeStruct((HYENA_ORDER, n_ph, 2 * q, d), F32),
        compiler_params=_params(3),
        name="hy_spec",
    )(dft_fwd, *([taps_f] * n_ph), *([taps_b] * n_ph), tw, ss)

    tci = d if q < DFT_FREQ_TILE else 2 * tc
    n_ci = d // tci

    def long_conv(order, u, a):
        yspec = pl.pallas_call(
            functools.partial(_hy_fwd_kernel, tp=tp),
            grid=(q // tf, n_ct, bsz),
            in_specs=[pl.BlockSpec((2 * tf, q), lambda f, j, b: (f, 0))]
            + [pl.BlockSpec((None, None, q, tc), functools.partial(lambda r, f, j, b: (b, r, 0, j), r))
               for r in range(n_ph)]
            + [pl.BlockSpec((None, n_ph, 2 * tf, tc), lambda f, j, b: (order, 0, f, j)), tw_spec],
            out_specs=pl.BlockSpec((None, n_ph, 2 * tf, tc), lambda f, j, b: (b, 0, f, j)),
            out_shape=jax.ShapeDtypeStruct((bsz, n_ph, 2 * q, d), BF16),
            compiler_params=_params(3),
            name="hy_fwd",
        )(dft_fwd, *([u] * n_ph), spec, tw)
        row_spec = pl.BlockSpec((None, None, tn, tci), lambda n, j, b, r: (b, r, n, j))
        return pl.pallas_call(
            functools.partial(_hy_inv_kernel, rows=min(DFT_ROW_GROUP, tn)),
            grid=(q // tn, n_ci, bsz, n_ph),
            in_specs=[
                pl.BlockSpec((tn, 2 * q), lambda n, j, b, r: (n, 0)),
                pl.BlockSpec((None, None, 2 * q, tci), lambda n, j, b, r: (b, r, 0, j)),
                row_spec,
                row_spec,
                pl.BlockSpec((None, 1, tci), lambda n, j, b, r: (order, 0, j)),
            ],
            out_specs=row_spec,
            out_shape=jax.ShapeDtypeStruct((bsz, n_ph, q, d), BF16),
            compiler_params=_params(4),
            name="hy_inv",
        )(dft_inv, yspec, a, u, fbias.astype(F32).reshape(HYENA_ORDER, 1, d))

    z = long_conv(0, v, x1)
    zz = long_conv(1, z, x2)

    in_specs = [
        pl.BlockSpec((tm, d), lambda b, i: (tile_blk(b, i), 0)),
        _mod_spec(d, layer, mod_row),
        pl.BlockSpec((None, n_ph, hp, d), lambda b, i: (b, 0, i, 0)),
        _const_spec((n_ph, tm, hp)),
        _const_spec((d, d)),
        _const_spec((1, d)),
    ]
    args = [h, mods, zz, jnp.asarray(pick, BF16), w_out, b_out.reshape(1, d)]
    aliases = {}
    if out_buf is not None:
        in_specs.append(pl.BlockSpec(memory_space=pl.ANY))
        args.append(out_buf)
        aliases = {len(args) - 1: 0}
    return pl.pallas_call(
        _hy_out_kernel,
        grid=(bsz, length // tm),
        in_specs=in_specs,
        out_specs=pl.BlockSpec((tm, d), lambda b, i: (tile_blk(b, i), 0)),
        out_shape=jax.ShapeDtypeStruct(h.shape, F32),
        input_output_aliases=aliases,
        compiler_params=_params(2),
        name="hy_out",
    )(*args)


def kernel(x, c, ctx, c_ctx, ada_w, ada_b, norm_g, ffn_w1, ffn_w3, ffn_w2, ret_w_in, ret_w_out, ret_decay,
           pool_w, pool_b, pool_scale, hy_w_in, hy_b_in, hy_w_short, hy_b_short, hy_w_pos, hy_b_pos,
           hy_w_mid, hy_b_mid, hy_freq, hy_w_filt, hy_bias, hy_w_out, hy_b_out, final_g):
    bsz, seq, d = x.shape
    t_ctx = ctx.shape[1]
    depth = ada_w.shape[0]
    n_mixers = 3
    assert bsz + 1 <= MOD_ROWS

    cond = jnp.concatenate([c, c_ctx[None, :], jnp.zeros((MOD_ROWS - bsz - 1, d), F32)], axis=0)
    mods = _ada_all(cond, ada_w, ada_b).reshape(depth, MOD_ROWS, ADA_CHUNKS, d)

    rows = _Rows(bsz, seq, t_ctx)
    ffn_stacks = (ffn_w1, ffn_w3, ffn_w2)
    ffn_weights = [tuple(w[0, 0].astype(BF16) for w in ffn_stacks)]

    def ffn(h, layer, half, ctx_live, mixer_jobs=(), final=False):
        jobs = [] if final else [(w, (layer, 1) if half == 0 else (layer + 1, 0)) for w in ffn_stacks]
        h, cast = _half_ffn(h, rows, mods, layer, half, norm_g[layer, 2 * half], ffn_weights[0], final_g,
                            ctx_live, cast_jobs=jobs + list(mixer_jobs), final=final)
        ffn_weights[0] = cast[:len(ffn_stacks)]
        return h, cast[len(ffn_stacks):]

    h = (x.reshape(bsz * seq, d), ctx.reshape(bsz * t_ctx, d))
    for layer in range(depth):
        kind = layer % n_mixers
        slot = layer // n_mixers
        last = layer == depth - 1
        ctx_out = not last
        ctx_live = ctx_out or kind == 0
        g_mix = norm_g[layer, 1]
        if kind == 0:
            h, (w_in, w_out) = ffn(h, layer, 0, ctx_live, [(ret_w_in, (slot,)), (ret_w_out, (slot,))])
            h = _retention_layer(h, rows, mods, layer, g_mix, w_in, w_out, ret_decay[slot], ctx_out)
        elif kind == 1:
            h, _ = ffn(h, layer, 0, ctx_live)
            h = _pool_layer(h, rows, GRID_W, mods, layer, g_mix, pool_w[slot], pool_b[slot], pool_scale[slot],
                            ctx_out)
        else:
            h, (w_in, w_out) = ffn(h, layer, 0, ctx_live, [(hy_w_in, (slot,)), (hy_w_out, (slot,))])
            hp = (w_in, hy_b_in[slot], hy_w_short[slot], hy_b_short[slot], hy_w_pos[slot],
                  hy_b_pos[slot], hy_w_mid[slot], hy_b_mid[slot], hy_freq[slot], hy_w_filt[slot],
                  hy_bias[slot], w_out, hy_b_out[slot])
            new_h = _hyena_layer(h, rows, "lat", None, mods, layer, g_mix, *hp)
            if ctx_out:
                new_h = _hyena_layer(h, rows, "ctx", new_h, mods, layer, g_mix, *hp)
            h = new_h
        h, _ = ffn(h, layer, 1, ctx_out, final=last)
    return h.reshape(bsz, seq, d)
```

```python
import functools
import math
from typing import NamedTuple

import numpy as np
import jax
import jax.numpy as jnp
from jax import lax
from jax.experimental import pallas as pl
from jax.experimental.pallas import tpu as pltpu

F32 = jnp.float32
BF16 = jnp.bfloat16

GRID_W = 64
ADA_CHUNKS = 9
NORM_EPS = 1e-6
RET_HEADS = 4
RET_CHUNK = 128
ROPE_BASE = 10000.0
POOL_WINDOWS = (2, 4, 8, 16)
HYENA_ORDER = 2
HYENA_EMB = 33
HYENA_BANDS = (HYENA_EMB - 1) // 2
HYENA_TARGET = 1e-2
HYENA_FAST = 0.3
HYENA_SLOW = 1.5
HYENA_MAX_DECAY = math.log(HYENA_TARGET) / HYENA_FAST
HYENA_MIN_DECAY = math.log(HYENA_TARGET) / HYENA_SLOW

V7X_LANES = 128
V7X_VMEM_LIMIT_BYTES = 56 * 1024 * 1024
MOD_ROWS = 8
FFN_CAST_SLABS = 16
POOL_TILE = 512
RET_TILE = 256
SCAN_CHUNK = 256
DFT_RADIX = 4
DFT_FREQ_TILE = 512
DFT_PACK = 256
DFT_ROW_GROUP = 512
DFT_TIME_TILE = 2048
DFT_COL_TILE = 256
HIGHEST = lax.Precision.HIGHEST


def _params(n_axes):
    return pltpu.CompilerParams(
        dimension_semantics=("arbitrary",) * n_axes,
        vmem_limit_bytes=V7X_VMEM_LIMIT_BYTES)


def _const_spec(shape):
    zeros = (0,) * len(shape)
    return pl.BlockSpec(shape, lambda *_: zeros, pipeline_mode=pl.Buffered(1))


def _mod_spec(d, layer, row_fn):
    return pl.BlockSpec((None, None, ADA_CHUNKS, d), lambda *idx: (layer, row_fn(*idx), 0, 0))


class _Rows(NamedTuple):
    bsz: int
    t_lat: int
    t_ctx: int

    @property
    def n_lat(self):
        return self.bsz * self.t_lat

    @property
    def n_all(self):
        return self.bsz * (self.t_lat + self.t_ctx)

    def length(self, part):
        return self.t_lat if part == "lat" else self.t_ctx

    def block(self, part, tile):
        per = self.length(part) // tile
        base = 0 if part == "lat" else self.n_lat // tile
        assert self.length(part) % tile == 0 and self.n_lat % tile == 0
        return lambda b, i: base + b * per + i


def _mm(a, b):
    return jnp.dot(a, b, preferred_element_type=F32)


def _mm_f32(a, b):
    return jnp.dot(a, b, preferred_element_type=F32, precision=HIGHEST)


def _silu(x):
    return x * jax.nn.sigmoid(x)


def _adaln(x, g, shift, scale):
    ms = jnp.mean(x * x, axis=-1, keepdims=True)
    y = x * lax.rsqrt(ms + NORM_EPS) * g
    return y * (1.0 + scale) + shift


def _ada_kernel(c_ref, w_ref, b_ref, o_ref):
    s = _silu(c_ref[...]).astype(BF16)
    o_ref[...] = _mm(s, w_ref[...].astype(BF16)) + b_ref[...]


def _ada_all(cond, ada_w, ada_b):
    depth, d, nd = ada_w.shape
    return pl.pallas_call(
        _ada_kernel,
        grid=(depth, nd // d),
        in_specs=[
            pl.BlockSpec((MOD_ROWS, d), lambda l, j: (0, 0)),
            pl.BlockSpec((None, d, d), lambda l, j: (l, 0, j)),
            pl.BlockSpec((None, 1, d), lambda l, j: (l, 0, j)),
        ],
        out_specs=pl.BlockSpec((None, MOD_ROWS, d), lambda l, j: (l, 0, j)),
        out_shape=jax.ShapeDtypeStruct((depth, MOD_ROWS, nd), F32),
        compiler_params=_params(2),
        name="ada",
    )(cond, ada_w, ada_b.reshape(depth, 1, nd))


def _ffn_kernel(*refs, base, final, n_lat, split_input, n_cast):
    n_x = 2 if split_input else 1
    x_refs, refs = refs[:n_x], refs[n_x:]
    mod_ref, g_ref, w1_ref, w3_ref, w2_ref, fg_ref = refs[:6]
    rest = refs[6:]
    if n_cast:
        n_jobs = (len(rest) - 1) // 2
        src_refs, o_ref, dst_refs = rest[:n_jobs], rest[n_jobs], rest[n_jobs + 1:]

        @pl.when(pl.program_id(0) < n_cast)
        def _():
            for src, dst in zip(src_refs, dst_refs):
                dst[...] = src[...].astype(BF16)
    else:
        (o_ref,) = rest
    if split_input:
        x = jnp.where(pl.program_id(0) >= n_lat, x_refs[1][...], x_refs[0][...])
    else:
        x = x_refs[0][...]
    u = _adaln(x, g_ref[...], mod_ref[base:base + 1, :], mod_ref[base + 1:base + 2, :]).astype(BF16)
    h1 = _mm(u, w1_ref[...])
    h3 = _mm(u, w3_ref[...])
    a = (_silu(h1) * h3).astype(BF16)
    y = x + (0.5 * mod_ref[base + 2:base + 3, :]) * _mm(a, w2_ref[...])
    if final:
        ms = jnp.mean(y * y, axis=-1, keepdims=True)
        y = y * lax.rsqrt(ms + NORM_EPS) * fg_ref[...]
    o_ref[...] = y


def _half_ffn(h, rows, mods, layer, half, g, weights, final_g, ctx_live, cast_jobs=(), final=False, tm=512):
    w1, w3, w2 = weights
    d, f = w1.shape
    base = 6 * half
    per_b = rows.t_lat // tm
    n_lat = rows.n_lat // tm
    n_ctx = rows.bsz * rows.t_ctx // tm if ctx_live else 0
    assert rows.t_lat % tm == 0 and (rows.bsz * rows.t_ctx) % tm == 0
    split_input = isinstance(h, tuple)

    if split_input:
        x_specs = [pl.BlockSpec((tm, d), lambda s: (jnp.minimum(s, n_lat - 1), 0)),
                   pl.BlockSpec((tm, d), lambda s: (jnp.maximum(s - n_lat, 0), 0))]
        x_args = list(h)
    else:
        x_specs = [pl.BlockSpec((tm, d), lambda s: (s, 0))]
        x_args = [h]
    in_specs = x_specs + [
        _mod_spec(d, layer, lambda s: jnp.where(s < n_lat, s // per_b, rows.bsz)),
        _const_spec((1, d)),
        _const_spec((d, f)),
        _const_spec((d, f)),
        _const_spec((f, d)),
        _const_spec((1, d)),
    ]
    args = x_args + [mods, g.reshape(1, d), w1, w3, w2, final_g.reshape(1, d)]
    out_specs = [pl.BlockSpec((tm, d), lambda s: (s, 0))]
    out_shape = [jax.ShapeDtypeStruct((rows.n_lat if final else rows.n_all, d), F32)]
    n_cast = min(FFN_CAST_SLABS, n_lat) if cast_jobs else 0
    for w, lead in cast_jobs:
        r, c = w.shape[-2:]
        assert r % (16 * n_cast) == 0 and len(lead) == w.ndim - 2
        in_specs.append(pl.BlockSpec((None,) * len(lead) + (r // n_cast, c),
                                     functools.partial(lambda lead, s: lead + (jnp.minimum(s, n_cast - 1), 0), lead)))
        out_specs.append(pl.BlockSpec((r // n_cast, c), lambda s: (jnp.minimum(s, n_cast - 1), 0)))
        out_shape.append(jax.ShapeDtypeStruct((r, c), BF16))
        args.append(w)
    outs = pl.pallas_call(
        functools.partial(_ffn_kernel, base=base, final=final, n_lat=n_lat, split_input=split_input,
                          n_cast=n_cast),
        grid=(n_lat + n_ctx,),
        in_specs=in_specs,
        out_specs=out_specs,
        out_shape=out_shape,
        compiler_params=_params(1),
        name="half_ffn",
    )(*args)
    return outs[0], tuple(outs[1:])


def _ret_proj_kernel(xc_ref, xl_ref, mod_ref, g_ref, w_ref, cos_ref, sin_ref, q_ref, k_ref, v_ref, gate_ref,
                     *, n_heads, dk, dv, ctx_tiles):
    x = jnp.where(pl.program_id(1) < ctx_tiles, xc_ref[...], xl_ref[...])
    u = _adaln(x, g_ref[...], mod_ref[3:4, :], mod_ref[4:5, :]).astype(BF16)
    p = _mm(u, w_ref[...])
    cos = cos_ref[...]
    sin = sin_ref[...]
    half = dk // 2
    qk = n_heads * dk
    k_scale = dk ** -0.5

    def rot(a):
        a1, a2 = a[:, :half], a[:, half:]
        return jnp.concatenate([a1 * cos - a2 * sin, a2 * cos + a1 * sin], axis=-1)

    for h in range(n_heads):
        q_ref[:, h * dk:(h + 1) * dk] = rot(p[:, h * dk:(h + 1) * dk]).astype(BF16)
        k_ref[:, h * dk:(h + 1) * dk] = (rot(p[:, qk + h * dk:qk + (h + 1) * dk]) * k_scale).astype(BF16)
    v_ref[...] = p[:, 2 * qk:2 * qk + n_heads * dv].astype(BF16)
    gate_ref[...] = p[:, 2 * qk + n_heads * dv:].astype(BF16)


def _ret_scan_kernel(q_ref, k_ref, v_ref, dmat_ref, qdec_ref, kdec_ref, cdec_ref, o_ref,
                     state_ref, fwd_ref, *, n_heads, dk, dv, n_chunks, bwd_chunk):
    s = pl.program_id(1)
    cs = q_ref.shape[0]

    @pl.when((s == 0) | (s == n_chunks))
    def _():
        state_ref[...] = jnp.zeros_like(state_ref)

    chunk = jnp.where(s < n_chunks, s, bwd_chunk(jnp.maximum(s - n_chunks, 0)))
    rows = pl.ds(pl.multiple_of(chunk * cs, cs), cs)
    outs = []
    for h in range(n_heads):
        q = q_ref[:, h * dk:(h + 1) * dk]
        k = k_ref[:, h * dk:(h + 1) * dk]
        v = v_ref[:, h * dv:(h + 1) * dv]
        scores = lax.dot_general(q, k, (((1,), (1,)), ((), ())), preferred_element_type=F32) * dmat_ref[h]
        state = state_ref[h]
        outs.append(_mm(scores.astype(BF16), v) + _mm(q, state.astype(BF16)) * qdec_ref[h])
        kd = (k.astype(F32) * kdec_ref[h]).astype(BF16)
        state_ref[h] = state * cdec_ref[h] + lax.dot_general(
            kd, v, (((0,), (0,)), ((), ())), preferred_element_type=F32)

    @pl.when(s < n_chunks)
    def _():
        for h in range(n_heads):
            fwd_ref[rows, h * dv:(h + 1) * dv] = outs[h].astype(fwd_ref.dtype)

    @pl.when(s >= n_chunks)
    def _():
        for h in range(n_heads):
            o = outs[h] + fwd_ref[rows, h * dv:(h + 1) * dv].astype(F32)
            mu = jnp.mean(o, axis=-1, keepdims=True)
            oc = o - mu
            var = jnp.mean(oc * oc, axis=-1, keepdims=True)
            o_ref[:, h * dv:(h + 1) * dv] = (oc * lax.rsqrt(var + NORM_EPS)).astype(o_ref.dtype)


def _ret_out_kernel(x_ref, mod_ref, y_ref, gate_ref, w_ref, o_ref):
    z = (_silu(gate_ref[...].astype(F32)) * y_ref[...].astype(F32)).astype(BF16)
    o_ref[...] = x_ref[...] + mod_ref[5:6, :] * _mm(z, w_ref[...])


def _retention_layer(h, rows, mods, layer, g, w_in, w_out, decay, ctx_out):
    bsz, t_lat, t_ctx = rows
    d = h.shape[1]
    t_all = t_ctx + t_lat
    n_heads = RET_HEADS
    dk = d // n_heads
    dv = 2 * d // n_heads
    qk = n_heads * dk
    vw = n_heads * dv
    tm = RET_TILE
    assert t_ctx % tm == 0 and t_lat % tm == 0
    ctx_tiles = t_ctx // tm
    ctx_row = bsz
    lat_blk, ctx_blk = rows.block("lat", tm), rows.block("ctx", tm)

    pos = jnp.arange(t_all, dtype=F32)
    inv = 1.0 / (ROPE_BASE ** jnp.linspace(0.0, 1.0, dk // 2, dtype=F32))
    ang = pos[:, None] * inv[None, :]

    q, k, v, gate = pl.pallas_call(
        functools.partial(_ret_proj_kernel, n_heads=n_heads, dk=dk, dv=dv, ctx_tiles=ctx_tiles),
        grid=(bsz, t_all // tm),
        in_specs=[
            pl.BlockSpec((tm, d), lambda b, i: (ctx_blk(b, jnp.minimum(i, ctx_tiles - 1)), 0)),
            pl.BlockSpec((tm, d), lambda b, i: (lat_blk(b, jnp.maximum(i - ctx_tiles, 0)), 0)),
            _mod_spec(d, layer, lambda b, i: jnp.where(i < ctx_tiles, ctx_row, b)),
            _const_spec((1, d)),
            _const_spec((d, 2 * qk + 2 * vw)),
            pl.BlockSpec((tm, dk // 2), lambda b, i: (i, 0)),
            pl.BlockSpec((tm, dk // 2), lambda b, i: (i, 0)),
        ],
        out_specs=[
            pl.BlockSpec((None, tm, qk), lambda b, i: (b, i, 0)),
            pl.BlockSpec((None, tm, qk), lambda b, i: (b, i, 0)),
            pl.BlockSpec((None, tm, vw), lambda b, i: (b, i, 0)),
            pl.BlockSpec((None, tm, vw), lambda b, i: (b, i, 0)),
        ],
        out_shape=[
            jax.ShapeDtypeStruct((bsz, t_all, qk), BF16),
            jax.ShapeDtypeStruct((bsz, t_all, qk), BF16),
            jax.ShapeDtypeStruct((bsz, t_all, vw), BF16),
            jax.ShapeDtypeStruct((bsz, t_all, vw), BF16),
        ],
        compiler_params=_params(2),
        name="ret_proj",
    )(h, h, mods, g.reshape(1, d), w_in, jnp.cos(ang), jnp.sin(ang))

    cs = SCAN_CHUNK
    assert t_ctx % cs == 0 and t_lat % cs == 0
    n_chunks = t_all // cs
    ctx_chunks = t_ctx // cs
    lg = jax.nn.log_sigmoid(decay.astype(F32))[:, :, None]
    n = jnp.arange(cs, dtype=F32)
    rel = n[:, None] - n[None, :]
    dm_f = jnp.where(rel >= 0, jnp.exp(lg[0][:, :, None] * jnp.maximum(rel, 0.0)), 0.0)
    dm_b = jnp.where(rel < 0, jnp.exp(lg[1][:, :, None] * jnp.maximum(-rel, 0.0)), 0.0)
    dmat = jnp.stack([dm_f, dm_b])
    qd = jnp.stack([jnp.exp(lg[0] * (n + 1.0)), jnp.exp(lg[1] * (cs - n))])
    kd = jnp.stack([jnp.exp(lg[0] * (cs - 1.0 - n)), jnp.exp(lg[1] * n)])
    cd = jnp.exp(lg * cs)
    qdec = jnp.broadcast_to(qd[..., None], (2, n_heads, cs, dv))
    kdec = jnp.broadcast_to(kd[..., None], (2, n_heads, cs, dk))
    cdec = jnp.broadcast_to(cd[..., None], (2, n_heads, 1, dv))

    def bwd_chunk(j):
        return jnp.where(j < ctx_chunks, ctx_chunks - 1 - j, n_chunks - 1 + ctx_chunks - j)

    def chunk(s):
        return jnp.where(s < n_chunks, s, bwd_chunk(jnp.maximum(s - n_chunks, 0)))

    def out_chunk(s):
        return bwd_chunk(jnp.maximum(s - n_chunks, 0))

    def table_spec(*shape):
        return pl.BlockSpec((None,) + shape, lambda b, s: (s // n_chunks, 0, 0, 0))

    y = pl.pallas_call(
        functools.partial(_ret_scan_kernel, n_heads=n_heads, dk=dk, dv=dv, n_chunks=n_chunks,
                          bwd_chunk=bwd_chunk),
        grid=(bsz, 2 * n_chunks),
        in_specs=[
            pl.BlockSpec((None, cs, qk), lambda b, s: (b, chunk(s), 0)),
            pl.BlockSpec((None, cs, qk), lambda b, s: (b, chunk(s), 0)),
            pl.BlockSpec((None, cs, vw), lambda b, s: (b, chunk(s), 0)),
            table_spec(n_heads, cs, cs),
            table_spec(n_heads, cs, dv),
            table_spec(n_heads, cs, dk),
            table_spec(n_heads, 1, dv),
        ],
        out_specs=pl.BlockSpec((None, cs, vw), lambda b, s: (b, out_chunk(s), 0)),
        out_shape=jax.ShapeDtypeStruct((bsz, t_all, vw), BF16),
        scratch_shapes=[pltpu.VMEM((n_heads, dk, dv), F32), pltpu.VMEM((t_all, vw), BF16)],
        compiler_params=_params(2),
        name="ret_scan",
    )(q, k, v, dmat, qdec, kdec, cdec)

    first = 0 if ctx_out else ctx_tiles

    def res_blk(b, i):
        j = i + first
        return jnp.where(j < ctx_tiles, ctx_blk(b, jnp.minimum(j, ctx_tiles - 1)),
                         lat_blk(b, jnp.maximum(j - ctx_tiles, 0)))

    return pl.pallas_call(
        _ret_out_kernel,
        grid=(bsz, t_all // tm - first),
        in_specs=[
            pl.BlockSpec((tm, d), lambda b, i: (res_blk(b, i), 0)),
            _mod_spec(d, layer, lambda b, i: jnp.where(i + first < ctx_tiles, ctx_row, b)),
            pl.BlockSpec((None, tm, vw), lambda b, i: (b, i + first, 0)),
            pl.BlockSpec((None, tm, vw), lambda b, i: (b, i + first, 0)),
            _const_spec((vw, d)),
        ],
        out_specs=pl.BlockSpec((tm, d), lambda b, i: (res_blk(b, i), 0)),
        out_shape=jax.ShapeDtypeStruct(h.shape, F32),
        compiler_params=_params(2),
        name="ret_out",
    )(h, mods, y, gate, w_out)


def _pool_tables(seg_len):
    assert POOL_TILE % seg_len == 0
    r = np.arange(POOL_TILE)
    seg, pos = r // seg_len, r % seg_len
    mats, invs = [], []
    for win in POOL_WINDOWS:
        lo = np.clip(pos - win // 2, 0, seg_len)
        hi = np.clip(pos - win // 2 + win, 0, seg_len)
        col_seg, col_pos = seg[None, :], pos[None, :]
        member = (col_seg == seg[:, None]) & (col_pos >= lo[:, None]) & (col_pos < hi[:, None])
        mats.append(member.astype(np.float32))
        invs.append((1.0 / (hi - lo)).astype(np.float32))
    return np.stack(mats), np.stack(invs)


def _pool_kernel(x_ref, mod_ref, g_ref, s_ref, inv_ref, w_ref, b_ref, sc_ref, o_ref, *, n_groups, dg):
    x = x_ref[...]
    u = _adaln(x, g_ref[...], mod_ref[3:4, :], mod_ref[4:5, :])
    gate = mod_ref[5:6, :]
    for gi in range(n_groups):
        sl = slice(gi * dg, (gi + 1) * dg)
        ug = u[:, sl]
        hi = ug.astype(BF16)
        lo = (ug - hi.astype(F32)).astype(BF16)
        win_sum = _mm(s_ref[gi], hi) + _mm(s_ref[gi], lo)
        dlt = win_sum * inv_ref[gi] - ug
        yg = _mm(dlt.astype(BF16), w_ref[gi])
        o_ref[:, sl] = x[:, sl] + gate[:, sl] * ((yg + b_ref[:, sl]) * sc_ref[:, sl])


def _pool_layer(h, rows, lat_seg, mods, layer, g, w_grp, b_grp, scale, ctx_out):
    bsz, t_lat, t_ctx = rows
    d = h.shape[1]
    n_groups = len(POOL_WINDOWS)
    dg = d // n_groups
    tabs = [_pool_tables(lat_seg), _pool_tables(t_ctx)]
    s_tab = jnp.asarray(np.stack([m for m, _ in tabs]), BF16)
    inv_tab = jnp.asarray(np.stack([np.broadcast_to(v[:, :, None], (n_groups, POOL_TILE, dg)) for _, v in tabs]), F32)
    assert t_lat % POOL_TILE == 0 and (bsz * t_ctx) % POOL_TILE == 0
    per_b = t_lat // POOL_TILE
    n_lat = rows.n_lat // POOL_TILE
    n_ctx = bsz * t_ctx // POOL_TILE if ctx_out else 0

    def table_spec(*shape):
        return pl.BlockSpec((None,) + shape, lambda s: (jnp.where(s < n_lat, 0, 1), 0, 0, 0))

    return pl.pallas_call(
        functools.partial(_pool_kernel, n_groups=n_groups, dg=dg),
        grid=(n_lat + n_ctx,),
        in_specs=[
            pl.BlockSpec((POOL_TILE, d), lambda s: (s, 0)),
            _mod_spec(d, layer, lambda s: jnp.where(s < n_lat, s // per_b, bsz)),
            _const_spec((1, d)),
            table_spec(n_groups, POOL_TILE, POOL_TILE),
            table_spec(n_groups, POOL_TILE, dg),
            _const_spec((n_groups, dg, dg)),
            _const_spec((1, d)),
            _const_spec((1, d)),
        ],
        out_specs=pl.BlockSpec((POOL_TILE, d), lambda s: (s, 0)),
        out_shape=jax.ShapeDtypeStruct(h.shape, F32),
        compiler_params=_params(1),
        name="pool",
    )(h, mods, g.reshape(1, d), s_tab, inv_tab, w_grp.astype(BF16), b_grp.reshape(1, d), scale.reshape(1, d))


def _dft_tables(n_in):
    mod = 4 * n_in
    sub_n = min(V7X_LANES, n_in)
    idx = jnp.arange(n_in, dtype=jnp.int32)[None, :]
    blk = jnp.arange(n_in // sub_n, dtype=jnp.int32)[:, None]
    sub = jnp.arange(sub_n, dtype=jnp.int32)[:, None]
    theta = 2.0 * math.pi / mod

    def cs(p):
        ph = (p % mod).astype(F32) * theta
        return jnp.cos(ph), jnp.sin(ph)

    fwd = cs(2 * sub_n * blk * idx) + cs((2 * sub + 1) * idx)
    inv = cs(sub_n * blk * (2 * idx + 1)) + cs(sub * (2 * idx + 1))
    return fwd, inv


def _dft_fwd_gen_kernel(pc_ref, ps_ref, qc_ref, qs_ref, o_ref, *, tf):
    ft = pl.program_id(0)
    sub_n = qc_ref.shape[0]
    blocks = tf // sub_n
    qc = qc_ref[...]
    qs = qs_ref[...]
    for jb in range(blocks):
        kb = ft * blocks + jb
        pc = pc_ref[pl.ds(kb, 1), :]
        ps = ps_ref[pl.ds(kb, 1), :]
        o_ref[jb * sub_n:(jb + 1) * sub_n, :] = (pc * qc - ps * qs).astype(BF16)
        o_ref[tf + jb * sub_n:tf + (jb + 1) * sub_n, :] = (ps * qc + pc * qs).astype(BF16)


def _dft_inv_gen_kernel(pc_ref, ps_ref, qc_ref, qs_ref, o_ref, *, tf, n_in):
    mb = pl.program_id(0)
    pc = pc_ref[pl.ds(mb, 1), :]
    ps = ps_ref[pl.ds(mb, 1), :]
    qc = qc_ref[...]
    qs = qs_ref[...]
    cos_all = pc * qc - ps * qs
    sin_all = ps * qc + pc * qs
    for ft in range(n_in // tf):
        o_ref[:, 2 * ft * tf:(2 * ft + 1) * tf] = cos_all[:, ft * tf:(ft + 1) * tf].astype(BF16)
        o_ref[:, (2 * ft + 1) * tf:(2 * ft + 2) * tf] = sin_all[:, ft * tf:(ft + 1) * tf].astype(BF16)


def _dft_matrices(n_in, tf):
    fwd_tabs, inv_tabs = _dft_tables(n_in)
    sub_n = min(V7X_LANES, n_in)
    nb = n_in // sub_n
    tabs = [_const_spec((nb, n_in)), _const_spec((nb, n_in)),
            _const_spec((sub_n, n_in)), _const_spec((sub_n, n_in))]
    fwd = pl.pallas_call(
        functools.partial(_dft_fwd_gen_kernel, tf=tf),
        grid=(n_in // tf,),
        in_specs=tabs,
        out_specs=pl.BlockSpec((2 * tf, n_in), lambda i: (i, 0)),
        out_shape=jax.ShapeDtypeStruct((2 * n_in, n_in), BF16),
        compiler_params=_params(1),
        name="dft_fwd_gen",
    )(*fwd_tabs)
    inv = pl.pallas_call(
        functools.partial(_dft_inv_gen_kernel, tf=tf, n_in=n_in),
        grid=(nb,),
        in_specs=tabs,
        out_specs=pl.BlockSpec((sub_n, 2 * n_in), lambda i: (i, 0)),
        out_shape=jax.ShapeDtypeStruct((n_in, 2 * n_in), BF16),
        compiler_params=_params(1),
        name="dft_inv_gen",
    )(*inv_tabs)
    return fwd, inv


def _hy_proj_kernel(x_ref, xp_ref, xn_ref, mod_ref, g_ref, w_ref, b_ref, ws_ref, bs_ref,
                    v_ref, x1_ref, x2_ref):
    i = pl.program_id(1)
    tm, d = x_ref.shape
    n_ph = v_ref.shape[0]
    hp = tm // n_ph
    g = g_ref[...]
    shift, scale = mod_ref[3:4, :], mod_ref[4:5, :]
    u = _adaln(x_ref[...], g, shift, scale).astype(BF16)
    u = jnp.swapaxes(u.reshape(hp, n_ph, d), 0, 1).reshape(tm, d)
    halo = jnp.concatenate([xp_ref[...], xn_ref[...]], axis=0)
    uh = _adaln(halo, g, shift, scale).astype(BF16)
    p = _mm(jnp.concatenate([u, uh], axis=0), w_ref[...]) + b_ref[...]
    ph = [p[r * hp:(r + 1) * hp] for r in range(n_ph)]
    n_halo = xp_ref.shape[0]
    before = jnp.where(i == 0, 0.0, p[tm + n_halo - 1:tm + n_halo])
    after = jnp.where(i == pl.num_programs(1) - 1, 0.0, p[tm + n_halo:tm + n_halo + 1])
    row = lax.broadcasted_iota(jnp.int32, ph[0].shape, 0)
    last_prev = jnp.where(row == 0, before, pltpu.roll(ph[-1], 1, axis=0))
    first_next = jnp.where(row == hp - 1, after, pltpu.roll(ph[0], hp - 1, axis=0))
    w0, w1, w2, bs = ws_ref[0:1, :], ws_ref[1:2, :], ws_ref[2:3, :], bs_ref[...]
    for r in range(n_ph):
        prev = ph[r - 1] if r > 0 else last_prev
        nxt = ph[r + 1] if r < n_ph - 1 else first_next
        conv = prev * w0 + ph[r] * w1 + nxt * w2 + bs
        v_ref[r] = conv[:, :d].astype(BF16)
        x1_ref[r] = conv[:, d:2 * d].astype(BF16)
        x2_ref[r] = conv[:, 2 * d:].astype(BF16)


def _hy_filter_kernel(feat_ref, wp_ref, bp_ref, wm_ref, bm_ref, fr_ref, wf_ref, dl_ref,
                      hf_ref, hb_ref, ss_ref, *, d, n_orders):
    start = (pl.program_id(0) == 0) & (pl.program_id(1) == 0)
    feat = feat_ref[...]
    fr = fr_ref[...]
    hdn = jnp.sin(fr * (_mm_f32(feat, wp_ref[...]) + bp_ref[...]))
    hdn = jnp.sin(fr * (_mm_f32(hdn, wm_ref[...]) + bm_ref[...]))
    h = _mm(hdn.astype(BF16), wf_ref[...])
    decay = jnp.exp(-feat[:, 0:1] * dl_ref[...])
    row = lax.broadcasted_iota(jnp.int32, decay.shape, 0)
    first = (row == 0) & start

    @pl.when(start)
    def _():
        ss_ref[...] = jnp.zeros_like(ss_ref)

    for o in range(n_orders):
        sl = slice(o * d, (o + 1) * d)
        hf = h[:, 2 * o * d:(2 * o + 1) * d] * decay
        hb = jnp.where(first, 0.0, h[:, (2 * o + 1) * d:(2 * o + 2) * d] * decay)
        hf_ref[:, sl] = hf.astype(BF16)
        hb_ref[:, sl] = hb.astype(BF16)
        ss_ref[:, sl] += jnp.sum(hf * hf + hb * hb, axis=0, keepdims=True)


def _twiddle(a, c, s, conj=False):
    a_re, a_s = a
    if conj:
        return a_re * c + a_s * s, a_s * c - a_re * s
    return a_re * c - a_s * s, a_s * c + a_re * s


def _add(a, b):
    return a[0] + b[0], a[1] + b[1]


def _sub(a, b):
    return a[0] - b[0], a[1] - b[1]


def _dit_blocks(dft, phase_refs, tw, tp):
    ph = []
    for ref in phase_refs:
        t = _mm(dft, ref[...])
        ph.append((t[0:tp], t[tp:]))
    c1, s1, c2, s2, c3, s3 = tw
    t2, t3 = _twiddle(ph[2], c1, s1), _twiddle(ph[3], c1, s1)
    e_a, e_b = _add(ph[0], t2), _sub(ph[0], t2)
    o_a, o_b = _add(ph[1], t3), _sub(ph[1], t3)
    ta, tb = _twiddle(o_a, c2, s2), _twiddle(o_b, c3, s3, conj=True)
    return [_add(e_a, ta), _sub(e_a, ta), _add(e_b, tb), _sub(e_b, tb)]


def _bin_group(f_ref, tw_ref, g, tp):
    re, sn = slice(2 * g * tp, (2 * g + 1) * tp), slice((2 * g + 1) * tp, (2 * g + 2) * tp)
    dft = f_ref[2 * g * tp:(2 * g + 2) * tp, :]
    tw = [tw_ref[i, g * tp:(g + 1) * tp, :] for i in range(6)]
    return re, sn, dft, tw


def _hy_spec_kernel(f_ref, f0, f1, f2, f3, b0, b1, b2, b3, tw_ref, ss_ref, o_ref, *, tp, scale):
    wgt = scale * lax.rsqrt(ss_ref[...] + NORM_EPS)
    for g in range(tw_ref.shape[1] // tp):
        re, sn, dft, tw = _bin_group(f_ref, tw_ref, g, tp)
        fwd = _dit_blocks(dft, (f0, f1, f2, f3), tw, tp)
        bwd = _dit_blocks(dft, (b0, b1, b2, b3), tw, tp)
        for blk in range(4):
            o_ref[blk, re, :] = (fwd[blk][0] + bwd[blk][0]) * wgt
            o_ref[blk, sn, :] = (fwd[blk][1] - bwd[blk][1]) * wgt


def _hy_fwd_kernel(f_ref, u0, u1, u2, u3, h_ref, tw_ref, y_ref, *, tp):
    for g in range(tw_ref.shape[1] // tp):
        re, sn, dft, tw = _bin_group(f_ref, tw_ref, g, tp)
        c1, s1, c2, s2, c3, s3 = tw
        x = _dit_blocks(dft, (u0, u1, u2, u3), tw, tp)
        y = []
        for blk in range(4):
            h_re, h_s = h_ref[blk, re, :], h_ref[blk, sn, :]
            x_re, x_s = x[blk]
            y.append((x_re * h_re - x_s * h_s, x_re * h_s + x_s * h_re))
        e_a, o_a = _add(y[0], y[1]), _twiddle(_sub(y[0], y[1]), c2, s2, conj=True)
        e_b, o_b = _add(y[2], y[3]), _twiddle(_sub(y[2], y[3]), c3, s3)
        out = [_add(e_a, e_b), _add(o_a, o_b),
               _twiddle(_sub(e_a, e_b), c1, s1, conj=True), _twiddle(_sub(o_a, o_b), c1, s1, conj=True)]
        for r in range(4):
            y_ref[r, re, :] = out[r][0].astype(BF16)
            y_ref[r, sn, :] = out[r][1].astype(BF16)


def _hy_inv_kernel(f_ref, y_ref, a_ref, u_ref, fb_ref, o_ref, *, rows):
    for r in range(f_ref.shape[0] // rows):
        sl = slice(r * rows, (r + 1) * rows)
        conv = _mm(f_ref[sl, :], y_ref[...])
        z = a_ref[sl, :].astype(F32) * (conv + u_ref[sl, :].astype(F32) * fb_ref[...])
        o_ref[sl, :] = z.astype(o_ref.dtype)


def _hy_out_kernel(x_ref, mod_ref, z_ref, w_ref, b_ref, *refs):
    o_ref = refs[-1]
    n_ph, hp, d = z_ref.shape
    z = jnp.stack([z_ref[r] for r in range(n_ph)], axis=1).reshape(n_ph * hp, d)
    o_ref[...] = x_ref[...] + mod_ref[5:6, :] * (_mm(z, w_ref[...]) + b_ref[...])


def _hyena_layer(h, rows, part, out_buf, mods, layer, g, w_in, b_in, w_short, b_short, w_pos, b_pos, w_mid,
                 b_mid, freq, w_filt, fbias, w_out, b_out):
    bsz = rows.bsz
    length = rows.length(part)
    d = h.shape[1]
    mod_row = (lambda b, i: b) if part == "lat" else (lambda b, i: bsz)
    n_full = 2 * length
    n_ph = DFT_RADIX
    q = length // n_ph
    tm = min(512, length)
    hp = tm // n_ph
    tc = DFT_COL_TILE
    n_ct = d // tc
    width = w_mid.shape[0]
    n_parts = 2 * HYENA_ORDER

    halo = 8
    ph_spec = pl.BlockSpec((None, n_ph, hp, d), lambda b, i: (b, 0, i, 0))
    tile_blk, halo_blk = rows.block(part, tm), rows.block(part, halo)
    v, x1, x2 = pl.pallas_call(
        _hy_proj_kernel,
        grid=(bsz, length // tm),
        in_specs=[
            pl.BlockSpec((tm, d), lambda b, i: (tile_blk(b, i), 0)),
            pl.BlockSpec((halo, d), lambda b, i: (halo_blk(b, jnp.maximum(i * (tm // halo) - 1, 0)), 0)),
            pl.BlockSpec((halo, d),
                         lambda b, i: (halo_blk(b, jnp.minimum((i + 1) * (tm // halo), length // halo - 1)), 0)),
            _mod_spec(d, layer, mod_row),
            _const_spec((1, d)),
            _const_spec((d, 3 * d)),
            _const_spec((1, 3 * d)),
            _const_spec((3, 3 * d)),
            _const_spec((1, 3 * d)),
        ],
        out_specs=[ph_spec] * 3,
        out_shape=[jax.ShapeDtypeStruct((bsz, n_ph, q, d), BF16)] * 3,
        compiler_params=_params(2),
        name="hy_proj",
    )(h, h, h, mods, g.reshape(1, d), w_in, b_in.reshape(1, 3 * d),
      w_short, b_short.reshape(1, 3 * d))

    t = jnp.linspace(0.0, 1.0, length, dtype=F32)[:, None]
    bands = jnp.linspace(1e-4, HYENA_BANDS - 1, HYENA_BANDS, dtype=F32)
    ang = (2.0 * math.pi / length) * jnp.arange(length, dtype=F32)[:, None] * bands[None, :]
    feat = jnp.concatenate([t, jnp.cos(ang), -jnp.sin(ang)], axis=-1)
    feat = jnp.pad(feat, ((0, 0), (0, V7X_LANES - HYENA_EMB)))
    w_pos_p = jnp.pad(w_pos.astype(F32), ((0, V7X_LANES - HYENA_EMB), (0, 0)))
    deltas = jnp.abs(jnp.linspace(HYENA_MIN_DECAY, HYENA_MAX_DECAY, d, dtype=F32)).reshape(1, d)
    feat = jnp.concatenate([feat[r::n_ph] for r in range(n_ph)], axis=0)
    tl = min(512, q)
    od = HYENA_ORDER * d
    tap_spec = pl.BlockSpec((tl, od), lambda r, i: (i, r))
    taps_f, taps_b, ss = pl.pallas_call(
        functools.partial(_hy_filter_kernel, d=d, n_orders=HYENA_ORDER),
        grid=(n_ph, q // tl),
        in_specs=[
            pl.BlockSpec((tl, V7X_LANES), lambda r, i: (r * (q // tl) + i, 0)),
            _const_spec((V7X_LANES, width)),
            _const_spec((1, width)),
            _const_spec((width, width)),
            _const_spec((1, width)),
            _const_spec((1, width)),
            _const_spec((width, n_parts * d)),
            _const_spec((1, d)),
        ],
        out_specs=[tap_spec, tap_spec, pl.BlockSpec((1, od), lambda r, i: (0, 0))],
        out_shape=[jax.ShapeDtypeStruct((q, n_ph * od), BF16), jax.ShapeDtypeStruct((q, n_ph * od), BF16),
                   jax.ShapeDtypeStruct((1, od), F32)],
        compiler_params=_params(2),
        name="hy_filter",
    )(feat, w_pos_p, b_pos.reshape(1, width), w_mid, b_mid.reshape(1, width), freq.reshape(1, width),
      w_filt.astype(BF16), deltas)

    tf = min(DFT_FREQ_TILE, q)
    tp = min(DFT_PACK, q)
    tn = min(DFT_TIME_TILE, q)
    dft_fwd, dft_inv = _dft_matrices(q, tp)
    odd = 2.0 * jnp.arange(q, dtype=F32) + 1.0
    angles = jnp.stack([(2.0 * math.pi / n_full) * odd, (math.pi / n_full) * odd,
                        (math.pi / n_full) * (length - odd)])
    tw = jnp.stack([jnp.cos(angles), jnp.sin(angles)], axis=1).reshape(6, q)
    tw = jnp.broadcast_to(tw[:, :, None], (6, q, tc))
    tw_spec = pl.BlockSpec((6, tf, tc), lambda f, *_: (0, f, 0))

    n_oc = od // tc

    def tap_phase(r):
        return pl.BlockSpec((q, tc), lambda f, o, j: (0, r * n_oc + o * n_ct + j))

    spec = pl.pallas_call(
        functools.partial(_hy_spec_kernel, tp=tp, scale=1.0 / length),
        grid=(q // tf, HYENA_ORDER, n_ct),
        in_specs=[pl.BlockSpec((2 * tf, q), lambda f, o, j: (f, 0))]
        + [tap_phase(r) for r in range(n_ph)] * 2
        + [tw_spec, pl.BlockSpec((1, tc), lambda f, o, j: (0, o * n_ct + j))],
        out_specs=pl.BlockSpec((None, n_ph, 2 * tf, tc), lambda f, o, j: (o, 0, f, j)),
        out_shape=jax.ShapeDtypeStruct((HYENA_ORDER, n_ph, 2 * q, d), F32),
        compiler_params=_params(3),
        name="hy_spec",
    )(dft_fwd, *([taps_f] * n_ph), *([taps_b] * n_ph), tw, ss)

    tci = d if q < DFT_FREQ_TILE else 2 * tc
    n_ci = d // tci

    def long_conv(order, u, a):
        yspec = pl.pallas_call(
            functools.partial(_hy_fwd_kernel, tp=tp),
            grid=(q // tf, n_ct, bsz),
            in_specs=[pl.BlockSpec((2 * tf, q), lambda f, j, b: (f, 0))]
            + [pl.BlockSpec((None, None, q, tc), functools.partial(lambda r, f, j, b: (b, r, 0, j), r))
               for r in range(n_ph)]
            + [pl.BlockSpec((None, n_ph, 2 * tf, tc), lambda f, j, b: (order, 0, f, j)), tw_spec],
            out_specs=pl.BlockSpec((None, n_ph, 2 * tf, tc), lambda f, j, b: (b, 0, f, j)),
            out_shape=jax.ShapeDtypeStruct((bsz, n_ph, 2 * q, d), BF16),
            compiler_params=_params(3),
            name="hy_fwd",
        )(dft_fwd, *([u] * n_ph), spec, tw)
        row_spec = pl.BlockSpec((None, None, tn, tci), lambda n, j, b, r: (b, r, n, j))
        return pl.pallas_call(
            functools.partial(_hy_inv_kernel, rows=min(DFT_ROW_GROUP, tn)),
            grid=(q // tn, n_ci, bsz, n_ph),
            in_specs=[
                pl.BlockSpec((tn, 2 * q), lambda n, j, b, r: (n, 0)),
                pl.BlockSpec((None, None, 2 * q, tci), lambda n, j, b, r: (b, r, 0, j)),
                row_spec,
                row_spec,
                pl.BlockSpec((None, 1, tci), lambda n, j, b, r: (order, 0, j)),
            ],
            out_specs=row_spec,
            out_shape=jax.ShapeDtypeStruct((bsz, n_ph, q, d), BF16),
            compiler_params=_params(4),
            name="hy_inv",
        )(dft_inv, yspec, a, u, fbias.astype(F32).reshape(HYENA_ORDER, 1, d))

    z = long_conv(0, v, x1)
    zz = long_conv(1, z, x2)

    in_specs = [
        pl.BlockSpec((tm, d), lambda b, i: (tile_blk(b, i), 0)),
        _mod_spec(d, layer, mod_row),
        pl.BlockSpec((None, n_ph, hp, d), lambda b, i: (b, 0, i, 0)),
        _const_spec((d, d)),
        _const_spec((1, d)),
    ]
    args = [h, mods, zz, w_out, b_out.reshape(1, d)]
    aliases = {}
    if out_buf is not None:
        in_specs.append(pl.BlockSpec(memory_space=pl.ANY))
        args.append(out_buf)
        aliases = {len(args) - 1: 0}
    return pl.pallas_call(
        _hy_out_kernel,
        grid=(bsz, length // tm),
        in_specs=in_specs,
        out_specs=pl.BlockSpec((tm, d), lambda b, i: (tile_blk(b, i), 0)),
        out_shape=jax.ShapeDtypeStruct(h.shape, F32),
        input_output_aliases=aliases,
        compiler_params=_params(2),
        name="hy_out",
    )(*args)


def kernel(x, c, ctx, c_ctx, ada_w, ada_b, norm_g, ffn_w1, ffn_w3, ffn_w2, ret_w_in, ret_w_out, ret_decay,
           pool_w, pool_b, pool_scale, hy_w_in, hy_b_in, hy_w_short, hy_b_short, hy_w_pos, hy_b_pos,
           hy_w_mid, hy_b_mid, hy_freq, hy_w_filt, hy_bias, hy_w_out, hy_b_out, final_g):
    bsz, seq, d = x.shape
    t_ctx = ctx.shape[1]
    depth = ada_w.shape[0]
    n_mixers = 3
    assert bsz + 1 <= MOD_ROWS

    cond = jnp.concatenate([c, c_ctx[None, :], jnp.zeros((MOD_ROWS - bsz - 1, d), F32)], axis=0)
    mods = _ada_all(cond, ada_w, ada_b).reshape(depth, MOD_ROWS, ADA_CHUNKS, d)

    rows = _Rows(bsz, seq, t_ctx)
    ffn_stacks = (ffn_w1, ffn_w3, ffn_w2)
    ffn_weights = [tuple(w[0, 0].astype(BF16) for w in ffn_stacks)]

    def ffn(h, layer, half, ctx_live, mixer_jobs=(), final=False):
        jobs = [] if final else [(w, (layer, 1) if half == 0 else (layer + 1, 0)) for w in ffn_stacks]
        h, cast = _half_ffn(h, rows, mods, layer, half, norm_g[layer, 2 * half], ffn_weights[0], final_g,
                            ctx_live, cast_jobs=jobs + list(mixer_jobs), final=final)
        ffn_weights[0] = cast[:len(ffn_stacks)]
        return h, cast[len(ffn_stacks):]

    h = (x.reshape(bsz * seq, d), ctx.reshape(bsz * t_ctx, d))
    for layer in range(depth):
        kind = layer % n_mixers
        slot = layer // n_mixers
        last = layer == depth - 1
        ctx_out = not last
        ctx_live = ctx_out or kind == 0
        g_mix = norm_g[layer, 1]
        if kind == 0:
            h, (w_in, w_out) = ffn(h, layer, 0, ctx_live, [(ret_w_in, (slot,)), (ret_w_out, (slot,))])
            h = _retention_layer(h, rows, mods, layer, g_mix, w_in, w_out, ret_decay[slot], ctx_out)
        elif kind == 1:
            h, _ = ffn(h, layer, 0, ctx_live)
            h = _pool_layer(h, rows, GRID_W, mods, layer, g_mix, pool_w[slot], pool_b[slot], pool_scale[slot],
                            ctx_out)
        else:
            h, (w_in, w_out) = ffn(h, layer, 0, ctx_live, [(hy_w_in, (slot,)), (hy_w_out, (slot,))])
            hp = (w_in, hy_b_in[slot], hy_w_short[slot], hy_b_short[slot], hy_w_pos[slot],
                  hy_b_pos[slot], hy_w_mid[slot], hy_b_mid[slot], hy_freq[slot], hy_w_filt[slot],
                  hy_bias[slot], w_out, hy_b_out[slot])
            new_h = _hyena_layer(h, rows, "lat", None, mods, layer, g_mix, *hp)
            if ctx_out:
                new_h = _hyena_layer(h, rows, "ctx", new_h, mods, layer, g_mix, *hp)
            h = new_h
        h, _ = ffn(h, layer, 1, ctx_out, final=last)
    return h.reshape(bsz, seq, d)
```

```python
import functools
import math
from typing import NamedTuple

import numpy as np
import jax
import jax.numpy as jnp
from jax import lax
from jax.experimental import pallas as pl
from jax.experimental.pallas import tpu as pltpu

F32 = jnp.float32
BF16 = jnp.bfloat16

GRID_W = 64
ADA_CHUNKS = 9
NORM_EPS = 1e-6
RET_HEADS = 4
RET_CHUNK = 128
ROPE_BASE = 10000.0
POOL_WINDOWS = (2, 4, 8, 16)
HYENA_ORDER = 2
HYENA_EMB = 33
HYENA_BANDS = (HYENA_EMB - 1) // 2
HYENA_TARGET = 1e-2
HYENA_FAST = 0.3
HYENA_SLOW = 1.5
HYENA_MAX_DECAY = math.log(HYENA_TARGET) / HYENA_FAST
HYENA_MIN_DECAY = math.log(HYENA_TARGET) / HYENA_SLOW

V7X_LANES = 128
V7X_VMEM_LIMIT_BYTES = 56 * 1024 * 1024
MOD_ROWS = 8
FFN_CAST_SLABS = 16
POOL_TILE = 512
RET_TILE = 256
SCAN_CHUNK = 256
DFT_RADIX = 4
DFT_FREQ_TILE = 512
DFT_PACK = 256
DFT_ROW_GROUP = 512
DFT_TIME_TILE = 2048
DFT_COL_TILE = 256
HIGHEST = lax.Precision.HIGHEST


def _params(n_axes):
    return pltpu.CompilerParams(
        dimension_semantics=("arbitrary",) * n_axes,
        vmem_limit_bytes=V7X_VMEM_LIMIT_BYTES)


def _const_spec(shape):
    zeros = (0,) * len(shape)
    return pl.BlockSpec(shape, lambda *_: zeros, pipeline_mode=pl.Buffered(1))


def _mod_spec(d, layer, row_fn):
    return pl.BlockSpec((None, None, ADA_CHUNKS, d), lambda *idx: (layer, row_fn(*idx), 0, 0))


class _Rows(NamedTuple):
    bsz: int
    t_lat: int
    t_ctx: int

    @property
    def n_lat(self):
        return self.bsz * self.t_lat

    @property
    def n_all(self):
        return self.bsz * (self.t_lat + self.t_ctx)

    def length(self, part):
        return self.t_lat if part == "lat" else self.t_ctx

    def block(self, part, tile):
        per = self.length(part) // tile
        base = 0 if part == "lat" else self.n_lat // tile
        assert self.length(part) % tile == 0 and self.n_lat % tile == 0
        return lambda b, i: base + b * per + i


def _mm(a, b):
    return jnp.dot(a, b, preferred_element_type=F32)


def _mm_f32(a, b):
    return jnp.dot(a, b, preferred_element_type=F32, precision=HIGHEST)


def _silu(x):
    return x * jax.nn.sigmoid(x)


def _adaln(x, g, shift, scale):
    ms = jnp.mean(x * x, axis=-1, keepdims=True)
    y = x * lax.rsqrt(ms + NORM_EPS) * g
    return y * (1.0 + scale) + shift


def _ada_kernel(c_ref, w_ref, b_ref, o_ref):
    s = _silu(c_ref[...]).astype(BF16)
    o_ref[...] = _mm(s, w_ref[...].astype(BF16)) + b_ref[...]


def _ada_all(cond, ada_w, ada_b):
    depth, d, nd = ada_w.shape
    return pl.pallas_call(
        _ada_kernel,
        grid=(depth, nd // d),
        in_specs=[
            pl.BlockSpec((MOD_ROWS, d), lambda l, j: (0, 0)),
            pl.BlockSpec((None, d, d), lambda l, j: (l, 0, j)),
            pl.BlockSpec((None, 1, d), lambda l, j: (l, 0, j)),
        ],
        out_specs=pl.BlockSpec((None, MOD_ROWS, d), lambda l, j: (l, 0, j)),
        out_shape=jax.ShapeDtypeStruct((depth, MOD_ROWS, nd), F32),
        compiler_params=_params(2),
        name="ada",
    )(cond, ada_w, ada_b.reshape(depth, 1, nd))


def _ffn_kernel(*refs, base, final, n_lat, split_input, n_cast):
    n_x = 2 if split_input else 1
    x_refs, refs = refs[:n_x], refs[n_x:]
    mod_ref, g_ref, w1_ref, w3_ref, w2_ref, fg_ref = refs[:6]
    rest = refs[6:]
    if n_cast:
        n_jobs = (len(rest) - 1) // 2
        src_refs, o_ref, dst_refs = rest[:n_jobs], rest[n_jobs], rest[n_jobs + 1:]

        @pl.when(pl.program_id(0) < n_cast)
        def _():
            for src, dst in zip(src_refs, dst_refs):
                dst[...] = src[...].astype(BF16)
    else:
        (o_ref,) = rest
    if split_input:
        x = jnp.where(pl.program_id(0) >= n_lat, x_refs[1][...], x_refs[0][...])
    else:
        x = x_refs[0][...]
    u = _adaln(x, g_ref[...], mod_ref[base:base + 1, :], mod_ref[base + 1:base + 2, :]).astype(BF16)
    h1 = _mm(u, w1_ref[...])
    h3 = _mm(u, w3_ref[...])
    a = (_silu(h1) * h3).astype(BF16)
    y = x + (0.5 * mod_ref[base + 2:base + 3, :]) * _mm(a, w2_ref[...])
    if final:
        ms = jnp.mean(y * y, axis=-1, keepdims=True)
        y = y * lax.rsqrt(ms + NORM_EPS) * fg_ref[...]
    o_ref[...] = y


def _half_ffn(h, rows, mods, layer, half, g, weights, final_g, ctx_live, cast_jobs=(), final=False, tm=512):
    w1, w3, w2 = weights
    d, f = w1.shape
    base = 6 * half
    per_b = rows.t_lat // tm
    n_lat = rows.n_lat // tm
    n_ctx = rows.bsz * rows.t_ctx // tm if ctx_live else 0
    assert rows.t_lat % tm == 0 and (rows.bsz * rows.t_ctx) % tm == 0
    split_input = isinstance(h, tuple)

    if split_input:
        x_specs = [pl.BlockSpec((tm, d), lambda s: (jnp.minimum(s, n_lat - 1), 0)),
                   pl.BlockSpec((tm, d), lambda s: (jnp.maximum(s - n_lat, 0), 0))]
        x_args = list(h)
    else:
        x_specs = [pl.BlockSpec((tm, d), lambda s: (s, 0))]
        x_args = [h]
    in_specs = x_specs + [
        _mod_spec(d, layer, lambda s: jnp.where(s < n_lat, s // per_b, rows.bsz)),
        _const_spec((1, d)),
        _const_spec((d, f)),
        _const_spec((d, f)),
        _const_spec((f, d)),
        _const_spec((1, d)),
    ]
    args = x_args + [mods, g.reshape(1, d), w1, w3, w2, final_g.reshape(1, d)]
    out_specs = [pl.BlockSpec((tm, d), lambda s: (s, 0))]
    out_shape = [jax.ShapeDtypeStruct((rows.n_lat if final else rows.n_all, d), F32)]
    n_cast = min(FFN_CAST_SLABS, n_lat) if cast_jobs else 0
    for w, lead in cast_jobs:
        r, c = w.shape[-2:]
        assert r % (16 * n_cast) == 0 and len(lead) == w.ndim - 2
        in_specs.append(pl.BlockSpec((None,) * len(lead) + (r // n_cast, c),
                                     functools.partial(lambda lead, s: lead + (jnp.minimum(s, n_cast - 1), 0), lead)))
        out_specs.append(pl.BlockSpec((r // n_cast, c), lambda s: (jnp.minimum(s, n_cast - 1), 0)))
        out_shape.append(jax.ShapeDtypeStruct((r, c), BF16))
        args.append(w)
    outs = pl.pallas_call(
        functools.partial(_ffn_kernel, base=base, final=final, n_lat=n_lat, split_input=split_input,
                          n_cast=n_cast),
        grid=(n_lat + n_ctx,),
        in_specs=in_specs,
        out_specs=out_specs,
        out_shape=out_shape,
        compiler_params=_params(1),
        name="half_ffn",
    )(*args)
    return outs[0], tuple(outs[1:])


def _ret_proj_kernel(xc_ref, xl_ref, mod_ref, g_ref, w_ref, cos_ref, sin_ref, q_ref, k_ref, v_ref, gate_ref,
                     *, n_heads, dk, dv, ctx_tiles):
    x = jnp.where(pl.program_id(1) < ctx_tiles, xc_ref[...], xl_ref[...])
    u = _adaln(x, g_ref[...], mod_ref[3:4, :], mod_ref[4:5, :]).astype(BF16)
    p = _mm(u, w_ref[...])
    cos = cos_ref[...]
    sin = sin_ref[...]
    half = dk // 2
    qk = n_heads * dk
    k_scale = dk ** -0.5

    def rot(a):
        a1, a2 = a[:, :half], a[:, half:]
        return jnp.concatenate([a1 * cos - a2 * sin, a2 * cos + a1 * sin], axis=-1)

    for h in range(n_heads):
        q_ref[:, h * dk:(h + 1) * dk] = rot(p[:, h * dk:(h + 1) * dk]).astype(BF16)
        k_ref[:, h * dk:(h + 1) * dk] = (rot(p[:, qk + h * dk:qk + (h + 1) * dk]) * k_scale).astype(BF16)
    v_ref[...] = p[:, 2 * qk:2 * qk + n_heads * dv].astype(BF16)
    gate_ref[...] = p[:, 2 * qk + n_heads * dv:].astype(BF16)


def _ret_scan_kernel(q_ref, k_ref, v_ref, dmat_ref, qdec_ref, kdec_ref, cdec_ref, o_ref,
                     state_ref, fwd_ref, *, n_heads, dk, dv, n_chunks, bwd_chunk):
    s = pl.program_id(1)
    cs = q_ref.shape[0]

    @pl.when((s == 0) | (s == n_chunks))
    def _():
        state_ref[...] = jnp.zeros_like(state_ref)

    chunk = jnp.where(s < n_chunks, s, bwd_chunk(jnp.maximum(s - n_chunks, 0)))
    rows = pl.ds(pl.multiple_of(chunk * cs, cs), cs)
    outs = []
    for h in range(n_heads):
        q = q_ref[:, h * dk:(h + 1) * dk]
        k = k_ref[:, h * dk:(h + 1) * dk]
        v = v_ref[:, h * dv:(h + 1) * dv]
        scores = lax.dot_general(q, k, (((1,), (1,)), ((), ())), preferred_element_type=F32) * dmat_ref[h]
        state = state_ref[h]
        outs.append(_mm(scores.astype(BF16), v) + _mm(q, state.astype(BF16)) * qdec_ref[h])
        kd = (k.astype(F32) * kdec_ref[h]).astype(BF16)
        state_ref[h] = state * cdec_ref[h] + lax.dot_general(
            kd, v, (((0,), (0,)), ((), ())), preferred_element_type=F32)

    @pl.when(s < n_chunks)
    def _():
        for h in range(n_heads):
            fwd_ref[rows, h * dv:(h + 1) * dv] = outs[h].astype(fwd_ref.dtype)

    @pl.when(s >= n_chunks)
    def _():
        for h in range(n_heads):
            o = outs[h] + fwd_ref[rows, h * dv:(h + 1) * dv].astype(F32)
            mu = jnp.mean(o, axis=-1, keepdims=True)
            oc = o - mu
            var = jnp.mean(oc * oc, axis=-1, keepdims=True)
            o_ref[:, h * dv:(h + 1) * dv] = (oc * lax.rsqrt(var + NORM_EPS)).astype(o_ref.dtype)


def _ret_out_kernel(x_ref, mod_ref, y_ref, gate_ref, w_ref, o_ref):
    z = (_silu(gate_ref[...].astype(F32)) * y_ref[...].astype(F32)).astype(BF16)
    o_ref[...] = x_ref[...] + mod_ref[5:6, :] * _mm(z, w_ref[...])


def _retention_layer(h, rows, mods, layer, g, w_in, w_out, decay, ctx_out):
    bsz, t_lat, t_ctx = rows
    d = h.shape[1]
    t_all = t_ctx + t_lat
    n_heads = RET_HEADS
    dk = d // n_heads
    dv = 2 * d // n_heads
    qk = n_heads * dk
    vw = n_heads * dv
    tm = RET_TILE
    assert t_ctx % tm == 0 and t_lat % tm == 0
    ctx_tiles = t_ctx // tm
    ctx_row = bsz
    lat_blk, ctx_blk = rows.block("lat", tm), rows.block("ctx", tm)

    pos = jnp.arange(t_all, dtype=F32)
    inv = 1.0 / (ROPE_BASE ** jnp.linspace(0.0, 1.0, dk // 2, dtype=F32))
    ang = pos[:, None] * inv[None, :]

    q, k, v, gate = pl.pallas_call(
        functools.partial(_ret_proj_kernel, n_heads=n_heads, dk=dk, dv=dv, ctx_tiles=ctx_tiles),
        grid=(bsz, t_all // tm),
        in_specs=[
            pl.BlockSpec((tm, d), lambda b, i: (ctx_blk(b, jnp.minimum(i, ctx_tiles - 1)), 0)),
            pl.BlockSpec((tm, d), lambda b, i: (lat_blk(b, jnp.maximum(i - ctx_tiles, 0)), 0)),
            _mod_spec(d, layer, lambda b, i: jnp.where(i < ctx_tiles, ctx_row, b)),
            _const_spec((1, d)),
            _const_spec((d, 2 * qk + 2 * vw)),
            pl.BlockSpec((tm, dk // 2), lambda b, i: (i, 0)),
            pl.BlockSpec((tm, dk // 2), lambda b, i: (i, 0)),
        ],
        out_specs=[
            pl.BlockSpec((None, tm, qk), lambda b, i: (b, i, 0)),
            pl.BlockSpec((None, tm, qk), lambda b, i: (b, i, 0)),
            pl.BlockSpec((None, tm, vw), lambda b, i: (b, i, 0)),
            pl.BlockSpec((None, tm, vw), lambda b, i: (b, i, 0)),
        ],
        out_shape=[
            jax.ShapeDtypeStruct((bsz, t_all, qk), BF16),
            jax.ShapeDtypeStruct((bsz, t_all, qk), BF16),
            jax.ShapeDtypeStruct((bsz, t_all, vw), BF16),
            jax.ShapeDtypeStruct((bsz, t_all, vw), BF16),
        ],
        compiler_params=_params(2),
        name="ret_proj",
    )(h, h, mods, g.reshape(1, d), w_in, jnp.cos(ang), jnp.sin(ang))

    cs = SCAN_CHUNK
    assert t_ctx % cs == 0 and t_lat % cs == 0
    n_chunks = t_all // cs
    ctx_chunks = t_ctx // cs
    lg = jax.nn.log_sigmoid(decay.astype(F32))[:, :, None]
    n = jnp.arange(cs, dtype=F32)
    rel = n[:, None] - n[None, :]
    dm_f = jnp.where(rel >= 0, jnp.exp(lg[0][:, :, None] * jnp.maximum(rel, 0.0)), 0.0)
    dm_b = jnp.where(rel < 0, jnp.exp(lg[1][:, :, None] * jnp.maximum(-rel, 0.0)), 0.0)
    dmat = jnp.stack([dm_f, dm_b])
    qd = jnp.stack([jnp.exp(lg[0] * (n + 1.0)), jnp.exp(lg[1] * (cs - n))])
    kd = jnp.stack([jnp.exp(lg[0] * (cs - 1.0 - n)), jnp.exp(lg[1] * n)])
    cd = jnp.exp(lg * cs)
    qdec = jnp.broadcast_to(qd[..., None], (2, n_heads, cs, dv))
    kdec = jnp.broadcast_to(kd[..., None], (2, n_heads, cs, dk))
    cdec = jnp.broadcast_to(cd[..., None], (2, n_heads, 1, dv))

    def bwd_chunk(j):
        return jnp.where(j < ctx_chunks, ctx_chunks - 1 - j, n_chunks - 1 + ctx_chunks - j)

    def chunk(s):
        return jnp.where(s < n_chunks, s, bwd_chunk(jnp.maximum(s - n_chunks, 0)))

    def out_chunk(s):
        return bwd_chunk(jnp.maximum(s - n_chunks, 0))

    def table_spec(*shape):
        return pl.BlockSpec((None,) + shape, lambda b, s: (s // n_chunks, 0, 0, 0))

    y = pl.pallas_call(
        functools.partial(_ret_scan_kernel, n_heads=n_heads, dk=dk, dv=dv, n_chunks=n_chunks,
                          bwd_chunk=bwd_chunk),
        grid=(bsz, 2 * n_chunks),
        in_specs=[
            pl.BlockSpec((None, cs, qk), lambda b, s: (b, chunk(s), 0)),
            pl.BlockSpec((None, cs, qk), lambda b, s: (b, chunk(s), 0)),
            pl.BlockSpec((None, cs, vw), lambda b, s: (b, chunk(s), 0)),
            table_spec(n_heads, cs, cs),
            table_spec(n_heads, cs, dv),
            table_spec(n_heads, cs, dk),
            table_spec(n_heads, 1, dv),
        ],
        out_specs=pl.BlockSpec((None, cs, vw), lambda b, s: (b, out_chunk(s), 0)),
        out_shape=jax.ShapeDtypeStruct((bsz, t_all, vw), BF16),
        scratch_shapes=[pltpu.VMEM((n_heads, dk, dv), F32), pltpu.VMEM((t_all, vw), BF16)],
        compiler_params=_params(2),
        name="ret_scan",
    )(q, k, v, dmat, qdec, kdec, cdec)

    first = 0 if ctx_out else ctx_tiles

    def res_blk(b, i):
        j = i + first
        return jnp.where(j < ctx_tiles, ctx_blk(b, jnp.minimum(j, ctx_tiles - 1)),
                         lat_blk(b, jnp.maximum(j - ctx_tiles, 0)))

    return pl.pallas_call(
        _ret_out_kernel,
        grid=(bsz, t_all // tm - first),
        in_specs=[
            pl.BlockSpec((tm, d), lambda b, i: (res_blk(b, i), 0)),
            _mod_spec(d, layer, lambda b, i: jnp.where(i + first < ctx_tiles, ctx_row, b)),
            pl.BlockSpec((None, tm, vw), lambda b, i: (b, i + first, 0)),
            pl.BlockSpec((None, tm, vw), lambda b, i: (b, i + first, 0)),
            _const_spec((vw, d)),
        ],
        out_specs=pl.BlockSpec((tm, d), lambda b, i: (res_blk(b, i), 0)),
        out_shape=jax.ShapeDtypeStruct(h.shape, F32),
        compiler_params=_params(2),
        name="ret_out",
    )(h, mods, y, gate, w_out)


def _pool_tables(seg_len):
    assert POOL_TILE % seg_len == 0
    r = np.arange(POOL_TILE)
    seg, pos = r // seg_len, r % seg_len
    mats, invs = [], []
    for win in POOL_WINDOWS:
        lo = np.clip(pos - win // 2, 0, seg_len)
        hi = np.clip(pos - win // 2 + win, 0, seg_len)
        col_seg, col_pos = seg[None, :], pos[None, :]
        member = (col_seg == seg[:, None]) & (col_pos >= lo[:, None]) & (col_pos < hi[:, None])
        mats.append(member.astype(np.float32))
        invs.append((1.0 / (hi - lo)).astype(np.float32))
    return np.stack(mats), np.stack(invs)


def _pool_kernel(x_ref, mod_ref, g_ref, s_ref, inv_ref, w_ref, b_ref, sc_ref, o_ref, *, n_groups, dg):
    x = x_ref[...]
    u = _adaln(x, g_ref[...], mod_ref[3:4, :], mod_ref[4:5, :])
    gate = mod_ref[5:6, :]
    for gi in range(n_groups):
        sl = slice(gi * dg, (gi + 1) * dg)
        ug = u[:, sl]
        hi = ug.astype(BF16)
        lo = (ug - hi.astype(F32)).astype(BF16)
        win_sum = _mm(s_ref[gi], hi) + _mm(s_ref[gi], lo)
        dlt = win_sum * inv_ref[gi] - ug
        yg = _mm(dlt.astype(BF16), w_ref[gi])
        o_ref[:, sl] = x[:, sl] + gate[:, sl] * ((yg + b_ref[:, sl]) * sc_ref[:, sl])


def _pool_layer(h, rows, lat_seg, mods, layer, g, w_grp, b_grp, scale, ctx_out):
    bsz, t_lat, t_ctx = rows
    d = h.shape[1]
    n_groups = len(POOL_WINDOWS)
    dg = d // n_groups
    tabs = [_pool_tables(lat_seg), _pool_tables(t_ctx)]
    s_tab = jnp.asarray(np.stack([m for m, _ in tabs]), BF16)
    inv_tab = jnp.asarray(np.stack([np.broadcast_to(v[:, :, None], (n_groups, POOL_TILE, dg)) for _, v in tabs]), F32)
    assert t_lat % POOL_TILE == 0 and (bsz * t_ctx) % POOL_TILE == 0
    per_b = t_lat // POOL_TILE
    n_lat = rows.n_lat // POOL_TILE
    n_ctx = bsz * t_ctx // POOL_TILE if ctx_out else 0

    def table_spec(*shape):
        return pl.BlockSpec((None,) + shape, lambda s: (jnp.where(s < n_lat, 0, 1), 0, 0, 0))

    return pl.pallas_call(
        functools.partial(_pool_kernel, n_groups=n_groups, dg=dg),
        grid=(n_lat + n_ctx,),
        in_specs=[
            pl.BlockSpec((POOL_TILE, d), lambda s: (s, 0)),
            _mod_spec(d, layer, lambda s: jnp.where(s < n_lat, s // per_b, bsz)),
            _const_spec((1, d)),
            table_spec(n_groups, POOL_TILE, POOL_TILE),
            table_spec(n_groups, POOL_TILE, dg),
            _const_spec((n_groups, dg, dg)),
            _const_spec((1, d)),
            _const_spec((1, d)),
        ],
        out_specs=pl.BlockSpec((POOL_TILE, d), lambda s: (s, 0)),
        out_shape=jax.ShapeDtypeStruct(h.shape, F32),
        compiler_params=_params(1),
        name="pool",
    )(h, mods, g.reshape(1, d), s_tab, inv_tab, w_grp.astype(BF16), b_grp.reshape(1, d), scale.reshape(1, d))


def _dft_tables(n_in):
    mod = 4 * n_in
    sub_n = min(V7X_LANES, n_in)
    idx = jnp.arange(n_in, dtype=jnp.int32)[None, :]
    blk = jnp.arange(n_in // sub_n, dtype=jnp.int32)[:, None]
    sub = jnp.arange(sub_n, dtype=jnp.int32)[:, None]
    theta = 2.0 * math.pi / mod

    def cs(p):
        ph = (p % mod).astype(F32) * theta
        return jnp.cos(ph), jnp.sin(ph)

    fwd = cs(2 * sub_n * blk * idx) + cs((2 * sub + 1) * idx)
    inv = cs(sub_n * blk * (2 * idx + 1)) + cs(sub * (2 * idx + 1))
    return fwd, inv


def _dft_fwd_gen_kernel(pc_ref, ps_ref, qc_ref, qs_ref, o_ref, *, tf):
    ft = pl.program_id(0)
    sub_n = qc_ref.shape[0]
    blocks = tf // sub_n
    qc = qc_ref[...]
    qs = qs_ref[...]
    for jb in range(blocks):
        kb = ft * blocks + jb
        pc = pc_ref[pl.ds(kb, 1), :]
        ps = ps_ref[pl.ds(kb, 1), :]
        o_ref[jb * sub_n:(jb + 1) * sub_n, :] = (pc * qc - ps * qs).astype(BF16)
        o_ref[tf + jb * sub_n:tf + (jb + 1) * sub_n, :] = (ps * qc + pc * qs).astype(BF16)


def _dft_inv_gen_kernel(pc_ref, ps_ref, qc_ref, qs_ref, o_ref, *, tf, n_in):
    mb = pl.program_id(0)
    pc = pc_ref[pl.ds(mb, 1), :]
    ps = ps_ref[pl.ds(mb, 1), :]
    qc = qc_ref[...]
    qs = qs_ref[...]
    cos_all = pc * qc - ps * qs
    sin_all = ps * qc + pc * qs
    for ft in range(n_in // tf):
        o_ref[:, 2 * ft * tf:(2 * ft + 1) * tf] = cos_all[:, ft * tf:(ft + 1) * tf].astype(BF16)
        o_ref[:, (2 * ft + 1) * tf:(2 * ft + 2) * tf] = sin_all[:, ft * tf:(ft + 1) * tf].astype(BF16)


def _dft_matrices(n_in, tf):
    fwd_tabs, inv_tabs = _dft_tables(n_in)
    sub_n = min(V7X_LANES, n_in)
    nb = n_in // sub_n
    tabs = [_const_spec((nb, n_in)), _const_spec((nb, n_in)),
            _const_spec((sub_n, n_in)), _const_spec((sub_n, n_in))]
    fwd = pl.pallas_call(
        functools.partial(_dft_fwd_gen_kernel, tf=tf),
        grid=(n_in // tf,),
        in_specs=tabs,
        out_specs=pl.BlockSpec((2 * tf, n_in), lambda i: (i, 0)),
        out_shape=jax.ShapeDtypeStruct((2 * n_in, n_in), BF16),
        compiler_params=_params(1),
        name="dft_fwd_gen",
    )(*fwd_tabs)
    inv = pl.pallas_call(
        functools.partial(_dft_inv_gen_kernel, tf=tf, n_in=n_in),
        grid=(nb,),
        in_specs=tabs,
        out_specs=pl.BlockSpec((sub_n, 2 * n_in), lambda i: (i, 0)),
        out_shape=jax.ShapeDtypeStruct((n_in, 2 * n_in), BF16),
        compiler_params=_params(1),
        name="dft_inv_gen",
    )(*inv_tabs)
    return fwd, inv


def _hy_proj_kernel(x_ref, xp_ref, xn_ref, mod_ref, g_ref, w_ref, b_ref, ws_ref, bs_ref,
                    v_ref, x1_ref, x2_ref):
    i = pl.program_id(1)
    tm, d = x_ref.shape
    n_ph = v_ref.shape[0]
    hp = tm // n_ph
    g = g_ref[...]
    shift, scale = mod_ref[3:4, :], mod_ref[4:5, :]
    u = _adaln(x_ref[...], g, shift, scale).astype(BF16)
    u = jnp.swapaxes(u.reshape(hp, n_ph, d), 0, 1).reshape(tm, d)
    halo = jnp.concatenate([xp_ref[...], xn_ref[...]], axis=0)
    uh = _adaln(halo, g, shift, scale).astype(BF16)
    p = _mm(jnp.concatenate([u, uh], axis=0), w_ref[...]) + b_ref[...]
    ph = [p[r * hp:(r + 1) * hp] for r in range(n_ph)]
    n_halo = xp_ref.shape[0]
    before = jnp.where(i == 0, 0.0, p[tm + n_halo - 1:tm + n_halo])
    after = jnp.where(i == pl.num_programs(1) - 1, 0.0, p[tm + n_halo:tm + n_halo + 1])
    row = lax.broadcasted_iota(jnp.int32, ph[0].shape, 0)
    last_prev = jnp.where(row == 0, before, pltpu.roll(ph[-1], 1, axis=0))
    first_next = jnp.where(row == hp - 1, after, pltpu.roll(ph[0], hp - 1, axis=0))
    w0, w1, w2, bs = ws_ref[0:1, :], ws_ref[1:2, :], ws_ref[2:3, :], bs_ref[...]
    for r in range(n_ph):
        prev = ph[r - 1] if r > 0 else last_prev
        nxt = ph[r + 1] if r < n_ph - 1 else first_next
        conv = prev * w0 + ph[r] * w1 + nxt * w2 + bs
        v_ref[r] = conv[:, :d].astype(BF16)
        x1_ref[r] = conv[:, d:2 * d].astype(BF16)
        x2_ref[r] = conv[:, 2 * d:].astype(BF16)


def _hy_filter_kernel(feat_ref, wp_ref, bp_ref, wm_ref, bm_ref, fr_ref, wf_ref, dl_ref,
                      hf_ref, hb_ref, ss_ref, *, d, n_orders):
    start = (pl.program_id(0) == 0) & (pl.program_id(1) == 0)
    feat = feat_ref[...]
    fr = fr_ref[...]
    hdn = jnp.sin(fr * (_mm_f32(feat, wp_ref[...]) + bp_ref[...]))
    hdn = jnp.sin(fr * (_mm_f32(hdn, wm_ref[...]) + bm_ref[...]))
    h = _mm(hdn.astype(BF16), wf_ref[...])
    decay = jnp.exp(-feat[:, 0:1] * dl_ref[...])
    row = lax.broadcasted_iota(jnp.int32, decay.shape, 0)
    first = (row == 0) & start

    @pl.when(start)
    def _():
        ss_ref[...] = jnp.zeros_like(ss_ref)

    for o in range(n_orders):
        sl = slice(o * d, (o + 1) * d)
        hf = h[:, 2 * o * d:(2 * o + 1) * d] * decay
        hb = jnp.where(first, 0.0, h[:, (2 * o + 1) * d:(2 * o + 2) * d] * decay)
        hf_ref[:, sl] = hf.astype(BF16)
        hb_ref[:, sl] = hb.astype(BF16)
        ss_ref[:, sl] += jnp.sum(hf * hf + hb * hb, axis=0, keepdims=True)


def _twiddle(a, c, s, conj=False):
    a_re, a_s = a
    if conj:
        return a_re * c + a_s * s, a_s * c - a_re * s
    return a_re * c - a_s * s, a_s * c + a_re * s


def _add(a, b):
    return a[0] + b[0], a[1] + b[1]


def _sub(a, b):
    return a[0] - b[0], a[1] - b[1]


def _dit_blocks(dft, phase_refs, tw, tp):
    ph = []
    for ref in phase_refs:
        t = _mm(dft, ref[...])
        ph.append((t[0:tp], t[tp:]))
    c1, s1, c2, s2, c3, s3 = tw
    t2, t3 = _twiddle(ph[2], c1, s1), _twiddle(ph[3], c1, s1)
    e_a, e_b = _add(ph[0], t2), _sub(ph[0], t2)
    o_a, o_b = _add(ph[1], t3), _sub(ph[1], t3)
    ta, tb = _twiddle(o_a, c2, s2), _twiddle(o_b, c3, s3, conj=True)
    return [_add(e_a, ta), _sub(e_a, ta), _add(e_b, tb), _sub(e_b, tb)]


def _bin_group(f_ref, tw_ref, g, tp):
    re, sn = slice(2 * g * tp, (2 * g + 1) * tp), slice((2 * g + 1) * tp, (2 * g + 2) * tp)
    dft = f_ref[2 * g * tp:(2 * g + 2) * tp, :]
    tw = [tw_ref[i, g * tp:(g + 1) * tp, :] for i in range(6)]
    return re, sn, dft, tw


def _hy_spec_kernel(f_ref, f0, f1, f2, f3, b0, b1, b2, b3, tw_ref, ss_ref, o_ref, *, tp, scale):
    wgt = scale * lax.rsqrt(ss_ref[...] + NORM_EPS)
    for g in range(tw_ref.shape[1] // tp):
        re, sn, dft, tw = _bin_group(f_ref, tw_ref, g, tp)
        fwd = _dit_blocks(dft, (f0, f1, f2, f3), tw, tp)
        bwd = _dit_blocks(dft, (b0, b1, b2, b3), tw, tp)
        for blk in range(4):
            o_ref[blk, re, :] = (fwd[blk][0] + bwd[blk][0]) * wgt
            o_ref[blk, sn, :] = (fwd[blk][1] - bwd[blk][1]) * wgt


def _hy_fwd_kernel(f_ref, u0, u1, u2, u3, h_ref, tw_ref, y_ref, *, tp):
    for g in range(tw_ref.shape[1] // tp):
        re, sn, dft, tw = _bin_group(f_ref, tw_ref, g, tp)
        c1, s1, c2, s2, c3, s3 = tw
        x = _dit_blocks(dft, (u0, u1, u2, u3), tw, tp)
        y = []
        for blk in range(4):
            h_re, h_s = h_ref[blk, re, :], h_ref[blk, sn, :]
            x_re, x_s = x[blk]
            y.append((x_re * h_re - x_s * h_s, x_re * h_s + x_s * h_re))
        e_a, o_a = _add(y[0], y[1]), _twiddle(_sub(y[0], y[1]), c2, s2, conj=True)
        e_b, o_b = _add(y[2], y[3]), _twiddle(_sub(y[2], y[3]), c3, s3)
        out = [_add(e_a, e_b), _add(o_a, o_b),
               _twiddle(_sub(e_a, e_b), c1, s1, conj=True), _twiddle(_sub(o_a, o_b), c1, s1, conj=True)]
        for r in range(4):
            y_ref[r, re, :] = out[r][0].astype(BF16)
            y_ref[r, sn, :] = out[r][1].astype(BF16)


def _hy_inv_kernel(f_ref, y_ref, a_ref, u_ref, fb_ref, o_ref, *, rows):
    for r in range(f_ref.shape[0] // rows):
        sl = slice(r * rows, (r + 1) * rows)
        conv = _mm(f_ref[sl, :], y_ref[...])
        z = a_ref[sl, :].astype(F32) * (conv + u_ref[sl, :].astype(F32) * fb_ref[...])
        o_ref[sl, :] = z.astype(o_ref.dtype)


def _hy_out_kernel(x_ref, mod_ref, z_ref, w_ref, b_ref, *refs):
    o_ref = refs[-1]
    n_ph, hp, d = z_ref.shape
    z = jnp.stack([z_ref[r] for r in range(n_ph)], axis=1).reshape(n_ph * hp, d)
    o_ref[...] = x_ref[...] + mod_ref[5:6, :] * (_mm(z, w_ref[...]) + b_ref[...])


def _hyena_layer(h, rows, part, out_buf, mods, layer, g, w_in, b_in, w_short, b_short, w_pos, b_pos, w_mid,
                 b_mid, freq, w_filt, fbias, w_out, b_out):
    bsz = rows.bsz
    length = rows.length(part)
    d = h.shape[1]
    mod_row = (lambda b, i: b) if part == "lat" else (lambda b, i: bsz)
    n_full = 2 * length
    n_ph = DFT_RADIX
    q = length // n_ph
    tm = min(512, length)
    hp = tm // n_ph
    tc = DFT_COL_TILE if q >= DFT_FREQ_TILE else d
    n_ct = d // tc
    width = w_mid.shape[0]
    n_parts = 2 * HYENA_ORDER

    halo = 8
    ph_spec = pl.BlockSpec((None, n_ph, hp, d), lambda b, i: (b, 0, i, 0))
    tile_blk, halo_blk = rows.block(part, tm), rows.block(part, halo)
    v, x1, x2 = pl.pallas_call(
        _hy_proj_kernel,
        grid=(bsz, length // tm),
        in_specs=[
            pl.BlockSpec((tm, d), lambda b, i: (tile_blk(b, i), 0)),
            pl.BlockSpec((halo, d), lambda b, i: (halo_blk(b, jnp.maximum(i * (tm // halo) - 1, 0)), 0)),
            pl.BlockSpec((halo, d),
                         lambda b, i: (halo_blk(b, jnp.minimum((i + 1) * (tm // halo), length // halo - 1)), 0)),
            _mod_spec(d, layer, mod_row),
            _const_spec((1, d)),
            _const_spec((d, 3 * d)),
            _const_spec((1, 3 * d)),
            _const_spec((3, 3 * d)),
            _const_spec((1, 3 * d)),
        ],
        out_specs=[ph_spec] * 3,
        out_shape=[jax.ShapeDtypeStruct((bsz, n_ph, q, d), BF16)] * 3,
        compiler_params=_params(2),
        name="hy_proj",
    )(h, h, h, mods, g.reshape(1, d), w_in, b_in.reshape(1, 3 * d),
      w_short, b_short.reshape(1, 3 * d))

    t = jnp.linspace(0.0, 1.0, length, dtype=F32)[:, None]
    bands = jnp.linspace(1e-4, HYENA_BANDS - 1, HYENA_BANDS, dtype=F32)
    ang = (2.0 * math.pi / length) * jnp.arange(length, dtype=F32)[:, None] * bands[None, :]
    feat = jnp.concatenate([t, jnp.cos(ang), -jnp.sin(ang)], axis=-1)
    feat = jnp.pad(feat, ((0, 0), (0, V7X_LANES - HYENA_EMB)))
    w_pos_p = jnp.pad(w_pos.astype(F32), ((0, V7X_LANES - HYENA_EMB), (0, 0)))
    deltas = jnp.abs(jnp.linspace(HYENA_MIN_DECAY, HYENA_MAX_DECAY, d, dtype=F32)).reshape(1, d)
    feat = jnp.concatenate([feat[r::n_ph] for r in range(n_ph)], axis=0)
    tl = min(512, q)
    od = HYENA_ORDER * d
    tap_spec = pl.BlockSpec((tl, od), lambda r, i: (i, r))
    taps_f, taps_b, ss = pl.pallas_call(
        functools.partial(_hy_filter_kernel, d=d, n_orders=HYENA_ORDER),
        grid=(n_ph, q // tl),
        in_specs=[
            pl.BlockSpec((tl, V7X_LANES), lambda r, i: (r * (q // tl) + i, 0)),
            _const_spec((V7X_LANES, width)),
            _const_spec((1, width)),
            _const_spec((width, width)),
            _const_spec((1, width)),
            _const_spec((1, width)),
            _const_spec((width, n_parts * d)),
            _const_spec((1, d)),
        ],
        out_specs=[tap_spec, tap_spec, pl.BlockSpec((1, od), lambda r, i: (0, 0))],
        out_shape=[jax.ShapeDtypeStruct((q, n_ph * od), BF16), jax.ShapeDtypeStruct((q, n_ph * od), BF16),
                   jax.ShapeDtypeStruct((1, od), F32)],
        compiler_params=_params(2),
        name="hy_filter",
    )(feat, w_pos_p, b_pos.reshape(1, width), w_mid, b_mid.reshape(1, width), freq.reshape(1, width),
      w_filt.astype(BF16), deltas)

    tf = min(DFT_FREQ_TILE, q)
    tp = min(DFT_PACK, q)
    tn = min(DFT_TIME_TILE, q)
    dft_fwd, dft_inv = _dft_matrices(q, tp)
    odd = 2.0 * jnp.arange(q, dtype=F32) + 1.0
    angles = jnp.stack([(2.0 * math.pi / n_full) * odd, (math.pi / n_full) * odd,
                        (math.pi / n_full) * (length - odd)])
    tw = jnp.stack([jnp.cos(angles), jnp.sin(angles)], axis=1).reshape(6, q)
    tw = jnp.broadcast_to(tw[:, :, None], (6, q, tc))
    tw_spec = pl.BlockSpec((6, tf, tc), lambda f, *_: (0, f, 0))

    n_oc = od // tc

    def tap_phase(r):
        return pl.BlockSpec((q, tc), lambda f, o, j: (0, r * n_oc + o * n_ct + j))

    spec = pl.pallas_call(
        functools.partial(_hy_spec_kernel, tp=tp, scale=1.0 / length),
        grid=(q // tf, HYENA_ORDER, n_ct),
        in_specs=[pl.BlockSpec((2 * tf, q), lambda f, o, j: (f, 0))]
        + [tap_phase(r) for r in range(n_ph)] * 2
        + [tw_spec, pl.BlockSpec((1, tc), lambda f, o, j: (0, o * n_ct + j))],
        out_specs=pl.BlockSpec((None, n_ph, 2 * tf, tc), lambda f, o, j: (o, 0, f, j)),
        out_shape=jax.ShapeDtypeStruct((HYENA_ORDER, n_ph, 2 * q, d), F32),
        compiler_params=_params(3),
        name="hy_spec",
    )(dft_fwd, *([taps_f] * n_ph), *([taps_b] * n_ph), tw, ss)

    tci = min(d, 2 * tc)
    n_ci = d // tci

    def long_conv(order, u, a):
        yspec = pl.pallas_call(
            functools.partial(_hy_fwd_kernel, tp=tp),
            grid=(q // tf, n_ct, bsz),
            in_specs=[pl.BlockSpec((2 * tf, q), lambda f, j, b: (f, 0))]
            + [pl.BlockSpec((None, None, q, tc), functools.partial(lambda r, f, j, b: (b, r, 0, j), r))
               for r in range(n_ph)]
            + [pl.BlockSpec((None, n_ph, 2 * tf, tc), lambda f, j, b: (order, 0, f, j)), tw_spec],
            out_specs=pl.BlockSpec((None, n_ph, 2 * tf, tc), lambda f, j, b: (b, 0, f, j)),
            out_shape=jax.ShapeDtypeStruct((bsz, n_ph, 2 * q, d), BF16),
            compiler_params=_params(3),
            name="hy_fwd",
        )(dft_fwd, *([u] * n_ph), spec, tw)
        row_spec = pl.BlockSpec((None, None, tn, tci), lambda n, j, b, r: (b, r, n, j))
        return pl.pallas_call(
            functools.partial(_hy_inv_kernel, rows=min(DFT_ROW_GROUP, tn)),
            grid=(q // tn, n_ci, bsz, n_ph),
            in_specs=[
                pl.BlockSpec((tn, 2 * q), lambda n, j, b, r: (n, 0)),
                pl.BlockSpec((None, None, 2 * q, tci), lambda n, j, b, r: (b, r, 0, j)),
                row_spec,
                row_spec,
                pl.BlockSpec((None, 1, tci), lambda n, j, b, r: (order, 0, j)),
            ],
            out_specs=row_spec,
            out_shape=jax.ShapeDtypeStruct((bsz, n_ph, q, d), BF16),
            compiler_params=_params(4),
            name="hy_inv",
        )(dft_inv, yspec, a, u, fbias.astype(F32).reshape(HYENA_ORDER, 1, d))

    z = long_conv(0, v, x1)
    zz = long_conv(1, z, x2)

    in_specs = [
        pl.BlockSpec((tm, d), lambda b, i: (tile_blk(b, i), 0)),
        _mod_spec(d, layer, mod_row),
        pl.BlockSpec((None, n_ph, hp, d), lambda b, i: (b, 0, i, 0)),
        _const_spec((d, d)),
        _const_spec((1, d)),
    ]
    args = [h, mods, zz, w_out, b_out.reshape(1, d)]
    aliases = {}
    if out_buf is not None:
        in_specs.append(pl.BlockSpec(memory_space=pl.ANY))
        args.append(out_buf)
        aliases = {len(args) - 1: 0}
    return pl.pallas_call(
        _hy_out_kernel,
        grid=(bsz, length // tm),
        in_specs=in_specs,
        out_specs=pl.BlockSpec((tm, d), lambda b, i: (tile_blk(b, i), 0)),
        out_shape=jax.ShapeDtypeStruct(h.shape, F32),
        input_output_aliases=aliases,
        compiler_params=_params(2),
        name="hy_out",
    )(*args)


def kernel(x, c, ctx, c_ctx, ada_w, ada_b, norm_g, ffn_w1, ffn_w3, ffn_w2, ret_w_in, ret_w_out, ret_decay,
           pool_w, pool_b, pool_scale, hy_w_in, hy_b_in, hy_w_short, hy_b_short, hy_w_pos, hy_b_pos,
           hy_w_mid, hy_b_mid, hy_freq, hy_w_filt, hy_bias, hy_w_out, hy_b_out, final_g):
    bsz, seq, d = x.shape
    t_ctx = ctx.shape[1]
    depth = ada_w.shape[0]
    n_mixers = 3
    assert bsz + 1 <= MOD_ROWS

    cond = jnp.concatenate([c, c_ctx[None, :], jnp.zeros((MOD_ROWS - bsz - 1, d), F32)], axis=0)
    mods = _ada_all(cond, ada_w, ada_b).reshape(depth, MOD_ROWS, ADA_CHUNKS, d)

    rows = _Rows(bsz, seq, t_ctx)
    ffn_stacks = (ffn_w1, ffn_w3, ffn_w2)
    ffn_weights = [tuple(w[0, 0].astype(BF16) for w in ffn_stacks)]

    def ffn(h, layer, half, ctx_live, mixer_jobs=(), final=False):
        jobs = [] if final else [(w, (layer, 1) if half == 0 else (layer + 1, 0)) for w in ffn_stacks]
        h, cast = _half_ffn(h, rows, mods, layer, half, norm_g[layer, 2 * half], ffn_weights[0], final_g,
                            ctx_live, cast_jobs=jobs + list(mixer_jobs), final=final)
        ffn_weights[0] = cast[:len(ffn_stacks)]
        return h, cast[len(ffn_stacks):]

    h = (x.reshape(bsz * seq, d), ctx.reshape(bsz * t_ctx, d))
    for layer in range(depth):
        kind = layer % n_mixers
        slot = layer // n_mixers
        last = layer == depth - 1
        ctx_out = not last
        ctx_live = ctx_out or kind == 0
        g_mix = norm_g[layer, 1]
        if kind == 0:
            h, (w_in, w_out) = ffn(h, layer, 0, ctx_live, [(ret_w_in, (slot,)), (ret_w_out, (slot,))])
            h = _retention_layer(h, rows, mods, layer, g_mix, w_in, w_out, ret_decay[slot], ctx_out)
        elif kind == 1:
            h, _ = ffn(h, layer, 0, ctx_live)
            h = _pool_layer(h, rows, GRID_W, mods, layer, g_mix, pool_w[slot], pool_b[slot], pool_scale[slot],
                            ctx_out)
        else:
            h, (w_in, w_out) = ffn(h, layer, 0, ctx_live, [(hy_w_in, (slot,)), (hy_w_out, (slot,))])
            hp = (w_in, hy_b_in[slot], hy_w_short[slot], hy_b_short[slot], hy_w_pos[slot],
                  hy_b_pos[slot], hy_w_mid[slot], hy_b_mid[slot], hy_freq[slot], hy_w_filt[slot],
                  hy_bias[slot], w_out, hy_b_out[slot])
            new_h = _hyena_layer(h, rows, "lat", None, mods, layer, g_mix, *hp)
            if ctx_out:
                new_h = _hyena_layer(h, rows, "ctx", new_h, mods, layer, g_mix, *hp)
            h = new_h
        h, _ = ffn(h, layer, 1, ctx_out, final=last)
    return h.reshape(bsz, seq, d)
```

```python
import functools
import math
from typing import NamedTuple

import numpy as np
import jax
import jax.numpy as jnp
from jax import lax
from jax.experimental import pallas as pl
from jax.experimental.pallas import tpu as pltpu

F32 = jnp.float32
BF16 = jnp.bfloat16

GRID_W = 64
ADA_CHUNKS = 9
NORM_EPS = 1e-6
RET_HEADS = 4
ROPE_BASE = 10000.0
POOL_WINDOWS = (2, 4, 8, 16)
HYENA_ORDER = 2
HYENA_EMB = 33
HYENA_BANDS = (HYENA_EMB - 1) // 2
HYENA_TARGET = 1e-2
HYENA_FAST = 0.3
HYENA_SLOW = 1.5
HYENA_MAX_DECAY = math.log(HYENA_TARGET) / HYENA_FAST
HYENA_MIN_DECAY = math.log(HYENA_TARGET) / HYENA_SLOW

V7X_LANES = 128
V7X_SUBLANES = 8
V7X_BF16_SUBLANES = 16
V7X_VMEM_LIMIT_BYTES = 56 * 1024 * 1024
MOD_ROWS = V7X_SUBLANES
ROW_TILE = 512
FFN_CAST_SLABS = 16
POOL_TILE = 512
RET_TILE = 256
RET_OUT_TILE = 1024
SCAN_CHUNK = 256
DFT_RADIX = 4
DFT_FREQ_TILE = 512
DFT_PACK = 256
DFT_ROW_GROUP = 512
DFT_TIME_TILE = 2048
DFT_COL_TILE = 256
HIGHEST = lax.Precision.HIGHEST


def _params(n_axes):
    return pltpu.CompilerParams(
        dimension_semantics=("arbitrary",) * n_axes,
        vmem_limit_bytes=V7X_VMEM_LIMIT_BYTES)


def _const_spec(shape):
    zeros = (0,) * len(shape)
    return pl.BlockSpec(shape, lambda *_: zeros, pipeline_mode=pl.Buffered(1))


def _mod_spec(d, layer, row_fn):
    return pl.BlockSpec((None, None, ADA_CHUNKS, d), lambda *idx: (layer, row_fn(*idx), 0, 0))


class _Rows(NamedTuple):
    bsz: int
    t_lat: int
    t_ctx: int

    @property
    def n_lat(self):
        return self.bsz * self.t_lat

    @property
    def n_all(self):
        return self.bsz * (self.t_lat + self.t_ctx)

    def length(self, part):
        return self.t_lat if part == "lat" else self.t_ctx

    def block(self, part, tile):
        per = self.length(part) // tile
        base = 0 if part == "lat" else self.n_lat // tile
        assert self.length(part) % tile == 0 and self.n_lat % tile == 0
        return lambda b, i: base + b * per + i


def _mm(a, b):
    return jnp.dot(a, b, preferred_element_type=F32)


def _mm_f32(a, b):
    return jnp.dot(a, b, preferred_element_type=F32, precision=HIGHEST)


def _silu(x):
    return x * jax.nn.sigmoid(x)


def _adaln(x, g, shift, scale):
    ms = jnp.mean(x * x, axis=-1, keepdims=True)
    y = x * lax.rsqrt(ms + NORM_EPS) * g
    return y * (1.0 + scale) + shift


def _ada_kernel(c_ref, w_ref, b_ref, o_ref):
    s = _silu(c_ref[...]).astype(BF16)
    o_ref[...] = _mm(s, w_ref[...].astype(BF16)) + b_ref[...]


def _ada_all(cond, ada_w, ada_b):
    depth, d, nd = ada_w.shape
    return pl.pallas_call(
        _ada_kernel,
        grid=(depth, nd // d),
        in_specs=[
            pl.BlockSpec((MOD_ROWS, d), lambda l, j: (0, 0)),
            pl.BlockSpec((None, d, d), lambda l, j: (l, 0, j)),
            pl.BlockSpec((None, 1, d), lambda l, j: (l, 0, j)),
        ],
        out_specs=pl.BlockSpec((None, MOD_ROWS, d), lambda l, j: (l, 0, j)),
        out_shape=jax.ShapeDtypeStruct((depth, MOD_ROWS, nd), F32),
        compiler_params=_params(2),
        name="ada",
    )(cond, ada_w, ada_b.reshape(depth, 1, nd))


def _ffn_kernel(*refs, base, final, n_lat, split_input, n_cast):
    n_x = 2 if split_input else 1
    x_refs, refs = refs[:n_x], refs[n_x:]
    mod_ref, g_ref, w1_ref, w3_ref, w2_ref, fg_ref = refs[:6]
    rest = refs[6:]
    if n_cast:
        n_jobs = (len(rest) - 1) // 2
        src_refs, o_ref, dst_refs = rest[:n_jobs], rest[n_jobs], rest[n_jobs + 1:]

        @pl.when(pl.program_id(0) < n_cast)
        def _():
            for src, dst in zip(src_refs, dst_refs):
                dst[...] = src[...].astype(BF16)
    else:
        (o_ref,) = rest
    if split_input:
        x = jnp.where(pl.program_id(0) >= n_lat, x_refs[1][...], x_refs[0][...])
    else:
        x = x_refs[0][...]
    u = _adaln(x, g_ref[...], mod_ref[base:base + 1, :], mod_ref[base + 1:base + 2, :]).astype(BF16)
    h1 = _mm(u, w1_ref[...])
    h3 = _mm(u, w3_ref[...])
    a = (_silu(h1) * h3).astype(BF16)
    y = x + (0.5 * mod_ref[base + 2:base + 3, :]) * _mm(a, w2_ref[...])
    if final:
        ms = jnp.mean(y * y, axis=-1, keepdims=True)
        y = y * lax.rsqrt(ms + NORM_EPS) * fg_ref[...]
    o_ref[...] = y


def _half_ffn(h, rows, mods, layer, half, g, weights, final_g, ctx_live, cast_jobs=(), final=False):
    w1, w3, w2 = weights
    d, f = w1.shape
    tm = ROW_TILE
    base = 6 * half
    per_b = rows.t_lat // tm
    n_lat = rows.n_lat // tm
    n_ctx = rows.bsz * rows.t_ctx // tm if ctx_live else 0
    assert rows.t_lat % tm == 0 and (rows.bsz * rows.t_ctx) % tm == 0
    split_input = isinstance(h, tuple)

    if split_input:
        x_specs = [pl.BlockSpec((tm, d), lambda s: (jnp.minimum(s, n_lat - 1), 0)),
                   pl.BlockSpec((tm, d), lambda s: (jnp.maximum(s - n_lat, 0), 0))]
        x_args = list(h)
    else:
        x_specs = [pl.BlockSpec((tm, d), lambda s: (s, 0))]
        x_args = [h]
    in_specs = x_specs + [
        _mod_spec(d, layer, lambda s: jnp.where(s < n_lat, s // per_b, rows.bsz)),
        _const_spec((1, d)),
        _const_spec((d, f)),
        _const_spec((d, f)),
        _const_spec((f, d)),
        _const_spec((1, d)),
    ]
    args = x_args + [mods, g.reshape(1, d), w1, w3, w2, final_g.reshape(1, d)]
    out_specs = [pl.BlockSpec((tm, d), lambda s: (s, 0))]
    out_shape = [jax.ShapeDtypeStruct((rows.n_lat if final else rows.n_all, d), F32)]
    n_cast = min(FFN_CAST_SLABS, n_lat) if cast_jobs else 0
    for w, lead in cast_jobs:
        r, c = w.shape[-2:]
        assert r % (V7X_BF16_SUBLANES * n_cast) == 0 and len(lead) == w.ndim - 2
        in_specs.append(pl.BlockSpec((None,) * len(lead) + (r // n_cast, c),
                                     functools.partial(lambda lead, s: lead + (jnp.minimum(s, n_cast - 1), 0), lead)))
        out_specs.append(pl.BlockSpec((r // n_cast, c), lambda s: (jnp.minimum(s, n_cast - 1), 0)))
        out_shape.append(jax.ShapeDtypeStruct((r, c), BF16))
        args.append(w)
    outs = pl.pallas_call(
        functools.partial(_ffn_kernel, base=base, final=final, n_lat=n_lat, split_input=split_input,
                          n_cast=n_cast),
        grid=(n_lat + n_ctx,),
        in_specs=in_specs,
        out_specs=out_specs,
        out_shape=out_shape,
        compiler_params=_params(1),
        name="half_ffn",
    )(*args)
    return outs[0], tuple(outs[1:])


def _ret_proj_kernel(xc_ref, xl_ref, mod_ref, g_ref, w_ref, cos_ref, sin_ref, q_ref, k_ref, v_ref, gate_ref,
                     *, n_heads, dk, dv, lat_tiles):
    x = jnp.where(pl.program_id(1) < lat_tiles, xl_ref[...], xc_ref[...])
    u = _adaln(x, g_ref[...], mod_ref[3:4, :], mod_ref[4:5, :]).astype(BF16)
    p = _mm(u, w_ref[...])
    cos = cos_ref[...]
    sin = sin_ref[...]
    half = dk // 2
    qk = n_heads * dk
    k_scale = dk ** -0.5

    def rot(a):
        a1, a2 = a[:, :half], a[:, half:]
        return jnp.concatenate([a1 * cos - a2 * sin, a2 * cos + a1 * sin], axis=-1)

    for h in range(n_heads):
        q_ref[:, h * dk:(h + 1) * dk] = rot(p[:, h * dk:(h + 1) * dk]).astype(BF16)
        k_ref[:, h * dk:(h + 1) * dk] = (rot(p[:, qk + h * dk:qk + (h + 1) * dk]) * k_scale).astype(BF16)
    v_ref[...] = p[:, 2 * qk:2 * qk + n_heads * dv].astype(BF16)
    gate_ref[...] = p[:, 2 * qk + n_heads * dv:].astype(BF16)


def _ret_scan_kernel(q_ref, k_ref, v_ref, dmat_ref, qdec_ref, kdec_ref, cdec_ref, o_ref,
                     state_ref, fwd_ref, *, n_heads, dk, dv, n_chunks, chunk_of_step):
    s = pl.program_id(1)
    cs = q_ref.shape[0]

    @pl.when((s == 0) | (s == n_chunks))
    def _():
        state_ref[...] = jnp.zeros_like(state_ref)

    rows = pl.ds(pl.multiple_of(chunk_of_step(s) * cs, cs), cs)
    outs = []
    for h in range(n_heads):
        q = q_ref[:, h * dk:(h + 1) * dk]
        k = k_ref[:, h * dk:(h + 1) * dk]
        v = v_ref[:, h * dv:(h + 1) * dv]
        scores = lax.dot_general(q, k, (((1,), (1,)), ((), ())), preferred_element_type=F32) * dmat_ref[h]
        state = state_ref[h]
        outs.append(_mm(scores.astype(BF16), v) + _mm(q, state.astype(BF16)) * qdec_ref[h])
        kd = (k.astype(F32) * kdec_ref[h]).astype(BF16)
        state_ref[h] = state * cdec_ref[h] + lax.dot_general(
            kd, v, (((0,), (0,)), ((), ())), preferred_element_type=F32)

    @pl.when(s < n_chunks)
    def _():
        for h in range(n_heads):
            fwd_ref[rows, h * dv:(h + 1) * dv] = outs[h].astype(fwd_ref.dtype)

    @pl.when(s >= n_chunks)
    def _():
        for h in range(n_heads):
            o = outs[h] + fwd_ref[rows, h * dv:(h + 1) * dv].astype(F32)
            mu = jnp.mean(o, axis=-1, keepdims=True)
            oc = o - mu
            var = jnp.mean(oc * oc, axis=-1, keepdims=True)
            o_ref[:, h * dv:(h + 1) * dv] = (oc * lax.rsqrt(var + NORM_EPS)).astype(o_ref.dtype)


def _ret_out_kernel(x_ref, mod_ref, y_ref, gate_ref, w_ref, o_ref):
    z =(_silu(gate_ref[...].astype(F32)) * y_ref[...].astype(F32)).astype(BF16)
    o_ref[...] = x_ref[...] + mod_ref[5:6, :] * _mm(z, w_ref[...])


def _retention_layer(h, rows, mods, layer, g, w_in, w_out, decay, ctx_out):
    bsz, t_lat, t_ctx = rows
    d = h.shape[1]
    t_all = t_ctx + t_lat
    n_heads = RET_HEADS
    dk = d // n_heads
    dv = 2 * d // n_heads
    qk = n_heads * dk
    vw = n_heads * dv
    tm = RET_TILE
    assert t_ctx % tm == 0 and t_lat % tm == 0
    lat_tiles = t_lat // tm
    ctx_row = bsz
    lat_blk, ctx_blk = rows.block("lat", tm), rows.block("ctx", tm)

    pos = jnp.concatenate([jnp.arange(t_lat, dtype=F32) + float(t_ctx), jnp.arange(t_ctx, dtype=F32)])
    inv = 1.0 / (ROPE_BASE ** jnp.linspace(0.0, 1.0, dk // 2, dtype=F32))
    ang = pos[:, None] * inv[None, :]

    q, k, v, gate = pl.pallas_call(
        functools.partial(_ret_proj_kernel, n_heads=n_heads, dk=dk, dv=dv, lat_tiles=lat_tiles),
        grid=(bsz, t_all // tm),
        in_specs=[
            pl.BlockSpec((tm, d), lambda b, i: (ctx_blk(b, jnp.maximum(i - lat_tiles, 0)), 0)),
            pl.BlockSpec((tm, d), lambda b, i: (lat_blk(b, jnp.minimum(i, lat_tiles - 1)), 0)),
            _mod_spec(d, layer, lambda b, i: jnp.where(i < lat_tiles, b, ctx_row)),
            _const_spec((1, d)),
            _const_spec((d, 2 * qk + 2 * vw)),
            pl.BlockSpec((tm, dk // 2), lambda b, i: (i, 0)),
            pl.BlockSpec((tm, dk // 2), lambda b, i: (i, 0)),
        ],
        out_specs=[
            pl.BlockSpec((None, tm, qk), lambda b, i: (b, i, 0)),
            pl.BlockSpec((None, tm, qk), lambda b, i: (b, i, 0)),
            pl.BlockSpec((None, tm, vw), lambda b, i: (b, i, 0)),
            pl.BlockSpec((None, tm, vw), lambda b, i: (b, i, 0)),
        ],
        out_shape=[
            jax.ShapeDtypeStruct((bsz, t_all, qk), BF16),
            jax.ShapeDtypeStruct((bsz, t_all, qk), BF16),
            jax.ShapeDtypeStruct((bsz, t_all, vw), BF16),
            jax.ShapeDtypeStruct((bsz, t_all, vw), BF16),
        ],
        compiler_params=_params(2),
        name="ret_proj",
    )(h, h, mods, g.reshape(1, d), w_in, jnp.cos(ang), jnp.sin(ang))

    cs = SCAN_CHUNK
    assert t_ctx % cs == 0 and t_lat % cs == 0
    n_chunks = t_all // cs
    ctx_chunks = t_ctx // cs
    lat_chunks = t_lat // cs
    lg = jax.nn.log_sigmoid(decay.astype(F32))[:, :, None]
    n = jnp.arange(cs, dtype=F32)
    rel = n[:, None] - n[None, :]
    dm_f = jnp.where(rel >= 0, jnp.exp(lg[0][:, :, None] * jnp.maximum(rel, 0.0)), 0.0)
    dm_b = jnp.where(rel < 0, jnp.exp(lg[1][:, :, None] * jnp.maximum(-rel, 0.0)), 0.0)
    dmat = jnp.stack([dm_f, dm_b])
    qd = jnp.stack([jnp.exp(lg[0] * (n + 1.0)), jnp.exp(lg[1] * (cs - n))])
    kd = jnp.stack([jnp.exp(lg[0] * (cs - 1.0 - n)), jnp.exp(lg[1] * n)])
    cd = jnp.exp(lg * cs)
    qdec = jnp.broadcast_to(qd[..., None], (2, n_heads, cs, dv))
    kdec = jnp.broadcast_to(kd[..., None], (2, n_heads, cs, dk))
    cdec = jnp.broadcast_to(cd[..., None], (2, n_heads, 1, dv))

    def fwd_chunk(j):
        return jnp.where(j < ctx_chunks, lat_chunks + j, j - ctx_chunks)

    def bwd_chunk(j):
        return n_chunks - 1 - j

    def chunk(s):
        return jnp.where(s < n_chunks, fwd_chunk(s), bwd_chunk(jnp.maximum(s - n_chunks, 0)))

    def out_chunk(s):
        return bwd_chunk(jnp.maximum(s - n_chunks, 0))

    def table_spec(*shape):
        return pl.BlockSpec((None,) + shape, lambda b, s: (s // n_chunks, 0, 0, 0))

    y = pl.pallas_call(
        functools.partial(_ret_scan_kernel, n_heads=n_heads, dk=dk, dv=dv, n_chunks=n_chunks,
                          chunk_of_step=chunk),
        grid=(bsz, 2 * n_chunks),
        in_specs=[
            pl.BlockSpec((None, cs, qk), lambda b, s: (b, chunk(s), 0)),
            pl.BlockSpec((None, cs, qk), lambda b, s: (b, chunk(s), 0)),
            pl.BlockSpec((None, cs, vw), lambda b, s: (b, chunk(s), 0)),
            table_spec(n_heads, cs, cs),
            table_spec(n_heads, cs, dv),
            table_spec(n_heads, cs, dk),
            table_spec(n_heads, 1, dv),
        ],
        out_specs=pl.BlockSpec((None, cs, vw), lambda b, s: (b, out_chunk(s), 0)),
        out_shape=jax.ShapeDtypeStruct((bsz, t_all, vw), BF16),
        scratch_shapes=[pltpu.VMEM((n_heads, dk, dv), F32), pltpu.VMEM((t_all, vw), BF16)],
        compiler_params=_params(2),
        name="ret_scan",
    )(q, k, v, dmat, qdec, kdec, cdec)

    def readout(buf, part, tile, first_tile, row_fn):
        blk = rows.block(part, tile)
        return pl.pallas_call(
            _ret_out_kernel,
            grid=(bsz, rows.length(part) // tile),
            in_specs=[
                pl.BlockSpec((tile, d), lambda b, i: (blk(b, i), 0)),
                _mod_spec(d, layer, row_fn),
                pl.BlockSpec((None, tile, vw), lambda b, i: (b, i + first_tile, 0)),
                pl.BlockSpec((None, tile, vw), lambda b, i: (b, i + first_tile, 0)),
                _const_spec((vw, d)),
            ],
            out_specs=pl.BlockSpec((tile, d), lambda b, i: (blk(b, i), 0)),
            out_shape=jax.ShapeDtypeStruct(h.shape, F32),
            input_output_aliases={0: 0},
            compiler_params=_params(2),
            name="ret_out",
        )(buf, mods, y, gate, w_out)

    h = readout(h, "lat", min(RET_OUT_TILE, t_lat), 0, lambda b, i: b)
    if ctx_out:
        h = readout(h, "ctx", tm, lat_tiles, lambda b, i: ctx_row)
    return h


def _pool_tables(seg_len):
    assert POOL_TILE % seg_len == 0
    r = np.arange(POOL_TILE)
    seg, pos = r // seg_len, r % seg_len
    mats, invs = [], []
    for win in POOL_WINDOWS:
        lo = np.clip(pos - win // 2, 0, seg_len)
        hi = np.clip(pos - win // 2 + win, 0, seg_len)
        col_seg, col_pos = seg[None, :], pos[None, :]
        member = (col_seg == seg[:, None]) & (col_pos >= lo[:, None]) & (col_pos < hi[:, None])
        mats.append(member.astype(np.float32))
        invs.append((1.0 / (hi - lo)).astype(np.float32))
    return np.stack(mats), np.stack(invs)


def _pool_kernel(x_ref, mod_ref, g_ref, s_ref, inv_ref, w_ref, b_ref, sc_ref, o_ref, *, n_groups, dg):
    x = x_ref[...]
    u = _adaln(x, g_ref[...], mod_ref[3:4, :], mod_ref[4:5, :])
    gate = mod_ref[5:6, :]
    for gi in range(n_groups):
        sl = slice(gi * dg, (gi + 1) * dg)
        ug = u[:, sl]
        hi = ug.astype(BF16)
        lo = (ug - hi.astype(F32)).astype(BF16)
        win_sum = _mm(s_ref[gi], hi) + _mm(s_ref[gi], lo)
        dlt = win_sum * inv_ref[gi] - ug
        yg = _mm(dlt.astype(BF16), w_ref[gi])
        o_ref[:, sl] = x[:, sl] + gate[:, sl] * ((yg + b_ref[:, sl]) * sc_ref[:, sl])


def _pool_layer(h, rows, lat_seg, mods, layer, g, w_grp, b_grp, scale, ctx_out):
    bsz, t_lat, t_ctx = rows
    d = h.shape[1]
    n_groups = len(POOL_WINDOWS)
    dg = d // n_groups
    tabs = [_pool_tables(lat_seg), _pool_tables(t_ctx)]
    s_tab = jnp.asarray(np.stack([m for m, _ in tabs]), BF16)
    inv_tab = jnp.asarray(np.stack([np.broadcast_to(v[:, :, None], (n_groups, POOL_TILE, dg)) for _, v in tabs]), F32)
    assert t_lat % POOL_TILE == 0 and (bsz * t_ctx) % POOL_TILE == 0
    per_b = t_lat // POOL_TILE
    n_lat = rows.n_lat // POOL_TILE
    n_ctx = bsz * t_ctx // POOL_TILE if ctx_out else 0

    def table_spec(*shape):
        return pl.BlockSpec((None,) + shape, lambda s: (jnp.where(s < n_lat, 0, 1), 0, 0, 0))

    return pl.pallas_call(
        functools.partial(_pool_kernel, n_groups=n_groups, dg=dg),
        grid=(n_lat + n_ctx,),
        in_specs=[
            pl.BlockSpec((POOL_TILE, d), lambda s: (s, 0)),
            _mod_spec(d, layer, lambda s: jnp.where(s < n_lat, s // per_b, bsz)),
            _const_spec((1, d)),
            table_spec(n_groups, POOL_TILE, POOL_TILE),
            table_spec(n_groups, POOL_TILE, dg),
            _const_spec((n_groups, dg, dg)),
            _const_spec((1, d)),
            _const_spec((1, d)),
        ],
        out_specs=pl.BlockSpec((POOL_TILE, d), lambda s: (s, 0)),
        out_shape=jax.ShapeDtypeStruct(h.shape, F32),
        compiler_params=_params(1),
        name="pool",
    )(h, mods, g.reshape(1, d), s_tab, inv_tab, w_grp.astype(BF16), b_grp.reshape(1, d), scale.reshape(1, d))


def _dft_tables(n_in):
    mod = 4 * n_in
    sub_n = min(V7X_LANES, n_in)
    idx = jnp.arange(n_in, dtype=jnp.int32)[None, :]
    blk = jnp.arange(n_in // sub_n, dtype=jnp.int32)[:, None]
    sub = jnp.arange(sub_n, dtype=jnp.int32)[:, None]
    theta = 2.0 * math.pi / mod

    def cs(p):
        ph = (p % mod).astype(F32) * theta
        return jnp.cos(ph), jnp.sin(ph)

    fwd = cs(2 * sub_n * blk * idx) + cs((2 * sub + 1) * idx)
    inv = cs(sub_n * blk * (2 * idx + 1)) + cs(sub * (2 * idx + 1))
    return fwd, inv


def _dft_fwd_gen_kernel(pc_ref, ps_ref, qc_ref, qs_ref, o_ref, *, tf):
    ft = pl.program_id(0)
    sub_n = qc_ref.shape[0]
    blocks = tf // sub_n
    qc = qc_ref[...]
    qs = qs_ref[...]
    for jb in range(blocks):
        kb = ft * blocks + jb
        pc = pc_ref[pl.ds(kb, 1), :]
        ps = ps_ref[pl.ds(kb, 1), :]
        o_ref[jb * sub_n:(jb + 1) * sub_n, :] = (pc * qc - ps * qs).astype(BF16)
        o_ref[tf + jb * sub_n:tf + (jb + 1) * sub_n, :] = (ps * qc + pc * qs).astype(BF16)


def _dft_inv_gen_kernel(pc_ref, ps_ref, qc_ref, qs_ref, o_ref, *, tf, n_in):
    mb = pl.program_id(0)
    pc = pc_ref[pl.ds(mb, 1), :]
    ps = ps_ref[pl.ds(mb, 1), :]
    qc = qc_ref[...]
    qs = qs_ref[...]
    cos_all = pc * qc - ps * qs
    sin_all = ps * qc + pc * qs
    for ft in range(n_in // tf):
        o_ref[:, 2 * ft * tf:(2 * ft + 1) * tf] = cos_all[:, ft * tf:(ft + 1) * tf].astype(BF16)
        o_ref[:, (2 * ft + 1) * tf:(2 * ft + 2) * tf] = sin_all[:, ft * tf:(ft + 1) * tf].astype(BF16)


def _dft_matrices(n_in, tf):
    fwd_tabs, inv_tabs = _dft_tables(n_in)
    sub_n = min(V7X_LANES, n_in)
    nb = n_in // sub_n
    tabs = [_const_spec((nb, n_in)), _const_spec((nb, n_in)),
            _const_spec((sub_n, n_in)), _const_spec((sub_n, n_in))]
    fwd = pl.pallas_call(
        functools.partial(_dft_fwd_gen_kernel, tf=tf),
        grid=(n_in // tf,),
        in_specs=tabs,
        out_specs=pl.BlockSpec((2 * tf, n_in), lambda i: (i, 0)),
        out_shape=jax.ShapeDtypeStruct((2 * n_in, n_in), BF16),
        compiler_params=_params(1),
        name="dft_fwd_gen",
    )(*fwd_tabs)
    inv = pl.pallas_call(
        functools.partial(_dft_inv_gen_kernel, tf=tf, n_in=n_in),
        grid=(nb,),
        in_specs=tabs,
        out_specs=pl.BlockSpec((sub_n, 2 * n_in), lambda i: (i, 0)),
        out_shape=jax.ShapeDtypeStruct((n_in, 2 * n_in), BF16),
        compiler_params=_params(1),
        name="dft_inv_gen",
    )(*inv_tabs)
    return fwd, inv


def _hy_proj_kernel(x_ref, xp_ref, xn_ref, mod_ref, g_ref, w_ref, b_ref, ws_ref, bs_ref,
                    v_ref, x1_ref, x2_ref):
    i = pl.program_id(1)
    tm, d = x_ref.shape
    n_ph = v_ref.shape[0]
    hp = tm // n_ph
    g = g_ref[...]
    shift, scale = mod_ref[3:4, :], mod_ref[4:5, :]
    u = _adaln(x_ref[...], g, shift, scale).astype(BF16)
    u = jnp.swapaxes(u.reshape(hp, n_ph, d), 0, 1).reshape(tm, d)
    halo = jnp.concatenate([xp_ref[...], xn_ref[...]], axis=0)
    uh = _adaln(halo, g, shift, scale).astype(BF16)
    p = _mm(jnp.concatenate([u, uh], axis=0), w_ref[...]) + b_ref[...]
    ph = [p[r * hp:(r + 1) * hp] for r in range(n_ph)]
    n_halo = xp_ref.shape[0]
    before = jnp.where(i == 0, 0.0, p[tm + n_halo - 1:tm + n_halo])
    after = jnp.where(i == pl.num_programs(1) - 1, 0.0, p[tm + n_halo:tm + n_halo + 1])
    row = lax.broadcasted_iota(jnp.int32, ph[0].shape, 0)
    last_prev = jnp.where(row == 0, before, pltpu.roll(ph[-1], 1, axis=0))
    first_next = jnp.where(row == hp - 1, after, pltpu.roll(ph[0], hp - 1, axis=0))
    w0, w1, w2, bs = ws_ref[0:1, :], ws_ref[1:2, :], ws_ref[2:3, :], bs_ref[...]
    for r in range(n_ph):
        prev = ph[r - 1] if r > 0 else last_prev
        nxt = ph[r + 1] if r < n_ph - 1 else first_next
        conv = prev * w0 + ph[r] * w1 + nxt * w2 + bs
        v_ref[r] = conv[:, :d].astype(BF16)
        x1_ref[r] = conv[:, d:2 * d].astype(BF16)
        x2_ref[r] = conv[:, 2 * d:].astype(BF16)


def _hy_filter_kernel(feat_ref, wp_ref, bp_ref, wm_ref, bm_ref, fr_ref, wf_ref, dl_ref,
                      hf_ref, hb_ref, ss_ref, *, d, n_orders):
    start = (pl.program_id(0) == 0) & (pl.program_id(1) == 0)
    feat = feat_ref[...]
    fr = fr_ref[...]
    hdn = jnp.sin(fr * (_mm_f32(feat, wp_ref[...]) + bp_ref[...]))
    hdn = jnp.sin(fr * (_mm_f32(hdn, wm_ref[...]) + bm_ref[...]))
    h = _mm(hdn.astype(BF16), wf_ref[...])
    decay = jnp.exp(-feat[:, 0:1] * dl_ref[...])
    row = lax.broadcasted_iota(jnp.int32, decay.shape, 0)
    first = (row == 0) & start

    @pl.when(start)
    def _():
        ss_ref[...] = jnp.zeros_like(ss_ref)

    for o in range(n_orders):
        sl = slice(o * d, (o + 1) * d)
        hf = h[:, 2 * o * d:(2 * o + 1) * d] * decay
        hb = jnp.where(first, 0.0, h[:, (2 * o + 1) * d:(2 * o + 2) * d] * decay)
        hf_ref[:, sl] = hf.astype(BF16)
        hb_ref[:, sl] = hb.astype(BF16)
        ss_ref[:, sl] += jnp.sum(hf * hf + hb * hb, axis=0, keepdims=True)


def _twiddle(a, c, s, conj=False):
    a_re, a_s = a
    if conj:
        return a_re * c + a_s * s, a_s * c - a_re * s
    return a_re * c - a_s * s, a_s * c + a_re * s


def _add(a, b):
    return a[0] + b[0], a[1] + b[1]


def _sub(a, b):
    return a[0] - b[0], a[1] - b[1]


def _dit_blocks(dft, phase_refs, tw, tp):
    ph = []
    for ref in phase_refs:
        t = _mm(dft, ref[...])
        ph.append((t[0:tp], t[tp:]))
    c1, s1, c2, s2, c3, s3 = tw
    t2, t3 = _twiddle(ph[2], c1, s1), _twiddle(ph[3], c1, s1)
    e_a, e_b = _add(ph[0], t2), _sub(ph[0], t2)
    o_a, o_b = _add(ph[1], t3), _sub(ph[1], t3)
    ta, tb = _twiddle(o_a, c2, s2), _twiddle(o_b, c3, s3, conj=True)
    return [_add(e_a, ta), _sub(e_a, ta), _add(e_b, tb), _sub(e_b, tb)]


def _bin_group(f_ref, tw_ref, g, tp):
    re, sn = slice(2 * g * tp, (2 * g + 1) * tp), slice((2 * g + 1) * tp, (2 * g + 2) * tp)
    dft = f_ref[2 * g * tp:(2 * g + 2) * tp, :]
    tw = [tw_ref[i, g * tp:(g + 1) * tp, :] for i in range(6)]
    return re, sn, dft, tw


def _hy_spec_kernel(f_ref, f0, f1, f2, f3, b0, b1, b2, b3, tw_ref, ss_ref, o_ref, *, tp, scale):
    wgt = scale * lax.rsqrt(ss_ref[...] + NORM_EPS)
    for g in range(tw_ref.shape[1] // tp):
        re, sn, dft, tw = _bin_group(f_ref, tw_ref, g, tp)
        fwd = _dit_blocks(dft, (f0, f1, f2, f3), tw, tp)
        bwd = _dit_blocks(dft, (b0, b1, b2, b3), tw, tp)
        for blk in range(4):
            o_ref[blk, re, :] = (fwd[blk][0] + bwd[blk][0]) * wgt
            o_ref[blk, sn, :] = (fwd[blk][1] - bwd[blk][1]) * wgt


def _hy_fwd_kernel(f_ref, u0, u1, u2, u3, h_ref, tw_ref, y_ref, *, tp):
    for g in range(tw_ref.shape[1] // tp):
        re, sn, dft, tw = _bin_group(f_ref, tw_ref, g, tp)
        c1, s1, c2, s2, c3, s3 = tw
        x = _dit_blocks(dft, (u0, u1, u2, u3), tw, tp)
        y = []
        for blk in range(4):
            h_re, h_s = h_ref[blk, re, :], h_ref[blk, sn, :]
            x_re, x_s = x[blk]
            y.append((x_re * h_re - x_s * h_s, x_re * h_s + x_s * h_re))
        e_a, o_a = _add(y[0], y[1]), _twiddle(_sub(y[0], y[1]), c2, s2, conj=True)
        e_b, o_b = _add(y[2], y[3]), _twiddle(_sub(y[2], y[3]), c3, s3)
        out = [_add(e_a, e_b), _add(o_a, o_b),
               _twiddle(_sub(e_a, e_b), c1, s1, conj=True), _twiddle(_sub(o_a, o_b), c1, s1, conj=True)]
        for r in range(4):
            y_ref[r, re, :] = out[r][0].astype(BF16)
            y_ref[r, sn, :] = out[r][1].astype(BF16)


def _hy_inv_kernel(f_ref, y_ref, a_ref, u_ref, fb_ref, o_ref, *, rows):
    for r in range(f_ref.shape[0] // rows):
        sl = slice(r * rows, (r + 1) * rows)
        conv = _mm(f_ref[sl, :], y_ref[...])
        z = a_ref[sl, :].astype(F32) * (conv + u_ref[sl, :].astype(F32) * fb_ref[...])
        o_ref[sl, :] = z.astype(o_ref.dtype)


def _hy_out_kernel(x_ref, mod_ref, z_ref, w_ref, b_ref, o_ref):
    n_ph, hp, d = z_ref.shape
    z = jnp.stack([z_ref[r] for r in range(n_ph)], axis=1).reshape(n_ph * hp, d)
    o_ref[...] = x_ref[...] + mod_ref[5:6, :] * (_mm(z, w_ref[...]) + b_ref[...])


def _hyena_layer(h, rows, part, mods, layer, g, w_in, b_in, w_short, b_short, w_pos, b_pos, w_mid,
                 b_mid, freq, w_filt, fbias, w_out, b_out):
    bsz = rows.bsz
    length = rows.length(part)
    d = h.shape[1]
    mod_row = (lambda b, i: b) if part == "lat" else (lambda b, i: bsz)
    n_full = 2 * length
    n_ph = DFT_RADIX
    q = length // n_ph
    tm = min(ROW_TILE, length)
    hp = tm // n_ph
    tc = DFT_COL_TILE if q >= DFT_FREQ_TILE else d
    n_ct = d // tc
    width = w_mid.shape[0]
    n_parts = 2 * HYENA_ORDER

    halo = V7X_SUBLANES
    ph_spec = pl.BlockSpec((None, n_ph, hp, d), lambda b, i: (b, 0, i, 0))
    tile_blk, halo_blk = rows.block(part, tm), rows.block(part, halo)
    v, x1, x2 = pl.pallas_call(
        _hy_proj_kernel,
        grid=(bsz, length // tm),
        in_specs=[
            pl.BlockSpec((tm, d), lambda b, i: (tile_blk(b, i), 0)),
            pl.BlockSpec((halo, d), lambda b, i: (halo_blk(b, jnp.maximum(i * (tm // halo) - 1, 0)), 0)),
            pl.BlockSpec((halo, d),
                         lambda b, i: (halo_blk(b, jnp.minimum((i + 1) * (tm // halo), length // halo - 1)), 0)),
            _mod_spec(d, layer, mod_row),
            _const_spec((1, d)),
            _const_spec((d, 3 * d)),
            _const_spec((1, 3 * d)),
            _const_spec((3, 3 * d)),
            _const_spec((1, 3 * d)),
        ],
        out_specs=[ph_spec] * 3,
        out_shape=[jax.ShapeDtypeStruct((bsz, n_ph, q, d), BF16)] * 3,
        compiler_params=_params(2),
        name="hy_proj",
    )(h, h, h, mods, g.reshape(1, d), w_in, b_in.reshape(1, 3 * d),
      w_short, b_short.reshape(1, 3 * d))

    t = jnp.linspace(0.0, 1.0, length, dtype=F32)[:, None]
    bands = jnp.linspace(1e-4, HYENA_BANDS - 1, HYENA_BANDS, dtype=F32)
    ang = (2.0 * math.pi / length) * jnp.arange(length, dtype=F32)[:, None] * bands[None, :]
    feat = jnp.concatenate([t, jnp.cos(ang), -jnp.sin(ang)], axis=-1)
    feat = jnp.pad(feat, ((0, 0), (0, V7X_LANES - HYENA_EMB)))
    w_pos_p = jnp.pad(w_pos.astype(F32), ((0, V7X_LANES - HYENA_EMB), (0, 0)))
    deltas = jnp.abs(jnp.linspace(HYENA_MIN_DECAY, HYENA_MAX_DECAY, d, dtype=F32)).reshape(1, d)
    feat = jnp.concatenate([feat[r::n_ph] for r in range(n_ph)], axis=0)
    tl = min(ROW_TILE, q)
    od = HYENA_ORDER * d
    tap_spec = pl.BlockSpec((tl, od), lambda r, i: (i, r))
    taps_f, taps_b, ss = pl.pallas_call(
        functools.partial(_hy_filter_kernel, d=d, n_orders=HYENA_ORDER),
        grid=(n_ph, q // tl),
        in_specs=[
            pl.BlockSpec((tl, V7X_LANES), lambda r, i: (r * (q // tl) + i, 0)),
            _const_spec((V7X_LANES, width)),
            _const_spec((1, width)),
            _const_spec((width, width)),
            _const_spec((1, width)),
            _const_spec((1, width)),
            _const_spec((width, n_parts * d)),
            _const_spec((1, d)),
        ],
        out_specs=[tap_spec, tap_spec, pl.BlockSpec((1, od), lambda r, i: (0, 0))],
        out_shape=[jax.ShapeDtypeStruct((q, n_ph * od), BF16), jax.ShapeDtypeStruct((q, n_ph * od), BF16),
                   jax.ShapeDtypeStruct((1, od), F32)],
        compiler_params=_params(2),
        name="hy_filter",
    )(feat, w_pos_p, b_pos.reshape(1, width), w_mid, b_mid.reshape(1, width), freq.reshape(1, width),
      w_filt.astype(BF16), deltas)

    tf = min(DFT_FREQ_TILE, q)
    tp = min(DFT_PACK, q)
    tn = min(DFT_TIME_TILE, q)
    dft_fwd, dft_inv = _dft_matrices(q, tp)
    odd = 2.0 * jnp.arange(q, dtype=F32) + 1.0
    angles = jnp.stack([(2.0 * math.pi / n_full) * odd, (math.pi / n_full) * odd,
                        (math.pi / n_full) * (length - odd)])
    tw = jnp.stack([jnp.cos(angles), jnp.sin(angles)], axis=1).reshape(6, q)
    tw = jnp.broadcast_to(tw[:, :, None], (6, q, tc))
    tw_spec = pl.BlockSpec((6, tf, tc), lambda f, *_: (0, f, 0))

    n_oc = od // tc

    def tap_phase(r):
        return pl.BlockSpec((q, tc), lambda f, o, j: (0, r * n_oc + o * n_ct + j))

    spec = pl.pallas_call(
        functools.partial(_hy_spec_kernel, tp=tp, scale=1.0 / length),
        grid=(q // tf, HYENA_ORDER, n_ct),
        in_specs=[pl.BlockSpec((2 * tf, q), lambda f, o, j: (f, 0))]
        + [tap_phase(r) for r in range(n_ph)] * 2
        + [tw_spec, pl.BlockSpec((1, tc), lambda f, o, j: (0, o * n_ct + j))],
        out_specs=pl.BlockSpec((None, n_ph, 2 * tf, tc), lambda f, o, j: (o, 0, f, j)),
        out_shape=jax.ShapeDtypeStruct((HYENA_ORDER, n_ph, 2 * q, d), F32),
        compiler_params=_params(3),
        name="hy_spec",
    )(dft_fwd, *([taps_f] * n_ph), *([taps_b] * n_ph), tw, ss)

    tci = min(d, 2 * tc)
    n_ci = d // tci

    def long_conv(order, u, a):
        yspec = pl.pallas_call(
            functools.partial(_hy_fwd_kernel, tp=tp),
            grid=(q // tf, n_ct, bsz),
            in_specs=[pl.BlockSpec((2 * tf, q), lambda f, j, b: (f, 0))]
            + [pl.BlockSpec((None, None, q, tc), functools.partial(lambda r, f, j, b: (b, r, 0, j), r))
               for r in range(n_ph)]
            + [pl.BlockSpec((None, n_ph, 2 * tf, tc), lambda f, j, b: (order, 0, f, j)), tw_spec],
            out_specs=pl.BlockSpec((None, n_ph, 2 * tf, tc), lambda f, j, b: (b, 0, f, j)),
            out_shape=jax.ShapeDtypeStruct((bsz, n_ph, 2 * q, d), BF16),
            compiler_params=_params(3),
            name="hy_fwd",
        )(dft_fwd, *([u] * n_ph), spec, tw)
        row_spec = pl.BlockSpec((None, None, tn, tci), lambda n, j, b, r: (b, r, n, j))
        return pl.pallas_call(
            functools.partial(_hy_inv_kernel, rows=min(DFT_ROW_GROUP, tn)),
            grid=(q // tn, n_ci, bsz, n_ph),
            in_specs=[
                pl.BlockSpec((tn, 2 * q), lambda n, j, b, r: (n, 0)),
                pl.BlockSpec((None, None, 2 * q, tci), lambda n, j, b, r: (b, r, 0, j)),
                row_spec,
                row_spec,
                pl.BlockSpec((None, 1, tci), lambda n, j, b, r: (order, 0, j)),
            ],
            out_specs=row_spec,
            out_shape=jax.ShapeDtypeStruct((bsz, n_ph, q, d), BF16),
            compiler_params=_params(4),
            name="hy_inv",
        )(dft_inv, yspec, a, u, fbias.astype(F32).reshape(HYENA_ORDER, 1, d))

    z = long_conv(0, v, x1)
    zz = long_conv(1, z, x2)

    return pl.pallas_call(
        _hy_out_kernel,
        grid=(bsz, length // tm),
        in_specs=[
            pl.BlockSpec((tm, d), lambda b, i: (tile_blk(b, i), 0)),
            _mod_spec(d, layer, mod_row),
            pl.BlockSpec((None, n_ph, hp, d), lambda b, i: (b, 0, i, 0)),
            _const_spec((d, d)),
            _const_spec((1, d)),
        ],
        out_specs=pl.BlockSpec((tm, d), lambda b, i: (tile_blk(b, i), 0)),
        out_shape=jax.ShapeDtypeStruct(h.shape, F32),
        input_output_aliases={0: 0},
        compiler_params=_params(2),
        name="hy_out",
    )(h, mods, zz, w_out, b_out.reshape(1, d))


def kernel(x, c, ctx, c_ctx, ada_w, ada_b, norm_g, ffn_w1, ffn_w3, ffn_w2, ret_w_in, ret_w_out, ret_decay,
           pool_w, pool_b, pool_scale, hy_w_in, hy_b_in, hy_w_short, hy_b_short, hy_w_pos, hy_b_pos,
           hy_w_mid, hy_b_mid, hy_freq, hy_w_filt, hy_bias, hy_w_out, hy_b_out, final_g):
    bsz, seq, d = x.shape
    t_ctx = ctx.shape[1]
    depth = ada_w.shape[0]
    n_mixers = 3
    assert bsz + 1 <= MOD_ROWS

    cond = jnp.concatenate([c, c_ctx[None, :], jnp.zeros((MOD_ROWS - bsz - 1, d), F32)], axis=0)
    mods = _ada_all(cond, ada_w, ada_b).reshape(depth, MOD_ROWS, ADA_CHUNKS, d)

    rows = _Rows(bsz, seq, t_ctx)
    ffn_stacks = (ffn_w1, ffn_w3, ffn_w2)
    ffn_weights = [tuple(w[0, 0].astype(BF16) for w in ffn_stacks)]

    def ffn(h, layer, half, ctx_live, mixer_jobs=(), final=False):
        jobs = [] if final else [(w, (layer, 1) if half == 0 else (layer + 1, 0)) for w in ffn_stacks]
        h, cast = _half_ffn(h, rows, mods, layer, half, norm_g[layer, 2 * half], ffn_weights[0], final_g,
                            ctx_live, cast_jobs=jobs + list(mixer_jobs), final=final)
        ffn_weights[0] = cast[:len(ffn_stacks)]
        return h, cast[len(ffn_stacks):]

    h = (x.reshape(bsz * seq, d), ctx.reshape(bsz * t_ctx, d))
    for layer in range(depth):
        kind = layer % n_mixers
        slot = layer // n_mixers
        last = layer == depth - 1
        ctx_out = not last
        ctx_live = ctx_out or kind == 0
        g_mix = norm_g[layer, 1]
        if kind == 0:
            h, (w_in, w_out) = ffn(h, layer, 0, ctx_live, [(ret_w_in, (slot,)), (ret_w_out, (slot,))])
            h = _retention_layer(h, rows, mods, layer, g_mix, w_in, w_out, ret_decay[slot], ctx_out)
        elif kind == 1:
            h, _ = ffn(h, layer, 0, ctx_live)
            h = _pool_layer(h, rows, GRID_W, mods, layer, g_mix, pool_w[slot], pool_b[slot], pool_scale[slot],
                            ctx_out)
        else:
            h, (w_in, w_out) = ffn(h, layer, 0, ctx_live, [(hy_w_in, (slot,)), (hy_w_out, (slot,))])
            hp = (w_in, hy_b_in[slot], hy_w_short[slot], hy_b_short[slot], hy_w_pos[slot],
                  hy_b_pos[slot], hy_w_mid[slot], hy_b_mid[slot], hy_freq[slot], hy_w_filt[slot],
                  hy_bias[slot], w_out, hy_b_out[slot])
            h = _hyena_layer(h, rows, "lat", mods, layer, g_mix, *hp)
            if ctx_out:
                h = _hyena_layer(h, rows, "ctx", mods, layer, g_mix, *hp)
        h, _ = ffn(h, layer, 1, ctx_out, final=last)
    return h.reshape(bsz, seq, d)
```

```python
import functools
import math
from typing import NamedTuple

import numpy as np
import jax
import jax.numpy as jnp
from jax import lax
from jax.experimental import pallas as pl
from jax.experimental.pallas import tpu as pltpu

F32 = jnp.float32
BF16 = jnp.bfloat16

GRID_W = 64
ADA_CHUNKS = 9
NORM_EPS = 1e-6
RET_HEADS = 4
ROPE_BASE = 10000.0
POOL_WINDOWS = (2, 4, 8, 16)
HYENA_ORDER = 2
HYENA_EMB = 33
HYENA_BANDS = (HYENA_EMB - 1) // 2
HYENA_TARGET = 1e-2
HYENA_FAST = 0.3
HYENA_SLOW = 1.5
HYENA_MAX_DECAY = math.log(HYENA_TARGET) / HYENA_FAST
HYENA_MIN_DECAY = math.log(HYENA_TARGET) / HYENA_SLOW

V7X_LANES = 128
V7X_SUBLANES = 8
V7X_BF16_SUBLANES = 16
V7X_VMEM_LIMIT_BYTES = 56 * 1024 * 1024
MOD_ROWS = V7X_SUBLANES
ROW_TILE = 512
FFN_CAST_SLABS = 16
POOL_TILE = 512
RET_TILE = 256
RET_OUT_TILE = 1024
SCAN_CHUNK = 256
DFT_RADIX = 4
DFT_FREQ_TILE = 512
DFT_PACK = 256
DFT_ROW_GROUP = 512
DFT_TIME_TILE = 2048
DFT_COL_TILE = 256
HIGHEST = lax.Precision.HIGHEST


def _params(n_axes):
    return pltpu.CompilerParams(
        dimension_semantics=("arbitrary",) * n_axes,
        vmem_limit_bytes=V7X_VMEM_LIMIT_BYTES)


def _const_spec(shape):
    zeros = (0,) * len(shape)
    return pl.BlockSpec(shape, lambda *_: zeros, pipeline_mode=pl.Buffered(1))


def _mod_spec(d, layer, row_fn):
    return pl.BlockSpec((None, None, ADA_CHUNKS, d), lambda *idx: (layer, row_fn(*idx), 0, 0))


class _Rows(NamedTuple):
    bsz: int
    t_lat: int
    t_ctx: int

    @property
    def n_lat(self):
        return self.bsz * self.t_lat

    @property
    def n_all(self):
        return self.bsz * (self.t_lat + self.t_ctx)

    def length(self, part):
        return self.t_lat if part == "lat" else self.t_ctx

    def block(self, part, tile):
        per = self.length(part) // tile
        base = 0 if part == "lat" else self.n_lat // tile
        assert self.length(part) % tile == 0 and self.n_lat % tile == 0
        return lambda b, i: base + b * per + i


def _mm(a, b):
    return jnp.dot(a, b, preferred_element_type=F32)


def _mm_f32(a, b):
    return jnp.dot(a, b, preferred_element_type=F32, precision=HIGHEST)


def _silu(x):
    return x * jax.nn.sigmoid(x)


def _adaln(x, g, shift, scale):
    ms = jnp.mean(x * x, axis=-1, keepdims=True)
    y = x * lax.rsqrt(ms + NORM_EPS) * g
    return y * (1.0 + scale) + shift


def _ada_kernel(c_ref, w_ref, b_ref, o_ref):
    s = _silu(c_ref[...]).astype(BF16)
    o_ref[...] = _mm(s, w_ref[...].astype(BF16)) + b_ref[...]


def _ada_all(cond, ada_w, ada_b):
    depth, d, nd = ada_w.shape
    return pl.pallas_call(
        _ada_kernel,
        grid=(depth, nd // d),
        in_specs=[
            pl.BlockSpec((MOD_ROWS, d), lambda l, j: (0, 0)),
            pl.BlockSpec((None, d, d), lambda l, j: (l, 0, j)),
            pl.BlockSpec((None, 1, d), lambda l, j: (l, 0, j)),
        ],
        out_specs=pl.BlockSpec((None, MOD_ROWS, d), lambda l, j: (l, 0, j)),
        out_shape=jax.ShapeDtypeStruct((depth, MOD_ROWS, nd), F32),
        compiler_params=_params(2),
        name="ada",
    )(cond, ada_w, ada_b.reshape(depth, 1, nd))


def _ffn_kernel(*refs, base, final, n_lat, split_input, n_cast):
    n_x = 2 if split_input else 1
    x_refs, refs = refs[:n_x], refs[n_x:]
    mod_ref, g_ref, w1_ref, w3_ref, w2_ref, fg_ref = refs[:6]
    rest = refs[6:]
    if n_cast:
        n_jobs = (len(rest) - 1) // 2
        src_refs, o_ref, dst_refs = rest[:n_jobs], rest[n_jobs], rest[n_jobs + 1:]

        @pl.when(pl.program_id(0) < n_cast)
        def _():
            for src, dst in zip(src_refs, dst_refs):
                dst[...] = src[...].astype(BF16)
    else:
        (o_ref,) = rest
    if split_input:
        x = jnp.where(pl.program_id(0) >= n_lat, x_refs[1][...], x_refs[0][...])
    else:
        x = x_refs[0][...]
    u = _adaln(x, g_ref[...], mod_ref[base:base + 1, :], mod_ref[base + 1:base + 2, :]).astype(BF16)
    h1 = _mm(u, w1_ref[...])
    h3 = _mm(u, w3_ref[...])
    a = (_silu(h1) * h3).astype(BF16)
    y = x + (0.5 * mod_ref[base + 2:base + 3, :]) * _mm(a, w2_ref[...])
    if final:
        ms = jnp.mean(y * y, axis=-1, keepdims=True)
        y = y * lax.rsqrt(ms + NORM_EPS) * fg_ref[...]
    o_ref[...] = y


def _half_ffn(h, rows, mods, layer, half, g, weights, final_g, ctx_live, cast_jobs=(), final=False):
    w1, w3, w2 = weights
    d, f = w1.shape
    tm = ROW_TILE
    base = 6 * half
    per_b = rows.t_lat // tm
    n_lat = rows.n_lat // tm
    n_ctx = rows.bsz * rows.t_ctx // tm if ctx_live else 0
    assert rows.t_lat % tm == 0 and (rows.bsz * rows.t_ctx) % tm == 0
    split_input = isinstance(h, tuple)

    if split_input:
        x_specs = [pl.BlockSpec((tm, d), lambda s: (jnp.minimum(s, n_lat - 1), 0)),
                   pl.BlockSpec((tm, d), lambda s: (jnp.maximum(s - n_lat, 0), 0))]
        x_args = list(h)
    else:
        x_specs = [pl.BlockSpec((tm, d), lambda s: (s, 0))]
        x_args = [h]
    in_specs = x_specs + [
        _mod_spec(d, layer, lambda s: jnp.where(s < n_lat, s // per_b, rows.bsz)),
        _const_spec((1, d)),
        _const_spec((d, f)),
        _const_spec((d, f)),
        _const_spec((f, d)),
        _const_spec((1, d)),
    ]
    args = x_args + [mods, g.reshape(1, d), w1, w3, w2, final_g.reshape(1, d)]
    out_specs = [pl.BlockSpec((tm, d), lambda s: (s, 0))]
    out_shape = [jax.ShapeDtypeStruct((rows.n_lat if final else rows.n_all, d), F32)]
    n_cast = min(FFN_CAST_SLABS, n_lat) if cast_jobs else 0
    for w, lead in cast_jobs:
        r, c = w.shape[-2:]
        assert r % (V7X_BF16_SUBLANES * n_cast) == 0 and len(lead) == w.ndim - 2
        in_specs.append(pl.BlockSpec((None,) * len(lead) + (r // n_cast, c),
                                     functools.partial(lambda lead, s: lead + (jnp.minimum(s, n_cast - 1), 0), lead)))
        out_specs.append(pl.BlockSpec((r // n_cast, c), lambda s: (jnp.minimum(s, n_cast - 1), 0)))
        out_shape.append(jax.ShapeDtypeStruct((r, c), BF16))
        args.append(w)
    outs = pl.pallas_call(
        functools.partial(_ffn_kernel, base=base, final=final, n_lat=n_lat, split_input=split_input,
                          n_cast=n_cast),
        grid=(n_lat + n_ctx,),
        in_specs=in_specs,
        out_specs=out_specs,
        out_shape=out_shape,
        compiler_params=_params(1),
        name="half_ffn",
    )(*args)
    return outs[0], tuple(outs[1:])


def _ret_proj_kernel(xc_ref, xl_ref, mod_ref, g_ref, w_ref, cos_ref, sin_ref, q_ref, k_ref, v_ref, gate_ref,
                     *, n_heads, dk, dv, lat_tiles):
    x = jnp.where(pl.program_id(1) < lat_tiles, xl_ref[...], xc_ref[...])
    u = _adaln(x, g_ref[...], mod_ref[3:4, :], mod_ref[4:5, :]).astype(BF16)
    p = _mm(u, w_ref[...])
    cos = cos_ref[...]
    sin = sin_ref[...]
    half = dk // 2
    qk = n_heads * dk
    k_scale = dk ** -0.5

    def rot(a):
        a1, a2 = a[:, :half], a[:, half:]
        return jnp.concatenate([a1 * cos - a2 * sin, a2 * cos + a1 * sin], axis=-1)

    for h in range(n_heads):
        q_ref[:, h * dk:(h + 1) * dk] = rot(p[:, h * dk:(h + 1) * dk]).astype(BF16)
        k_ref[:, h * dk:(h + 1) * dk] = (rot(p[:, qk + h * dk:qk + (h + 1) * dk]) * k_scale).astype(BF16)
    v_ref[...] = p[:, 2 * qk:2 * qk + n_heads * dv].astype(BF16)
    gate_ref[...] = p[:, 2 * qk + n_heads * dv:].astype(BF16)


def _ret_scan_kernel(q_ref, k_ref, v_ref, dmat_ref, qdec_ref, kdec_ref, cdec_ref, o_ref,
                     state_ref, fwd_ref, *, n_heads, dk, dv, n_chunks, chunk_of_step):
    s = pl.program_id(1)
    cs = q_ref.shape[0]

    @pl.when((s == 0) | (s == n_chunks))
    def _():
        state_ref[...] = jnp.zeros_like(state_ref)

    rows = pl.ds(pl.multiple_of(chunk_of_step(s) * cs, cs), cs)
    outs = []
    for h in range(n_heads):
        q = q_ref[:, h * dk:(h + 1) * dk]
        k = k_ref[:, h * dk:(h + 1) * dk]
        v = v_ref[:, h * dv:(h + 1) * dv]
        scores = lax.dot_general(q, k, (((1,), (1,)), ((), ())), preferred_element_type=F32) * dmat_ref[h]
        state = state_ref[h]
        outs.append(_mm(scores.astype(BF16), v) + _mm(q, state.astype(BF16)) * qdec_ref[h])
        kd = (k.astype(F32) * kdec_ref[h]).astype(BF16)
        state_ref[h] = state * cdec_ref[h] + lax.dot_general(
            kd, v, (((0,), (0,)), ((), ())), preferred_element_type=F32)

    @pl.when(s < n_chunks)
    def _():
        for h in range(n_heads):
            fwd_ref[rows, h * dv:(h + 1) * dv] = outs[h].astype(fwd_ref.dtype)

    @pl.when(s >= n_chunks)
    def _():
        for h in range(n_heads):
            o = outs[h] + fwd_ref[rows, h * dv:(h + 1) * dv].astype(F32)
            mu = jnp.mean(o, axis=-1, keepdims=True)
            oc = o - mu
            var = jnp.mean(oc * oc, axis=-1, keepdims=True)
            o_ref[:, h * dv:(h + 1) * dv] = (oc * lax.rsqrt(var + NORM_EPS)).astype(o_ref.dtype)


def _ret_out_kernel(x_ref, mod_ref, y_ref, gate_ref, w_ref, o_ref):
    z =(_silu(gate_ref[...].astype(F32)) * y_ref[...].astype(F32)).astype(BF16)
    o_ref[...] = x_ref[...] + mod_ref[5:6, :] * _mm(z, w_ref[...])


def _retention_layer(h, rows, mods, layer, g, w_in, w_out, decay, ctx_out):
    bsz, t_lat, t_ctx = rows
    d = h.shape[1]
    t_all = t_ctx + t_lat
    n_heads = RET_HEADS
    dk = d // n_heads
    dv = 2 * d // n_heads
    qk = n_heads * dk
    vw = n_heads * dv
    tm = RET_TILE
    assert t_ctx % tm == 0 and t_lat % tm == 0
    lat_tiles = t_lat // tm
    ctx_row = bsz
    lat_blk, ctx_blk = rows.block("lat", tm), rows.block("ctx", tm)

    pos = jnp.concatenate([jnp.arange(t_lat, dtype=F32) + float(t_ctx), jnp.arange(t_ctx, dtype=F32)])
    inv = 1.0 / (ROPE_BASE ** jnp.linspace(0.0, 1.0, dk // 2, dtype=F32))
    ang = pos[:, None] * inv[None, :]

    q, k, v, gate = pl.pallas_call(
        functools.partial(_ret_proj_kernel, n_heads=n_heads, dk=dk, dv=dv, lat_tiles=lat_tiles),
        grid=(bsz, t_all // tm),
        in_specs=[
            pl.BlockSpec((tm, d), lambda b, i: (ctx_blk(b, jnp.maximum(i - lat_tiles, 0)), 0)),
            pl.BlockSpec((tm, d), lambda b, i: (lat_blk(b, jnp.minimum(i, lat_tiles - 1)), 0)),
            _mod_spec(d, layer, lambda b, i: jnp.where(i < lat_tiles, b, ctx_row)),
            _const_spec((1, d)),
            _const_spec((d, 2 * qk + 2 * vw)),
            pl.BlockSpec((tm, dk // 2), lambda b, i: (i, 0)),
            pl.BlockSpec((tm, dk // 2), lambda b, i: (i, 0)),
        ],
        out_specs=[
            pl.BlockSpec((None, tm, qk), lambda b, i: (b, i, 0)),
            pl.BlockSpec((None, tm, qk), lambda b, i: (b, i, 0)),
            pl.BlockSpec((None, tm, vw), lambda b, i: (b, i, 0)),
            pl.BlockSpec((None, tm, vw), lambda b, i: (b, i, 0)),
        ],
        out_shape=[
            jax.ShapeDtypeStruct((bsz, t_all, qk), BF16),
            jax.ShapeDtypeStruct((bsz, t_all, qk), BF16),
            jax.ShapeDtypeStruct((bsz, t_all, vw), BF16),
            jax.ShapeDtypeStruct((bsz, t_all, vw), BF16),
        ],
        compiler_params=_params(2),
        name="ret_proj",
    )(h, h, mods, g.reshape(1, d), w_in, jnp.cos(ang), jnp.sin(ang))

    cs = SCAN_CHUNK
    assert t_ctx % cs == 0 and t_lat % cs == 0
    n_chunks = t_all // cs
    ctx_chunks = t_ctx // cs
    lat_chunks = t_lat // cs
    lg = jax.nn.log_sigmoid(decay.astype(F32))[:, :, None]
    n = jnp.arange(cs, dtype=F32)
    rel = n[:, None] - n[None, :]
    dm_f = jnp.where(rel >= 0, jnp.exp(lg[0][:, :, None] * jnp.maximum(rel, 0.0)), 0.0)
    dm_b = jnp.where(rel < 0, jnp.exp(lg[1][:, :, None] * jnp.maximum(-rel, 0.0)), 0.0)
    dmat = jnp.stack([dm_f, dm_b])
    qd = jnp.stack([jnp.exp(lg[0] * (n + 1.0)), jnp.exp(lg[1] * (cs - n))])
    kd = jnp.stack([jnp.exp(lg[0] * (cs - 1.0 - n)), jnp.exp(lg[1] * n)])
    cd = jnp.exp(lg * cs)
    qdec = jnp.broadcast_to(qd[..., None], (2, n_heads, cs, dv))
    kdec = jnp.broadcast_to(kd[..., None], (2, n_heads, cs, dk))
    cdec = jnp.broadcast_to(cd[..., None], (2, n_heads, 1, dv))

    def fwd_chunk(j):
        return jnp.where(j < ctx_chunks, lat_chunks + j, j - ctx_chunks)

    def bwd_chunk(j):
        return n_chunks - 1 - j

    def chunk(s):
        return jnp.where(s < n_chunks, fwd_chunk(s), bwd_chunk(jnp.maximum(s - n_chunks, 0)))

    def out_chunk(s):
        return bwd_chunk(jnp.maximum(s - n_chunks, 0))

    def table_spec(*shape):
        return pl.BlockSpec((None,) + shape, lambda b, s: (s // n_chunks, 0, 0, 0))

    y = pl.pallas_call(
        functools.partial(_ret_scan_kernel, n_heads=n_heads, dk=dk, dv=dv, n_chunks=n_chunks,
                          chunk_of_step=chunk),
        grid=(bsz, 2 * n_chunks),
        in_specs=[
            pl.BlockSpec((None, cs, qk), lambda b, s: (b, chunk(s), 0)),
            pl.BlockSpec((None, cs, qk), lambda b, s: (b, chunk(s), 0)),
            pl.BlockSpec((None, cs, vw), lambda b, s: (b, chunk(s), 0)),
            table_spec(n_heads, cs, cs),
            table_spec(n_heads, cs, dv),
            table_spec(n_heads, cs, dk),
            table_spec(n_heads, 1, dv),
        ],
        out_specs=pl.BlockSpec((None, cs, vw), lambda b, s: (b, out_chunk(s), 0)),
        out_shape=jax.ShapeDtypeStruct((bsz, t_all, vw), BF16),
        scratch_shapes=[pltpu.VMEM((n_heads, dk, dv), F32), pltpu.VMEM((t_all, vw), BF16)],
        compiler_params=_params(2),
        name="ret_scan",
    )(q, k, v, dmat, qdec, kdec, cdec)

    def readout(buf, part, tile, first_tile, row_fn):
        blk = rows.block(part, tile)
        return pl.pallas_call(
            _ret_out_kernel,
            grid=(bsz, rows.length(part) // tile),
            in_specs=[
                pl.BlockSpec((tile, d), lambda b, i: (blk(b, i), 0)),
                _mod_spec(d, layer, row_fn),
                pl.BlockSpec((None, tile, vw), lambda b, i: (b, i + first_tile, 0)),
                pl.BlockSpec((None, tile, vw), lambda b, i: (b, i + first_tile, 0)),
                _const_spec((vw, d)),
            ],
            out_specs=pl.BlockSpec((tile, d), lambda b, i: (blk(b, i), 0)),
            out_shape=jax.ShapeDtypeStruct(h.shape, F32),
            input_output_aliases={0: 0},
            compiler_params=_params(2),
            name="ret_out",
        )(buf, mods, y, gate, w_out)

    h = readout(h, "lat", min(RET_OUT_TILE, t_lat), 0, lambda b, i: b)
    if ctx_out:
        h = readout(h, "ctx", tm, lat_tiles, lambda b, i: ctx_row)
    return h


def _pool_tables(seg_len):
    assert POOL_TILE % seg_len == 0
    r = np.arange(POOL_TILE)
    seg, pos = r // seg_len, r % seg_len
    mats, invs = [], []
    for win in POOL_WINDOWS:
        lo = np.clip(pos - win // 2, 0, seg_len)
        hi = np.clip(pos - win // 2 + win, 0, seg_len)
        col_seg, col_pos = seg[None, :], pos[None, :]
        member = (col_seg == seg[:, None]) & (col_pos >= lo[:, None]) & (col_pos < hi[:, None])
        mats.append(member.astype(np.float32))
        invs.append((1.0 / (hi - lo)).astype(np.float32))
    return np.stack(mats), np.stack(invs)


def _pool_kernel(x_ref, mod_ref, g_ref, s_ref, inv_ref, w_ref, b_ref, sc_ref, o_ref, *, n_groups, dg):
    x = x_ref[...]
    u = _adaln(x, g_ref[...], mod_ref[3:4, :], mod_ref[4:5, :])
    gate = mod_ref[5:6, :]
    for gi in range(n_groups):
        sl = slice(gi * dg, (gi + 1) * dg)
        ug = u[:, sl]
        hi = ug.astype(BF16)
        lo = (ug - hi.astype(F32)).astype(BF16)
        win_sum = _mm(s_ref[gi], hi) + _mm(s_ref[gi], lo)
        dlt = win_sum * inv_ref[gi] - ug
        yg = _mm(dlt.astype(BF16), w_ref[gi])
        o_ref[:, sl] = x[:, sl] + gate[:, sl] * ((yg + b_ref[:, sl]) * sc_ref[:, sl])


def _pool_layer(h, rows, lat_seg, mods, layer, g, w_grp, b_grp, scale, ctx_out):
    bsz, t_lat, t_ctx = rows
    d = h.shape[1]
    n_groups = len(POOL_WINDOWS)
    dg = d // n_groups
    tabs = [_pool_tables(lat_seg), _pool_tables(t_ctx)]
    s_tab = jnp.asarray(np.stack([m for m, _ in tabs]), BF16)
    inv_tab = jnp.asarray(np.stack([np.broadcast_to(v[:, :, None], (n_groups, POOL_TILE, dg)) for _, v in tabs]), F32)
    assert t_lat % POOL_TILE == 0 and (bsz * t_ctx) % POOL_TILE == 0
    per_b = t_lat // POOL_TILE
    n_lat = rows.n_lat // POOL_TILE
    n_ctx = bsz * t_ctx // POOL_TILE if ctx_out else 0

    def table_spec(*shape):
        return pl.BlockSpec((None,) + shape, lambda s: (jnp.where(s < n_lat, 0, 1), 0, 0, 0))

    return pl.pallas_call(
        functools.partial(_pool_kernel, n_groups=n_groups, dg=dg),
        grid=(n_lat + n_ctx,),
        in_specs=[
            pl.BlockSpec((POOL_TILE, d), lambda s: (s, 0)),
            _mod_spec(d, layer, lambda s: jnp.where(s < n_lat, s // per_b, bsz)),
            _const_spec((1, d)),
            table_spec(n_groups, POOL_TILE, POOL_TILE),
            table_spec(n_groups, POOL_TILE, dg),
            _const_spec((n_groups, dg, dg)),
            _const_spec((1, d)),
            _const_spec((1, d)),
        ],
        out_specs=pl.BlockSpec((POOL_TILE, d), lambda s: (s, 0)),
        out_shape=jax.ShapeDtypeStruct(h.shape, F32),
        compiler_params=_params(1),
        name="pool",
    )(h, mods, g.reshape(1, d), s_tab, inv_tab, w_grp.astype(BF16), b_grp.reshape(1, d), scale.reshape(1, d))


def _dft_tables(n_in):
    mod = 4 * n_in
    sub_n = min(V7X_LANES, n_in)
    idx = jnp.arange(n_in, dtype=jnp.int32)[None, :]
    blk = jnp.arange(n_in // sub_n, dtype=jnp.int32)[:, None]
    sub = jnp.arange(sub_n, dtype=jnp.int32)[:, None]
    theta = 2.0 * math.pi / mod

    def cs(p):
        ph = (p % mod).astype(F32) * theta
        return jnp.cos(ph), jnp.sin(ph)

    fwd = cs(2 * sub_n * blk * idx) + cs((2 * sub + 1) * idx)
    inv = cs(sub_n * blk * (2 * idx + 1)) + cs(sub * (2 * idx + 1))
    return fwd, inv


def _dft_fwd_gen_kernel(pc_ref, ps_ref, qc_ref, qs_ref, o_ref, *, tf):
    ft = pl.program_id(0)
    sub_n = qc_ref.shape[0]
    blocks = tf // sub_n
    qc = qc_ref[...]
    qs = qs_ref[...]
    for jb in range(blocks):
        kb = ft * blocks + jb
        pc = pc_ref[pl.ds(kb, 1), :]
        ps = ps_ref[pl.ds(kb, 1), :]
        o_ref[jb * sub_n:(jb + 1) * sub_n, :] = (pc * qc - ps * qs).astype(BF16)
        o_ref[tf + jb * sub_n:tf + (jb + 1) * sub_n, :] = (ps * qc + pc * qs).astype(BF16)


def _dft_inv_gen_kernel(pc_ref, ps_ref, qc_ref, qs_ref, o_ref, *, tf, n_in):
    mb = pl.program_id(0)
    pc = pc_ref[pl.ds(mb, 1), :]
    ps = ps_ref[pl.ds(mb, 1), :]
    qc = qc_ref[...]
    qs = qs_ref[...]
    cos_all = pc * qc - ps * qs
    sin_all = ps * qc + pc * qs
    for ft in range(n_in // tf):
        o_ref[:, 2 * ft * tf:(2 * ft + 1) * tf] = cos_all[:, ft * tf:(ft + 1) * tf].astype(BF16)
        o_ref[:, (2 * ft + 1) * tf:(2 * ft + 2) * tf] = sin_all[:, ft * tf:(ft + 1) * tf].astype(BF16)


def _dft_matrices(n_in, tf):
    fwd_tabs, inv_tabs = _dft_tables(n_in)
    sub_n = min(V7X_LANES, n_in)
    nb = n_in // sub_n
    tabs = [_const_spec((nb, n_in)), _const_spec((nb, n_in)),
            _const_spec((sub_n, n_in)), _const_spec((sub_n, n_in))]
    fwd = pl.pallas_call(
        functools.partial(_dft_fwd_gen_kernel, tf=tf),
        grid=(n_in // tf,),
        in_specs=tabs,
        out_specs=pl.BlockSpec((2 * tf, n_in), lambda i: (i, 0)),
        out_shape=jax.ShapeDtypeStruct((2 * n_in, n_in), BF16),
        compiler_params=_params(1),
        name="dft_fwd_gen",
    )(*fwd_tabs)
    inv = pl.pallas_call(
        functools.partial(_dft_inv_gen_kernel, tf=tf, n_in=n_in),
        grid=(nb,),
        in_specs=tabs,
        out_specs=pl.BlockSpec((sub_n, 2 * n_in), lambda i: (i, 0)),
        out_shape=jax.ShapeDtypeStruct((n_in, 2 * n_in), BF16),
        compiler_params=_params(1),
        name="dft_inv_gen",
    )(*inv_tabs)
    return fwd, inv


def _hy_proj_kernel(x_ref, xp_ref, xn_ref, mod_ref, g_ref, w_ref, b_ref, ws_ref, bs_ref,
                    v_ref, x1_ref, x2_ref):
    i = pl.program_id(1)
    tm, d = x_ref.shape
    n_ph = v_ref.shape[0]
    hp = tm // n_ph
    g = g_ref[...]
    shift, scale = mod_ref[3:4, :], mod_ref[4:5, :]
    u = _adaln(x_ref[...], g, shift, scale).astype(BF16)
    u = jnp.swapaxes(u.reshape(hp, n_ph, d), 0, 1).reshape(tm, d)
    halo = jnp.concatenate([xp_ref[...], xn_ref[...]], axis=0)
    uh = _adaln(halo, g, shift, scale).astype(BF16)
    p = _mm(jnp.concatenate([u, uh], axis=0), w_ref[...]) + b_ref[...]
    ph = [p[r * hp:(r + 1) * hp] for r in range(n_ph)]
    n_halo = xp_ref.shape[0]
    before = jnp.where(i == 0, 0.0, p[tm + n_halo - 1:tm + n_halo])
    after = jnp.where(i == pl.num_programs(1) - 1, 0.0, p[tm + n_halo:tm + n_halo + 1])
    row = lax.broadcasted_iota(jnp.int32, ph[0].shape, 0)
    last_prev = jnp.where(row == 0, before, pltpu.roll(ph[-1], 1, axis=0))
    first_next = jnp.where(row == hp - 1, after, pltpu.roll(ph[0], hp - 1, axis=0))
    w0, w1, w2, bs = ws_ref[0:1, :], ws_ref[1:2, :], ws_ref[2:3, :], bs_ref[...]
    for r in range(n_ph):
        prev = ph[r - 1] if r > 0 else last_prev
        nxt = ph[r + 1] if r < n_ph - 1 else first_next
        conv = prev * w0 + ph[r] * w1 + nxt * w2 + bs
        v_ref[r] = conv[:, :d].astype(BF16)
        x1_ref[r] = conv[:, d:2 * d].astype(BF16)
        x2_ref[r] = conv[:, 2 * d:].astype(BF16)


def _hy_filter_kernel(feat_ref, wp_ref, bp_ref, wm_ref, bm_ref, fr_ref, wf_ref, dl_ref,
                      hf_ref, hb_ref, ss_ref, *, d, n_orders):
    start = (pl.program_id(0) == 0) & (pl.program_id(1) == 0)
    feat = feat_ref[...]
    fr = fr_ref[...]
    hdn = jnp.sin(fr * (_mm_f32(feat, wp_ref[...]) + bp_ref[...]))
    hdn = jnp.sin(fr * (_mm_f32(hdn, wm_ref[...]) + bm_ref[...]))
    h = _mm(hdn.astype(BF16), wf_ref[...])
    decay = jnp.exp(-feat[:, 0:1] * dl_ref[...])
    row = lax.broadcasted_iota(jnp.int32, decay.shape, 0)
    first = (row == 0) & start

    @pl.when(start)
    def _():
        ss_ref[...] = jnp.zeros_like(ss_ref)

    for o in range(n_orders):
        sl = slice(o * d, (o + 1) * d)
        hf = h[:, 2 * o * d:(2 * o + 1) * d] * decay
        hb = jnp.where(first, 0.0, h[:, (2 * o + 1) * d:(2 * o + 2) * d] * decay)
        hf_ref[:, sl] = hf.astype(BF16)
        hb_ref[:, sl] = hb.astype(BF16)
        ss_ref[:, sl] += jnp.sum(hf * hf + hb * hb, axis=0, keepdims=True)


def _twiddle(a, c, s, conj=False):
    a_re, a_s = a
    if conj:
        return a_re * c + a_s * s, a_s * c - a_re * s
    return a_re * c - a_s * s, a_s * c + a_re * s


def _add(a, b):
    return a[0] + b[0], a[1] + b[1]


def _sub(a, b):
    return a[0] - b[0], a[1] - b[1]


def _dit_blocks(dft, phase_refs, tw, tp):
    ph = []
    for ref in phase_refs:
        t = _mm(dft, ref[...])
        ph.append((t[0:tp], t[tp:]))
    c1, s1, c2, s2, c3, s3 = tw
    t2, t3 = _twiddle(ph[2], c1, s1), _twiddle(ph[3], c1, s1)
    e_a, e_b = _add(ph[0], t2), _sub(ph[0], t2)
    o_a, o_b = _add(ph[1], t3), _sub(ph[1], t3)
    ta, tb = _twiddle(o_a, c2, s2), _twiddle(o_b, c3, s3, conj=True)
    return [_add(e_a, ta), _sub(e_a, ta), _add(e_b, tb), _sub(e_b, tb)]


def _bin_group(f_ref, tw_ref, g, tp):
    re, sn = slice(2 * g * tp, (2 * g + 1) * tp), slice((2 * g + 1) * tp, (2 * g + 2) * tp)
    dft = f_ref[2 * g * tp:(2 * g + 2) * tp, :]
    tw = [tw_ref[i, g * tp:(g + 1) * tp, :] for i in range(6)]
    return re, sn, dft, tw


def _hy_spec_kernel(f_ref, f0, f1, f2, f3, b0, b1, b2, b3, tw_ref, ss_ref, o_ref, *, tp, scale):
    wgt = scale * lax.rsqrt(ss_ref[...] + NORM_EPS)
    for g in range(tw_ref.shape[1] // tp):
        re, sn, dft, tw = _bin_group(f_ref, tw_ref, g, tp)
        fwd = _dit_blocks(dft, (f0, f1, f2, f3), tw, tp)
        bwd = _dit_blocks(dft, (b0, b1, b2, b3), tw, tp)
        for blk in range(4):
            o_ref[blk, re, :] = (fwd[blk][0] + bwd[blk][0]) * wgt
            o_ref[blk, sn, :] = (fwd[blk][1] - bwd[blk][1]) * wgt


def _hy_fwd_kernel(f_ref, u0, u1, u2, u3, h_ref, tw_ref, y_ref, *, tp):
    for g in range(tw_ref.shape[1] // tp):
        re, sn, dft, tw = _bin_group(f_ref, tw_ref, g, tp)
        c1, s1, c2, s2, c3, s3 = tw
        x = _dit_blocks(dft, (u0, u1, u2, u3), tw, tp)
        y = []
        for blk in range(4):
            h_re, h_s = h_ref[blk, re, :], h_ref[blk, sn, :]
            x_re, x_s = x[blk]
            y.append((x_re * h_re - x_s * h_s, x_re * h_s + x_s * h_re))
        e_a, o_a = _add(y[0], y[1]), _twiddle(_sub(y[0], y[1]), c2, s2, conj=True)
        e_b, o_b = _add(y[2], y[3]), _twiddle(_sub(y[2], y[3]), c3, s3)
        out = [_add(e_a, e_b), _add(o_a, o_b),
               _twiddle(_sub(e_a, e_b), c1, s1, conj=True), _twiddle(_sub(o_a, o_b), c1, s1, conj=True)]
        for r in range(4):
            y_ref[r, re, :] = out[r][0].astype(BF16)
            y_ref[r, sn, :] = out[r][1].astype(BF16)


def _hy_inv_kernel(f_ref, y_ref, a_ref, u_ref, fb_ref, o_ref, *, rows):
    for r in range(f_ref.shape[0] // rows):
        sl = slice(r * rows, (r + 1) * rows)
        conv = _mm(f_ref[sl, :], y_ref[...])
        z = a_ref[sl, :].astype(F32) * (conv + u_ref[sl, :].astype(F32) * fb_ref[...])
        o_ref[sl, :] = z.astype(o_ref.dtype)


def _hy_out_kernel(x_ref, mod_ref, z_ref, w_ref, b_ref, o_ref):
    n_ph, hp, d = z_ref.shape
    z = jnp.stack([z_ref[r] for r in range(n_ph)], axis=1).reshape(n_ph * hp, d)
    o_ref[...] = x_ref[...] + mod_ref[5:6, :] * (_mm(z, w_ref[...]) + b_ref[...])


def _hyena_layer(h, rows, part, mods, layer, g, w_in, b_in, w_short, b_short, w_pos, b_pos, w_mid,
                 b_mid, freq, w_filt, fbias, w_out, b_out):
    bsz = rows.bsz
    length = rows.length(part)
    d = h.shape[1]
    mod_row = (lambda b, i: b) if part == "lat" else (lambda b, i: bsz)
    n_full = 2 * length
    n_ph = DFT_RADIX
    q = length // n_ph
    tm = min(ROW_TILE, length)
    hp = tm // n_ph
    tc = DFT_COL_TILE if q >= DFT_FREQ_TILE else d
    n_ct = d // tc
    width = w_mid.shape[0]
    n_parts = 2 * HYENA_ORDER

    halo = V7X_SUBLANES
    ph_spec = pl.BlockSpec((None, n_ph, hp, d), lambda b, i: (b, 0, i, 0))
    tile_blk, halo_blk = rows.block(part, tm), rows.block(part, halo)
    v, x1, x2 = pl.pallas_call(
        _hy_proj_kernel,
        grid=(bsz, length // tm),
        in_specs=[
            pl.BlockSpec((tm, d), lambda b, i: (tile_blk(b, i), 0)),
            pl.BlockSpec((halo, d), lambda b, i: (halo_blk(b, jnp.maximum(i * (tm // halo) - 1, 0)), 0)),
            pl.BlockSpec((halo, d),
                         lambda b, i: (halo_blk(b, jnp.minimum((i + 1) * (tm // halo), length // halo - 1)), 0)),
            _mod_spec(d, layer, mod_row),
            _const_spec((1, d)),
            _const_spec((d, 3 * d)),
            _const_spec((1, 3 * d)),
            _const_spec((3, 3 * d)),
            _const_spec((1, 3 * d)),
        ],
        out_specs=[ph_spec] * 3,
        out_shape=[jax.ShapeDtypeStruct((bsz, n_ph, q, d), BF16)] * 3,
        compiler_params=_params(2),
        name="hy_proj",
    )(h, h, h, mods, g.reshape(1, d), w_in, b_in.reshape(1, 3 * d),
      w_short, b_short.reshape(1, 3 * d))

    t = jnp.linspace(0.0, 1.0, length, dtype=F32)[:, None]
    bands = jnp.linspace(1e-4, HYENA_BANDS - 1, HYENA_BANDS, dtype=F32)
    ang = (2.0 * math.pi / length) * jnp.arange(length, dtype=F32)[:, None] * bands[None, :]
    feat = jnp.concatenate([t, jnp.cos(ang), -jnp.sin(ang)], axis=-1)
    feat = jnp.pad(feat, ((0, 0), (0, V7X_LANES - HYENA_EMB)))
    w_pos_p = jnp.pad(w_pos.astype(F32), ((0, V7X_LANES - HYENA_EMB), (0, 0)))
    deltas = jnp.abs(jnp.linspace(HYENA_MIN_DECAY, HYENA_MAX_DECAY, d, dtype=F32)).reshape(1, d)
    feat = jnp.concatenate([feat[r::n_ph] for r in range(n_ph)], axis=0)
    tl = min(ROW_TILE, q)
    od = HYENA_ORDER * d
    tap_spec = pl.BlockSpec((tl, od), lambda r, i: (i, r))
    taps_f, taps_b, ss = pl.pallas_call(
        functools.partial(_hy_filter_kernel, d=d, n_orders=HYENA_ORDER),
        grid=(n_ph, q // tl),
        in_specs=[
            pl.BlockSpec((tl, V7X_LANES), lambda r, i: (r * (q // tl) + i, 0)),
            _const_spec((V7X_LANES, width)),
            _const_spec((1, width)),
            _const_spec((width, width)),
            _const_spec((1, width)),
            _const_spec((1, width)),
            _const_spec((width, n_parts * d)),
            _const_spec((1, d)),
        ],
        out_specs=[tap_spec, tap_spec, pl.BlockSpec((1, od), lambda r, i: (0, 0))],
        out_shape=[jax.ShapeDtypeStruct((q, n_ph * od), BF16), jax.ShapeDtypeStruct((q, n_ph * od), BF16),
                   jax.ShapeDtypeStruct((1, od), F32)],
        compiler_params=_params(2),
        name="hy_filter",
    )(feat, w_pos_p, b_pos.reshape(1, width), w_mid, b_mid.reshape(1, width), freq.reshape(1, width),
      w_filt.astype(BF16), deltas)

    tf = min(DFT_FREQ_TILE, q)
    tp = min(DFT_PACK, q)
    tn = min(DFT_TIME_TILE, q)
    dft_fwd, dft_inv = _dft_matrices(q, tp)
    odd = 2.0 * jnp.arange(q, dtype=F32) + 1.0
    angles = jnp.stack([(2.0 * math.pi / n_full) * odd, (math.pi / n_full) * odd,
                        (math.pi / n_full) * (length - odd)])
    tw = jnp.stack([jnp.cos(angles), jnp.sin(angles)], axis=1).reshape(6, q)
    tw = jnp.broadcast_to(tw[:, :, None], (6, q, tc))
    tw_spec = pl.BlockSpec((6, tf, tc), lambda f, *_: (0, f, 0))

    n_oc = od // tc

    def tap_phase(r):
        return pl.BlockSpec((q, tc), lambda f, o, j: (0, r * n_oc + o * n_ct + j))

    spec = pl.pallas_call(
        functools.partial(_hy_spec_kernel, tp=tp, scale=1.0 / length),
        grid=(q // tf, HYENA_ORDER, n_ct),
        in_specs=[pl.BlockSpec((2 * tf, q), lambda f, o, j: (f, 0))]
        + [tap_phase(r) for r in range(n_ph)] * 2
        + [tw_spec, pl.BlockSpec((1, tc), lambda f, o, j: (0, o * n_ct + j))],
        out_specs=pl.BlockSpec((None, n_ph, 2 * tf, tc), lambda f, o, j: (o, 0, f, j)),
        out_shape=jax.ShapeDtypeStruct((HYENA_ORDER, n_ph, 2 * q, d), F32),
        compiler_params=_params(3),
        name="hy_spec",
    )(dft_fwd, *([taps_f] * n_ph), *([taps_b] * n_ph), tw, ss)

    tci = d
    n_ci = d // tci

    def long_conv(order, u, a):
        yspec = pl.pallas_call(
            functools.partial(_hy_fwd_kernel, tp=tp),
            grid=(q // tf, n_ct, bsz),
            in_specs=[pl.BlockSpec((2 * tf, q), lambda f, j, b: (f, 0))]
            + [pl.BlockSpec((None, None, q, tc), functools.partial(lambda r, f, j, b: (b, r, 0, j), r))
               for r in range(n_ph)]
            + [pl.BlockSpec((None, n_ph, 2 * tf, tc), lambda f, j, b: (order, 0, f, j)), tw_spec],
            out_specs=pl.BlockSpec((None, n_ph, 2 * tf, tc), lambda f, j, b: (b, 0, f, j)),
            out_shape=jax.ShapeDtypeStruct((bsz, n_ph, 2 * q, d), BF16),
            compiler_params=_params(3),
            name="hy_fwd",
        )(dft_fwd, *([u] * n_ph), spec, tw)
        row_spec = pl.BlockSpec((None, None, tn, tci), lambda n, j, b, r: (b, r, n, j))
        return pl.pallas_call(
            functools.partial(_hy_inv_kernel, rows=min(DFT_ROW_GROUP, tn)),
            grid=(q // tn, n_ci, bsz, n_ph),
            in_specs=[
                pl.BlockSpec((tn, 2 * q), lambda n, j, b, r: (n, 0)),
                pl.BlockSpec((None, None, 2 * q, tci), lambda n, j, b, r: (b, r, 0, j)),
                row_spec,
                row_spec,
                pl.BlockSpec((None, 1, tci), lambda n, j, b, r: (order, 0, j)),
            ],
            out_specs=row_spec,
            out_shape=jax.ShapeDtypeStruct((bsz, n_ph, q, d), BF16),
            compiler_params=_params(4),
            name="hy_inv",
        )(dft_inv, yspec, a, u, fbias.astype(F32).reshape(HYENA_ORDER, 1, d))

    z = long_conv(0, v, x1)
    zz = long_conv(1, z, x2)

    return pl.pallas_call(
        _hy_out_kernel,
        grid=(bsz, length // tm),
        in_specs=[
            pl.BlockSpec((tm, d), lambda b, i: (tile_blk(b, i), 0)),
            _mod_spec(d, layer, mod_row),
            pl.BlockSpec((None, n_ph, hp, d), lambda b, i: (b, 0, i, 0)),
            _const_spec((d, d)),
            _const_spec((1, d)),
        ],
        out_specs=pl.BlockSpec((tm, d), lambda b, i: (tile_blk(b, i), 0)),
        out_shape=jax.ShapeDtypeStruct(h.shape, F32),
        input_output_aliases={0: 0},
        compiler_params=_params(2),
        name="hy_out",
    )(h, mods, zz, w_out, b_out.reshape(1, d))


def kernel(x, c, ctx, c_ctx, ada_w, ada_b, norm_g, ffn_w1, ffn_w3, ffn_w2, ret_w_in, ret_w_out, ret_decay,
           pool_w, pool_b, pool_scale, hy_w_in, hy_b_in, hy_w_short, hy_b_short, hy_w_pos, hy_b_pos,
           hy_w_mid, hy_b_mid, hy_freq, hy_w_filt, hy_bias, hy_w_out, hy_b_out, final_g):
    bsz, seq, d = x.shape
    t_ctx = ctx.shape[1]
    depth = ada_w.shape[0]
    n_mixers = 3
    assert bsz + 1 <= MOD_ROWS

    cond = jnp.concatenate([c, c_ctx[None, :], jnp.zeros((MOD_ROWS - bsz - 1, d), F32)], axis=0)
    mods = _ada_all(cond, ada_w, ada_b).reshape(depth, MOD_ROWS, ADA_CHUNKS, d)

    rows = _Rows(bsz, seq, t_ctx)
    ffn_stacks = (ffn_w1, ffn_w3, ffn_w2)
    ffn_weights = [tuple(w[0, 0].astype(BF16) for w in ffn_stacks)]

    def ffn(h, layer, half, ctx_live, mixer_jobs=(), final=False):
        jobs = [] if final else [(w, (layer, 1) if half == 0 else (layer + 1, 0)) for w in ffn_stacks]
        h, cast = _half_ffn(h, rows, mods, layer, half, norm_g[layer, 2 * half], ffn_weights[0], final_g,
                            ctx_live, cast_jobs=jobs + list(mixer_jobs), final=final)
        ffn_weights[0] = cast[:len(ffn_stacks)]
        return h, cast[len(ffn_stacks):]

    h = (x.reshape(bsz * seq, d), ctx.reshape(bsz * t_ctx, d))
    for layer in range(depth):
        kind = layer % n_mixers
        slot = layer // n_mixers
        last = layer == depth - 1
        ctx_out = not last
        ctx_live = ctx_out or kind == 0
        g_mix = norm_g[layer, 1]
        if kind == 0:
            h, (w_in, w_out) = ffn(h, layer, 0, ctx_live, [(ret_w_in, (slot,)), (ret_w_out, (slot,))])
            h = _retention_layer(h, rows, mods, layer, g_mix, w_in, w_out, ret_decay[slot], ctx_out)
        elif kind == 1:
            h, _ = ffn(h, layer, 0, ctx_live)
            h = _pool_layer(h, rows, GRID_W, mods, layer, g_mix, pool_w[slot], pool_b[slot], pool_scale[slot],
                            ctx_out)
        else:
            h, (w_in, w_out) = ffn(h, layer, 0, ctx_live, [(hy_w_in, (slot,)), (hy_w_out, (slot,))])
            hp = (w_in, hy_b_in[slot], hy_w_short[slot], hy_b_short[slot], hy_w_pos[slot],
                  hy_b_pos[slot], hy_w_mid[slot], hy_b_mid[slot], hy_freq[slot], hy_w_filt[slot],
                  hy_bias[slot], w_out, hy_b_out[slot])
            h = _hyena_layer(h, rows, "lat", mods, layer, g_mix, *hp)
            if ctx_out:
                h = _hyena_layer(h, rows, "ctx", mods, layer, g_mix, *hp)
        h, _ = ffn(h, layer, 1, ctx_out, final=last)
    return h.reshape(bsz, seq, d)
```

```python
import functools
import math
from typing import NamedTuple

import numpy as np
import jax
import jax.numpy as jnp
from jax import lax
from jax.experimental import pallas as pl
from jax.experimental.pallas import tpu as pltpu

F32 = jnp.float32
BF16 = jnp.bfloat16

GRID_W = 64
ADA_CHUNKS = 9
NORM_EPS = 1e-6
RET_HEADS = 4
ROPE_BASE = 10000.0
POOL_WINDOWS = (2, 4, 8, 16)
HYENA_ORDER = 2
HYENA_EMB = 33
HYENA_BANDS = (HYENA_EMB - 1) // 2
HYENA_TARGET = 1e-2
HYENA_FAST = 0.3
HYENA_SLOW = 1.5
HYENA_MAX_DECAY = math.log(HYENA_TARGET) / HYENA_FAST
HYENA_MIN_DECAY = math.log(HYENA_TARGET) / HYENA_SLOW

V7X_LANES = 128
V7X_SUBLANES = 8
V7X_BF16_SUBLANES = 16
V7X_VMEM_LIMIT_BYTES = 56 * 1024 * 1024
MOD_ROWS = V7X_SUBLANES
ROW_TILE = 512
FFN_CAST_SLABS = 16
POOL_TILE = 512
RET_TILE = 256
RET_OUT_TILE = 1024
SCAN_CHUNK = 256
DFT_RADIX = 4
DFT_FREQ_TILE = 512
DFT_PACK = 256
DFT_ROW_GROUP = 512
DFT_TIME_TILE = 2048
DFT_COL_TILE = 256
HIGHEST = lax.Precision.HIGHEST


def _params(n_axes):
    return pltpu.CompilerParams(
        dimension_semantics=("arbitrary",) * n_axes,
        vmem_limit_bytes=V7X_VMEM_LIMIT_BYTES)


def _const_spec(shape):
    zeros = (0,) * len(shape)
    return pl.BlockSpec(shape, lambda *_: zeros, pipeline_mode=pl.Buffered(1))


def _mod_spec(d, layer, row_fn):
    return pl.BlockSpec((None, None, ADA_CHUNKS, d), lambda *idx: (layer, row_fn(*idx), 0, 0))


class _Rows(NamedTuple):
    bsz: int
    t_lat: int
    t_ctx: int

    @property
    def n_lat(self):
        return self.bsz * self.t_lat

    @property
    def n_all(self):
        return self.bsz * (self.t_lat + self.t_ctx)

    def length(self, part):
        return self.t_lat if part == "lat" else self.t_ctx

    def block(self, part, tile):
        per = self.length(part) // tile
        base = 0 if part == "lat" else self.n_lat // tile
        assert self.length(part) % tile == 0 and self.n_lat % tile == 0
        return lambda b, i: base + b * per + i


def _mm(a, b):
    return jnp.dot(a, b, preferred_element_type=F32)


def _mm_f32(a, b):
    return jnp.dot(a, b, preferred_element_type=F32, precision=HIGHEST)


def _silu(x):
    return x * jax.nn.sigmoid(x)


def _adaln(x, g, shift, scale):
    ms = jnp.mean(x * x, axis=-1, keepdims=True)
    y = x * lax.rsqrt(ms + NORM_EPS) * g
    return y * (1.0 + scale) + shift


def _ada_kernel(c_ref, w_ref, b_ref, o_ref):
    s = _silu(c_ref[...]).astype(BF16)
    o_ref[...] = _mm(s, w_ref[...].astype(BF16)) + b_ref[...]


def _ada_all(cond, ada_w, ada_b):
    depth, d, nd = ada_w.shape
    return pl.pallas_call(
        _ada_kernel,
        grid=(depth, nd // d),
        in_specs=[
            pl.BlockSpec((MOD_ROWS, d), lambda l, j: (0, 0)),
            pl.BlockSpec((None, d, d), lambda l, j: (l, 0, j)),
            pl.BlockSpec((None, 1, d), lambda l, j: (l, 0, j)),
        ],
        out_specs=pl.BlockSpec((None, MOD_ROWS, d), lambda l, j: (l, 0, j)),
        out_shape=jax.ShapeDtypeStruct((depth, MOD_ROWS, nd), F32),
        compiler_params=_params(2),
        name="ada",
    )(cond, ada_w, ada_b.reshape(depth, 1, nd))


def _ffn_kernel(*refs, base, final, n_lat, split_input, n_cast):
    n_x = 2 if split_input else 1
    x_refs, refs = refs[:n_x], refs[n_x:]
    mod_ref, g_ref, w1_ref, w3_ref, w2_ref, fg_ref = refs[:6]
    rest = refs[6:]
    if n_cast:
        n_jobs = (len(rest) - 1) // 2
        src_refs, o_ref, dst_refs = rest[:n_jobs], rest[n_jobs], rest[n_jobs + 1:]

        @pl.when(pl.program_id(0) < n_cast)
        def _():
            for src, dst in zip(src_refs, dst_refs):
                dst[...] = src[...].astype(BF16)
    else:
        (o_ref,) = rest
    if split_input:
        x = jnp.where(pl.program_id(0) >= n_lat, x_refs[1][...], x_refs[0][...])
    else:
        x = x_refs[0][...]
    u = _adaln(x, g_ref[...], mod_ref[base:base + 1, :], mod_ref[base + 1:base + 2, :]).astype(BF16)
    h1 = _mm(u, w1_ref[...])
    h3 = _mm(u, w3_ref[...])
    a = (_silu(h1) * h3).astype(BF16)
    y = x + (0.5 * mod_ref[base + 2:base + 3, :]) * _mm(a, w2_ref[...])
    if final:
        ms = jnp.mean(y * y, axis=-1, keepdims=True)
        y = y * lax.rsqrt(ms + NORM_EPS) * fg_ref[...]
    o_ref[...] = y


def _half_ffn(h, rows, mods, layer, half, g, weights, final_g, ctx_live, cast_jobs=(), final=False):
    w1, w3, w2 = weights
    d, f = w1.shape
    tm = ROW_TILE
    base = 6 * half
    per_b = rows.t_lat // tm
    n_lat = rows.n_lat // tm
    n_ctx = rows.bsz * rows.t_ctx // tm if ctx_live else 0
    assert rows.t_lat % tm == 0 and (rows.bsz * rows.t_ctx) % tm == 0
    split_input = isinstance(h, tuple)

    if split_input:
        x_specs = [pl.BlockSpec((tm, d), lambda s: (jnp.minimum(s, n_lat - 1), 0)),
                   pl.BlockSpec((tm, d), lambda s: (jnp.maximum(s - n_lat, 0), 0))]
        x_args = list(h)
    else:
        x_specs = [pl.BlockSpec((tm, d), lambda s: (s, 0))]
        x_args = [h]
    in_specs = x_specs + [
        _mod_spec(d, layer, lambda s: jnp.where(s < n_lat, s // per_b, rows.bsz)),
        _const_spec((1, d)),
        _const_spec((d, f)),
        _const_spec((d, f)),
        _const_spec((f, d)),
        _const_spec((1, d)),
    ]
    args = x_args + [mods, g.reshape(1, d), w1, w3, w2, final_g.reshape(1, d)]
    out_specs = [pl.BlockSpec((tm, d), lambda s: (s, 0))]
    out_shape = [jax.ShapeDtypeStruct((rows.n_lat if final else rows.n_all, d), F32)]
    n_cast = min(FFN_CAST_SLABS, n_lat) if cast_jobs else 0
    for w, lead in cast_jobs:
        r, c = w.shape[-2:]
        assert r % (V7X_BF16_SUBLANES * n_cast) == 0 and len(lead) == w.ndim - 2
        in_specs.append(pl.BlockSpec((None,) * len(lead) + (r // n_cast, c),
                                     functools.partial(lambda lead, s: lead + (jnp.minimum(s, n_cast - 1), 0), lead)))
        out_specs.append(pl.BlockSpec((r // n_cast, c), lambda s: (jnp.minimum(s, n_cast - 1), 0)))
        out_shape.append(jax.ShapeDtypeStruct((r, c), BF16))
        args.append(w)
    outs = pl.pallas_call(
        functools.partial(_ffn_kernel, base=base, final=final, n_lat=n_lat, split_input=split_input,
                          n_cast=n_cast),
        grid=(n_lat + n_ctx,),
        in_specs=in_specs,
        out_specs=out_specs,
        out_shape=out_shape,
        compiler_params=_params(1),
        name="half_ffn",
    )(*args)
    return outs[0], tuple(outs[1:])


def _ret_proj_kernel(x_ref, mod_ref, g_ref, w_ref, cos_ref, sin_ref, q_ref, k_ref, v_ref, gate_ref,
                     *, n_heads, dk, dv):
    u = _adaln(x_ref[...], g_ref[...], mod_ref[3:4, :], mod_ref[4:5, :]).astype(BF16)
    p = _mm(u, w_ref[...])
    cos = cos_ref[...]
    sin = sin_ref[...]
    half = dk // 2
    qk = n_heads * dk
    k_scale = dk ** -0.5

    def rot(a):
        a1, a2 = a[:, :half], a[:, half:]
        return jnp.concatenate([a1 * cos - a2 * sin, a2 * cos + a1 * sin], axis=-1)

    for h in range(n_heads):
        q_ref[:, h * dk:(h + 1) * dk] = rot(p[:, h * dk:(h + 1) * dk]).astype(BF16)
        k_ref[:, h * dk:(h + 1) * dk] = (rot(p[:, qk + h * dk:qk + (h + 1) * dk]) * k_scale).astype(BF16)
    v_ref[...] = p[:, 2 * qk:2 * qk + n_heads * dv].astype(BF16)
    gate_ref[...] = p[:, 2 * qk + n_heads * dv:].astype(BF16)


def _ret_scan_kernel(q_ref, k_ref, v_ref, dmat_ref, qdec_ref, kdec_ref, cdec_ref, o_ref,
                     state_ref, fwd_ref, *, n_heads, dk, dv, n_chunks, chunk_of_step):
    s = pl.program_id(1)
    cs = q_ref.shape[0]

    @pl.when((s == 0) | (s == n_chunks))
    def _():
        state_ref[...] = jnp.zeros_like(state_ref)

    rows = pl.ds(pl.multiple_of(chunk_of_step(s) * cs, cs), cs)
    outs = []
    for h in range(n_heads):
        q = q_ref[:, h * dk:(h + 1) * dk]
        k = k_ref[:, h * dk:(h + 1) * dk]
        v = v_ref[:, h * dv:(h + 1) * dv]
        scores = lax.dot_general(q, k, (((1,), (1,)), ((), ())), preferred_element_type=F32) * dmat_ref[h]
        state = state_ref[h]
        outs.append(_mm(scores.astype(BF16), v) + _mm(q, state.astype(BF16)) * qdec_ref[h])
        kd = (k.astype(F32) * kdec_ref[h]).astype(BF16)
        state_ref[h] = state * cdec_ref[h] + lax.dot_general(
            kd, v, (((0,), (0,)), ((), ())), preferred_element_type=F32)

    @pl.when(s < n_chunks)
    def _():
        for h in range(n_heads):
            fwd_ref[rows, h * dv:(h + 1) * dv] = outs[h].astype(fwd_ref.dtype)

    @pl.when(s >= n_chunks)
    def _():
        for h in range(n_heads):
            o = outs[h] + fwd_ref[rows, h * dv:(h + 1) * dv].astype(F32)
            mu = jnp.mean(o, axis=-1, keepdims=True)
            oc = o - mu
            var = jnp.mean(oc * oc, axis=-1, keepdims=True)
            o_ref[:, h * dv:(h + 1) * dv] = (oc * lax.rsqrt(var + NORM_EPS)).astype(o_ref.dtype)


def _ret_out_kernel(x_ref, mod_ref, y_ref, gate_ref, w_ref, o_ref):
    z =(_silu(gate_ref[...].astype(F32)) * y_ref[...].astype(F32)).astype(BF16)
    o_ref[...] = x_ref[...] + mod_ref[5:6, :] * _mm(z, w_ref[...])


def _retention_layer(h, rows, mods, layer, g, w_in, w_out, decay, ctx_out):
    bsz, t_lat, t_ctx = rows
    d = h.shape[1]
    t_all = t_ctx + t_lat
    n_heads = RET_HEADS
    dk = d // n_heads
    dv = 2 * d // n_heads
    qk = n_heads * dk
    vw = n_heads * dv
    tm = RET_TILE
    assert t_ctx % tm == 0 and t_lat % tm == 0
    lat_tiles = t_lat // tm
    ctx_row = bsz

    pos = jnp.concatenate([jnp.arange(t_lat, dtype=F32) + float(t_ctx), jnp.arange(t_ctx, dtype=F32)])
    inv = 1.0 / (ROPE_BASE ** jnp.linspace(0.0, 1.0, dk // 2, dtype=F32))
    ang = pos[:, None] * inv[None, :]

    ctx_tiles = t_ctx // tm
    n_lat_tiles = bsz * lat_tiles

    def store(s):
        c = jnp.maximum(s - n_lat_tiles, 0)
        is_lat = s < n_lat_tiles
        return (jnp.where(is_lat, s // lat_tiles, c // ctx_tiles),
                jnp.where(is_lat, s % lat_tiles, lat_tiles + c % ctx_tiles))

    q, k, v, gate = pl.pallas_call(
        functools.partial(_ret_proj_kernel, n_heads=n_heads, dk=dk, dv=dv),
        grid=(n_lat_tiles + bsz * ctx_tiles,),
        in_specs=[
            pl.BlockSpec((tm, d), lambda s: (s, 0)),
            _mod_spec(d, layer, lambda s: jnp.where(s < n_lat_tiles, s // lat_tiles, ctx_row)),
            _const_spec((1, d)),
            _const_spec((d, 2 * qk + 2 * vw)),
            pl.BlockSpec((tm, dk // 2), lambda s: (store(s)[1], 0)),
            pl.BlockSpec((tm, dk // 2), lambda s: (store(s)[1], 0)),
        ],
        out_specs=[
            pl.BlockSpec((None, tm, qk), lambda s: store(s) + (0,)),
            pl.BlockSpec((None, tm, qk), lambda s: store(s) + (0,)),
            pl.BlockSpec((None, tm, vw), lambda s: store(s) + (0,)),
            pl.BlockSpec((None, tm, vw), lambda s: store(s) + (0,)),
        ],
        out_shape=[
            jax.ShapeDtypeStruct((bsz, t_all, qk), BF16),
            jax.ShapeDtypeStruct((bsz, t_all, qk), BF16),
            jax.ShapeDtypeStruct((bsz, t_all, vw), BF16),
            jax.ShapeDtypeStruct((bsz, t_all, vw), BF16),
        ],
        compiler_params=_params(1),
        name="ret_proj",
    )(h, mods, g.reshape(1, d), w_in, jnp.cos(ang), jnp.sin(ang))

    cs = SCAN_CHUNK
    assert t_ctx % cs == 0 and t_lat % cs == 0
    n_chunks = t_all // cs
    ctx_chunks = t_ctx // cs
    lat_chunks = t_lat // cs
    lg = jax.nn.log_sigmoid(decay.astype(F32))[:, :, None]
    n = jnp.arange(cs, dtype=F32)
    rel = n[:, None] - n[None, :]
    dm_f = jnp.where(rel >= 0, jnp.exp(lg[0][:, :, None] * jnp.maximum(rel, 0.0)), 0.0)
    dm_b = jnp.where(rel < 0, jnp.exp(lg[1][:, :, None] * jnp.maximum(-rel, 0.0)), 0.0)
    dmat = jnp.stack([dm_f, dm_b])
    qd = jnp.stack([jnp.exp(lg[0] * (n + 1.0)), jnp.exp(lg[1] * (cs - n))])
    kd = jnp.stack([jnp.exp(lg[0] * (cs - 1.0 - n)), jnp.exp(lg[1] * n)])
    cd = jnp.exp(lg * cs)
    qdec = jnp.broadcast_to(qd[..., None], (2, n_heads, cs, dv))
    kdec = jnp.broadcast_to(kd[..., None], (2, n_heads, cs, dk))
    cdec = jnp.broadcast_to(cd[..., None], (2, n_heads, 1, dv))

    def fwd_chunk(j):
        return jnp.where(j < ctx_chunks, lat_chunks + j, j - ctx_chunks)

    def bwd_chunk(j):
        return n_chunks - 1 - j

    def chunk(s):
        return jnp.where(s < n_chunks, fwd_chunk(s), bwd_chunk(jnp.maximum(s - n_chunks, 0)))

    def out_chunk(s):
        return bwd_chunk(jnp.maximum(s - n_chunks, 0))

    def table_spec(*shape):
        return pl.BlockSpec((None,) + shape, lambda b, s: (s // n_chunks, 0, 0, 0))

    y = pl.pallas_call(
        functools.partial(_ret_scan_kernel, n_heads=n_heads, dk=dk, dv=dv, n_chunks=n_chunks,
                          chunk_of_step=chunk),
        grid=(bsz, 2 * n_chunks),
        in_specs=[
            pl.BlockSpec((None, cs, qk), lambda b, s: (b, chunk(s), 0)),
            pl.BlockSpec((None, cs, qk), lambda b, s: (b, chunk(s), 0)),
            pl.BlockSpec((None, cs, vw), lambda b, s: (b, chunk(s), 0)),
            table_spec(n_heads, cs, cs),
            table_spec(n_heads, cs, dv),
            table_spec(n_heads, cs, dk),
            table_spec(n_heads, 1, dv),
        ],
        out_specs=pl.BlockSpec((None, cs, vw), lambda b, s: (b, out_chunk(s), 0)),
        out_shape=jax.ShapeDtypeStruct((bsz, t_all, vw), BF16),
        scratch_shapes=[pltpu.VMEM((n_heads, dk, dv), F32), pltpu.VMEM((t_all, vw), BF16)],
        compiler_params=_params(2),
        name="ret_scan",
    )(q, k, v, dmat, qdec, kdec, cdec)

    def readout(buf, part, tile, first_tile, row_fn):
        blk = rows.block(part, tile)
        return pl.pallas_call(
            _ret_out_kernel,
            grid=(bsz, rows.length(part) // tile),
            in_specs=[
                pl.BlockSpec((tile, d), lambda b, i: (blk(b, i), 0)),
                _mod_spec(d, layer, row_fn),
                pl.BlockSpec((None, tile, vw), lambda b, i: (b, i + first_tile, 0)),
                pl.BlockSpec((None, tile, vw), lambda b, i: (b, i + first_tile, 0)),
                _const_spec((vw, d)),
            ],
            out_specs=pl.BlockSpec((tile, d), lambda b, i: (blk(b, i), 0)),
            out_shape=jax.ShapeDtypeStruct(h.shape, F32),
            input_output_aliases={0: 0},
            compiler_params=_params(2),
            name="ret_out",
        )(buf, mods, y, gate, w_out)

    h = readout(h, "lat", min(RET_OUT_TILE, t_lat), 0, lambda b, i: b)
    if ctx_out:
        h = readout(h, "ctx", tm, lat_tiles, lambda b, i: ctx_row)
    return h


def _pool_tables(seg_len):
    assert POOL_TILE % seg_len == 0
    r = np.arange(POOL_TILE)
    seg, pos = r // seg_len, r % seg_len
    mats, invs = [], []
    for win in POOL_WINDOWS:
        lo = np.clip(pos - win // 2, 0, seg_len)
        hi = np.clip(pos - win // 2 + win, 0, seg_len)
        col_seg, col_pos = seg[None, :], pos[None, :]
        member = (col_seg == seg[:, None]) & (col_pos >= lo[:, None]) & (col_pos < hi[:, None])
        mats.append(member.astype(np.float32))
        invs.append((1.0 / (hi - lo)).astype(np.float32))
    return np.stack(mats), np.stack(invs)


def _pool_kernel(x_ref, mod_ref, g_ref, s_ref, inv_ref, w_ref, b_ref, sc_ref, o_ref, *, n_groups, dg):
    x = x_ref[...]
    u = _adaln(x, g_ref[...], mod_ref[3:4, :], mod_ref[4:5, :])
    gate = mod_ref[5:6, :]
    for gi in range(n_groups):
        sl = slice(gi * dg, (gi + 1) * dg)
        ug = u[:, sl]
        hi = ug.astype(BF16)
        lo = (ug - hi.astype(F32)).astype(BF16)
        win_sum = _mm(s_ref[gi], hi) + _mm(s_ref[gi], lo)
        dlt = win_sum * inv_ref[gi] - ug
        yg = _mm(dlt.astype(BF16), w_ref[gi])
        o_ref[:, sl] = x[:, sl] + gate[:, sl] * ((yg + b_ref[:, sl]) * sc_ref[:, sl])


def _pool_layer(h, rows, lat_seg, mods, layer, g, w_grp, b_grp, scale, ctx_out):
    bsz, t_lat, t_ctx = rows
    d = h.shape[1]
    n_groups = len(POOL_WINDOWS)
    dg = d // n_groups
    tabs = [_pool_tables(lat_seg), _pool_tables(t_ctx)]
    s_tab = jnp.asarray(np.stack([m for m, _ in tabs]), BF16)
    inv_tab = jnp.asarray(np.stack([np.broadcast_to(v[:, :, None], (n_groups, POOL_TILE, dg)) for _, v in tabs]), F32)
    assert t_lat % POOL_TILE == 0 and (bsz * t_ctx) % POOL_TILE == 0
    per_b = t_lat // POOL_TILE
    n_lat = rows.n_lat // POOL_TILE
    n_ctx = bsz * t_ctx // POOL_TILE if ctx_out else 0

    def table_spec(*shape):
        return pl.BlockSpec((None,) + shape, lambda s: (jnp.where(s < n_lat, 0, 1), 0, 0, 0))

    return pl.pallas_call(
        functools.partial(_pool_kernel, n_groups=n_groups, dg=dg),
        grid=(n_lat + n_ctx,),
        in_specs=[
            pl.BlockSpec((POOL_TILE, d), lambda s: (s, 0)),
            _mod_spec(d, layer, lambda s: jnp.where(s < n_lat, s // per_b, bsz)),
            _const_spec((1, d)),
            table_spec(n_groups, POOL_TILE, POOL_TILE),
            table_spec(n_groups, POOL_TILE, dg),
            _const_spec((n_groups, dg, dg)),
            _const_spec((1, d)),
            _const_spec((1, d)),
        ],
        out_specs=pl.BlockSpec((POOL_TILE, d), lambda s: (s, 0)),
        out_shape=jax.ShapeDtypeStruct(h.shape, F32),
        compiler_params=_params(1),
        name="pool",
    )(h, mods, g.reshape(1, d), s_tab, inv_tab, w_grp.astype(BF16), b_grp.reshape(1, d), scale.reshape(1, d))


def _dft_tables(n_in):
    mod = 4 * n_in
    sub_n = min(V7X_LANES, n_in)
    idx = jnp.arange(n_in, dtype=jnp.int32)[None, :]
    blk = jnp.arange(n_in // sub_n, dtype=jnp.int32)[:, None]
    sub = jnp.arange(sub_n, dtype=jnp.int32)[:, None]
    theta = 2.0 * math.pi / mod

    def cs(p):
        ph = (p % mod).astype(F32) * theta
        return jnp.cos(ph), jnp.sin(ph)

    fwd = cs(2 * sub_n * blk * idx) + cs((2 * sub + 1) * idx)
    inv = cs(sub_n * blk * (2 * idx + 1)) + cs(sub * (2 * idx + 1))
    return fwd, inv


def _dft_fwd_gen_kernel(pc_ref, ps_ref, qc_ref, qs_ref, o_ref, *, tf):
    ft = pl.program_id(0)
    sub_n = qc_ref.shape[0]
    blocks = tf // sub_n
    qc = qc_ref[...]
    qs = qs_ref[...]
    for jb in range(blocks):
        kb = ft * blocks + jb
        pc = pc_ref[pl.ds(kb, 1), :]
        ps = ps_ref[pl.ds(kb, 1), :]
        o_ref[jb * sub_n:(jb + 1) * sub_n, :] = (pc * qc - ps * qs).astype(BF16)
        o_ref[tf + jb * sub_n:tf + (jb + 1) * sub_n, :] = (ps * qc + pc * qs).astype(BF16)


def _dft_inv_gen_kernel(pc_ref, ps_ref, qc_ref, qs_ref, o_ref, *, tf, n_in):
    mb = pl.program_id(0)
    pc = pc_ref[pl.ds(mb, 1), :]
    ps = ps_ref[pl.ds(mb, 1), :]
    qc = qc_ref[...]
    qs = qs_ref[...]
    cos_all = pc * qc - ps * qs
    sin_all = ps * qc + pc * qs
    for ft in range(n_in // tf):
        o_ref[:, 2 * ft * tf:(2 * ft + 1) * tf] = cos_all[:, ft * tf:(ft + 1) * tf].astype(BF16)
        o_ref[:, (2 * ft + 1) * tf:(2 * ft + 2) * tf] = sin_all[:, ft * tf:(ft + 1) * tf].astype(BF16)


def _dft_matrices(n_in, tf):
    fwd_tabs, inv_tabs = _dft_tables(n_in)
    sub_n = min(V7X_LANES, n_in)
    nb = n_in // sub_n
    tabs = [_const_spec((nb, n_in)), _const_spec((nb, n_in)),
            _const_spec((sub_n, n_in)), _const_spec((sub_n, n_in))]
    fwd = pl.pallas_call(
        functools.partial(_dft_fwd_gen_kernel, tf=tf),
        grid=(n_in // tf,),
        in_specs=tabs,
        out_specs=pl.BlockSpec((2 * tf, n_in), lambda i: (i, 0)),
        out_shape=jax.ShapeDtypeStruct((2 * n_in, n_in), BF16),
        compiler_params=_params(1),
        name="dft_fwd_gen",
    )(*fwd_tabs)
    inv = pl.pallas_call(
        functools.partial(_dft_inv_gen_kernel, tf=tf, n_in=n_in),
        grid=(nb,),
        in_specs=tabs,
        out_specs=pl.BlockSpec((sub_n, 2 * n_in), lambda i: (i, 0)),
        out_shape=jax.ShapeDtypeStruct((n_in, 2 * n_in), BF16),
        compiler_params=_params(1),
        name="dft_inv_gen",
    )(*inv_tabs)
    return fwd, inv


def _hy_proj_kernel(x_ref, xp_ref, xn_ref, mod_ref, g_ref, w_ref, b_ref, ws_ref, bs_ref,
                    v_ref, x1_ref, x2_ref):
    i = pl.program_id(1)
    tm, d = x_ref.shape
    n_ph = v_ref.shape[0]
    hp = tm // n_ph
    g = g_ref[...]
    shift, scale = mod_ref[3:4, :], mod_ref[4:5, :]
    u = _adaln(x_ref[...], g, shift, scale).astype(BF16)
    u = jnp.swapaxes(u.reshape(hp, n_ph, d), 0, 1).reshape(tm, d)
    halo = jnp.concatenate([xp_ref[...], xn_ref[...]], axis=0)
    uh = _adaln(halo, g, shift, scale).astype(BF16)
    p = _mm(jnp.concatenate([u, uh], axis=0), w_ref[...]) + b_ref[...]
    ph = [p[r * hp:(r + 1) * hp] for r in range(n_ph)]
    n_halo = xp_ref.shape[0]
    before = jnp.where(i == 0, 0.0, p[tm + n_halo - 1:tm + n_halo])
    after = jnp.where(i == pl.num_programs(1) - 1, 0.0, p[tm + n_halo:tm + n_halo + 1])
    row = lax.broadcasted_iota(jnp.int32, ph[0].shape, 0)
    last_prev = jnp.where(row == 0, before, pltpu.roll(ph[-1], 1, axis=0))
    first_next = jnp.where(row == hp - 1, after, pltpu.roll(ph[0], hp - 1, axis=0))
    w0, w1, w2, bs = ws_ref[0:1, :], ws_ref[1:2, :], ws_ref[2:3, :], bs_ref[...]
    for r in range(n_ph):
        prev = ph[r - 1] if r > 0 else last_prev
        nxt = ph[r + 1] if r < n_ph - 1 else first_next
        conv = prev * w0 + ph[r] * w1 + nxt * w2 + bs
        v_ref[r] = conv[:, :d].astype(BF16)
        x1_ref[r] = conv[:, d:2 * d].astype(BF16)
        x2_ref[r] = conv[:, 2 * d:].astype(BF16)


def _hy_filter_kernel(feat_ref, wp_ref, bp_ref, wm_ref, bm_ref, fr_ref, wf_ref, dl_ref,
                      hf_ref, hb_ref, ss_ref, *, d, n_orders):
    start = (pl.program_id(0) == 0) & (pl.program_id(1) == 0)
    feat = feat_ref[...]
    fr = fr_ref[...]
    hdn = jnp.sin(fr * (_mm_f32(feat, wp_ref[...]) + bp_ref[...]))
    hdn = jnp.sin(fr * (_mm_f32(hdn, wm_ref[...]) + bm_ref[...]))
    h = _mm(hdn.astype(BF16), wf_ref[...])
    decay = jnp.exp(-feat[:, 0:1] * dl_ref[...])
    row = lax.broadcasted_iota(jnp.int32, decay.shape, 0)
    first = (row == 0) & start

    @pl.when(start)
    def _():
        ss_ref[...] = jnp.zeros_like(ss_ref)

    for o in range(n_orders):
        sl = slice(o * d, (o + 1) * d)
        hf = h[:, 2 * o * d:(2 * o + 1) * d] * decay
        hb = jnp.where(first, 0.0, h[:, (2 * o + 1) * d:(2 * o + 2) * d] * decay)
        hf_ref[:, sl] = hf.astype(BF16)
        hb_ref[:, sl] = hb.astype(BF16)
        ss_ref[:, sl] += jnp.sum(hf * hf + hb * hb, axis=0, keepdims=True)


def _twiddle(a, c, s, conj=False):
    a_re, a_s = a
    if conj:
        return a_re * c + a_s * s, a_s * c - a_re * s
    return a_re * c - a_s * s, a_s * c + a_re * s


def _add(a, b):
    return a[0] + b[0], a[1] + b[1]


def _sub(a, b):
    return a[0] - b[0], a[1] - b[1]


def _dit_blocks(dft, phase_refs, tw, tp):
    ph = []
    for ref in phase_refs:
        t = _mm(dft, ref[...])
        ph.append((t[0:tp], t[tp:]))
    c1, s1, c2, s2, c3, s3 = tw
    t2, t3 = _twiddle(ph[2], c1, s1), _twiddle(ph[3], c1, s1)
    e_a, e_b = _add(ph[0], t2), _sub(ph[0], t2)
    o_a, o_b = _add(ph[1], t3), _sub(ph[1], t3)
    ta, tb = _twiddle(o_a, c2, s2), _twiddle(o_b, c3, s3, conj=True)
    return [_add(e_a, ta), _sub(e_a, ta), _add(e_b, tb), _sub(e_b, tb)]


def _bin_group(f_ref, tw_ref, g, tp):
    re, sn = slice(2 * g * tp, (2 * g + 1) * tp), slice((2 * g + 1) * tp, (2 * g + 2) * tp)
    dft = f_ref[2 * g * tp:(2 * g + 2) * tp, :]
    tw = [tw_ref[i, g * tp:(g + 1) * tp, :] for i in range(6)]
    return re, sn, dft, tw


def _hy_spec_kernel(f_ref, f0, f1, f2, f3, b0, b1, b2, b3, tw_ref, ss_ref, o_ref, *, tp, scale):
    wgt = scale * lax.rsqrt(ss_ref[...] + NORM_EPS)
    for g in range(tw_ref.shape[1] // tp):
        re, sn, dft, tw = _bin_group(f_ref, tw_ref, g, tp)
        fwd = _dit_blocks(dft, (f0, f1, f2, f3), tw, tp)
        bwd = _dit_blocks(dft, (b0, b1, b2, b3), tw, tp)
        for blk in range(4):
            o_ref[blk, re, :] = (fwd[blk][0] + bwd[blk][0]) * wgt
            o_ref[blk, sn, :] = (fwd[blk][1] - bwd[blk][1]) * wgt


def _hy_fwd_kernel(f_ref, u0, u1, u2, u3, h_ref, tw_ref, y_ref, *, tp):
    for g in range(tw_ref.shape[1] // tp):
        re, sn, dft, tw = _bin_group(f_ref, tw_ref, g, tp)
        c1, s1, c2, s2, c3, s3 = tw
        x = _dit_blocks(dft, (u0, u1, u2, u3), tw, tp)
        y = []
        for blk in range(4):
            h_re, h_s = h_ref[blk, re, :], h_ref[blk, sn, :]
            x_re, x_s = x[blk]
            y.append((x_re * h_re - x_s * h_s, x_re * h_s + x_s * h_re))
        e_a, o_a = _add(y[0], y[1]), _twiddle(_sub(y[0], y[1]), c2, s2, conj=True)
        e_b, o_b = _add(y[2], y[3]), _twiddle(_sub(y[2], y[3]), c3, s3)
        out = [_add(e_a, e_b), _add(o_a, o_b),
               _twiddle(_sub(e_a, e_b), c1, s1, conj=True), _twiddle(_sub(o_a, o_b), c1, s1, conj=True)]
        for r in range(4):
            y_ref[r, re, :] = out[r][0].astype(BF16)
            y_ref[r, sn, :] = out[r][1].astype(BF16)


def _hy_inv_kernel(f_ref, y_ref, a_ref, u_ref, fb_ref, o_ref, *, rows):
    for r in range(f_ref.shape[0] // rows):
        sl = slice(r * rows, (r + 1) * rows)
        conv = _mm(f_ref[sl, :], y_ref[...])
        z = a_ref[sl, :].astype(F32) * (conv + u_ref[sl, :].astype(F32) * fb_ref[...])
        o_ref[sl, :] = z.astype(o_ref.dtype)


def _hy_out_kernel(x_ref, mod_ref, z_ref, w_ref, b_ref, o_ref):
    n_ph, hp, d = z_ref.shape
    z = jnp.stack([z_ref[r] for r in range(n_ph)], axis=1).reshape(n_ph * hp, d)
    o_ref[...] = x_ref[...] + mod_ref[5:6, :] * (_mm(z, w_ref[...]) + b_ref[...])


def _hyena_layer(h, rows, part, mods, layer, g, w_in, b_in, w_short, b_short, w_pos, b_pos, w_mid,
                 b_mid, freq, w_filt, fbias, w_out, b_out):
    bsz = rows.bsz
    length = rows.length(part)
    d = h.shape[1]
    mod_row = (lambda b, i: b) if part == "lat" else (lambda b, i: bsz)
    n_full = 2 * length
    n_ph = DFT_RADIX
    q = length // n_ph
    tm = min(ROW_TILE, length)
    hp = tm // n_ph
    tc = DFT_COL_TILE if q >= DFT_FREQ_TILE else d
    n_ct = d // tc
    width = w_mid.shape[0]
    n_parts = 2 * HYENA_ORDER

    halo = V7X_SUBLANES
    ph_spec = pl.BlockSpec((None, n_ph, hp, d), lambda b, i: (b, 0, i, 0))
    tile_blk, halo_blk = rows.block(part, tm), rows.block(part, halo)
    v, x1, x2 = pl.pallas_call(
        _hy_proj_kernel,
        grid=(bsz, length // tm),
        in_specs=[
            pl.BlockSpec((tm, d), lambda b, i: (tile_blk(b, i), 0)),
            pl.BlockSpec((halo, d), lambda b, i: (halo_blk(b, jnp.maximum(i * (tm // halo) - 1, 0)), 0)),
            pl.BlockSpec((halo, d),
                         lambda b, i: (halo_blk(b, jnp.minimum((i + 1) * (tm // halo), length // halo - 1)), 0)),
            _mod_spec(d, layer, mod_row),
            _const_spec((1, d)),
            _const_spec((d, 3 * d)),
            _const_spec((1, 3 * d)),
            _const_spec((3, 3 * d)),
            _const_spec((1, 3 * d)),
        ],
        out_specs=[ph_spec] * 3,
        out_shape=[jax.ShapeDtypeStruct((bsz, n_ph, q, d), BF16)] * 3,
        compiler_params=_params(2),
        name="hy_proj",
    )(h, h, h, mods, g.reshape(1, d), w_in, b_in.reshape(1, 3 * d),
      w_short, b_short.reshape(1, 3 * d))

    t = jnp.linspace(0.0, 1.0, length, dtype=F32)[:, None]
    bands = jnp.linspace(1e-4, HYENA_BANDS - 1, HYENA_BANDS, dtype=F32)
    ang = (2.0 * math.pi / length) * jnp.arange(length, dtype=F32)[:, None] * bands[None, :]
    feat = jnp.concatenate([t, jnp.cos(ang), -jnp.sin(ang)], axis=-1)
    feat = jnp.pad(feat, ((0, 0), (0, V7X_LANES - HYENA_EMB)))
    w_pos_p = jnp.pad(w_pos.astype(F32), ((0, V7X_LANES - HYENA_EMB), (0, 0)))
    deltas = jnp.abs(jnp.linspace(HYENA_MIN_DECAY, HYENA_MAX_DECAY, d, dtype=F32)).reshape(1, d)
    feat = jnp.concatenate([feat[r::n_ph] for r in range(n_ph)], axis=0)
    tl = min(ROW_TILE, q)
    od = HYENA_ORDER * d
    tap_spec = pl.BlockSpec((tl, od), lambda r, i: (i, r))
    taps_f, taps_b, ss = pl.pallas_call(
        functools.partial(_hy_filter_kernel, d=d, n_orders=HYENA_ORDER),
        grid=(n_ph, q // tl),
        in_specs=[
            pl.BlockSpec((tl, V7X_LANES), lambda r, i: (r * (q // tl) + i, 0)),
            _const_spec((V7X_LANES, width)),
            _const_spec((1, width)),
            _const_spec((width, width)),
            _const_spec((1, width)),
            _const_spec((1, width)),
            _const_spec((width, n_parts * d)),
            _const_spec((1, d)),
        ],
        out_specs=[tap_spec, tap_spec, pl.BlockSpec((1, od), lambda r, i: (0, 0))],
        out_shape=[jax.ShapeDtypeStruct((q, n_ph * od), BF16), jax.ShapeDtypeStruct((q, n_ph * od), BF16),
                   jax.ShapeDtypeStruct((1, od), F32)],
        compiler_params=_params(2),
        name="hy_filter",
    )(feat, w_pos_p, b_pos.reshape(1, width), w_mid, b_mid.reshape(1, width), freq.reshape(1, width),
      w_filt.astype(BF16), deltas)

    tf = min(DFT_FREQ_TILE, q)
    tp = min(DFT_PACK, q)
    tn = min(DFT_TIME_TILE, q)
    dft_fwd, dft_inv = _dft_matrices(q, tp)
    odd = 2.0 * jnp.arange(q, dtype=F32) + 1.0
    angles = jnp.stack([(2.0 * math.pi / n_full) * odd, (math.pi / n_full) * odd,
                        (math.pi / n_full) * (length - odd)])
    tw = jnp.stack([jnp.cos(angles), jnp.sin(angles)], axis=1).reshape(6, q)
    tw = jnp.broadcast_to(tw[:, :, None], (6, q, tc))
    tw_spec = pl.BlockSpec((6, tf, tc), lambda f, *_: (0, f, 0))

    n_oc = od // tc

    def tap_phase(r):
        return pl.BlockSpec((q, tc), lambda f, o, j: (0, r * n_oc + o * n_ct + j))

    spec = pl.pallas_call(
        functools.partial(_hy_spec_kernel, tp=tp, scale=1.0 / length),
        grid=(q // tf, HYENA_ORDER, n_ct),
        in_specs=[pl.BlockSpec((2 * tf, q), lambda f, o, j: (f, 0))]
        + [tap_phase(r) for r in range(n_ph)] * 2
        + [tw_spec, pl.BlockSpec((1, tc), lambda f, o, j: (0, o * n_ct + j))],
        out_specs=pl.BlockSpec((None, n_ph, 2 * tf, tc), lambda f, o, j: (o, 0, f, j)),
        out_shape=jax.ShapeDtypeStruct((HYENA_ORDER, n_ph, 2 * q, d), F32),
        compiler_params=_params(3),
        name="hy_spec",
    )(dft_fwd, *([taps_f] * n_ph), *([taps_b] * n_ph), tw, ss)

    tci = d
    n_ci = d // tci

    def long_conv(order, u, a):
        yspec = pl.pallas_call(
            functools.partial(_hy_fwd_kernel, tp=tp),
            grid=(q // tf, n_ct, bsz),
            in_specs=[pl.BlockSpec((2 * tf, q), lambda f, j, b: (f, 0))]
            + [pl.BlockSpec((None, None, q, tc), functools.partial(lambda r, f, j, b: (b, r, 0, j), r))
               for r in range(n_ph)]
            + [pl.BlockSpec((None, n_ph, 2 * tf, tc), lambda f, j, b: (order, 0, f, j)), tw_spec],
            out_specs=pl.BlockSpec((None, n_ph, 2 * tf, tc), lambda f, j, b: (b, 0, f, j)),
            out_shape=jax.ShapeDtypeStruct((bsz, n_ph, 2 * q, d), BF16),
            compiler_params=_params(3),
            name="hy_fwd",
        )(dft_fwd, *([u] * n_ph), spec, tw)
        row_spec = pl.BlockSpec((None, None, tn, tci), lambda n, j, b, r: (b, r, n, j))
        return pl.pallas_call(
            functools.partial(_hy_inv_kernel, rows=min(DFT_ROW_GROUP, tn)),
            grid=(q // tn, n_ci, bsz, n_ph),
            in_specs=[
                pl.BlockSpec((tn, 2 * q), lambda n, j, b, r: (n, 0)),
                pl.BlockSpec((None, None, 2 * q, tci), lambda n, j, b, r: (b, r, 0, j)),
                row_spec,
                row_spec,
                pl.BlockSpec((None, 1, tci), lambda n, j, b, r: (order, 0, j)),
            ],
            out_specs=row_spec,
            out_shape=jax.ShapeDtypeStruct((bsz, n_ph, q, d), BF16),
            compiler_params=_params(4),
            name="hy_inv",
        )(dft_inv, yspec, a, u, fbias.astype(F32).reshape(HYENA_ORDER, 1, d))

    z = long_conv(0, v, x1)
    zz = long_conv(1, z, x2)

    return pl.pallas_call(
        _hy_out_kernel,
        grid=(bsz, length // tm),
        in_specs=[
            pl.BlockSpec((tm, d), lambda b, i: (tile_blk(b, i), 0)),
            _mod_spec(d, layer, mod_row),
            pl.BlockSpec((None, n_ph, hp, d), lambda b, i: (b, 0, i, 0)),
            _const_spec((d, d)),
            _const_spec((1, d)),
        ],
        out_specs=pl.BlockSpec((tm, d), lambda b, i: (tile_blk(b, i), 0)),
        out_shape=jax.ShapeDtypeStruct(h.shape, F32),
        input_output_aliases={0: 0},
        compiler_params=_params(2),
        name="hy_out",
    )(h, mods, zz, w_out, b_out.reshape(1, d))


def kernel(x, c, ctx, c_ctx, ada_w, ada_b, norm_g, ffn_w1, ffn_w3, ffn_w2, ret_w_in, ret_w_out, ret_decay,
           pool_w, pool_b, pool_scale, hy_w_in, hy_b_in, hy_w_short, hy_b_short, hy_w_pos, hy_b_pos,
           hy_w_mid, hy_b_mid, hy_freq, hy_w_filt, hy_bias, hy_w_out, hy_b_out, final_g):
    bsz, seq, d = x.shape
    t_ctx = ctx.shape[1]
    depth = ada_w.shape[0]
    n_mixers = 3
    assert bsz + 1 <= MOD_ROWS

    cond = jnp.concatenate([c, c_ctx[None, :], jnp.zeros((MOD_ROWS - bsz - 1, d), F32)], axis=0)
    mods = _ada_all(cond, ada_w, ada_b).reshape(depth, MOD_ROWS, ADA_CHUNKS, d)

    rows = _Rows(bsz, seq, t_ctx)
    ffn_stacks = (ffn_w1, ffn_w3, ffn_w2)
    ffn_weights = [tuple(w[0, 0].astype(BF16) for w in ffn_stacks)]

    def ffn(h, layer, half, ctx_live, mixer_jobs=(), final=False):
        jobs = [] if final else [(w, (layer, 1) if half == 0 else (layer + 1, 0)) for w in ffn_stacks]
        h, cast = _half_ffn(h, rows, mods, layer, half, norm_g[layer, 2 * half], ffn_weights[0], final_g,
                            ctx_live, cast_jobs=jobs + list(mixer_jobs), final=final)
        ffn_weights[0] = cast[:len(ffn_stacks)]
        return h, cast[len(ffn_stacks):]

    h = (x.reshape(bsz * seq, d), ctx.reshape(bsz * t_ctx, d))
    for layer in range(depth):
        kind = layer % n_mixers
        slot = layer // n_mixers
        last = layer == depth - 1
        ctx_out = not last
        ctx_live = ctx_out or kind == 0
        g_mix = norm_g[layer, 1]
        if kind == 0:
            h, (w_in, w_out) = ffn(h, layer, 0, ctx_live, [(ret_w_in, (slot,)), (ret_w_out, (slot,))])
            h = _retention_layer(h, rows, mods, layer, g_mix, w_in, w_out, ret_decay[slot], ctx_out)
        elif kind == 1:
            h, _ = ffn(h, layer, 0, ctx_live)
            h = _pool_layer(h, rows, GRID_W, mods, layer, g_mix, pool_w[slot], pool_b[slot], pool_scale[slot],
                            ctx_out)
        else:
            h, (w_in, w_out) = ffn(h, layer, 0, ctx_live, [(hy_w_in, (slot,)), (hy_w_out, (slot,))])
            hp = (w_in, hy_b_in[slot], hy_w_short[slot], hy_b_short[slot], hy_w_pos[slot],
                  hy_b_pos[slot], hy_w_mid[slot], hy_b_mid[slot], hy_freq[slot], hy_w_filt[slot],
                  hy_bias[slot], w_out, hy_b_out[slot])
            h = _hyena_layer(h, rows, "lat", mods, layer, g_mix, *hp)
            if ctx_out:
                h = _hyena_layer(h, rows, "ctx", mods, layer, g_mix, *hp)
        h, _ = ffn(h, layer, 1, ctx_out, final=last)
    return h.reshape(bsz, seq, d)
```

```python
import functools
import math
from typing import NamedTuple

import numpy as np
import jax
import jax.numpy as jnp
from jax import lax
from jax.experimental import pallas as pl
from jax.experimental.pallas import tpu as pltpu

F32 = jnp.float32
BF16 = jnp.bfloat16

GRID_W = 64
ADA_CHUNKS = 9
NORM_EPS = 1e-6
RET_HEADS = 4
ROPE_BASE = 10000.0
POOL_WINDOWS = (2, 4, 8, 16)
HYENA_ORDER = 2
HYENA_EMB = 33
HYENA_BANDS = (HYENA_EMB - 1) // 2
HYENA_TARGET = 1e-2
HYENA_FAST = 0.3
HYENA_SLOW = 1.5
HYENA_MAX_DECAY = math.log(HYENA_TARGET) / HYENA_FAST
HYENA_MIN_DECAY = math.log(HYENA_TARGET) / HYENA_SLOW

V7X_LANES = 128
V7X_SUBLANES = 8
V7X_BF16_SUBLANES = 16
V7X_VMEM_LIMIT_BYTES = 56 * 1024 * 1024
MOD_ROWS = V7X_SUBLANES
ROW_TILE = 512
FFN_CAST_SLABS = 16
POOL_TILE = 512
RET_TILE = 256
RET_OUT_TILE = 1024
SCAN_CHUNK = 256
DFT_RADIX = 4
DFT_FREQ_TILE = 512
DFT_PACK = 256
DFT_ROW_GROUP = 1024
DFT_TIME_TILE = 2048
DFT_COL_TILE = 256
HIGHEST = lax.Precision.HIGHEST


def _params(n_axes):
    return pltpu.CompilerParams(
        dimension_semantics=("arbitrary",) * n_axes,
        vmem_limit_bytes=V7X_VMEM_LIMIT_BYTES)


def _const_spec(shape):
    zeros = (0,) * len(shape)
    return pl.BlockSpec(shape, lambda *_: zeros, pipeline_mode=pl.Buffered(1))


def _mod_spec(d, layer, row_fn):
    return pl.BlockSpec((None, None, ADA_CHUNKS, d), lambda *idx: (layer, row_fn(*idx), 0, 0))


class _Rows(NamedTuple):
    bsz: int
    t_lat: int
    t_ctx: int

    @property
    def n_lat(self):
        return self.bsz * self.t_lat

    @property
    def n_all(self):
        return self.bsz * (self.t_lat + self.t_ctx)

    def length(self, part):
        return self.t_lat if part == "lat" else self.t_ctx

    def block(self, part, tile):
        per = self.length(part) // tile
        base = 0 if part == "lat" else self.n_lat // tile
        assert self.length(part) % tile == 0 and self.n_lat % tile == 0
        return lambda b, i: base + b * per + i


def _mm(a, b):
    return jnp.dot(a, b, preferred_element_type=F32)


def _mm_f32(a, b):
    return jnp.dot(a, b, preferred_element_type=F32, precision=HIGHEST)


def _silu(x):
    return x * jax.nn.sigmoid(x)


def _adaln(x, g, shift, scale):
    ms = jnp.mean(x * x, axis=-1, keepdims=True)
    y = x * lax.rsqrt(ms + NORM_EPS) * g
    return y * (1.0 + scale) + shift


def _ada_kernel(c_ref, w_ref, b_ref, o_ref):
    s = _silu(c_ref[...]).astype(BF16)
    o_ref[...] = _mm(s, w_ref[...].astype(BF16)) + b_ref[...]


def _ada_all(cond, ada_w, ada_b):
    depth, d, nd = ada_w.shape
    return pl.pallas_call(
        _ada_kernel,
        grid=(depth, nd // d),
        in_specs=[
            pl.BlockSpec((MOD_ROWS, d), lambda l, j: (0, 0)),
            pl.BlockSpec((None, d, d), lambda l, j: (l, 0, j)),
            pl.BlockSpec((None, 1, d), lambda l, j: (l, 0, j)),
        ],
        out_specs=pl.BlockSpec((None, MOD_ROWS, d), lambda l, j: (l, 0, j)),
        out_shape=jax.ShapeDtypeStruct((depth, MOD_ROWS, nd), F32),
        compiler_params=_params(2),
        name="ada",
    )(cond, ada_w, ada_b.reshape(depth, 1, nd))


def _ffn_kernel(*refs, base, final, n_lat, split_input, n_cast):
    n_x = 2 if split_input else 1
    x_refs, refs = refs[:n_x], refs[n_x:]
    mod_ref, g_ref, w1_ref, w3_ref, w2_ref, fg_ref = refs[:6]
    rest = refs[6:]
    if n_cast:
        n_jobs = (len(rest) - 1) // 2
        src_refs, o_ref, dst_refs = rest[:n_jobs], rest[n_jobs], rest[n_jobs + 1:]

        @pl.when(pl.program_id(0) < n_cast)
        def _():
            for src, dst in zip(src_refs, dst_refs):
                dst[...] = src[...].astype(BF16)
    else:
        (o_ref,) = rest
    if split_input:
        x = jnp.where(pl.program_id(0) >= n_lat, x_refs[1][...], x_refs[0][...])
    else:
        x = x_refs[0][...]
    u = _adaln(x, g_ref[...], mod_ref[base:base + 1, :], mod_ref[base + 1:base + 2, :]).astype(BF16)
    h1 = _mm(u, w1_ref[...])
    h3 = _mm(u, w3_ref[...])
    a = (_silu(h1) * h3).astype(BF16)
    y = x + (0.5 * mod_ref[base + 2:base + 3, :]) * _mm(a, w2_ref[...])
    if final:
        ms = jnp.mean(y * y, axis=-1, keepdims=True)
        y = y * lax.rsqrt(ms + NORM_EPS) * fg_ref[...]
    o_ref[...] = y


def _half_ffn(h, rows, mods, layer, half, g, weights, final_g, ctx_live, cast_jobs=(), final=False):
    w1, w3, w2 = weights
    d, f = w1.shape
    tm = ROW_TILE
    base = 6 * half
    per_b = rows.t_lat // tm
    n_lat = rows.n_lat // tm
    n_ctx = rows.bsz * rows.t_ctx // tm if ctx_live else 0
    assert rows.t_lat % tm == 0 and (rows.bsz * rows.t_ctx) % tm == 0
    split_input = isinstance(h, tuple)

    if split_input:
        x_specs = [pl.BlockSpec((tm, d), lambda s: (jnp.minimum(s, n_lat - 1), 0)),
                   pl.BlockSpec((tm, d), lambda s: (jnp.maximum(s - n_lat, 0), 0))]
        x_args = list(h)
    else:
        x_specs = [pl.BlockSpec((tm, d), lambda s: (s, 0))]
        x_args = [h]
    in_specs = x_specs + [
        _mod_spec(d, layer, lambda s: jnp.where(s < n_lat, s // per_b, rows.bsz)),
        _const_spec((1, d)),
        _const_spec((d, f)),
        _const_spec((d, f)),
        _const_spec((f, d)),
        _const_spec((1, d)),
    ]
    args = x_args + [mods, g.reshape(1, d), w1, w3, w2, final_g.reshape(1, d)]
    out_specs = [pl.BlockSpec((tm, d), lambda s: (s, 0))]
    out_shape = [jax.ShapeDtypeStruct((rows.n_lat if final else rows.n_all, d), F32)]
    n_cast = min(FFN_CAST_SLABS, n_lat) if cast_jobs else 0
    for w, lead in cast_jobs:
        r, c = w.shape[-2:]
        assert r % (V7X_BF16_SUBLANES * n_cast) == 0 and len(lead) == w.ndim - 2
        in_specs.append(pl.BlockSpec((None,) * len(lead) + (r // n_cast, c),
                                     functools.partial(lambda lead, s: lead + (jnp.minimum(s, n_cast - 1), 0), lead)))
        out_specs.append(pl.BlockSpec((r // n_cast, c), lambda s: (jnp.minimum(s, n_cast - 1), 0)))
        out_shape.append(jax.ShapeDtypeStruct((r, c), BF16))
        args.append(w)
    outs = pl.pallas_call(
        functools.partial(_ffn_kernel, base=base, final=final, n_lat=n_lat, split_input=split_input,
                          n_cast=n_cast),
        grid=(n_lat + n_ctx,),
        in_specs=in_specs,
        out_specs=out_specs,
        out_shape=out_shape,
        compiler_params=_params(1),
        name="half_ffn",
    )(*args)
    return outs[0], tuple(outs[1:])


def _ret_proj_kernel(xc_ref, xl_ref, mod_ref, g_ref, w_ref, cos_ref, sin_ref, q_ref, k_ref, v_ref, gate_ref,
                     *, n_heads, dk, dv, lat_tiles):
    x = jnp.where(pl.program_id(1) < lat_tiles, xl_ref[...], xc_ref[...])
    u = _adaln(x, g_ref[...], mod_ref[3:4, :], mod_ref[4:5, :]).astype(BF16)
    p = _mm(u, w_ref[...])
    cos = cos_ref[...]
    sin = sin_ref[...]
    half = dk // 2
    qk = n_heads * dk
    k_scale = dk ** -0.5

    def rot(a):
        a1, a2 = a[:, :half], a[:, half:]
        return jnp.concatenate([a1 * cos - a2 * sin, a2 * cos + a1 * sin], axis=-1)

    for h in range(n_heads):
        q_ref[:, h * dk:(h + 1) * dk] = rot(p[:, h * dk:(h + 1) * dk]).astype(BF16)
        k_ref[:, h * dk:(h + 1) * dk] = (rot(p[:, qk + h * dk:qk + (h + 1) * dk]) * k_scale).astype(BF16)
    v_ref[...] = p[:, 2 * qk:2 * qk + n_heads * dv].astype(BF16)
    gate_ref[...] = p[:, 2 * qk + n_heads * dv:].astype(BF16)


def _ret_scan_kernel(q_ref, k_ref, v_ref, dmat_ref, qdec_ref, kdec_ref, cdec_ref, o_ref,
                     state_ref, fwd_ref, *, n_heads, dk, dv, n_chunks, chunk_of_step):
    s = pl.program_id(1)
    cs = q_ref.shape[0]

    @pl.when((s == 0) | (s == n_chunks))
    def _():
        state_ref[...] = jnp.zeros_like(state_ref)

    rows = pl.ds(pl.multiple_of(chunk_of_step(s) * cs, cs), cs)
    outs = []
    for h in range(n_heads):
        q = q_ref[:, h * dk:(h + 1) * dk]
        k = k_ref[:, h * dk:(h + 1) * dk]
        v = v_ref[:, h * dv:(h + 1) * dv]
        scores = lax.dot_general(q, k, (((1,), (1,)), ((), ())), preferred_element_type=F32) * dmat_ref[h]
        state = state_ref[h]
        outs.append(_mm(scores.astype(BF16), v) + _mm(q, state.astype(BF16)) * qdec_ref[h])
        kd = (k.astype(F32) * kdec_ref[h]).astype(BF16)
        state_ref[h] = state * cdec_ref[h] + lax.dot_general(
            kd, v, (((0,), (0,)), ((), ())), preferred_element_type=F32)

    @pl.when(s < n_chunks)
    def _():
        for h in range(n_heads):
            fwd_ref[rows, h * dv:(h + 1) * dv] = outs[h].astype(fwd_ref.dtype)

    @pl.when(s >= n_chunks)
    def _():
        for h in range(n_heads):
            o = outs[h] + fwd_ref[rows, h * dv:(h + 1) * dv].astype(F32)
            mu = jnp.mean(o, axis=-1, keepdims=True)
            oc = o - mu
            var = jnp.mean(oc * oc, axis=-1, keepdims=True)
            o_ref[:, h * dv:(h + 1) * dv] = (oc * lax.rsqrt(var + NORM_EPS)).astype(o_ref.dtype)


def _ret_out_kernel(x_ref, mod_ref, y_ref, gate_ref, w_ref, o_ref):
    z =(_silu(gate_ref[...].astype(F32)) * y_ref[...].astype(F32)).astype(BF16)
    o_ref[...] = x_ref[...] + mod_ref[5:6, :] * _mm(z, w_ref[...])


def _retention_layer(h, rows, mods, layer, g, w_in, w_out, decay, ctx_out):
    bsz, t_lat, t_ctx = rows
    d = h.shape[1]
    t_all = t_ctx + t_lat
    n_heads = RET_HEADS
    dk = d // n_heads
    dv = 2 * d // n_heads
    qk = n_heads * dk
    vw = n_heads * dv
    tm = RET_TILE
    assert t_ctx % tm == 0 and t_lat % tm == 0
    lat_tiles = t_lat // tm
    ctx_row = bsz
    lat_blk, ctx_blk = rows.block("lat", tm), rows.block("ctx", tm)

    pos = jnp.concatenate([jnp.arange(t_lat, dtype=F32) + float(t_ctx), jnp.arange(t_ctx, dtype=F32)])
    inv = 1.0 / (ROPE_BASE ** jnp.linspace(0.0, 1.0, dk // 2, dtype=F32))
    ang = pos[:, None] * inv[None, :]

    q, k, v, gate = pl.pallas_call(
        functools.partial(_ret_proj_kernel, n_heads=n_heads, dk=dk, dv=dv, lat_tiles=lat_tiles),
        grid=(bsz, t_all // tm),
        in_specs=[
            pl.BlockSpec((tm, d), lambda b, i: (ctx_blk(b, jnp.maximum(i - lat_tiles, 0)), 0)),
            pl.BlockSpec((tm, d), lambda b, i: (lat_blk(b, jnp.minimum(i, lat_tiles - 1)), 0)),
            _mod_spec(d, layer, lambda b, i: jnp.where(i < lat_tiles, b, ctx_row)),
            _const_spec((1, d)),
            _const_spec((d, 2 * qk + 2 * vw)),
            pl.BlockSpec((tm, dk // 2), lambda b, i: (i, 0)),
            pl.BlockSpec((tm, dk // 2), lambda b, i: (i, 0)),
        ],
        out_specs=[
            pl.BlockSpec((None, tm, qk), lambda b, i: (b, i, 0)),
            pl.BlockSpec((None, tm, qk), lambda b, i: (b, i, 0)),
            pl.BlockSpec((None, tm, vw), lambda b, i: (b, i, 0)),
            pl.BlockSpec((None, tm, vw), lambda b, i: (b, i, 0)),
        ],
        out_shape=[
            jax.ShapeDtypeStruct((bsz, t_all, qk), BF16),
            jax.ShapeDtypeStruct((bsz, t_all, qk), BF16),
            jax.ShapeDtypeStruct((bsz, t_all, vw), BF16),
            jax.ShapeDtypeStruct((bsz, t_all, vw), BF16),
        ],
        compiler_params=_params(2),
        name="ret_proj",
    )(h, h, mods, g.reshape(1, d), w_in, jnp.cos(ang), jnp.sin(ang))

    cs = SCAN_CHUNK
    assert t_ctx % cs == 0 and t_lat % cs == 0
    n_chunks = t_all // cs
    ctx_chunks = t_ctx // cs
    lat_chunks = t_lat // cs
    lg = jax.nn.log_sigmoid(decay.astype(F32))[:, :, None]
    n = jnp.arange(cs, dtype=F32)
    rel = n[:, None] - n[None, :]
    dm_f = jnp.where(rel >= 0, jnp.exp(lg[0][:, :, None] * jnp.maximum(rel, 0.0)), 0.0)
    dm_b = jnp.where(rel < 0, jnp.exp(lg[1][:, :, None] * jnp.maximum(-rel, 0.0)), 0.0)
    dmat = jnp.stack([dm_f, dm_b])
    qd = jnp.stack([jnp.exp(lg[0] * (n + 1.0)), jnp.exp(lg[1] * (cs - n))])
    kd = jnp.stack([jnp.exp(lg[0] * (cs - 1.0 - n)), jnp.exp(lg[1] * n)])
    cd = jnp.exp(lg * cs)
    qdec = jnp.broadcast_to(qd[..., None], (2, n_heads, cs, dv))
    kdec = jnp.broadcast_to(kd[..., None], (2, n_heads, cs, dk))
    cdec = jnp.broadcast_to(cd[..., None], (2, n_heads, 1, dv))

    def fwd_chunk(j):
        return jnp.where(j < ctx_chunks, lat_chunks + j, j - ctx_chunks)

    def bwd_chunk(j):
        return n_chunks - 1 - j

    def chunk(s):
        return jnp.where(s < n_chunks, fwd_chunk(s), bwd_chunk(jnp.maximum(s - n_chunks, 0)))

    def out_chunk(s):
        return bwd_chunk(jnp.maximum(s - n_chunks, 0))

    def table_spec(*shape):
        return pl.BlockSpec((None,) + shape, lambda b, s: (s // n_chunks, 0, 0, 0))

    y = pl.pallas_call(
        functools.partial(_ret_scan_kernel, n_heads=n_heads, dk=dk, dv=dv, n_chunks=n_chunks,
                          chunk_of_step=chunk),
        grid=(bsz, 2 * n_chunks),
        in_specs=[
            pl.BlockSpec((None, cs, qk), lambda b, s: (b, chunk(s), 0)),
            pl.BlockSpec((None, cs, qk), lambda b, s: (b, chunk(s), 0)),
            pl.BlockSpec((None, cs, vw), lambda b, s: (b, chunk(s), 0)),
            table_spec(n_heads, cs, cs),
            table_spec(n_heads, cs, dv),
            table_spec(n_heads, cs, dk),
            table_spec(n_heads, 1, dv),
        ],
        out_specs=pl.BlockSpec((None, cs, vw), lambda b, s: (b, out_chunk(s), 0)),
        out_shape=jax.ShapeDtypeStruct((bsz, t_all, vw), BF16),
        scratch_shapes=[pltpu.VMEM((n_heads, dk, dv), F32), pltpu.VMEM((t_all, vw), BF16)],
        compiler_params=_params(2),
        name="ret_scan",
    )(q, k, v, dmat, qdec, kdec, cdec)

    def readout(buf, part, tile, first_tile, row_fn):
        blk = rows.block(part, tile)
        return pl.pallas_call(
            _ret_out_kernel,
            grid=(bsz, rows.length(part) // tile),
            in_specs=[
                pl.BlockSpec((tile, d), lambda b, i: (blk(b, i), 0)),
                _mod_spec(d, layer, row_fn),
                pl.BlockSpec((None, tile, vw), lambda b, i: (b, i + first_tile, 0)),
                pl.BlockSpec((None, tile, vw), lambda b, i: (b, i + first_tile, 0)),
                _const_spec((vw, d)),
            ],
            out_specs=pl.BlockSpec((tile, d), lambda b, i: (blk(b, i), 0)),
            out_shape=jax.ShapeDtypeStruct(h.shape, F32),
            input_output_aliases={0: 0},
            compiler_params=_params(2),
            name="ret_out",
        )(buf, mods, y, gate, w_out)

    h = readout(h, "lat", min(RET_OUT_TILE, t_lat), 0, lambda b, i: b)
    if ctx_out:
        h = readout(h, "ctx", tm, lat_tiles, lambda b, i: ctx_row)
    return h


def _pool_tables(seg_len):
    assert POOL_TILE % seg_len == 0
    r = np.arange(POOL_TILE)
    seg, pos = r // seg_len, r % seg_len
    mats, invs = [], []
    for win in POOL_WINDOWS:
        lo = np.clip(pos - win // 2, 0, seg_len)
        hi = np.clip(pos - win // 2 + win, 0, seg_len)
        col_seg, col_pos = seg[None, :], pos[None, :]
        member = (col_seg == seg[:, None]) & (col_pos >= lo[:, None]) & (col_pos < hi[:, None])
        mats.append(member.astype(np.float32))
        invs.append((1.0 / (hi - lo)).astype(np.float32))
    return np.stack(mats), np.stack(invs)


def _pool_kernel(x_ref, mod_ref, g_ref, s_ref, inv_ref, w_ref, b_ref, sc_ref, o_ref, *, n_groups, dg):
    x = x_ref[...]
    u = _adaln(x, g_ref[...], mod_ref[3:4, :], mod_ref[4:5, :])
    gate = mod_ref[5:6, :]
    for gi in range(n_groups):
        sl = slice(gi * dg, (gi + 1) * dg)
        ug = u[:, sl]
        hi = ug.astype(BF16)
        lo = (ug - hi.astype(F32)).astype(BF16)
        win_sum = _mm(s_ref[gi], hi) + _mm(s_ref[gi], lo)
        dlt = win_sum * inv_ref[gi] - ug
        yg = _mm(dlt.astype(BF16), w_ref[gi])
        o_ref[:, sl] = x[:, sl] + gate[:, sl] * ((yg + b_ref[:, sl]) * sc_ref[:, sl])


def _pool_layer(h, rows, lat_seg, mods, layer, g, w_grp, b_grp, scale, ctx_out):
    bsz, t_lat, t_ctx = rows
    d = h.shape[1]
    n_groups = len(POOL_WINDOWS)
    dg = d // n_groups
    tabs = [_pool_tables(lat_seg), _pool_tables(t_ctx)]
    s_tab = jnp.asarray(np.stack([m for m, _ in tabs]), BF16)
    inv_tab = jnp.asarray(np.stack([np.broadcast_to(v[:, :, None], (n_groups, POOL_TILE, dg)) for _, v in tabs]), F32)
    assert t_lat % POOL_TILE == 0 and (bsz * t_ctx) % POOL_TILE == 0
    per_b = t_lat // POOL_TILE
    n_lat = rows.n_lat // POOL_TILE
    n_ctx = bsz * t_ctx // POOL_TILE if ctx_out else 0

    def table_spec(*shape):
        return pl.BlockSpec((None,) + shape, lambda s: (jnp.where(s < n_lat, 0, 1), 0, 0, 0))

    return pl.pallas_call(
        functools.partial(_pool_kernel, n_groups=n_groups, dg=dg),
        grid=(n_lat + n_ctx,),
        in_specs=[
            pl.BlockSpec((POOL_TILE, d), lambda s: (s, 0)),
            _mod_spec(d, layer, lambda s: jnp.where(s < n_lat, s // per_b, bsz)),
            _const_spec((1, d)),
            table_spec(n_groups, POOL_TILE, POOL_TILE),
            table_spec(n_groups, POOL_TILE, dg),
            _const_spec((n_groups, dg, dg)),
            _const_spec((1, d)),
            _const_spec((1, d)),
        ],
        out_specs=pl.BlockSpec((POOL_TILE, d), lambda s: (s, 0)),
        out_shape=jax.ShapeDtypeStruct(h.shape, F32),
        compiler_params=_params(1),
        name="pool",
    )(h, mods, g.reshape(1, d), s_tab, inv_tab, w_grp.astype(BF16), b_grp.reshape(1, d), scale.reshape(1, d))


def _dft_tables(n_in):
    mod = 4 * n_in
    sub_n = min(V7X_LANES, n_in)
    idx = jnp.arange(n_in, dtype=jnp.int32)[None, :]
    blk = jnp.arange(n_in // sub_n, dtype=jnp.int32)[:, None]
    sub = jnp.arange(sub_n, dtype=jnp.int32)[:, None]
    theta = 2.0 * math.pi / mod

    def cs(p):
        ph = (p % mod).astype(F32) * theta
        return jnp.cos(ph), jnp.sin(ph)

    fwd = cs(2 * sub_n * blk * idx) + cs((2 * sub + 1) * idx)
    inv = cs(sub_n * blk * (2 * idx + 1)) + cs(sub * (2 * idx + 1))
    return fwd, inv


def _dft_fwd_gen_kernel(pc_ref, ps_ref, qc_ref, qs_ref, o_ref, *, tf):
    ft = pl.program_id(0)
    sub_n = qc_ref.shape[0]
    blocks = tf // sub_n
    qc = qc_ref[...]
    qs = qs_ref[...]
    for jb in range(blocks):
        kb = ft * blocks + jb
        pc = pc_ref[pl.ds(kb, 1), :]
        ps = ps_ref[pl.ds(kb, 1), :]
        o_ref[jb * sub_n:(jb + 1) * sub_n, :] = (pc * qc - ps * qs).astype(BF16)
        o_ref[tf + jb * sub_n:tf + (jb + 1) * sub_n, :] = (ps * qc + pc * qs).astype(BF16)


def _dft_inv_gen_kernel(pc_ref, ps_ref, qc_ref, qs_ref, o_ref, *, tf, n_in):
    mb = pl.program_id(0)
    pc = pc_ref[pl.ds(mb, 1), :]
    ps = ps_ref[pl.ds(mb, 1), :]
    qc = qc_ref[...]
    qs = qs_ref[...]
    cos_all = pc * qc - ps * qs
    sin_all = ps * qc + pc * qs
    for ft in range(n_in // tf):
        o_ref[:, 2 * ft * tf:(2 * ft + 1) * tf] = cos_all[:, ft * tf:(ft + 1) * tf].astype(BF16)
        o_ref[:, (2 * ft + 1) * tf:(2 * ft + 2) * tf] = sin_all[:, ft * tf:(ft + 1) * tf].astype(BF16)


def _dft_matrices(n_in, tf):
    fwd_tabs, inv_tabs = _dft_tables(n_in)
    sub_n = min(V7X_LANES, n_in)
    nb = n_in // sub_n
    tabs = [_const_spec((nb, n_in)), _const_spec((nb, n_in)),
            _const_spec((sub_n, n_in)), _const_spec((sub_n, n_in))]
    fwd = pl.pallas_call(
        functools.partial(_dft_fwd_gen_kernel, tf=tf),
        grid=(n_in // tf,),
        in_specs=tabs,
        out_specs=pl.BlockSpec((2 * tf, n_in), lambda i: (i, 0)),
        out_shape=jax.ShapeDtypeStruct((2 * n_in, n_in), BF16),
        compiler_params=_params(1),
        name="dft_fwd_gen",
    )(*fwd_tabs)
    inv = pl.pallas_call(
        functools.partial(_dft_inv_gen_kernel, tf=tf, n_in=n_in),
        grid=(nb,),
        in_specs=tabs,
        out_specs=pl.BlockSpec((sub_n, 2 * n_in), lambda i: (i, 0)),
        out_shape=jax.ShapeDtypeStruct((n_in, 2 * n_in), BF16),
        compiler_params=_params(1),
        name="dft_inv_gen",
    )(*inv_tabs)
    return fwd, inv


def _hy_proj_kernel(x_ref, xp_ref, xn_ref, mod_ref, g_ref, w_ref, b_ref, ws_ref, bs_ref,
                    v_ref, x1_ref, x2_ref):
    i = pl.program_id(1)
    tm, d = x_ref.shape
    n_ph = v_ref.shape[0]
    hp = tm // n_ph
    g = g_ref[...]
    shift, scale = mod_ref[3:4, :], mod_ref[4:5, :]
    u = _adaln(x_ref[...], g, shift, scale).astype(BF16)
    u = jnp.swapaxes(u.reshape(hp, n_ph, d), 0, 1).reshape(tm, d)
    halo = jnp.concatenate([xp_ref[...], xn_ref[...]], axis=0)
    uh = _adaln(halo, g, shift, scale).astype(BF16)
    p = _mm(jnp.concatenate([u, uh], axis=0), w_ref[...]) + b_ref[...]
    ph = [p[r * hp:(r + 1) * hp] for r in range(n_ph)]
    n_halo = xp_ref.shape[0]
    before = jnp.where(i == 0, 0.0, p[tm + n_halo - 1:tm + n_halo])
    after = jnp.where(i == pl.num_programs(1) - 1, 0.0, p[tm + n_halo:tm + n_halo + 1])
    row = lax.broadcasted_iota(jnp.int32, ph[0].shape, 0)
    last_prev = jnp.where(row == 0, before, pltpu.roll(ph[-1], 1, axis=0))
    first_next = jnp.where(row == hp - 1, after, pltpu.roll(ph[0], hp - 1, axis=0))
    w0, w1, w2, bs = ws_ref[0:1, :], ws_ref[1:2, :], ws_ref[2:3, :], bs_ref[...]
    for r in range(n_ph):
        prev = ph[r - 1] if r > 0 else last_prev
        nxt = ph[r + 1] if r < n_ph - 1 else first_next
        conv = prev * w0 + ph[r] * w1 + nxt * w2 + bs
        v_ref[r] = conv[:, :d].astype(BF16)
        x1_ref[r] = conv[:, d:2 * d].astype(BF16)
        x2_ref[r] = conv[:, 2 * d:].astype(BF16)


def _hy_filter_kernel(feat_ref, wp_ref, bp_ref, wm_ref, bm_ref, fr_ref, wf_ref, dl_ref,
                      hf_ref, hb_ref, ss_ref, *, d, n_orders):
    start = (pl.program_id(0) == 0) & (pl.program_id(1) == 0)
    feat = feat_ref[...]
    fr = fr_ref[...]
    hdn = jnp.sin(fr * (_mm_f32(feat, wp_ref[...]) + bp_ref[...]))
    hdn = jnp.sin(fr * (_mm_f32(hdn, wm_ref[...]) + bm_ref[...]))
    h = _mm(hdn.astype(BF16), wf_ref[...])
    decay = jnp.exp(-feat[:, 0:1] * dl_ref[...])
    row = lax.broadcasted_iota(jnp.int32, decay.shape, 0)
    first = (row == 0) & start

    @pl.when(start)
    def _():
        ss_ref[...] = jnp.zeros_like(ss_ref)

    for o in range(n_orders):
        sl = slice(o * d, (o + 1) * d)
        hf = h[:, 2 * o * d:(2 * o + 1) * d] * decay
        hb = jnp.where(first, 0.0, h[:, (2 * o + 1) * d:(2 * o + 2) * d] * decay)
        hf_ref[:, sl] = hf.astype(BF16)
        hb_ref[:, sl] = hb.astype(BF16)
        ss_ref[:, sl] += jnp.sum(hf * hf + hb * hb, axis=0, keepdims=True)


def _twiddle(a, c, s, conj=False):
    a_re, a_s = a
    if conj:
        return a_re * c + a_s * s, a_s * c - a_re * s
    return a_re * c - a_s * s, a_s * c + a_re * s


def _add(a, b):
    return a[0] + b[0], a[1] + b[1]


def _sub(a, b):
    return a[0] - b[0], a[1] - b[1]


def _dit_blocks(dft, phase_refs, tw, tp):
    ph = []
    for ref in phase_refs:
        t = _mm(dft, ref[...])
        ph.append((t[0:tp], t[tp:]))
    c1, s1, c2, s2, c3, s3 = tw
    t2, t3 = _twiddle(ph[2], c1, s1), _twiddle(ph[3], c1, s1)
    e_a, e_b = _add(ph[0], t2), _sub(ph[0], t2)
    o_a, o_b = _add(ph[1], t3), _sub(ph[1], t3)
    ta, tb = _twiddle(o_a, c2, s2), _twiddle(o_b, c3, s3, conj=True)
    return [_add(e_a, ta), _sub(e_a, ta), _add(e_b, tb), _sub(e_b, tb)]


def _bin_group(f_ref, tw_ref, g, tp):
    re, sn = slice(2 * g * tp, (2 * g + 1) * tp), slice((2 * g + 1) * tp, (2 * g + 2) * tp)
    dft = f_ref[2 * g * tp:(2 * g + 2) * tp, :]
    tw = [tw_ref[i, g * tp:(g + 1) * tp, :] for i in range(6)]
    return re, sn, dft, tw


def _hy_spec_kernel(f_ref, f0, f1, f2, f3, b0, b1, b2, b3, tw_ref, ss_ref, o_ref, *, tp, scale):
    wgt = scale * lax.rsqrt(ss_ref[...] + NORM_EPS)
    for g in range(tw_ref.shape[1] // tp):
        re, sn, dft, tw = _bin_group(f_ref, tw_ref, g, tp)
        fwd = _dit_blocks(dft, (f0, f1, f2, f3), tw, tp)
        bwd = _dit_blocks(dft, (b0, b1, b2, b3), tw, tp)
        for blk in range(4):
            o_ref[blk, re, :] = (fwd[blk][0] + bwd[blk][0]) * wgt
            o_ref[blk, sn, :] = (fwd[blk][1] - bwd[blk][1]) * wgt


def _hy_fwd_kernel(f_ref, u0, u1, u2, u3, h_ref, tw_ref, y_ref, *, tp):
    for g in range(tw_ref.shape[1] // tp):
        re, sn, dft, tw = _bin_group(f_ref, tw_ref, g, tp)
        c1, s1, c2, s2, c3, s3 = tw
        x = _dit_blocks(dft, (u0, u1, u2, u3), tw, tp)
        y = []
        for blk in range(4):
            h_re, h_s = h_ref[blk, re, :], h_ref[blk, sn, :]
            x_re, x_s = x[blk]
            y.append((x_re * h_re - x_s * h_s, x_re * h_s + x_s * h_re))
        e_a, o_a = _add(y[0], y[1]), _twiddle(_sub(y[0], y[1]), c2, s2, conj=True)
        e_b, o_b = _add(y[2], y[3]), _twiddle(_sub(y[2], y[3]), c3, s3)
        out = [_add(e_a, e_b), _add(o_a, o_b),
               _twiddle(_sub(e_a, e_b), c1, s1, conj=True), _twiddle(_sub(o_a, o_b), c1, s1, conj=True)]
        for r in range(4):
            y_ref[r, re, :] = out[r][0].astype(BF16)
            y_ref[r, sn, :] = out[r][1].astype(BF16)


def _hy_inv_kernel(f_ref, y_ref, a_ref, u_ref, fb_ref, o_ref, *, rows):
    for r in range(f_ref.shape[0] // rows):
        sl = slice(r * rows, (r + 1) * rows)
        conv = _mm(f_ref[sl, :], y_ref[...])
        z = a_ref[sl, :].astype(F32) * (conv + u_ref[sl, :].astype(F32) * fb_ref[...])
        o_ref[sl, :] = z.astype(o_ref.dtype)


def _hy_out_kernel(x_ref, mod_ref, z_ref, w_ref, b_ref, o_ref):
    n_ph, hp, d = z_ref.shape
    z = jnp.stack([z_ref[r] for r in range(n_ph)], axis=1).reshape(n_ph * hp, d)
    o_ref[...] = x_ref[...] + mod_ref[5:6, :] * (_mm(z, w_ref[...]) + b_ref[...])


def _hyena_layer(h, rows, part, mods, layer, g, w_in, b_in, w_short, b_short, w_pos, b_pos, w_mid,
                 b_mid, freq, w_filt, fbias, w_out, b_out):
    bsz = rows.bsz
    length = rows.length(part)
    d = h.shape[1]
    mod_row = (lambda b, i: b) if part == "lat" else (lambda b, i: bsz)
    n_full = 2 * length
    n_ph = DFT_RADIX
    q = length // n_ph
    tm = min(ROW_TILE, length)
    hp = tm // n_ph
    tc = DFT_COL_TILE if q >= DFT_FREQ_TILE else d
    n_ct = d // tc
    width = w_mid.shape[0]
    n_parts = 2 * HYENA_ORDER

    halo = V7X_SUBLANES
    ph_spec = pl.BlockSpec((None, n_ph, hp, d), lambda b, i: (b, 0, i, 0))
    tile_blk, halo_blk = rows.block(part, tm), rows.block(part, halo)
    v, x1, x2 = pl.pallas_call(
        _hy_proj_kernel,
        grid=(bsz, length // tm),
        in_specs=[
            pl.BlockSpec((tm, d), lambda b, i: (tile_blk(b, i), 0)),
            pl.BlockSpec((halo, d), lambda b, i: (halo_blk(b, jnp.maximum(i * (tm // halo) - 1, 0)), 0)),
            pl.BlockSpec((halo, d),
                         lambda b, i: (halo_blk(b, jnp.minimum((i + 1) * (tm // halo), length // halo - 1)), 0)),
            _mod_spec(d, layer, mod_row),
            _const_spec((1, d)),
            _const_spec((d, 3 * d)),
            _const_spec((1, 3 * d)),
            _const_spec((3, 3 * d)),
            _const_spec((1, 3 * d)),
        ],
        out_specs=[ph_spec] * 3,
        out_shape=[jax.ShapeDtypeStruct((bsz, n_ph, q, d), BF16)] * 3,
        compiler_params=_params(2),
        name="hy_proj",
    )(h, h, h, mods, g.reshape(1, d), w_in, b_in.reshape(1, 3 * d),
      w_short, b_short.reshape(1, 3 * d))

    t = jnp.linspace(0.0, 1.0, length, dtype=F32)[:, None]
    bands = jnp.linspace(1e-4, HYENA_BANDS - 1, HYENA_BANDS, dtype=F32)
    ang = (2.0 * math.pi / length) * jnp.arange(length, dtype=F32)[:, None] * bands[None, :]
    feat = jnp.concatenate([t, jnp.cos(ang), -jnp.sin(ang)], axis=-1)
    feat = jnp.pad(feat, ((0, 0), (0, V7X_LANES - HYENA_EMB)))
    w_pos_p = jnp.pad(w_pos.astype(F32), ((0, V7X_LANES - HYENA_EMB), (0, 0)))
    deltas = jnp.abs(jnp.linspace(HYENA_MIN_DECAY, HYENA_MAX_DECAY, d, dtype=F32)).reshape(1, d)
    feat = jnp.concatenate([feat[r::n_ph] for r in range(n_ph)], axis=0)
    tl = min(ROW_TILE, q)
    od = HYENA_ORDER * d
    tap_spec = pl.BlockSpec((tl, od), lambda r, i: (i, r))
    taps_f, taps_b, ss = pl.pallas_call(
        functools.partial(_hy_filter_kernel, d=d, n_orders=HYENA_ORDER),
        grid=(n_ph, q // tl),
        in_specs=[
            pl.BlockSpec((tl, V7X_LANES), lambda r, i: (r * (q // tl) + i, 0)),
            _const_spec((V7X_LANES, width)),
            _const_spec((1, width)),
            _const_spec((width, width)),
            _const_spec((1, width)),
            _const_spec((1, width)),
            _const_spec((width, n_parts * d)),
            _const_spec((1, d)),
        ],
        out_specs=[tap_spec, tap_spec, pl.BlockSpec((1, od), lambda r, i: (0, 0))],
        out_shape=[jax.ShapeDtypeStruct((q, n_ph * od), BF16), jax.ShapeDtypeStruct((q, n_ph * od), BF16),
                   jax.ShapeDtypeStruct((1, od), F32)],
        compiler_params=_params(2),
        name="hy_filter",
    )(feat, w_pos_p, b_pos.reshape(1, width), w_mid, b_mid.reshape(1, width), freq.reshape(1, width),
      w_filt.astype(BF16), deltas)

    tf = min(DFT_FREQ_TILE, q)
    tp = min(DFT_PACK, q)
    tn = min(DFT_TIME_TILE, q)
    dft_fwd, dft_inv = _dft_matrices(q, tp)
    odd = 2.0 * jnp.arange(q, dtype=F32) + 1.0
    angles = jnp.stack([(2.0 * math.pi / n_full) * odd, (math.pi / n_full) * odd,
                        (math.pi / n_full) * (length - odd)])
    tw = jnp.stack([jnp.cos(angles), jnp.sin(angles)], axis=1).reshape(6, q)
    tw = jnp.broadcast_to(tw[:, :, None], (6, q, tc))
    tw_spec = pl.BlockSpec((6, tf, tc), lambda f, *_: (0, f, 0))

    n_oc = od // tc

    def tap_phase(r):
        return pl.BlockSpec((q, tc), lambda f, o, j: (0, r * n_oc + o * n_ct + j))

    spec = pl.pallas_call(
        functools.partial(_hy_spec_kernel, tp=tp, scale=1.0 / length),
        grid=(q // tf, HYENA_ORDER, n_ct),
        in_specs=[pl.BlockSpec((2 * tf, q), lambda f, o, j: (f, 0))]
        + [tap_phase(r) for r in range(n_ph)] * 2
        + [tw_spec, pl.BlockSpec((1, tc), lambda f, o, j: (0, o * n_ct + j))],
        out_specs=pl.BlockSpec((None, n_ph, 2 * tf, tc), lambda f, o, j: (o, 0, f, j)),
        out_shape=jax.ShapeDtypeStruct((HYENA_ORDER, n_ph, 2 * q, d), F32),
        compiler_params=_params(3),
        name="hy_spec",
    )(dft_fwd, *([taps_f] * n_ph), *([taps_b] * n_ph), tw, ss)

    tci = d
    n_ci = d // tci

    def long_conv(order, u, a):
        yspec = pl.pallas_call(
            functools.partial(_hy_fwd_kernel, tp=tp),
            grid=(q // tf, n_ct, bsz),
            in_specs=[pl.BlockSpec((2 * tf, q), lambda f, j, b: (f, 0))]
            + [pl.BlockSpec((None, None, q, tc), functools.partial(lambda r, f, j, b: (b, r, 0, j), r))
               for r in range(n_ph)]
            + [pl.BlockSpec((None, n_ph, 2 * tf, tc), lambda f, j, b: (order, 0, f, j)), tw_spec],
            out_specs=pl.BlockSpec((None, n_ph, 2 * tf, tc), lambda f, j, b: (b, 0, f, j)),
            out_shape=jax.ShapeDtypeStruct((bsz, n_ph, 2 * q, d), BF16),
            compiler_params=_params(3),
            name="hy_fwd",
        )(dft_fwd, *([u] * n_ph), spec, tw)
        row_spec = pl.BlockSpec((None, None, tn, tci), lambda n, j, b, r: (b, r, n, j))
        return pl.pallas_call(
            functools.partial(_hy_inv_kernel, rows=min(DFT_ROW_GROUP, tn)),
            grid=(q // tn, n_ci, bsz, n_ph),
            in_specs=[
                pl.BlockSpec((tn, 2 * q), lambda n, j, b, r: (n, 0)),
                pl.BlockSpec((None, None, 2 * q, tci), lambda n, j, b, r: (b, r, 0, j)),
                row_spec,
                row_spec,
                pl.BlockSpec((None, 1, tci), lambda n, j, b, r: (order, 0, j)),
            ],
            out_specs=row_spec,
            out_shape=jax.ShapeDtypeStruct((bsz, n_ph, q, d), BF16),
            compiler_params=_params(4),
            name="hy_inv",
        )(dft_inv, yspec, a, u, fbias.astype(F32).reshape(HYENA_ORDER, 1, d))

    z = long_conv(0, v, x1)
    zz = long_conv(1, z, x2)

    return pl.pallas_call(
        _hy_out_kernel,
        grid=(bsz, length // tm),
        in_specs=[
            pl.BlockSpec((tm, d), lambda b, i: (tile_blk(b, i), 0)),
            _mod_spec(d, layer, mod_row),
            pl.BlockSpec((None, n_ph, hp, d), lambda b, i: (b, 0, i, 0)),
            _const_spec((d, d)),
            _const_spec((1, d)),
        ],
        out_specs=pl.BlockSpec((tm, d), lambda b, i: (tile_blk(b, i), 0)),
        out_shape=jax.ShapeDtypeStruct(h.shape, F32),
        input_output_aliases={0: 0},
        compiler_params=_params(2),
        name="hy_out",
    )(h, mods, zz, w_out, b_out.reshape(1, d))


def kernel(x, c, ctx, c_ctx, ada_w, ada_b, norm_g, ffn_w1, ffn_w3, ffn_w2, ret_w_in, ret_w_out, ret_decay,
           pool_w, pool_b, pool_scale, hy_w_in, hy_b_in, hy_w_short, hy_b_short, hy_w_pos, hy_b_pos,
           hy_w_mid, hy_b_mid, hy_freq, hy_w_filt, hy_bias, hy_w_out, hy_b_out, final_g):
    bsz, seq, d = x.shape
    t_ctx = ctx.shape[1]
    depth = ada_w.shape[0]
    n_mixers = 3
    assert bsz + 1 <= MOD_ROWS

    cond = jnp.concatenate([c, c_ctx[None, :], jnp.zeros((MOD_ROWS - bsz - 1, d), F32)], axis=0)
    mods = _ada_all(cond, ada_w, ada_b).reshape(depth, MOD_ROWS, ADA_CHUNKS, d)

    rows = _Rows(bsz, seq, t_ctx)
    ffn_stacks = (ffn_w1, ffn_w3, ffn_w2)
    ffn_weights = [tuple(w[0, 0].astype(BF16) for w in ffn_stacks)]

    def ffn(h, layer, half, ctx_live, mixer_jobs=(), final=False):
        jobs = [] if final else [(w, (layer, 1) if half == 0 else (layer + 1, 0)) for w in ffn_stacks]
        h, cast = _half_ffn(h, rows, mods, layer, half, norm_g[layer, 2 * half], ffn_weights[0], final_g,
                            ctx_live, cast_jobs=jobs + list(mixer_jobs), final=final)
        ffn_weights[0] = cast[:len(ffn_stacks)]
        return h, cast[len(ffn_stacks):]

    h = (x.reshape(bsz * seq, d), ctx.reshape(bsz * t_ctx, d))
    for layer in range(depth):
        kind = layer % n_mixers
        slot = layer // n_mixers
        last = layer == depth - 1
        ctx_out = not last
        ctx_live = ctx_out or kind == 0
        g_mix = norm_g[layer, 1]
        if kind == 0:
            h, (w_in, w_out) = ffn(h, layer, 0, ctx_live, [(ret_w_in, (slot,)), (ret_w_out, (slot,))])
            h = _retention_layer(h, rows, mods, layer, g_mix, w_in, w_out, ret_decay[slot], ctx_out)
        elif kind == 1:
            h, _ = ffn(h, layer, 0, ctx_live)
            h = _pool_layer(h, rows, GRID_W, mods, layer, g_mix, pool_w[slot], pool_b[slot], pool_scale[slot],
                            ctx_out)
        else:
            h, (w_in, w_out) = ffn(h, layer, 0, ctx_live, [(hy_w_in, (slot,)), (hy_w_out, (slot,))])
            hp = (w_in, hy_b_in[slot], hy_w_short[slot], hy_b_short[slot], hy_w_pos[slot],
                  hy_b_pos[slot], hy_w_mid[slot], hy_b_mid[slot], hy_freq[slot], hy_w_filt[slot],
                  hy_bias[slot], w_out, hy_b_out[slot])
            h = _hyena_layer(h, rows, "lat", mods, layer, g_mix, *hp)
            if ctx_out:
                h = _hyena_layer(h, rows, "ctx", mods, layer, g_mix, *hp)
        h, _ = ffn(h, layer, 1, ctx_out, final=last)
    return h.reshape(bsz, seq, d)
```
